```python
import math
import jax
import jax.numpy as jnp
from jax import lax
import numpy as np

D_MODEL = 1024
BATCH = 2
SEQ = 8192
DEPTH = 2

GRID_W = 64
CTX_LEN = 256
FOUR_GROUPS = 4
FOUR_GROUP_W = D_MODEL // 16
FOUR_WIDTH = FOUR_GROUPS * FOUR_GROUP_W
RET_HEADS = 4
RET_HEAD_DIM = (3 * D_MODEL // 8) // 4
RET_WIDTH = RET_HEADS * RET_HEAD_DIM
NA_HEADS = 6
NA_HEAD_DIM = (3 * D_MODEL // 8) // 6
NA_WIDTH = NA_HEADS * NA_HEAD_DIM
PROJ_WIDTH = FOUR_WIDTH + 4 * RET_WIDTH + 3 * NA_WIDTH
SPLIT_IDX = (FOUR_WIDTH,
             FOUR_WIDTH + RET_WIDTH,
             FOUR_WIDTH + 2 * RET_WIDTH,
             FOUR_WIDTH + 3 * RET_WIDTH,
             FOUR_WIDTH + 4 * RET_WIDTH,
             FOUR_WIDTH + 4 * RET_WIDTH + NA_WIDTH,
             FOUR_WIDTH + 4 * RET_WIDTH + 2 * NA_WIDTH)
RET_CHUNK = 128
NA_WIN_ROWS = 8
NA_WIN_COLS = 16
NA_QBLOCK_COLS = 16
NA_KBLOCK_COLS = NA_QBLOCK_COLS + NA_WIN_COLS
D_FF = ((8 * D_MODEL // 3 + 255) // 256) * 256
ROPE_BASE = 10000.0
NORM_EPS = 1e-6
NEG_INF = -1e30

kernel_name = "hymba_fnet_retnet_natten_dit"


def rmsnorm(x, g):
    x32 = x.astype(jnp.float32)
    y = x32 * lax.rsqrt(jnp.mean(x32 * x32, axis=-1, keepdims=True) + NORM_EPS)
    return (y * g.astype(jnp.float32)).astype(x.dtype)


def ada_params(cvec, w, b):
    return jnp.split(jax.nn.silu(cvec) @ w + b, 6, axis=-1)


def swiglu(h, w1, w3, w2):
    return (jax.nn.silu(h @ w1) * (h @ w3)) @ w2


def rope_1d(x, pos):
    half = x.shape[-1] // 2
    inv = ROPE_BASE ** (-jnp.arange(half, dtype=jnp.float32) / half)
    ang = pos.astype(jnp.float32)[:, None] * inv[None, :]
    cos, sin = jnp.cos(ang), jnp.sin(ang)
    x1, x2 = x[..., :half], x[..., half:]
    return jnp.concatenate([x1 * cos - x2 * sin, x1 * sin + x2 * cos], axis=-1).astype(x.dtype)


def rope_2d(x, rows, cols):
    d = x.shape[-1] // 2
    return jnp.concatenate([rope_1d(x[..., :d], rows), rope_1d(x[..., d:], cols)], axis=-1)


def fourier_mix(u, w_four):
    B, L, _ = u.shape
    g = u.astype(jnp.float32).reshape(B, L, FOUR_GROUPS, FOUR_GROUP_W)
    f = jnp.fft.fft2(g, axes=(1, 3), norm="ortho").real
    return f.reshape(B, L, FOUR_WIDTH).astype(u.dtype) @ w_four


def retention_final_state(k, v, log_gamma):
    L = k.shape[2]
    w = jnp.exp((L - 1 - jnp.arange(L, dtype=jnp.float32))[None, :] * log_gamma[:, None])
    return jnp.einsum('bhld,bhle->bhde', k * w[None, :, :, None], v)


def retention_chunkwise(q, k, v, log_gamma, state0):
    B, H, L, dk = q.shape
    dv = v.shape[-1]
    C = RET_CHUNK
    N = L // C
    qc = q.reshape(B, H, N, C, dk)
    kc = k.reshape(B, H, N, C, dk)
    vc = v.reshape(B, H, N, C, dv)
    idx = jnp.arange(C, dtype=jnp.float32)
    diff = idx[:, None] - idx[None, :]
    intra_decay = jnp.where(diff >= 0, jnp.exp(jnp.maximum(diff, 0.0)[None] * log_gamma[:, None, None]), 0.0)
    s = jnp.einsum('bhnid,bhnjd->bhnij', qc, kc) * intra_decay[None, :, None]
    intra = jnp.einsum('bhnij,bhnje->bhnie', s, vc)
    q_decay = jnp.exp((idx + 1.0)[None, :] * log_gamma[:, None])
    k_decay = jnp.exp((C - 1.0 - idx)[None, :] * log_gamma[:, None])
    kv = jnp.einsum('bhnjd,bhnje->nbhde', kc * k_decay[None, :, None, :, None], vc)
    chunk_decay = jnp.exp(C * log_gamma)[None, :, None, None]

    def step(state, kv_n):
        return chunk_decay * state + kv_n, state

    _, s_prev = lax.scan(step, state0, kv)
    cross = jnp.einsum('bhnid,nbhde->bhnie', qc, s_prev) * q_decay[None, :, None, :, None]
    return (intra + cross).reshape(B, H, L, dv)


def bidirectional_retention(q_lat, k_lat, v_lat, q_ctx, k_ctx, v_ctx, log_gamma, with_ctx_out):
    flip = lambda t: t[:, :, ::-1]
    s_fwd = retention_final_state(k_ctx, v_ctx, log_gamma[0])
    s_bwd = retention_final_state(flip(k_ctx), flip(v_ctx), log_gamma[1])
    o_lat = (retention_chunkwise(q_lat, k_lat, v_lat, log_gamma[0], s_fwd)
             + flip(retention_chunkwise(flip(q_lat), flip(k_lat), flip(v_lat), log_gamma[1], s_bwd)))
    if not with_ctx_out:
        return o_lat, None
    zero = jnp.zeros_like(s_fwd)
    o_ctx = (retention_chunkwise(q_ctx, k_ctx, v_ctx, log_gamma[0], zero)
             + flip(retention_chunkwise(flip(q_ctx), flip(k_ctx), flip(v_ctx), log_gamma[1], zero)))
    return o_lat, o_ctx


def retention_readout(o, gate, gain):
    B, H, L, dv = o.shape
    o = o * lax.rsqrt(jnp.mean(o * o, axis=-1, keepdims=True) + NORM_EPS)
    o = o.transpose(0, 2, 1, 3).reshape(B, L, H * dv) * gain.astype(jnp.float32)
    return (o * jax.nn.silu(gate.astype(jnp.float32))).astype(gate.dtype)


def neighborhood_attention(q, k, v, k_ctx, v_ctx, rpb):
    B, L, H, d = q.shape
    rows = L // GRID_W
    wr = min(NA_WIN_ROWS, rows)
    nj = GRID_W // NA_QBLOCK_COLS
    r = jnp.arange(rows)
    ridx = jnp.clip(r - wr // 2, 0, rows - wr)[:, None] + jnp.arange(wr)[None, :]
    j = jnp.arange(nj)
    cidx = jnp.clip(j * NA_QBLOCK_COLS - NA_WIN_COLS // 2, 0, GRID_W - NA_KBLOCK_COLS)[:, None] \
        + jnp.arange(NA_KBLOCK_COLS)[None, :]
    tok = (ridx[:, None, :, None] * GRID_W + cidx[None, :, None, :]).reshape(rows, nj, wr * NA_KBLOCK_COLS)
    kg = jnp.take(k, tok, axis=1)
    vg = jnp.take(v, tok, axis=1)
    qb = q.reshape(B, rows, nj, NA_QBLOCK_COLS, H, d)
    qcol = j[:, None] * NA_QBLOCK_COLS + jnp.arange(NA_QBLOCK_COLS)[None, :]
    win_start = jnp.clip(qcol - NA_WIN_COLS // 2, 0, GRID_W - NA_WIN_COLS)
    kcol = cidx[:, None, :]
    col_ok = (kcol >= win_start[..., None]) & (kcol < win_start[..., None] + NA_WIN_COLS)
    dr = ridx - r[:, None] + NA_WIN_ROWS - 1
    dc = jnp.clip(kcol - qcol[..., None] + NA_WIN_COLS - 1, 0, 2 * NA_WIN_COLS - 2)
    bias = rpb[:, dr[:, None, None, :, None], dc[None, :, :, None, :]]
    nk = wr * NA_KBLOCK_COLS
    bias = bias.reshape(H, rows, nj, NA_QBLOCK_COLS, nk).astype(jnp.float32)
    mask = jnp.broadcast_to(col_ok[:, :, None, :], (nj, NA_QBLOCK_COLS, wr, NA_KBLOCK_COLS)).reshape(nj, NA_QBLOCK_COLS, nk)
    bias = jnp.where(mask[None, None], bias, NEG_INF)
    scale = d ** -0.5
    s_lat = jnp.einsum('brjqhd,brjkhd->bhrjqk', qb, kg).astype(jnp.float32) * scale + bias[None]
    s_ctx = jnp.einsum('brjqhd,bchd->bhrjqc', qb, k_ctx).astype(jnp.float32) * scale
    p = jax.nn.softmax(jnp.concatenate([s_lat, s_ctx], axis=-1), axis=-1)
    p_lat, p_ctx = p[..., :nk].astype(v.dtype), p[..., nk:].astype(v.dtype)
    out = (jnp.einsum('bhrjqk,brjkhd->brjqhd', p_lat, vg)
           + jnp.einsum('bhrjqc,bchd->brjqhd', p_ctx, v_ctx))
    return out.reshape(B, L, H * d)


def context_attention(q, k, v):
    B, Lc, H, d = q.shape
    s = jnp.einsum('bqhd,bkhd->bhqk', q, k).astype(jnp.float32) * d ** -0.5
    p = jax.nn.softmax(s, axis=-1).astype(v.dtype)
    return jnp.einsum('bhqk,bkhd->bqhd', p, v).reshape(B, Lc, H * d)


def token_mixer(h_lat, h_ctx, w_in, decay_logit, ret_g, w_four, rpb, w_out, with_ctx_out):
    B, L, _ = h_lat.shape
    Lc = h_ctx.shape[1]
    f_l, rq_l, rk_l, rv_l, rg_l, nq_l, nk_l, nv_l = jnp.split(h_lat @ w_in, SPLIT_IDX, axis=-1)
    f_c, rq_c, rk_c, rv_c, rg_c, nq_c, nk_c, nv_c = jnp.split(h_ctx @ w_in, SPLIT_IDX, axis=-1)
    pos = jnp.arange(L)
    prow, pcol = pos // GRID_W, pos % GRID_W

    def ret_heads(t, n):
        return t.reshape(B, n, RET_HEADS, RET_HEAD_DIM).transpose(0, 2, 1, 3).astype(jnp.float32)

    k_scale = RET_HEAD_DIM ** -0.5
    rq = rope_2d(ret_heads(rq_l, L), prow, pcol)
    rk = rope_2d(ret_heads(rk_l, L), prow, pcol) * k_scale
    rv = ret_heads(rv_l, L)
    rq_ctx = ret_heads(rq_c, Lc) if with_ctx_out else None
    rk_ctx = ret_heads(rk_c, Lc) * k_scale
    rv_ctx = ret_heads(rv_c, Lc)
    log_gamma = jax.nn.log_sigmoid(decay_logit.astype(jnp.float32))
    o_lat, o_ctx = bidirectional_retention(rq, rk, rv, rq_ctx, rk_ctx, rv_ctx, log_gamma, with_ctx_out)

    na_heads = lambda t, n: t.reshape(B, n, NA_HEADS, NA_HEAD_DIM)
    nk_ctx, nv_ctx = na_heads(nk_c, Lc), na_heads(nv_c, Lc)
    y_lat = jnp.concatenate([
        fourier_mix(f_l, w_four),
        retention_readout(o_lat, rg_l, ret_g),
        neighborhood_attention(na_heads(nq_l, L), na_heads(nk_l, L), na_heads(nv_l, L), nk_ctx, nv_ctx, rpb),
    ], axis=-1) @ w_out
    if not with_ctx_out:
        return y_lat, None
    y_ctx = jnp.concatenate([
        fourier_mix(f_c, w_four),
        retention_readout(o_ctx, rg_c, ret_g),
        context_attention(na_heads(nq_c, Lc), nk_ctx, nv_ctx),
    ], axis=-1) @ w_out
    return y_lat, y_ctx


def setup_inputs(seed: int = 0) -> dict:
    key = jax.random.key(seed)
    ks = jax.random.split(key, 20)
    nrm = lambda k, shape, s: jax.random.normal(k, shape, jnp.float32) * s
    gamma0 = 1.0 - 2.0 ** (-5.0 - np.arange(RET_HEADS))
    logit0 = jnp.asarray(np.log(gamma0 / (1.0 - gamma0)).astype(np.float32))
    return {
        "x": nrm(ks[0], (BATCH, SEQ, D_MODEL), 1.0),
        "c": nrm(ks[1], (BATCH, D_MODEL), 1.0),
        "ctx": nrm(ks[2], (BATCH, CTX_LEN, D_MODEL), 1.0),
        "c_ctx": nrm(ks[3], (D_MODEL,), 1.0),
        "w_ada": nrm(ks[4], (DEPTH, D_MODEL, 6 * D_MODEL), 0.5 * D_MODEL ** -0.5),
        "b_ada": nrm(ks[5], (DEPTH, 6 * D_MODEL), 0.01),
        "g_mix": 1.0 + nrm(ks[6], (DEPTH, D_MODEL), 0.05),
        "w_in": nrm(ks[7], (DEPTH, D_MODEL, PROJ_WIDTH), D_MODEL ** -0.5),
        "ret_decay_logit": logit0[None, None, :] + nrm(ks[8], (DEPTH, 2, RET_HEADS), 0.1),
        "ret_norm_g": 1.0 + nrm(ks[9], (DEPTH, RET_WIDTH), 0.05),
        "w_four": nrm(ks[10], (DEPTH, FOUR_WIDTH, FOUR_WIDTH), FOUR_WIDTH ** -0.5),
        "na_rpb": nrm(ks[11], (DEPTH, NA_HEADS, 2 * NA_WIN_ROWS - 1, 2 * NA_WIN_COLS - 1), 0.1),
        "w_out": nrm(ks[12], (DEPTH, D_MODEL, D_MODEL), D_MODEL ** -0.5),
        "g_ffn": 1.0 + nrm(ks[13], (DEPTH, D_MODEL), 0.05),
        "w1": nrm(ks[14], (DEPTH, D_MODEL, D_FF), D_MODEL ** -0.5),
        "w3": nrm(ks[15], (DEPTH, D_MODEL, D_FF), D_MODEL ** -0.5),
        "w2": nrm(ks[16], (DEPTH, D_FF, D_MODEL), D_FF ** -0.5),
        "g_final": 1.0 + nrm(ks[17], (D_MODEL,), 0.05),
    }


def reference(x, c, ctx, c_ctx, w_ada, b_ada, g_mix, w_in, ret_decay_logit, ret_norm_g, w_four,
              na_rpb, w_out, g_ffn, w1, w3, w2, g_final):
    for i in range(DEPTH):
        last = i == DEPTH - 1
        sh_a, sc_a, ga_a, sh_f, sc_f, ga_f = [m[:, None, :] for m in ada_params(c, w_ada[i], b_ada[i])]
        csh_a, csc_a, cga_a, csh_f, csc_f, cga_f = ada_params(c_ctx, w_ada[i], b_ada[i])
        h = rmsnorm(x, g_mix[i]) * (1.0 + sc_a) + sh_a
        hc = rmsnorm(ctx, g_mix[i]) * (1.0 + csc_a) + csh_a
        y, yc = token_mixer(h, hc, w_in[i], ret_decay_logit[i], ret_norm_g[i], w_four[i], na_rpb[i],
                            w_out[i], not last)
        x = x + ga_a * y
        x = x + ga_f * swiglu(rmsnorm(x, g_ffn[i]) * (1.0 + sc_f) + sh_f, w1[i], w3[i], w2[i])
        if not last:
            ctx = ctx + cga_a * yc
            ctx = ctx + cga_f * swiglu(rmsnorm(ctx, g_ffn[i]) * (1.0 + csc_f) + csh_f, w1[i], w3[i], w2[i])
    return rmsnorm(x, g_final)
```

```python
import functools

import numpy as np
import jax
import jax.numpy as jnp
from jax import lax
from jax.experimental import pallas as pl
from jax.experimental.pallas import tpu as pltpu

F32 = jnp.float32
BF16 = jnp.bfloat16

D = 1024
B = 2
L = 8192
LC = 256
DEPTH = 2
GRID_W = 64
GRID_H = L // GRID_W
FW = 256
FGW = 64
RH = 4
RD = 96
NH = 6
ND = 64
NPAIR = NH // 2
DFF = 2816
EPS = 1e-6
NEG = -1e30
LANE = 128

C_RQ, C_RK, C_RV, C_RG = 0, 512, 1024, 1536
C_NQ, C_NK, C_NV = 2048, 2432, 2816
NP = 3200
YW = FW + RH * LANE + NH * ND

TM = 512
RET_C = 256
NA_R = 4
NA_KR = NA_R + 8
NA_NB = GRID_H // NA_R
DFT_N1 = 64
DFT_N2 = 128
VMEM_LIMIT = 56 * 1024 * 1024


def _qk_lane_dims():
    m = np.full((LANE,), -1, np.int64)
    m[0:24] = np.arange(0, 24)
    m[24:48] = np.arange(48, 72)
    m[64:88] = np.arange(24, 48)
    m[88:112] = np.arange(72, 96)
    return m


def _v_lane_dims():
    m = np.full((LANE,), -1, np.int64)
    m[:RD] = np.arange(RD)
    return m


def _proj_sources():
    src_f = np.arange(FW)
    src = np.full((NP,), -1, np.int64)
    qk, vv = _qk_lane_dims(), _v_lane_dims()
    for dst0, base, lanes in ((C_RQ, 256, qk), (C_RK, 640, qk), (C_RV, 1024, vv), (C_RG, 1408, vv)):
        for h in range(RH):
            ok = lanes >= 0
            src[dst0 + LANE * h + np.nonzero(ok)[0]] = base + RD * h + lanes[ok]
    src[C_NQ:NP] = 1792 + np.arange(NP - C_NQ)
    return src_f, src


def _ycat_sources():
    src = np.full((YW,), -1, np.int64)
    src[:FW] = np.arange(FW)
    for h in range(RH):
        src[FW + LANE * h: FW + LANE * h + RD] = FW + RD * h + np.arange(RD)
    src[FW + RH * LANE:] = FW + RH * RD + np.arange(NH * ND)
    return src


def _gain_sources():
    src = np.full((RH * LANE,), -1, np.int64)
    for h in range(RH):
        src[LANE * h: LANE * h + RD] = RD * h + np.arange(RD)
    return src


def _rope_tables():
    pos = np.arange(L)
    prow, pcol = pos // GRID_W, pos % GRID_W
    half = RD // 4
    inv = 10000.0 ** (-np.arange(half, dtype=np.float64) / half)
    ar = prow[:, None] * inv[None, :]
    ac = pcol[:, None] * inv[None, :]
    cos = np.ones((L, LANE), np.float64)
    sin = np.zeros((L, LANE), np.float64)
    for off, sign in ((0, -1.0), (64, 1.0)):
        cos[:, off:off + 24] = np.cos(ar)
        cos[:, off + 24:off + 48] = np.cos(ac)
        sin[:, off:off + 24] = sign * np.sin(ar)
        sin[:, off + 24:off + 48] = sign * np.sin(ac)
    return cos.astype(np.float32), sin.astype(np.float32)


def _dft_tables():
    k1 = np.arange(DFT_N1)
    a = 2 * np.pi * ((k1[:, None] * k1[None, :]) % DFT_N1) / DFT_N1
    a1 = np.concatenate([np.cos(a), -np.sin(a)], axis=0)
    k2 = np.arange(DFT_N2)
    l2 = np.arange(DFT_N2)
    kk = k1[:, None, None] + DFT_N1 * k2[None, :, None]
    ang = 2 * np.pi * ((kk * l2[None, None, :]) % L) / L
    ct, st = np.cos(ang) / np.sqrt(L), np.sin(ang) / np.sqrt(L)
    m3 = np.concatenate([np.concatenate([ct, st], axis=2), np.concatenate([st, -ct], axis=2)], axis=1)
    c = np.arange(FW)
    same = (c[:, None] // FGW) == (c[None, :] // FGW)
    ac = 2 * np.pi * (((c[:, None] % FGW) * (c[None, :] % FGW)) % FGW) / FGW
    e = np.concatenate([np.where(same, np.cos(ac), 0.0), -np.where(same, np.sin(ac), 0.0)], axis=0) / 8.0
    p = np.arange(LC)
    ap = 2 * np.pi * ((p[:, None] * p[None, :]) % LC) / LC
    return tuple(t.astype(np.float32) for t in (a1, m3, e, np.cos(ap) / 16.0, np.sin(ap) / 16.0))


def _na_bias_indices():
    dr = np.zeros((3, NA_R * GRID_W, NA_KR * GRID_W), np.int64)
    dc = np.zeros_like(dr)
    ok = np.zeros(dr.shape, bool)
    for var, r0 in enumerate((0, 2 * NA_R, GRID_H - NA_R)):
        ks = int(np.clip(r0 - 4, 0, GRID_H - NA_KR))
        r = (r0 + np.arange(NA_R))[:, None, None, None]
        qc = np.arange(GRID_W)[None, :, None, None]
        kr = (ks + np.arange(NA_KR))[None, None, :, None]
        kc = np.arange(GRID_W)[None, None, None, :]
        rs = np.clip(r - 4, 0, GRID_H - 8)
        ws = np.clip(qc - 8, 0, GRID_W - 16)
        v = (kr >= rs) & (kr < rs + 8) & (kc >= ws) & (kc < ws + 16)
        shape = (NA_R, GRID_W, NA_KR, GRID_W)
        v = np.broadcast_to(v, shape).reshape(NA_R * GRID_W, NA_KR * GRID_W)
        ok[var] = v
        dr[var] = np.clip(np.broadcast_to(kr - r + 7, shape), 0, 14).reshape(v.shape)
        dc[var] = np.clip(np.broadcast_to(kc - qc + 15, shape), 0, 30).reshape(v.shape)
    return dr, dc, ok


def _dot(a, b):
    return jnp.dot(a, b, preferred_element_type=F32)


def _dot_nt(a, b):
    return lax.dot_general(a, b, (((1,), (1,)), ((), ())), preferred_element_type=F32)


def _silu(x):
    return x * jax.nn.sigmoid(x)


def _rmsnorm(x, g):
    return x * lax.rsqrt(jnp.mean(x * x, axis=-1, keepdims=True) + EPS) * g


def _ada_kernel(c_ref, w_ref, b_ref, o_ref):
    s = _silu(c_ref[...]).astype(BF16)
    o_ref[0] = _dot(s, w_ref[0].astype(BF16)) + b_ref[0]


def _proj_kernel(*refs, rope):
    if rope:
        x_ref, mod_ref, g_ref, wf_ref, w_ref, cos_ref, sin_ref, f_ref, o_ref = refs
    else:
        x_ref, mod_ref, g_ref, wf_ref, w_ref, f_ref, o_ref = refs
    h = _rmsnorm(x_ref[...], g_ref[...]) * (1.0 + mod_ref[0, 1:2, :]) + mod_ref[0, 0:1, :]
    hb = h.astype(BF16)
    f_ref[...] = _dot(hb, wf_ref[...]).astype(BF16)
    k_scale = RD ** -0.5
    for c0, scale in ((C_RQ, None), (C_RK, k_scale)):
        t4 = _dot(hb, w_ref[:, c0:c0 + RH * LANE])
        for hh in range(RH):
            t = t4[:, LANE * hh:LANE * (hh + 1)]
            if rope:
                t = t * cos_ref[...] + pltpu.roll(t, 64, 1) * sin_ref[...]
            if scale is not None:
                t = t * scale
            o_ref[:, c0 + LANE * hh:c0 + LANE * (hh + 1)] = t.astype(BF16)
    for c0, c1 in ((C_RV, C_RG), (C_RG, C_NQ), (C_NQ, C_NK), (C_NK, C_NV), (C_NV, NP)):
        o_ref[:, c0:c1] = _dot(hb, w_ref[:, c0:c1]).astype(BF16)


def _four1_kernel(a_ref, x_ref, y_ref):
    y_ref[0] = _dot(a_ref[...], x_ref[0]).astype(BF16)


def _four3_kernel(m_ref, y_ref, e_ref, wf_ref, o_ref):
    m = m_ref[0]
    for b in range(B):
        z = _dot(m[:, :DFT_N2], y_ref[b, 0, 0]) + _dot(m[:, DFT_N2:], y_ref[b, 1, 0])
        fr = _dot(z[:DFT_N2].astype(BF16), e_ref[:FW, :]) + _dot(z[DFT_N2:].astype(BF16), e_ref[FW:, :])
        o_ref[b] = _dot(fr.astype(BF16), wf_ref[...]).astype(BF16)


def _decay_matrix(n, lgf, lgb):
    ii = lax.broadcasted_iota(jnp.int32, (n, n), 0)
    jj = lax.broadcasted_iota(jnp.int32, (n, n), 1)
    diff = (ii - jj).astype(F32)
    fwd = jnp.where(diff >= 0, jnp.exp(jnp.maximum(diff, 0.0) * lgf), 0.0)
    bwd = jnp.where(diff <= 0, jnp.exp(jnp.maximum(-diff, 0.0) * lgb), 0.0)
    return fwd + bwd


def _ret_readout(o, gate, gain):
    ms = jnp.sum(o * o, axis=-1, keepdims=True) * (1.0 / RD)
    return (o * lax.rsqrt(ms + EPS) * gain * _silu(gate.astype(F32))).astype(BF16)


def _ret_kernel(lg_ref, q_ref, k_ref, v_ref, g_ref, s0_ref, gain_ref, o_ref, acc_ref, st_ref):
    h = pl.program_id(1)
    lgf = lg_ref[0, h]
    lgb = lg_ref[1, h]
    c = RET_C
    n = L // c
    dmat = _decay_matrix(c, lgf, lgb)
    idx = lax.broadcasted_iota(jnp.int32, (c, LANE), 0).astype(F32)
    qdf = jnp.exp((idx + 1.0) * lgf)
    kdf = jnp.exp((c - 1.0 - idx) * lgf)
    qdb = jnp.exp((c - idx) * lgb)
    kdb = jnp.exp(idx * lgb)
    cdf = jnp.exp(jnp.full((1, LANE), float(c), F32) * lgf)
    cdb = jnp.exp(jnp.full((1, LANE), float(c), F32) * lgb)
    gain = gain_ref[...]

    def kv_update(k, v, kdec, cdec):
        kt = (k.astype(F32) * kdec).T.astype(BF16)
        st_ref[...] = st_ref[...] * cdec + _dot(kt, v)

    st_ref[...] = s0_ref[0, 0, 0]

    def fwd(i, carry):
        r0 = pl.multiple_of(i * c, c)
        q = q_ref[pl.ds(r0, c), :]
        k = k_ref[pl.ds(r0, c), :]
        v = v_ref[pl.ds(r0, c), :]
        s = _dot_nt(q, k) * dmat
        o = _dot(s.astype(BF16), v) + _dot(q, st_ref[...].astype(BF16)) * qdf
        acc_ref[pl.ds(r0, c), :] = o
        kv_update(k, v, kdf, cdf)
        return carry

    lax.fori_loop(0, n, fwd, 0)

    st_ref[...] = s0_ref[0, 0, 1]

    def bwd(t, carry):
        r0 = pl.multiple_of((n - 1 - t) * c, c)
        q = q_ref[pl.ds(r0, c), :]
        k = k_ref[pl.ds(r0, c), :]
        v = v_ref[pl.ds(r0, c), :]
        o = acc_ref[pl.ds(r0, c), :] + _dot(q, st_ref[...].astype(BF16)) * qdb
        o_ref[pl.ds(r0, c), :] = _ret_readout(o, g_ref[pl.ds(r0, c), :], gain)
        kv_update(k, v, kdb, cdb)
        return carry

    lax.fori_loop(0, n, bwd, 0)


def _ctx_state_kernel(lg_ref, k_ref, v_ref, o_ref):
    h = pl.program_id(1)
    idx = lax.broadcasted_iota(jnp.int32, (LC, LANE), 0).astype(F32)
    k = k_ref[...].astype(F32)
    v = v_ref[...]
    wf = jnp.exp((LC - 1.0 - idx) * lg_ref[0, h])
    wb = jnp.exp(idx * lg_ref[1, h])
    o_ref[0, 0, 0] = _dot((k * wf).T.astype(BF16), v)
    o_ref[0, 0, 1] = _dot((k * wb).T.astype(BF16), v)


def _pair_attention(q, score_fn, value_fn):
    lane = lax.broadcasted_iota(jnp.int32, q.shape, 1)
    outs = []
    for hh in range(2):
        sel = (lane >= ND * hh) & (lane < ND * (hh + 1))
        qm = jnp.where(sel, q, jnp.zeros_like(q)) * jnp.asarray(ND ** -0.5, q.dtype)
        scores = score_fn(qm, hh)
        m = functools.reduce(jnp.maximum, [jnp.max(s, axis=-1, keepdims=True) for s in scores])
        ps = [jnp.exp(s - m) for s in scores]
        den = functools.reduce(jnp.add, [jnp.sum(p, axis=-1, keepdims=True) for p in ps])
        outs.append(value_fn([p.astype(BF16) for p in ps]) / den)
    return jnp.where(lane < ND, outs[0], outs[1])


def _ctx_mix_kernel(lg_ref, f_ref, p_ref, gain_ref, wf_ref, c_ref, s_ref, e_ref, fo_ref, ro_ref, no_ref):
    f = f_ref[...]
    pr = _dot(c_ref[...], f)
    qi = _dot(s_ref[...], f)
    fr = _dot(pr.astype(BF16), e_ref[:FW, :]) + _dot(qi.astype(BF16), e_ref[FW:, :])
    fo_ref[...] = _dot(fr.astype(BF16), wf_ref[...]).astype(BF16)
    for h in range(RH):
        sl = lambda c0: slice(c0 + LANE * h, c0 + LANE * (h + 1))
        dmat = _decay_matrix(LC, lg_ref[0, h], lg_ref[1, h])
        s = _dot_nt(p_ref[:, sl(C_RQ)], p_ref[:, sl(C_RK)]) * dmat
        o = _dot(s.astype(BF16), p_ref[:, sl(C_RV)])
        ro_ref[:, LANE * h:LANE * (h + 1)] = _ret_readout(o, p_ref[:, sl(C_RG)], gain_ref[:, LANE * h:LANE * (h + 1)])
    for pair in range(NPAIR):
        sl = lambda c0: slice(c0 + LANE * pair, c0 + LANE * (pair + 1))
        k = p_ref[:, sl(C_NK)]
        v = p_ref[:, sl(C_NV)]
        out = _pair_attention(p_ref[:, sl(C_NQ)], lambda qm, hh: [_dot_nt(qm, k)], lambda ps: _dot(ps[0], v))
        no_ref[:, LANE * pair:LANE * (pair + 1)] = out.astype(BF16)


def _natten_kernel(q_ref, k_ref, v_ref, kc_ref, vc_ref, bias_ref, o_ref):
    i = pl.program_id(2)
    ks = jnp.clip(NA_R * i - 4, 0, GRID_H - NA_KR)
    k0 = pl.multiple_of(ks * GRID_W, NA_R * GRID_W)
    nk = NA_KR * GRID_W
    kw = k_ref[pl.ds(k0, nk), :]
    vw = v_ref[pl.ds(k0, nk), :]
    kc = kc_ref[...]
    vc = vc_ref[...]

    def scores(qm, hh):
        return [_dot_nt(qm, kw) + bias_ref[hh, 0].astype(F32), _dot_nt(qm, kc)]

    def values(ps):
        return _dot(ps[0], vw) + _dot(ps[1], vc)

    o_ref[...] = _pair_attention(q_ref[...], scores, values).astype(BF16)


def _ffn_kernel(*refs, final):
    if final:
        x_ref, f_ref, r_ref, n_ref, mod_ref, g_ref, wo_ref, w1_ref, w3_ref, w2_ref, gf_ref, o_ref, acc_ref = refs
    else:
        x_ref, f_ref, r_ref, n_ref, mod_ref, g_ref, wo_ref, w1_ref, w3_ref, w2_ref, o_ref, acc_ref = refs
    r0, n0 = FW, FW + RH * LANE
    y = _dot(f_ref[...], wo_ref[:r0, :]) + _dot(r_ref[...], wo_ref[r0:n0, :]) + _dot(n_ref[...], wo_ref[n0:, :])
    x1 = x_ref[...] + mod_ref[0, 2:3, :] * y
    hb = (_rmsnorm(x1, g_ref[...]) * (1.0 + mod_ref[0, 4:5, :]) + mod_ref[0, 3:4, :]).astype(BF16)
    fc = 256
    for c in range(DFF // fc):
        a = _dot(hb, w1_ref[:, c * fc:(c + 1) * fc])
        b = _dot(hb, w3_ref[:, c * fc:(c + 1) * fc])
        t = _dot((_silu(a) * b).astype(BF16), w2_ref[c * fc:(c + 1) * fc, :])
        if c == 0:
            acc_ref[...] = t
        else:
            acc_ref[...] += t
    x2 = x1 + mod_ref[0, 5:6, :] * acc_ref[...]
    if final:
        x2 = _rmsnorm(x2, gf_ref[...])
    o_ref[...] = x2


def _params(sem):
    return pltpu.CompilerParams(dimension_semantics=sem, vmem_limit_bytes=VMEM_LIMIT)


def _const_spec(shape):
    nd = len(shape)
    return pl.BlockSpec(shape, lambda *_: (0,) * nd, pipeline_mode=pl.Buffered(1))


def _smem_spec():
    return pl.BlockSpec(memory_space=pltpu.SMEM)


def _ada(cvec, w_ada, b_ada):
    tn = 1024
    return pl.pallas_call(
        _ada_kernel,
        grid=(DEPTH, 6 * D // tn),
        in_specs=[pl.BlockSpec((8, D), lambda i, j: (0, 0)),
                  pl.BlockSpec((1, D, tn), lambda i, j: (i, 0, j)),
                  pl.BlockSpec((1, 1, tn), lambda i, j: (i, 0, j))],
        out_specs=pl.BlockSpec((1, 8, tn), lambda i, j: (i, 0, j)),
        out_shape=jax.ShapeDtypeStruct((DEPTH, 8, 6 * D), F32),
        compiler_params=_params(("arbitrary", "arbitrary")),
        name="ada",
    )(cvec, w_ada, b_ada.reshape(DEPTH, 1, 6 * D))


def _proj(x2d, mods, g, wf, w, rope_tabs, *, ctx):
    rows = x2d.shape[0]
    nt = rows // TM
    per_batch = L // TM
    mod_idx = (lambda i: (2, 0, 0)) if ctx else (lambda i: (i // per_batch, 0, 0))
    in_specs = [pl.BlockSpec((TM, D), lambda i: (i, 0)),
                pl.BlockSpec((1, 6, D), mod_idx),
                _const_spec((1, D)),
                _const_spec((D, FW)),
                _const_spec((D, NP))]
    args = [x2d, mods, g.reshape(1, D), wf, w]
    if not ctx:
        in_specs += [pl.BlockSpec((TM, LANE), lambda i: (i % per_batch, 0))] * 2
        args += list(rope_tabs)
    return pl.pallas_call(
        functools.partial(_proj_kernel, rope=not ctx),
        grid=(nt,),
        in_specs=in_specs,
        out_specs=[pl.BlockSpec((TM, FW), lambda i: (i, 0)), pl.BlockSpec((TM, NP), lambda i: (i, 0))],
        out_shape=[jax.ShapeDtypeStruct((rows, FW), BF16), jax.ShapeDtypeStruct((rows, NP), BF16)],
        compiler_params=_params(("arbitrary",)),
        name="proj_ctx" if ctx else "proj",
    )(*args)


def _fourier(f2d, wf, a1, m3, e):
    t = 4096
    cols = DFT_N2 * FW
    y = pl.pallas_call(
        _four1_kernel,
        grid=(B, cols // t),
        in_specs=[_const_spec((2 * DFT_N1, DFT_N1)),
                  pl.BlockSpec((1, DFT_N1, t), lambda b, j: (b, 0, j))],
        out_specs=pl.BlockSpec((1, 2 * DFT_N1, t), lambda b, j: (b, 0, j)),
        out_shape=jax.ShapeDtypeStruct((B, 2 * DFT_N1, cols), BF16),
        compiler_params=_params(("arbitrary", "arbitrary")),
        name="four1",
    )(a1, f2d.reshape(B, DFT_N1, cols))
    out = pl.pallas_call(
        _four3_kernel,
        grid=(DFT_N1,),
        in_specs=[pl.BlockSpec((1, 2 * DFT_N2, 2 * DFT_N2), lambda k: (k, 0, 0)),
                  pl.BlockSpec((B, 2, 1, DFT_N2, FW), lambda k: (0, 0, k, 0, 0)),
                  _const_spec((2 * FW, FW)),
                  _const_spec((FW, FW))],
        out_specs=pl.BlockSpec((B, DFT_N2, FW), lambda k: (0, 0, k)),
        out_shape=jax.ShapeDtypeStruct((B, DFT_N2, DFT_N1 * FW), BF16),
        compiler_params=_params(("arbitrary",)),
        name="four3",
    )(m3, y.reshape(B, 2, DFT_N1, DFT_N2, FW), e, wf)
    return out.reshape(B * L, FW)


def _retention(p, s0, lg, gain):
    col = lambda c0: (lambda b, h: (b, c0 // LANE + h))
    return pl.pallas_call(
        _ret_kernel,
        grid=(B, RH),
        in_specs=[_smem_spec(),
                  pl.BlockSpec((L, LANE), col(C_RQ)),
                  pl.BlockSpec((L, LANE), col(C_RK)),
                  pl.BlockSpec((L, LANE), col(C_RV)),
                  pl.BlockSpec((L, LANE), col(C_RG)),
                  pl.BlockSpec((1, 1, 2, LANE, LANE), lambda b, h: (b, h, 0, 0, 0)),
                  pl.BlockSpec((1, LANE), lambda b, h: (0, h))],
        out_specs=pl.BlockSpec((L, LANE), lambda b, h: (b, h)),
        out_shape=jax.ShapeDtypeStruct((B * L, RH * LANE), BF16),
        scratch_shapes=[pltpu.VMEM((L, LANE), F32), pltpu.VMEM((LANE, LANE), F32)],
        compiler_params=_params(("arbitrary", "arbitrary")),
        name="ret",
    )(lg, p, p, p, p, s0, gain)


def _ctx_state(pc, lg):
    col = lambda c0: (lambda b, h: (b, c0 // LANE + h))
    return pl.pallas_call(
        _ctx_state_kernel,
        grid=(B, RH),
        in_specs=[_smem_spec(),
                  pl.BlockSpec((LC, LANE), col(C_RK)),
                  pl.BlockSpec((LC, LANE), col(C_RV))],
        out_specs=pl.BlockSpec((1, 1, 2, LANE, LANE), lambda b, h: (b, h, 0, 0, 0)),
        out_shape=jax.ShapeDtypeStruct((B, RH, 2, LANE, LANE), F32),
        compiler_params=_params(("arbitrary", "arbitrary")),
        name="ctx_state",
    )(lg, pc, pc)


def _ctx_mix(fc, pc, lg, gain, wf, c256, s256, e):
    return pl.pallas_call(
        _ctx_mix_kernel,
        grid=(B,),
        in_specs=[_smem_spec(),
                  pl.BlockSpec((LC, FW), lambda b: (b, 0)),
                  pl.BlockSpec((LC, NP), lambda b: (b, 0)),
                  _const_spec((1, RH * LANE)),
                  _const_spec((FW, FW)),
                  _const_spec((LC, LC)),
                  _const_spec((LC, LC)),
                  _const_spec((2 * FW, FW))],
        out_specs=[pl.BlockSpec((LC, FW), lambda b: (b, 0)),
                   pl.BlockSpec((LC, RH * LANE), lambda b: (b, 0)),
                   pl.BlockSpec((LC, NH * ND), lambda b: (b, 0))],
        out_shape=[jax.ShapeDtypeStruct((B * LC, FW), BF16),
                   jax.ShapeDtypeStruct((B * LC, RH * LANE), BF16),
                   jax.ShapeDtypeStruct((B * LC, NH * ND), BF16)],
        compiler_params=_params(("arbitrary",)),
        name="ctx_mix",
    )(lg, fc, pc, gain, wf, c256, s256, e)


def _natten(p, pc, bias):
    nq = NA_R * GRID_W
    variant = lambda i: jnp.where(i == 0, 0, jnp.where(i == NA_NB - 1, 2, 1))
    return pl.pallas_call(
        _natten_kernel,
        grid=(B, NPAIR, NA_NB),
        in_specs=[pl.BlockSpec((nq, LANE), lambda b, pr, i: (b * NA_NB + i, C_NQ // LANE + pr)),
                  pl.BlockSpec((L, LANE), lambda b, pr, i: (b, C_NK // LANE + pr)),
                  pl.BlockSpec((L, LANE), lambda b, pr, i: (b, C_NV // LANE + pr)),
                  pl.BlockSpec((LC, LANE), lambda b, pr, i: (b, C_NK // LANE + pr)),
                  pl.BlockSpec((LC, LANE), lambda b, pr, i: (b, C_NV // LANE + pr)),
                  pl.BlockSpec((2, 1, nq, NA_KR * GRID_W), lambda b, pr, i: (pr, variant(i), 0, 0))],
        out_specs=pl.BlockSpec((nq, LANE), lambda b, pr, i: (b * NA_NB + i, pr)),
        out_shape=jax.ShapeDtypeStruct((B * L, NH * ND), BF16),
        compiler_params=_params(("arbitrary", "arbitrary", "arbitrary")),
        name="natten",
    )(p, p, p, pc, pc, bias)


def _ffn(x2d, four, ret, na, mods, g, wo, w1, w3, w2, g_final, *, ctx):
    rows = x2d.shape[0]
    per_batch = L // TM
    mod_idx = (lambda i: (2, 0, 0)) if ctx else (lambda i: (i // per_batch, 0, 0))
    final = g_final is not None
    row = lambda w: pl.BlockSpec((TM, w), lambda i: (i, 0))
    in_specs = [row(D), row(FW), row(RH * LANE), row(NH * ND),
                pl.BlockSpec((1, 6, D), mod_idx),
                _const_spec((1, D)),
                _const_spec((YW, D)), _const_spec((D, DFF)), _const_spec((D, DFF)), _const_spec((DFF, D))]
    args = [x2d, four, ret, na, mods, g.reshape(1, D), wo, w1, w3, w2]
    if final:
        in_specs.append(_const_spec((1, D)))
        args.append(g_final.reshape(1, D))
    return pl.pallas_call(
        functools.partial(_ffn_kernel, final=final),
        grid=(rows // TM,),
        in_specs=in_specs,
        out_specs=row(D),
        out_shape=jax.ShapeDtypeStruct((rows, D), F32),
        scratch_shapes=[pltpu.VMEM((TM, D), F32)],
        compiler_params=_params(("arbitrary",)),
        name="ffn_ctx" if ctx else ("ffn_final" if final else "ffn"),
    )(*args)


def _take_padded(a, src, axis):
    out = jnp.take(a, jnp.asarray(np.maximum(src, 0)), axis=axis)
    shape = [1] * a.ndim
    shape[axis] = src.shape[0]
    return jnp.where(jnp.asarray(src >= 0).reshape(shape), out, jnp.zeros((), a.dtype))


def kernel(x, c, ctx, c_ctx, w_ada, b_ada, g_mix, w_in, ret_decay_logit, ret_norm_g, w_four,
           na_rpb, w_out, g_ffn, w1, w3, w2, g_final):
    src_f, src_p = _proj_sources()
    src_y = _ycat_sources()
    src_g = _gain_sources()
    rope_tabs = tuple(jnp.asarray(t) for t in _rope_tables())
    a1, m3, e, c256, s256 = (jnp.asarray(t).astype(BF16) for t in _dft_tables())
    dr, dc, ok = _na_bias_indices()

    cvec = jnp.zeros((8, D), F32).at[0:B].set(c).at[B].set(c_ctx)
    mods_all = _ada(cvec, w_ada, b_ada).reshape(DEPTH, 8, 6, D)

    xl = x.reshape(B * L, D)
    xc = ctx.reshape(B * LC, D)
    for i in range(DEPTH):
        last = i == DEPTH - 1
        mods = mods_all[i]
        wf_in = w_in[i][:, src_f].astype(BF16)
        w_in_p = _take_padded(w_in[i], src_p, 1).astype(BF16)
        wo_p = _take_padded(w_out[i], src_y, 0).astype(BF16)
        gain = _take_padded(ret_norm_g[i], src_g, 0).reshape(1, RH * LANE)
        wfour = w_four[i].astype(BF16)
        lg = jax.nn.log_sigmoid(ret_decay_logit[i].astype(F32))
        bias = jnp.where(jnp.asarray(ok)[None], na_rpb[i][:, dr, dc], NEG).astype(BF16)
        w1b, w3b, w2b = w1[i].astype(BF16), w3[i].astype(BF16), w2[i].astype(BF16)

        f_l, p_l = _proj(xl, mods, g_mix[i], wf_in, w_in_p, rope_tabs, ctx=False)
        f_c, p_c = _proj(xc, mods, g_mix[i], wf_in, w_in_p, None, ctx=True)
        s0 = _ctx_state(p_c, lg)
        four_l = _fourier(f_l, wfour, a1, m3, e)
        ret_l = _retention(p_l, s0, lg, gain)
        na_l = _natten(p_l, p_c, bias)
        if not last:
            four_c, ret_c, na_c = _ctx_mix(f_c, p_c, lg, gain, wfour, c256, s256, e)
            xc = _ffn(xc, four_c, ret_c, na_c, mods, g_ffn[i], wo_p, w1b, w3b, w2b, None, ctx=True)
        xl = _ffn(xl, four_l, ret_l, na_l, mods, g_ffn[i], wo_p, w1b, w3b, w2b,
                  g_final if last else None, ctx=False)
    return xl.reshape(B, L, D)
```

```python
import functools

import numpy as np
import jax
import jax.numpy as jnp
from jax import lax
from jax.experimental import pallas as pl
from jax.experimental.pallas import tpu as pltpu

F32 = jnp.float32
BF16 = jnp.bfloat16

D = 1024
B = 2
L = 8192
LC = 256
DEPTH = 2
GRID_W = 64
GRID_H = L // GRID_W
FW = 256
FGW = 64
RH = 4
RD = 96
NH = 6
ND = 64
NPAIR = NH // 2
DFF = 2816
EPS = 1e-6
NEG = -1e30
LANE = 128

C_RQ, C_RK, C_RV, C_RG = 0, 512, 1024, 1536
C_NQ, C_NK, C_NV = 2048, 2432, 2816
NP = 3200
YW = FW + RH * LANE + NH * ND

TM = 512
RET_C = 256
NA_R = 4
NA_KR = NA_R + 8
NA_NB = GRID_H // NA_R
DFT_N1 = 64
DFT_N2 = 128
VMEM_LIMIT = 56 * 1024 * 1024


def _qk_lane_dims():
    m = np.full((LANE,), -1, np.int64)
    m[0:24] = np.arange(0, 24)
    m[24:48] = np.arange(48, 72)
    m[64:88] = np.arange(24, 48)
    m[88:112] = np.arange(72, 96)
    return m


def _v_lane_dims():
    m = np.full((LANE,), -1, np.int64)
    m[:RD] = np.arange(RD)
    return m


def _proj_sources():
    src_f = np.arange(FW)
    src = np.full((NP,), -1, np.int64)
    qk, vv = _qk_lane_dims(), _v_lane_dims()
    for dst0, base, lanes in ((C_RQ, 256, qk), (C_RK, 640, qk), (C_RV, 1024, vv), (C_RG, 1408, vv)):
        for h in range(RH):
            ok = lanes >= 0
            src[dst0 + LANE * h + np.nonzero(ok)[0]] = base + RD * h + lanes[ok]
    src[C_NQ:NP] = 1792 + np.arange(NP - C_NQ)
    return src_f, src


def _ycat_sources():
    src = np.full((YW,), -1, np.int64)
    src[:FW] = np.arange(FW)
    for h in range(RH):
        src[FW + LANE * h: FW + LANE * h + RD] = FW + RD * h + np.arange(RD)
    src[FW + RH * LANE:] = FW + RH * RD + np.arange(NH * ND)
    return src


def _gain_sources():
    src = np.full((RH * LANE,), -1, np.int64)
    for h in range(RH):
        src[LANE * h: LANE * h + RD] = RD * h + np.arange(RD)
    return src


def _rope_tables():
    pos = np.arange(L)
    prow, pcol = pos // GRID_W, pos % GRID_W
    half = RD // 4
    inv = 10000.0 ** (-np.arange(half, dtype=np.float64) / half)
    ar = prow[:, None] * inv[None, :]
    ac = pcol[:, None] * inv[None, :]
    cos = np.ones((L, LANE), np.float64)
    sin = np.zeros((L, LANE), np.float64)
    for off, sign in ((0, -1.0), (64, 1.0)):
        cos[:, off:off + 24] = np.cos(ar)
        cos[:, off + 24:off + 48] = np.cos(ac)
        sin[:, off:off + 24] = sign * np.sin(ar)
        sin[:, off + 24:off + 48] = sign * np.sin(ac)
    return cos.astype(np.float32), sin.astype(np.float32)


def _dft_tables():
    k1 = np.arange(DFT_N1)
    a = 2 * np.pi * ((k1[:, None] * k1[None, :]) % DFT_N1) / DFT_N1
    a1 = np.concatenate([np.cos(a), -np.sin(a)], axis=0)
    k2 = np.arange(DFT_N2)
    l2 = np.arange(DFT_N2)
    kk = k1[:, None, None] + DFT_N1 * k2[None, :, None]
    ang = 2 * np.pi * ((kk * l2[None, None, :]) % L) / L
    ct, st = np.cos(ang) / np.sqrt(L), np.sin(ang) / np.sqrt(L)
    m3 = np.concatenate([np.concatenate([ct, st], axis=2), np.concatenate([st, -ct], axis=2)], axis=1)
    c = np.arange(FW)
    same = (c[:, None] // FGW) == (c[None, :] // FGW)
    ac = 2 * np.pi * (((c[:, None] % FGW) * (c[None, :] % FGW)) % FGW) / FGW
    e = np.concatenate([np.where(same, np.cos(ac), 0.0), -np.where(same, np.sin(ac), 0.0)], axis=0) / 8.0
    p = np.arange(LC)
    ap = 2 * np.pi * ((p[:, None] * p[None, :]) % LC) / LC
    return tuple(t.astype(np.float32) for t in (a1, m3, e, np.cos(ap) / 16.0, np.sin(ap) / 16.0))


NA_DR = 16


def _na_toeplitz_tables():
    src = np.full((LANE,), -1, np.int64)
    src[0:16] = np.arange(0, 16) + 15
    src[113:128] = np.arange(113, 128) - 113
    src[49:80] = np.arange(49, 80) - 49
    qc = np.arange(GRID_W)[:, None]
    kc = np.arange(LANE)[None, :] % GRID_W
    ws = np.clip(qc - 8, 0, GRID_W - 16)
    mask = np.where((kc >= ws) & (kc < ws + 16), 0.0, NEG).astype(np.float32)
    return src, mask


def _dot(a, b):
    return jnp.dot(a, b, preferred_element_type=F32)


def _dot_nt(a, b):
    return lax.dot_general(a, b, (((1,), (1,)), ((), ())), preferred_element_type=F32)


def _silu(x):
    return x * jax.nn.sigmoid(x)


def _rmsnorm(x, g):
    return x * lax.rsqrt(jnp.mean(x * x, axis=-1, keepdims=True) + EPS) * g


def _ada_kernel(c_ref, w_ref, b_ref, o_ref):
    s = _silu(c_ref[...]).astype(BF16)
    o_ref[0] = _dot(s, w_ref[0].astype(BF16)) + b_ref[0]


def _proj_kernel(*refs, rope):
    if rope:
        x_ref, mod_ref, g_ref, wf_ref, w_ref, cos_ref, sin_ref, f_ref, o_ref = refs
    else:
        x_ref, mod_ref, g_ref, wf_ref, w_ref, f_ref, o_ref = refs
    h = _rmsnorm(x_ref[...], g_ref[...]) * (1.0 + mod_ref[0, 1:2, :]) + mod_ref[0, 0:1, :]
    hb = h.astype(BF16)
    f_ref[...] = _dot(hb, wf_ref[...]).astype(BF16)
    k_scale = RD ** -0.5
    for c0, scale in ((C_RQ, None), (C_RK, k_scale)):
        t4 = _dot(hb, w_ref[:, c0:c0 + RH * LANE])
        for hh in range(RH):
            t = t4[:, LANE * hh:LANE * (hh + 1)]
            if rope:
                t = t * cos_ref[...] + pltpu.roll(t, 64, 1) * sin_ref[...]
            if scale is not None:
                t = t * scale
            o_ref[:, c0 + LANE * hh:c0 + LANE * (hh + 1)] = t.astype(BF16)
    for c0, c1 in ((C_RV, C_RG), (C_RG, C_NQ), (C_NQ, C_NK), (C_NK, C_NV), (C_NV, NP)):
        o_ref[:, c0:c1] = _dot(hb, w_ref[:, c0:c1]).astype(BF16)


def _four1_kernel(a_ref, x_ref, y_ref):
    y_ref[0] = _dot(a_ref[...], x_ref[0]).astype(BF16)


def _four3_kernel(m_ref, y_ref, e_ref, wf_ref, o_ref):
    m = m_ref[0]
    for b in range(B):
        z = _dot(m[:, :DFT_N2], y_ref[b, 0, 0]) + _dot(m[:, DFT_N2:], y_ref[b, 1, 0])
        fr = _dot(z[:DFT_N2].astype(BF16), e_ref[:FW, :]) + _dot(z[DFT_N2:].astype(BF16), e_ref[FW:, :])
        o_ref[b] = _dot(fr.astype(BF16), wf_ref[...]).astype(BF16)


def _decay_matrix(n, lgf, lgb):
    ii = lax.broadcasted_iota(jnp.int32, (n, n), 0)
    jj = lax.broadcasted_iota(jnp.int32, (n, n), 1)
    diff = (ii - jj).astype(F32)
    fwd = jnp.where(diff >= 0, jnp.exp(jnp.maximum(diff, 0.0) * lgf), 0.0)
    bwd = jnp.where(diff <= 0, jnp.exp(jnp.maximum(-diff, 0.0) * lgb), 0.0)
    return fwd + bwd


def _ret_readout(o, gate, gain):
    ms = jnp.sum(o * o, axis=-1, keepdims=True) * (1.0 / RD)
    return (o * lax.rsqrt(ms + EPS) * gain * _silu(gate.astype(F32))).astype(BF16)


def _ret_kernel(lg_ref, q_ref, k_ref, v_ref, g_ref, s0_ref, gain_ref, o_ref, acc_ref, st_ref):
    h = pl.program_id(1)
    lgf = lg_ref[0, h]
    lgb = lg_ref[1, h]
    c = RET_C
    n = L // c
    dmat = _decay_matrix(c, lgf, lgb)
    idx = lax.broadcasted_iota(jnp.int32, (c, LANE), 0).astype(F32)
    qdf = jnp.exp((idx + 1.0) * lgf)
    kdf = jnp.exp((c - 1.0 - idx) * lgf)
    qdb = jnp.exp((c - idx) * lgb)
    kdb = jnp.exp(idx * lgb)
    cdf = jnp.exp(jnp.full((1, LANE), float(c), F32) * lgf)
    cdb = jnp.exp(jnp.full((1, LANE), float(c), F32) * lgb)
    gain = gain_ref[...]

    def kv_update(k, v, kdec, cdec):
        kt = (k.astype(F32) * kdec).T.astype(BF16)
        st_ref[...] = st_ref[...] * cdec + _dot(kt, v)

    st_ref[...] = s0_ref[0, 0, 0]

    def fwd(i, carry):
        r0 = pl.multiple_of(i * c, c)
        q = q_ref[pl.ds(r0, c), :]
        k = k_ref[pl.ds(r0, c), :]
        v = v_ref[pl.ds(r0, c), :]
        s = _dot_nt(q, k) * dmat
        o = _dot(s.astype(BF16), v) + _dot(q, st_ref[...].astype(BF16)) * qdf
        acc_ref[pl.ds(r0, c), :] = o
        kv_update(k, v, kdf, cdf)
        return carry

    lax.fori_loop(0, n, fwd, 0)

    st_ref[...] = s0_ref[0, 0, 1]

    def bwd(t, carry):
        r0 = pl.multiple_of((n - 1 - t) * c, c)
        q = q_ref[pl.ds(r0, c), :]
        k = k_ref[pl.ds(r0, c), :]
        v = v_ref[pl.ds(r0, c), :]
        o = acc_ref[pl.ds(r0, c), :] + _dot(q, st_ref[...].astype(BF16)) * qdb
        o_ref[pl.ds(r0, c), :] = _ret_readout(o, g_ref[pl.ds(r0, c), :], gain)
        kv_update(k, v, kdb, cdb)
        return carry

    lax.fori_loop(0, n, bwd, 0)


def _ctx_state_kernel(lg_ref, k_ref, v_ref, o_ref):
    h = pl.program_id(1)
    idx = lax.broadcasted_iota(jnp.int32, (LC, LANE), 0).astype(F32)
    k = k_ref[...].astype(F32)
    v = v_ref[...]
    wf = jnp.exp((LC - 1.0 - idx) * lg_ref[0, h])
    wb = jnp.exp(idx * lg_ref[1, h])
    o_ref[0, 0, 0] = _dot((k * wf).T.astype(BF16), v)
    o_ref[0, 0, 1] = _dot((k * wb).T.astype(BF16), v)


def _pair_attention(q, score_fn, value_fn):
    lane = lax.broadcasted_iota(jnp.int32, q.shape, 1)
    outs = []
    for hh in range(2):
        sel = (lane >= ND * hh) & (lane < ND * (hh + 1))
        qm = jnp.where(sel, q, jnp.zeros_like(q)) * jnp.asarray(ND ** -0.5, q.dtype)
        scores = score_fn(qm, hh)
        m = functools.reduce(jnp.maximum, [jnp.max(s, axis=-1, keepdims=True) for s in scores])
        ps = [jnp.exp(s - m) for s in scores]
        den = functools.reduce(jnp.add, [jnp.sum(p, axis=-1, keepdims=True) for p in ps])
        outs.append(value_fn([p.astype(BF16) for p in ps]) / den)
    return jnp.where(lane < ND, outs[0], outs[1])


def _ctx_mix_kernel(lg_ref, f_ref, p_ref, gain_ref, wf_ref, c_ref, s_ref, e_ref, fo_ref, ro_ref, no_ref):
    f = f_ref[...]
    pr = _dot(c_ref[...], f)
    qi = _dot(s_ref[...], f)
    fr = _dot(pr.astype(BF16), e_ref[:FW, :]) + _dot(qi.astype(BF16), e_ref[FW:, :])
    fo_ref[...] = _dot(fr.astype(BF16), wf_ref[...]).astype(BF16)
    for h in range(RH):
        sl = lambda c0: slice(c0 + LANE * h, c0 + LANE * (h + 1))
        dmat = _decay_matrix(LC, lg_ref[0, h], lg_ref[1, h])
        s = _dot_nt(p_ref[:, sl(C_RQ)], p_ref[:, sl(C_RK)]) * dmat
        o = _dot(s.astype(BF16), p_ref[:, sl(C_RV)])
        ro_ref[:, LANE * h:LANE * (h + 1)] = _ret_readout(o, p_ref[:, sl(C_RG)], gain_ref[:, LANE * h:LANE * (h + 1)])
    for pair in range(NPAIR):
        sl = lambda c0: slice(c0 + LANE * pair, c0 + LANE * (pair + 1))
        k = p_ref[:, sl(C_NK)]
        v = p_ref[:, sl(C_NV)]
        out = _pair_attention(p_ref[:, sl(C_NQ)], lambda qm, hh: [_dot_nt(qm, k)], lambda ps: _dot(ps[0], v))
        no_ref[:, LANE * pair:LANE * (pair + 1)] = out.astype(BF16)


def _na_toeplitz_kernel(base_ref, mask_ref, o_ref):
    qc = lax.broadcasted_iota(jnp.int32, (GRID_W, LANE), 0)
    for dr in range(NA_DR - 1):
        t = jnp.broadcast_to(base_ref[0, dr:dr + 1, :], (GRID_W, LANE))
        for bit in range(6):
            t = jnp.where((qc >> bit) & 1 == 1, pltpu.roll(t, 1 << bit, 1), t)
        o_ref[0, dr] = t + mask_ref[...]
    o_ref[0, NA_DR - 1] = jnp.full((GRID_W, LANE), NEG, F32)


def _natten_kernel(q_ref, k_ref, v_ref, kc_ref, vc_ref, tz_ref, o_ref):
    i = pl.program_id(2)
    ks = jnp.clip(NA_R * i - 4, 0, GRID_H - NA_KR)
    k0 = pl.multiple_of(ks * GRID_W, NA_R * GRID_W)
    nk = NA_KR * GRID_W
    kw = k_ref[pl.ds(k0, nk), :]
    vw = v_ref[pl.ds(k0, nk), :]
    kc = kc_ref[...]
    vc = vc_ref[...]

    slot = {}
    for rl in range(NA_R):
        r = NA_R * i + rl
        rs = jnp.clip(r - 4, 0, GRID_H - 8)
        for kl in range(NA_KR):
            kr = ks + kl
            slot[rl, kl] = jnp.where((kr >= rs) & (kr < rs + 8), kr - r + 7, NA_DR - 1)
    low_half = lax.broadcasted_iota(jnp.int32, (GRID_W, LANE), 1) < GRID_W

    def bias(hh):
        rows = []
        for rl in range(NA_R):
            tiles = [jnp.where(low_half, tz_ref[hh, slot[rl, 2 * m]], tz_ref[hh, slot[rl, 2 * m + 1]])
                     for m in range(NA_KR // 2)]
            rows.append(jnp.concatenate(tiles, axis=1))
        return jnp.concatenate(rows, axis=0)

    def scores(qm, hh):
        return [_dot_nt(qm, kw) + bias(hh), _dot_nt(qm, kc)]

    def values(ps):
        return _dot(ps[0], vw) + _dot(ps[1], vc)

    o_ref[...] = _pair_attention(q_ref[...], scores, values).astype(BF16)


def _ffn_kernel(*refs, final):
    if final:
        x_ref, f_ref, r_ref, n_ref, mod_ref, g_ref, wo_ref, w1_ref, w3_ref, w2_ref, gf_ref, o_ref, acc_ref = refs
    else:
        x_ref, f_ref, r_ref, n_ref, mod_ref, g_ref, wo_ref, w1_ref, w3_ref, w2_ref, o_ref, acc_ref = refs
    r0, n0 = FW, FW + RH * LANE
    y = _dot(f_ref[...], wo_ref[:r0, :]) + _dot(r_ref[...], wo_ref[r0:n0, :]) + _dot(n_ref[...], wo_ref[n0:, :])
    x1 = x_ref[...] + mod_ref[0, 2:3, :] * y
    hb = (_rmsnorm(x1, g_ref[...]) * (1.0 + mod_ref[0, 4:5, :]) + mod_ref[0, 3:4, :]).astype(BF16)
    fc = 256
    for c in range(DFF // fc):
        a = _dot(hb, w1_ref[:, c * fc:(c + 1) * fc])
        b = _dot(hb, w3_ref[:, c * fc:(c + 1) * fc])
        t = _dot((_silu(a) * b).astype(BF16), w2_ref[c * fc:(c + 1) * fc, :])
        if c == 0:
            acc_ref[...] = t
        else:
            acc_ref[...] += t
    x2 = x1 + mod_ref[0, 5:6, :] * acc_ref[...]
    if final:
        x2 = _rmsnorm(x2, gf_ref[...])
    o_ref[...] = x2


def _params(sem):
    return pltpu.CompilerParams(dimension_semantics=sem, vmem_limit_bytes=VMEM_LIMIT)


def _const_spec(shape):
    nd = len(shape)
    return pl.BlockSpec(shape, lambda *_: (0,) * nd, pipeline_mode=pl.Buffered(1))


def _smem_spec():
    return pl.BlockSpec(memory_space=pltpu.SMEM)


def _ada(cvec, w_ada, b_ada):
    tn = 1024
    return pl.pallas_call(
        _ada_kernel,
        grid=(DEPTH, 6 * D // tn),
        in_specs=[pl.BlockSpec((8, D), lambda i, j: (0, 0)),
                  pl.BlockSpec((1, D, tn), lambda i, j: (i, 0, j)),
                  pl.BlockSpec((1, 1, tn), lambda i, j: (i, 0, j))],
        out_specs=pl.BlockSpec((1, 8, tn), lambda i, j: (i, 0, j)),
        out_shape=jax.ShapeDtypeStruct((DEPTH, 8, 6 * D), F32),
        compiler_params=_params(("arbitrary", "arbitrary")),
        name="ada",
    )(cvec, w_ada, b_ada.reshape(DEPTH, 1, 6 * D))


def _proj(x2d, mods, g, wf, w, rope_tabs, *, ctx):
    rows = x2d.shape[0]
    nt = rows // TM
    per_batch = L // TM
    mod_idx = (lambda i: (2, 0, 0)) if ctx else (lambda i: (i // per_batch, 0, 0))
    in_specs = [pl.BlockSpec((TM, D), lambda i: (i, 0)),
                pl.BlockSpec((1, 6, D), mod_idx),
                _const_spec((1, D)),
                _const_spec((D, FW)),
                _const_spec((D, NP))]
    args = [x2d, mods, g.reshape(1, D), wf, w]
    if not ctx:
        in_specs += [pl.BlockSpec((TM, LANE), lambda i: (i % per_batch, 0))] * 2
        args += list(rope_tabs)
    return pl.pallas_call(
        functools.partial(_proj_kernel, rope=not ctx),
        grid=(nt,),
        in_specs=in_specs,
        out_specs=[pl.BlockSpec((TM, FW), lambda i: (i, 0)), pl.BlockSpec((TM, NP), lambda i: (i, 0))],
        out_shape=[jax.ShapeDtypeStruct((rows, FW), BF16), jax.ShapeDtypeStruct((rows, NP), BF16)],
        compiler_params=_params(("arbitrary",)),
        name="proj_ctx" if ctx else "proj",
    )(*args)


def _fourier(f2d, wf, a1, m3, e):
    t = 4096
    cols = DFT_N2 * FW
    y = pl.pallas_call(
        _four1_kernel,
        grid=(B, cols // t),
        in_specs=[_const_spec((2 * DFT_N1, DFT_N1)),
                  pl.BlockSpec((1, DFT_N1, t), lambda b, j: (b, 0, j))],
        out_specs=pl.BlockSpec((1, 2 * DFT_N1, t), lambda b, j: (b, 0, j)),
        out_shape=jax.ShapeDtypeStruct((B, 2 * DFT_N1, cols), BF16),
        compiler_params=_params(("arbitrary", "arbitrary")),
        name="four1",
    )(a1, f2d.reshape(B, DFT_N1, cols))
    out = pl.pallas_call(
        _four3_kernel,
        grid=(DFT_N1,),
        in_specs=[pl.BlockSpec((1, 2 * DFT_N2, 2 * DFT_N2), lambda k: (k, 0, 0)),
                  pl.BlockSpec((B, 2, 1, DFT_N2, FW), lambda k: (0, 0, k, 0, 0)),
                  _const_spec((2 * FW, FW)),
                  _const_spec((FW, FW))],
        out_specs=pl.BlockSpec((B, DFT_N2, FW), lambda k: (0, 0, k)),
        out_shape=jax.ShapeDtypeStruct((B, DFT_N2, DFT_N1 * FW), BF16),
        compiler_params=_params(("arbitrary",)),
        name="four3",
    )(m3, y.reshape(B, 2, DFT_N1, DFT_N2, FW), e, wf)
    return out.reshape(B * L, FW)


def _retention(p, s0, lg, gain):
    col = lambda c0: (lambda b, h: (b, c0 // LANE + h))
    return pl.pallas_call(
        _ret_kernel,
        grid=(B, RH),
        in_specs=[_smem_spec(),
                  pl.BlockSpec((L, LANE), col(C_RQ)),
                  pl.BlockSpec((L, LANE), col(C_RK)),
                  pl.BlockSpec((L, LANE), col(C_RV)),
                  pl.BlockSpec((L, LANE), col(C_RG)),
                  pl.BlockSpec((1, 1, 2, LANE, LANE), lambda b, h: (b, h, 0, 0, 0)),
                  pl.BlockSpec((1, LANE), lambda b, h: (0, h))],
        out_specs=pl.BlockSpec((L, LANE), lambda b, h: (b, h)),
        out_shape=jax.ShapeDtypeStruct((B * L, RH * LANE), BF16),
        scratch_shapes=[pltpu.VMEM((L, LANE), F32), pltpu.VMEM((LANE, LANE), F32)],
        compiler_params=_params(("arbitrary", "arbitrary")),
        name="ret",
    )(lg, p, p, p, p, s0, gain)


def _ctx_state(pc, lg):
    col = lambda c0: (lambda b, h: (b, c0 // LANE + h))
    return pl.pallas_call(
        _ctx_state_kernel,
        grid=(B, RH),
        in_specs=[_smem_spec(),
                  pl.BlockSpec((LC, LANE), col(C_RK)),
                  pl.BlockSpec((LC, LANE), col(C_RV))],
        out_specs=pl.BlockSpec((1, 1, 2, LANE, LANE), lambda b, h: (b, h, 0, 0, 0)),
        out_shape=jax.ShapeDtypeStruct((B, RH, 2, LANE, LANE), F32),
        compiler_params=_params(("arbitrary", "arbitrary")),
        name="ctx_state",
    )(lg, pc, pc)


def _ctx_mix(fc, pc, lg, gain, wf, c256, s256, e):
    return pl.pallas_call(
        _ctx_mix_kernel,
        grid=(B,),
        in_specs=[_smem_spec(),
                  pl.BlockSpec((LC, FW), lambda b: (b, 0)),
                  pl.BlockSpec((LC, NP), lambda b: (b, 0)),
                  _const_spec((1, RH * LANE)),
                  _const_spec((FW, FW)),
                  _const_spec((LC, LC)),
                  _const_spec((LC, LC)),
                  _const_spec((2 * FW, FW))],
        out_specs=[pl.BlockSpec((LC, FW), lambda b: (b, 0)),
                   pl.BlockSpec((LC, RH * LANE), lambda b: (b, 0)),
                   pl.BlockSpec((LC, NH * ND), lambda b: (b, 0))],
        out_shape=[jax.ShapeDtypeStruct((B * LC, FW), BF16),
                   jax.ShapeDtypeStruct((B * LC, RH * LANE), BF16),
                   jax.ShapeDtypeStruct((B * LC, NH * ND), BF16)],
        compiler_params=_params(("arbitrary",)),
        name="ctx_mix",
    )(lg, fc, pc, gain, wf, c256, s256, e)


def _na_toeplitz(base, mask):
    return pl.pallas_call(
        _na_toeplitz_kernel,
        grid=(NH,),
        in_specs=[pl.BlockSpec((1, NA_DR, LANE), lambda h: (h, 0, 0)),
                  _const_spec((GRID_W, LANE))],
        out_specs=pl.BlockSpec((1, NA_DR, GRID_W, LANE), lambda h: (h, 0, 0, 0)),
        out_shape=jax.ShapeDtypeStruct((NH, NA_DR, GRID_W, LANE), F32),
        compiler_params=_params(("arbitrary",)),
        name="na_toeplitz",
    )(base, mask)


def _natten(p, pc, tz):
    nq = NA_R * GRID_W
    return pl.pallas_call(
        _natten_kernel,
        grid=(B, NPAIR, NA_NB),
        in_specs=[pl.BlockSpec((nq, LANE), lambda b, pr, i: (b * NA_NB + i, C_NQ // LANE + pr)),
                  pl.BlockSpec((L, LANE), lambda b, pr, i: (b, C_NK // LANE + pr)),
                  pl.BlockSpec((L, LANE), lambda b, pr, i: (b, C_NV // LANE + pr)),
                  pl.BlockSpec((LC, LANE), lambda b, pr, i: (b, C_NK // LANE + pr)),
                  pl.BlockSpec((LC, LANE), lambda b, pr, i: (b, C_NV // LANE + pr)),
                  pl.BlockSpec((2, NA_DR, GRID_W, LANE), lambda b, pr, i: (pr, 0, 0, 0))],
        out_specs=pl.BlockSpec((nq, LANE), lambda b, pr, i: (b * NA_NB + i, pr)),
        out_shape=jax.ShapeDtypeStruct((B * L, NH * ND), BF16),
        compiler_params=_params(("arbitrary", "arbitrary", "arbitrary")),
        name="natten",
    )(p, p, p, pc, pc, tz)


def _ffn(x2d, four, ret, na, mods, g, wo, w1, w3, w2, g_final, *, ctx):
    rows = x2d.shape[0]
    per_batch = L // TM
    mod_idx = (lambda i: (2, 0, 0)) if ctx else (lambda i: (i // per_batch, 0, 0))
    final = g_final is not None
    row = lambda w: pl.BlockSpec((TM, w), lambda i: (i, 0))
    in_specs = [row(D), row(FW), row(RH * LANE), row(NH * ND),
                pl.BlockSpec((1, 6, D), mod_idx),
                _const_spec((1, D)),
                _const_spec((YW, D)), _const_spec((D, DFF)), _const_spec((D, DFF)), _const_spec((DFF, D))]
    args = [x2d, four, ret, na, mods, g.reshape(1, D), wo, w1, w3, w2]
    if final:
        in_specs.append(_const_spec((1, D)))
        args.append(g_final.reshape(1, D))
    return pl.pallas_call(
        functools.partial(_ffn_kernel, final=final),
        grid=(rows // TM,),
        in_specs=in_specs,
        out_specs=row(D),
        out_shape=jax.ShapeDtypeStruct((rows, D), F32),
        scratch_shapes=[pltpu.VMEM((TM, D), F32)],
        compiler_params=_params(("arbitrary",)),
        name="ffn_ctx" if ctx else ("ffn_final" if final else "ffn"),
    )(*args)


def _take_padded(a, src, axis):
    out = jnp.take(a, jnp.asarray(np.maximum(src, 0)), axis=axis)
    shape = [1] * a.ndim
    shape[axis] = src.shape[0]
    return jnp.where(jnp.asarray(src >= 0).reshape(shape), out, jnp.zeros((), a.dtype))


def kernel(x, c, ctx, c_ctx, w_ada, b_ada, g_mix, w_in, ret_decay_logit, ret_norm_g, w_four,
           na_rpb, w_out, g_ffn, w1, w3, w2, g_final):
    src_f, src_p = _proj_sources()
    src_y = _ycat_sources()
    src_g = _gain_sources()
    rope_tabs = tuple(jnp.asarray(t) for t in _rope_tables())
    a1, m3, e, c256, s256 = (jnp.asarray(t).astype(BF16) for t in _dft_tables())
    tz_src, tz_mask = _na_toeplitz_tables()

    cvec = jnp.zeros((8, D), F32).at[0:B].set(c).at[B].set(c_ctx)
    mods_all = _ada(cvec, w_ada, b_ada).reshape(DEPTH, 8, 6, D)

    xl = x.reshape(B * L, D)
    xc = ctx.reshape(B * LC, D)
    for i in range(DEPTH):
        last = i == DEPTH - 1
        mods = mods_all[i]
        wf_in = w_in[i][:, src_f].astype(BF16)
        w_in_p = _take_padded(w_in[i], src_p, 1).astype(BF16)
        wo_p = _take_padded(w_out[i], src_y, 0).astype(BF16)
        gain = _take_padded(ret_norm_g[i], src_g, 0).reshape(1, RH * LANE)
        wfour = w_four[i].astype(BF16)
        lg = jax.nn.log_sigmoid(ret_decay_logit[i].astype(F32))
        tz_base = jnp.pad(_take_padded(na_rpb[i].astype(F32), tz_src, 2), ((0, 0), (0, 1), (0, 0)))
        tz = _na_toeplitz(tz_base, jnp.asarray(tz_mask))
        w1b, w3b, w2b = w1[i].astype(BF16), w3[i].astype(BF16), w2[i].astype(BF16)

        f_l, p_l = _proj(xl, mods, g_mix[i], wf_in, w_in_p, rope_tabs, ctx=False)
        f_c, p_c = _proj(xc, mods, g_mix[i], wf_in, w_in_p, None, ctx=True)
        s0 = _ctx_state(p_c, lg)
        four_l = _fourier(f_l, wfour, a1, m3, e)
        ret_l = _retention(p_l, s0, lg, gain)
        na_l = _natten(p_l, p_c, tz)
        if not last:
            four_c, ret_c, na_c = _ctx_mix(f_c, p_c, lg, gain, wfour, c256, s256, e)
            xc = _ffn(xc, four_c, ret_c, na_c, mods, g_ffn[i], wo_p, w1b, w3b, w2b, None, ctx=True)
        xl = _ffn(xl, four_l, ret_l, na_l, mods, g_ffn[i], wo_p, w1b, w3b, w2b,
                  g_final if last else None, ctx=False)
    return xl.reshape(B, L, D)
```

```python
import functools

import numpy as np
import jax
import jax.numpy as jnp
from jax import lax
from jax.experimental import pallas as pl
from jax.experimental.pallas import tpu as pltpu

F32 = jnp.float32
BF16 = jnp.bfloat16

D = 1024
B = 2
L = 8192
LC = 256
DEPTH = 2
GRID_W = 64
GRID_H = L // GRID_W
FW = 256
FGW = 64
RH = 4
RD = 96
NH = 6
ND = 64
NPAIR = NH // 2
DFF = 2816
EPS = 1e-6
NEG = -1e30
LOG2E = float(np.log2(np.e))
LANE = 128

C_RQ, C_RK, C_RV, C_RG = 0, 512, 1024, 1536
C_NQ, C_NK, C_NV = 2048, 2432, 2816
NP = 3200
YW = FW + RH * LANE + NH * ND

TM = 512
RET_C = 256
NA_R = 4
NA_KR = NA_R + 8
NA_NB = GRID_H // NA_R
DFT_N1 = 64
DFT_N2 = 128
F1_R = 16
F3_K = 8
VMEM_LIMIT = 56 * 1024 * 1024


def _qk_lane_dims():
    m = np.full((LANE,), -1, np.int64)
    m[0:24] = np.arange(0, 24)
    m[24:48] = np.arange(48, 72)
    m[64:88] = np.arange(24, 48)
    m[88:112] = np.arange(72, 96)
    return m


def _v_lane_dims():
    m = np.full((LANE,), -1, np.int64)
    m[:RD] = np.arange(RD)
    return m


def _proj_sources():
    src_f = np.arange(FW)
    src = np.full((NP,), -1, np.int64)
    qk, vv = _qk_lane_dims(), _v_lane_dims()
    for dst0, base, lanes in ((C_RQ, 256, qk), (C_RK, 640, qk), (C_RV, 1024, vv), (C_RG, 1408, vv)):
        for h in range(RH):
            ok = lanes >= 0
            src[dst0 + LANE * h + np.nonzero(ok)[0]] = base + RD * h + lanes[ok]
    src[C_NQ:NP] = 1792 + np.arange(NP - C_NQ)
    return src_f, src


def _ycat_sources():
    src = np.full((YW,), -1, np.int64)
    src[:FW] = np.arange(FW)
    for h in range(RH):
        src[FW + LANE * h: FW + LANE * h + RD] = FW + RD * h + np.arange(RD)
    src[FW + RH * LANE:] = FW + RH * RD + np.arange(NH * ND)
    return src


def _gain_sources():
    src = np.full((RH * LANE,), -1, np.int64)
    for h in range(RH):
        src[LANE * h: LANE * h + RD] = RD * h + np.arange(RD)
    return src


def _rope_tables():
    pos = np.arange(L)
    prow, pcol = pos // GRID_W, pos % GRID_W
    half = RD // 4
    inv = 10000.0 ** (-np.arange(half, dtype=np.float64) / half)
    ar = prow[:, None] * inv[None, :]
    ac = pcol[:, None] * inv[None, :]
    cos = np.ones((L, LANE), np.float64)
    sin = np.zeros((L, LANE), np.float64)
    for off, sign in ((0, -1.0), (64, 1.0)):
        cos[:, off:off + 24] = np.cos(ar)
        cos[:, off + 24:off + 48] = np.cos(ac)
        sin[:, off:off + 24] = sign * np.sin(ar)
        sin[:, off + 24:off + 48] = sign * np.sin(ac)
    return cos.astype(np.float32), sin.astype(np.float32)


def _dft_tables():
    k1 = np.arange(DFT_N1)
    a = 2 * np.pi * ((k1[:, None] * k1[None, :]) % DFT_N1) / DFT_N1
    a1 = np.kron(np.concatenate([np.cos(a), -np.sin(a)], axis=0), np.eye(F1_R))
    k2 = np.arange(DFT_N2)
    l2 = np.arange(DFT_N2)
    kk = k1[:, None, None] + DFT_N1 * k2[None, :, None]
    ang = 2 * np.pi * ((kk * l2[None, None, :]) % L) / L
    ct, st = np.cos(ang) / np.sqrt(L), np.sin(ang) / np.sqrt(L)
    m3 = np.concatenate([np.concatenate([ct, st], axis=2), np.concatenate([st, -ct], axis=2)], axis=1)
    c = np.arange(FW)
    same = (c[:, None] // FGW) == (c[None, :] // FGW)
    ac = 2 * np.pi * (((c[:, None] % FGW) * (c[None, :] % FGW)) % FGW) / FGW
    e = np.concatenate([np.where(same, np.cos(ac), 0.0), -np.where(same, np.sin(ac), 0.0)], axis=0) / 8.0
    p = np.arange(LC)
    ap = 2 * np.pi * ((p[:, None] * p[None, :]) % LC) / LC
    return tuple(t.astype(np.float32) for t in (a1, m3, e, np.cos(ap) / 16.0, np.sin(ap) / 16.0))


NA_DR = 16


def _na_toeplitz_tables():
    src = np.full((LANE,), -1, np.int64)
    src[0:16] = np.arange(0, 16) + 15
    src[113:128] = np.arange(113, 128) - 113
    src[49:80] = np.arange(49, 80) - 49
    qc = np.arange(GRID_W)[:, None]
    kc = np.arange(LANE)[None, :] % GRID_W
    ws = np.clip(qc - 8, 0, GRID_W - 16)
    mask = np.where((kc >= ws) & (kc < ws + 16), 0.0, NEG).astype(np.float32)
    return src, mask


def _dot(a, b):
    return jnp.dot(a, b, preferred_element_type=F32)


def _dot_nt(a, b):
    return lax.dot_general(a, b, (((1,), (1,)), ((), ())), preferred_element_type=F32)


def _silu(x):
    return x * jax.nn.sigmoid(x)


def _rmsnorm(x, g):
    return x * lax.rsqrt(jnp.mean(x * x, axis=-1, keepdims=True) + EPS) * g


def _ada_kernel(c_ref, w_ref, b_ref, o_ref):
    s = _silu(c_ref[...]).astype(BF16)
    o_ref[0] = _dot(s, w_ref[0].astype(BF16)) + b_ref[0]


def _proj_kernel(*refs, rope):
    if rope:
        x_ref, mod_ref, g_ref, wf_ref, w_ref, cos_ref, sin_ref, f_ref, o_ref = refs
    else:
        x_ref, mod_ref, g_ref, wf_ref, w_ref, f_ref, o_ref = refs
    h = _rmsnorm(x_ref[...], g_ref[...]) * (1.0 + mod_ref[0, 1:2, :]) + mod_ref[0, 0:1, :]
    hb = h.astype(BF16)
    f_ref[...] = _dot(hb, wf_ref[...]).astype(BF16)
    k_scale = RD ** -0.5
    for c0, scale in ((C_RQ, None), (C_RK, k_scale)):
        t4 = _dot(hb, w_ref[:, c0:c0 + RH * LANE])
        for hh in range(RH):
            t = t4[:, LANE * hh:LANE * (hh + 1)]
            if rope:
                t = t * cos_ref[...] + pltpu.roll(t, 64, 1) * sin_ref[...]
            if scale is not None:
                t = t * scale
            o_ref[:, c0 + LANE * hh:c0 + LANE * (hh + 1)] = t.astype(BF16)
    for c0, c1 in ((C_RV, C_RG), (C_RG, C_NQ), (C_NQ, C_NK), (C_NK, C_NV), (C_NV, NP)):
        o_ref[:, c0:c1] = _dot(hb, w_ref[:, c0:c1]).astype(BF16)


def _four1_kernel(a_ref, x_ref, y_ref):
    x = x_ref[0].reshape(DFT_N1 * F1_R, FW)
    y = _dot(a_ref[...], x).astype(BF16)
    y_ref[0] = y.reshape(2 * DFT_N1, F1_R, FW)


def _four3_kernel(m_ref, y_ref, e_ref, wf_ref, o_ref):
    for kk in range(F3_K):
        m = m_ref[kk]
        for b in range(B):
            z = _dot(m[:, :DFT_N2], y_ref[b, 0, kk]) + _dot(m[:, DFT_N2:], y_ref[b, 1, kk])
            fr = _dot(z[:DFT_N2].astype(BF16), e_ref[:FW, :]) + _dot(z[DFT_N2:].astype(BF16), e_ref[FW:, :])
            o_ref[b, :, FW * kk:FW * (kk + 1)] = _dot(fr.astype(BF16), wf_ref[...]).astype(BF16)


def _decay_matrix(n, lgf, lgb):
    ii = lax.broadcasted_iota(jnp.int32, (n, n), 0)
    jj = lax.broadcasted_iota(jnp.int32, (n, n), 1)
    diff = (ii - jj).astype(F32)
    fwd = jnp.where(diff >= 0, jnp.exp(jnp.maximum(diff, 0.0) * lgf), 0.0)
    bwd = jnp.where(diff <= 0, jnp.exp(jnp.maximum(-diff, 0.0) * lgb), 0.0)
    return fwd + bwd


def _ret_readout(o, gate, gain):
    ms = jnp.sum(o * o, axis=-1, keepdims=True) * (1.0 / RD)
    return (o * lax.rsqrt(ms + EPS) * gain * _silu(gate.astype(F32))).astype(BF16)


def _ret_kernel(lg_ref, q_ref, k_ref, v_ref, g_ref, s0_ref, gain_ref, o_ref, acc_ref, stf_ref, stb_ref):
    h = pl.program_id(1)
    lgf = lg_ref[0, h]
    lgb = lg_ref[1, h]
    c = RET_C
    n = L // c
    dmat = _decay_matrix(c, lgf, lgb)
    idx = lax.broadcasted_iota(jnp.int32, (c, LANE), 0).astype(F32)
    qdf = jnp.exp((idx + 1.0) * lgf)
    kdf = jnp.exp((c - 1.0 - idx) * lgf)
    qdb = jnp.exp((c - idx) * lgb)
    kdb = jnp.exp(idx * lgb)
    cdf = jnp.exp(jnp.full((1, LANE), float(c), F32) * lgf)
    cdb = jnp.exp(jnp.full((1, LANE), float(c), F32) * lgb)
    gain = gain_ref[...]

    dirs = ((stf_ref, qdf, kdf, cdf), (stb_ref, qdb, kdb, cdb))
    stf_ref[...] = s0_ref[0, 0, 0]
    stb_ref[...] = s0_ref[0, 0, 1]

    def visit(r0, direction, second):
        st_ref, qdec, kdec, cdec = dirs[direction]
        rows = pl.ds(pl.multiple_of(r0, c), c)
        q = q_ref[rows, :]
        k = k_ref[rows, :]
        v = v_ref[rows, :]
        cross = _dot(q, st_ref[...].astype(BF16)) * qdec
        if second:
            o_ref[rows, :] = _ret_readout(acc_ref[rows, :] + cross, g_ref[rows, :], gain)
        else:
            s = _dot_nt(q, k) * dmat
            acc_ref[rows, :] = _dot(s.astype(BF16), v) + cross
        kt = (k.astype(F32) * kdec).T.astype(BF16)
        st_ref[...] = st_ref[...] * cdec + _dot(kt, v)

    def body(second):
        def step(i, carry):
            visit(i * c, 0, second)
            visit((n - 1 - i) * c, 1, second)
            return carry
        return step

    lax.fori_loop(0, n // 2, body(False), 0)
    lax.fori_loop(n // 2, n, body(True), 0)


def _ctx_state_kernel(lg_ref, k_ref, v_ref, o_ref):
    h = pl.program_id(1)
    idx = lax.broadcasted_iota(jnp.int32, (LC, LANE), 0).astype(F32)
    k = k_ref[...].astype(F32)
    v = v_ref[...]
    wf = jnp.exp((LC - 1.0 - idx) * lg_ref[0, h])
    wb = jnp.exp(idx * lg_ref[1, h])
    o_ref[0, 0, 0] = _dot((k * wf).T.astype(BF16), v)
    o_ref[0, 0, 1] = _dot((k * wb).T.astype(BF16), v)


def _pair_attention(q, score_fn, value_fn):
    lane = lax.broadcasted_iota(jnp.int32, q.shape, 1)
    outs = []
    for hh in range(2):
        sel = (lane >= ND * hh) & (lane < ND * (hh + 1))
        qm = jnp.where(sel, q, jnp.zeros_like(q))
        scores = score_fn(qm, hh)
        m = functools.reduce(jnp.maximum, [jnp.max(s, axis=-1, keepdims=True) for s in scores])
        ps = [jnp.exp2(s - m) for s in scores]
        den = functools.reduce(jnp.add, [jnp.sum(p, axis=-1, keepdims=True) for p in ps])
        outs.append(value_fn([p.astype(BF16) for p in ps]) / den)
    return jnp.where(lane < ND, outs[0], outs[1])


def _ctx_mix_kernel(lg_ref, f_ref, p_ref, gain_ref, wf_ref, c_ref, s_ref, e_ref, fo_ref, ro_ref, no_ref):
    f = f_ref[...]
    pr = _dot(c_ref[...], f)
    qi = _dot(s_ref[...], f)
    fr = _dot(pr.astype(BF16), e_ref[:FW, :]) + _dot(qi.astype(BF16), e_ref[FW:, :])
    fo_ref[...] = _dot(fr.astype(BF16), wf_ref[...]).astype(BF16)
    for h in range(RH):
        sl = lambda c0: slice(c0 + LANE * h, c0 + LANE * (h + 1))
        dmat = _decay_matrix(LC, lg_ref[0, h], lg_ref[1, h])
        s = _dot_nt(p_ref[:, sl(C_RQ)], p_ref[:, sl(C_RK)]) * dmat
        o = _dot(s.astype(BF16), p_ref[:, sl(C_RV)])
        ro_ref[:, LANE * h:LANE * (h + 1)] = _ret_readout(o, p_ref[:, sl(C_RG)], gain_ref[:, LANE * h:LANE * (h + 1)])
    for pair in range(NPAIR):
        sl = lambda c0: slice(c0 + LANE * pair, c0 + LANE * (pair + 1))
        k = p_ref[:, sl(C_NK)]
        v = p_ref[:, sl(C_NV)]
        out = _pair_attention(p_ref[:, sl(C_NQ)], lambda qm, hh: [_dot_nt(qm, k)], lambda ps: _dot(ps[0], v))
        no_ref[:, LANE * pair:LANE * (pair + 1)] = out.astype(BF16)


def _na_key_start(r0):
    return min(max(r0 - 4, 0), GRID_H - NA_KR)


def _na_bias_kernel(base_ref, mask_ref, o_ref, tz_ref):
    qc = lax.broadcasted_iota(jnp.int32, (GRID_W, LANE), 0)
    for dr in range(NA_DR - 1):
        t = jnp.broadcast_to(base_ref[0, dr:dr + 1, :], (GRID_W, LANE))
        for bit in range(6):
            t = jnp.where((qc >> bit) & 1 == 1, pltpu.roll(t, 1 << bit, 1), t)
        tz_ref[dr] = t * LOG2E + mask_ref[...]
    tz_ref[NA_DR - 1] = jnp.full((GRID_W, LANE), NEG, F32)
    low_half = lax.broadcasted_iota(jnp.int32, (GRID_W, LANE), 1) < GRID_W
    for var, r0 in enumerate((0, 2 * NA_R, GRID_H - NA_R)):
        ks = _na_key_start(r0)
        for rl in range(NA_R):
            r = r0 + rl
            rs = min(max(r - 4, 0), GRID_H - 8)
            slot = [kr - r + 7 if rs <= kr < rs + 8 else NA_DR - 1 for kr in range(ks, ks + NA_KR)]
            for m in range(NA_KR // 2):
                o_ref[0, var, GRID_W * rl:GRID_W * (rl + 1), LANE * m:LANE * (m + 1)] = jnp.where(
                    low_half, tz_ref[slot[2 * m]], tz_ref[slot[2 * m + 1]])


def _natten_kernel(q_ref, k_ref, v_ref, kc_ref, vc_ref, bias_ref, o_ref):
    nq = NA_R * GRID_W
    nk = NA_KR * GRID_W
    kc = kc_ref[...]
    vc = vc_ref[...]

    def block(i, carry):
        ks = jnp.clip(NA_R * i - 4, 0, GRID_H - NA_KR)
        var = jnp.where(i == 0, 0, jnp.where(i == NA_NB - 1, 2, 1))
        krows = pl.ds(pl.multiple_of(ks * GRID_W, nq), nk)
        qrows = pl.ds(pl.multiple_of(i * nq, nq), nq)
        kw = k_ref[krows, :]
        vw = v_ref[krows, :]

        def scores(qm, hh):
            return [_dot_nt(qm, kw) + bias_ref[hh, var], _dot_nt(qm, kc)]

        def values(ps):
            return _dot(ps[0], vw) + _dot(ps[1], vc)

        o_ref[qrows, :] = _pair_attention(q_ref[qrows, :], scores, values).astype(BF16)
        return carry

    lax.fori_loop(0, NA_NB, block, 0)


def _ffn_kernel(*refs, final):
    if final:
        x_ref, f_ref, r_ref, n_ref, mod_ref, g_ref, wo_ref, w1_ref, w3_ref, w2_ref, gf_ref, o_ref, acc_ref = refs
    else:
        x_ref, f_ref, r_ref, n_ref, mod_ref, g_ref, wo_ref, w1_ref, w3_ref, w2_ref, o_ref, acc_ref = refs
    r0, n0 = FW, FW + RH * LANE
    y = _dot(f_ref[...], wo_ref[:r0, :]) + _dot(r_ref[...], wo_ref[r0:n0, :]) + _dot(n_ref[...], wo_ref[n0:, :])
    x1 = x_ref[...] + mod_ref[0, 2:3, :] * y
    hb = (_rmsnorm(x1, g_ref[...]) * (1.0 + mod_ref[0, 4:5, :]) + mod_ref[0, 3:4, :]).astype(BF16)
    fc = 256
    for c in range(DFF // fc):
        a = _dot(hb, w1_ref[:, c * fc:(c + 1) * fc])
        b = _dot(hb, w3_ref[:, c * fc:(c + 1) * fc])
        t = _dot((_silu(a) * b).astype(BF16), w2_ref[c * fc:(c + 1) * fc, :])
        if c == 0:
            acc_ref[...] = t
        else:
            acc_ref[...] += t
    x2 = x1 + mod_ref[0, 5:6, :] * acc_ref[...]
    if final:
        x2 = _rmsnorm(x2, gf_ref[...])
    o_ref[...] = x2


def _params(sem):
    return pltpu.CompilerParams(dimension_semantics=sem, vmem_limit_bytes=VMEM_LIMIT)


def _const_spec(shape):
    nd = len(shape)
    return pl.BlockSpec(shape, lambda *_: (0,) * nd, pipeline_mode=pl.Buffered(1))


def _smem_spec():
    return pl.BlockSpec(memory_space=pltpu.SMEM)


def _ada(cvec, w_ada, b_ada):
    tn = 1024
    return pl.pallas_call(
        _ada_kernel,
        grid=(DEPTH, 6 * D // tn),
        in_specs=[pl.BlockSpec((8, D), lambda i, j: (0, 0)),
                  pl.BlockSpec((1, D, tn), lambda i, j: (i, 0, j)),
                  pl.BlockSpec((1, 1, tn), lambda i, j: (i, 0, j))],
        out_specs=pl.BlockSpec((1, 8, tn), lambda i, j: (i, 0, j)),
        out_shape=jax.ShapeDtypeStruct((DEPTH, 8, 6 * D), F32),
        compiler_params=_params(("arbitrary", "arbitrary")),
        name="ada",
    )(cvec, w_ada, b_ada.reshape(DEPTH, 1, 6 * D))


def _proj(x2d, mods, g, wf, w, rope_tabs, *, ctx):
    rows = x2d.shape[0]
    nt = rows // TM
    per_batch = L // TM
    mod_idx = (lambda i: (2, 0, 0)) if ctx else (lambda i: (i // per_batch, 0, 0))
    in_specs = [pl.BlockSpec((TM, D), lambda i: (i, 0)),
                pl.BlockSpec((1, 6, D), mod_idx),
                _const_spec((1, D)),
                _const_spec((D, FW)),
                _const_spec((D, NP))]
    args = [x2d, mods, g.reshape(1, D), wf, w]
    if not ctx:
        in_specs += [pl.BlockSpec((TM, LANE), lambda i: (i % per_batch, 0))] * 2
        args += list(rope_tabs)
    return pl.pallas_call(
        functools.partial(_proj_kernel, rope=not ctx),
        grid=(nt,),
        in_specs=in_specs,
        out_specs=[pl.BlockSpec((TM, FW), lambda i: (i, 0)), pl.BlockSpec((TM, NP), lambda i: (i, 0))],
        out_shape=[jax.ShapeDtypeStruct((rows, FW), BF16), jax.ShapeDtypeStruct((rows, NP), BF16)],
        compiler_params=_params(("arbitrary",)),
        name="proj_ctx" if ctx else "proj",
    )(*args)


def _fourier(f2d, wf, a1, m3, e):
    y = pl.pallas_call(
        _four1_kernel,
        grid=(B, DFT_N2 // F1_R),
        in_specs=[_const_spec((2 * DFT_N1 * F1_R, DFT_N1 * F1_R)),
                  pl.BlockSpec((1, DFT_N1, F1_R, FW), lambda b, j: (b, 0, j, 0))],
        out_specs=pl.BlockSpec((1, 2 * DFT_N1, F1_R, FW), lambda b, j: (b, 0, j, 0)),
        out_shape=jax.ShapeDtypeStruct((B, 2 * DFT_N1, DFT_N2, FW), BF16),
        compiler_params=_params(("arbitrary", "arbitrary")),
        name="four1",
    )(a1, f2d.reshape(B, DFT_N1, DFT_N2, FW))
    out = pl.pallas_call(
        _four3_kernel,
        grid=(DFT_N1 // F3_K,),
        in_specs=[pl.BlockSpec((F3_K, 2 * DFT_N2, 2 * DFT_N2), lambda k: (k, 0, 0)),
                  pl.BlockSpec((B, 2, F3_K, DFT_N2, FW), lambda k: (0, 0, k, 0, 0)),
                  _const_spec((2 * FW, FW)),
                  _const_spec((FW, FW))],
        out_specs=pl.BlockSpec((B, DFT_N2, F3_K * FW), lambda k: (0, 0, k)),
        out_shape=jax.ShapeDtypeStruct((B, DFT_N2, DFT_N1 * FW), BF16),
        compiler_params=_params(("arbitrary",)),
        name="four3",
    )(m3, y.reshape(B, 2, DFT_N1, DFT_N2, FW), e, wf)
    return out.reshape(B * L, FW)


def _retention(p, s0, lg, gain):
    col = lambda c0: (lambda b, h: (b, c0 // LANE + h))
    return pl.pallas_call(
        _ret_kernel,
        grid=(B, RH),
        in_specs=[_smem_spec(),
                  pl.BlockSpec((L, LANE), col(C_RQ)),
                  pl.BlockSpec((L, LANE), col(C_RK)),
                  pl.BlockSpec((L, LANE), col(C_RV)),
                  pl.BlockSpec((L, LANE), col(C_RG)),
                  pl.BlockSpec((1, 1, 2, LANE, LANE), lambda b, h: (b, h, 0, 0, 0)),
                  pl.BlockSpec((1, LANE), lambda b, h: (0, h))],
        out_specs=pl.BlockSpec((L, LANE), lambda b, h: (b, h)),
        out_shape=jax.ShapeDtypeStruct((B * L, RH * LANE), BF16),
        scratch_shapes=[pltpu.VMEM((L, LANE), F32), pltpu.VMEM((LANE, LANE), F32), pltpu.VMEM((LANE, LANE), F32)],
        compiler_params=_params(("arbitrary", "arbitrary")),
        name="ret",
    )(lg, p, p, p, p, s0, gain)


def _ctx_state(pc, lg):
    col = lambda c0: (lambda b, h: (b, c0 // LANE + h))
    return pl.pallas_call(
        _ctx_state_kernel,
        grid=(B, RH),
        in_specs=[_smem_spec(),
                  pl.BlockSpec((LC, LANE), col(C_RK)),
                  pl.BlockSpec((LC, LANE), col(C_RV))],
        out_specs=pl.BlockSpec((1, 1, 2, LANE, LANE), lambda b, h: (b, h, 0, 0, 0)),
        out_shape=jax.ShapeDtypeStruct((B, RH, 2, LANE, LANE), F32),
        compiler_params=_params(("arbitrary", "arbitrary")),
        name="ctx_state",
    )(lg, pc, pc)


def _ctx_mix(fc, pc, lg, gain, wf, c256, s256, e):
    return pl.pallas_call(
        _ctx_mix_kernel,
        grid=(B,),
        in_specs=[_smem_spec(),
                  pl.BlockSpec((LC, FW), lambda b: (b, 0)),
                  pl.BlockSpec((LC, NP), lambda b: (b, 0)),
                  _const_spec((1, RH * LANE)),
                  _const_spec((FW, FW)),
                  _const_spec((LC, LC)),
                  _const_spec((LC, LC)),
                  _const_spec((2 * FW, FW))],
        out_specs=[pl.BlockSpec((LC, FW), lambda b: (b, 0)),
                   pl.BlockSpec((LC, RH * LANE), lambda b: (b, 0)),
                   pl.BlockSpec((LC, NH * ND), lambda b: (b, 0))],
        out_shape=[jax.ShapeDtypeStruct((B * LC, FW), BF16),
                   jax.ShapeDtypeStruct((B * LC, RH * LANE), BF16),
                   jax.ShapeDtypeStruct((B * LC, NH * ND), BF16)],
        compiler_params=_params(("arbitrary",)),
        name="ctx_mix",
    )(lg, fc, pc, gain, wf, c256, s256, e)


def _na_bias(base, mask):
    nq, nk = NA_R * GRID_W, NA_KR * GRID_W
    return pl.pallas_call(
        _na_bias_kernel,
        grid=(NH,),
        in_specs=[pl.BlockSpec((1, NA_DR, LANE), lambda h: (h, 0, 0)),
                  _const_spec((GRID_W, LANE))],
        out_specs=pl.BlockSpec((1, 3, nq, nk), lambda h: (h, 0, 0, 0)),
        out_shape=jax.ShapeDtypeStruct((NH, 3, nq, nk), F32),
        scratch_shapes=[pltpu.VMEM((NA_DR, GRID_W, LANE), F32)],
        compiler_params=_params(("arbitrary",)),
        name="na_bias",
    )(base, mask)


def _natten(p, pc, bias):
    nq, nk = NA_R * GRID_W, NA_KR * GRID_W
    return pl.pallas_call(
        _natten_kernel,
        grid=(B, NPAIR),
        in_specs=[pl.BlockSpec((L, LANE), lambda b, pr: (b, C_NQ // LANE + pr)),
                  pl.BlockSpec((L, LANE), lambda b, pr: (b, C_NK // LANE + pr)),
                  pl.BlockSpec((L, LANE), lambda b, pr: (b, C_NV // LANE + pr)),
                  pl.BlockSpec((LC, LANE), lambda b, pr: (b, C_NK // LANE + pr)),
                  pl.BlockSpec((LC, LANE), lambda b, pr: (b, C_NV // LANE + pr)),
                  pl.BlockSpec((2, 3, nq, nk), lambda b, pr: (pr, 0, 0, 0))],
        out_specs=pl.BlockSpec((L, LANE), lambda b, pr: (b, pr)),
        out_shape=jax.ShapeDtypeStruct((B * L, NH * ND), BF16),
        compiler_params=_params(("arbitrary", "arbitrary")),
        name="natten",
    )(p, p, p, pc, pc, bias)


def _ffn(x2d, four, ret, na, mods, g, wo, w1, w3, w2, g_final, *, ctx):
    rows = x2d.shape[0]
    per_batch = L // TM
    mod_idx = (lambda i: (2, 0, 0)) if ctx else (lambda i: (i // per_batch, 0, 0))
    final = g_final is not None
    row = lambda w: pl.BlockSpec((TM, w), lambda i: (i, 0))
    in_specs = [row(D), row(FW), row(RH * LANE), row(NH * ND),
                pl.BlockSpec((1, 6, D), mod_idx),
                _const_spec((1, D)),
                _const_spec((YW, D)), _const_spec((D, DFF)), _const_spec((D, DFF)), _const_spec((DFF, D))]
    args = [x2d, four, ret, na, mods, g.reshape(1, D), wo, w1, w3, w2]
    if final:
        in_specs.append(_const_spec((1, D)))
        args.append(g_final.reshape(1, D))
    return pl.pallas_call(
        functools.partial(_ffn_kernel, final=final),
        grid=(rows // TM,),
        in_specs=in_specs,
        out_specs=row(D),
        out_shape=jax.ShapeDtypeStruct((rows, D), F32),
        scratch_shapes=[pltpu.VMEM((TM, D), F32)],
        compiler_params=_params(("arbitrary",)),
        name="ffn_ctx" if ctx else ("ffn_final" if final else "ffn"),
    )(*args)


def _take_padded(a, src, axis):
    out = jnp.take(a, jnp.asarray(np.maximum(src, 0)), axis=axis)
    shape = [1] * a.ndim
    shape[axis] = src.shape[0]
    return jnp.where(jnp.asarray(src >= 0).reshape(shape), out, jnp.zeros((), a.dtype))


def kernel(x, c, ctx, c_ctx, w_ada, b_ada, g_mix, w_in, ret_decay_logit, ret_norm_g, w_four,
           na_rpb, w_out, g_ffn, w1, w3, w2, g_final):
    src_f, src_p = _proj_sources()
    src_y = _ycat_sources()
    src_g = _gain_sources()
    rope_tabs = tuple(jnp.asarray(t) for t in _rope_tables())
    a1, m3, e, c256, s256 = (jnp.asarray(t).astype(BF16) for t in _dft_tables())
    tz_src, tz_mask = _na_toeplitz_tables()
    col_scale = np.ones((1, NP), np.float32)
    col_scale[0, C_NQ:C_NK] = LOG2E * ND ** -0.5

    cvec = jnp.zeros((8, D), F32).at[0:B].set(c).at[B].set(c_ctx)
    mods_all = _ada(cvec, w_ada, b_ada).reshape(DEPTH, 8, 6, D)

    xl = x.reshape(B * L, D)
    xc = ctx.reshape(B * LC, D)
    for i in range(DEPTH):
        last = i == DEPTH - 1
        mods = mods_all[i]
        wf_in = w_in[i][:, src_f].astype(BF16)
        w_in_p = (_take_padded(w_in[i], src_p, 1) * jnp.asarray(col_scale)).astype(BF16)
        wo_p = _take_padded(w_out[i], src_y, 0).astype(BF16)
        gain = _take_padded(ret_norm_g[i], src_g, 0).reshape(1, RH * LANE)
        wfour = w_four[i].astype(BF16)
        lg = jax.nn.log_sigmoid(ret_decay_logit[i].astype(F32))
        tz_base = jnp.pad(_take_padded(na_rpb[i].astype(F32), tz_src, 2), ((0, 0), (0, 1), (0, 0)))
        na_bias = _na_bias(tz_base, jnp.asarray(tz_mask))
        w1b, w3b, w2b = w1[i].astype(BF16), w3[i].astype(BF16), w2[i].astype(BF16)

        f_l, p_l = _proj(xl, mods, g_mix[i], wf_in, w_in_p, rope_tabs, ctx=False)
        f_c, p_c = _proj(xc, mods, g_mix[i], wf_in, w_in_p, None, ctx=True)
        s0 = _ctx_state(p_c, lg)
        four_l = _fourier(f_l, wfour, a1, m3, e)
        ret_l = _retention(p_l, s0, lg, gain)
        na_l = _natten(p_l, p_c, na_bias)
        if not last:
            four_c, ret_c, na_c = _ctx_mix(f_c, p_c, lg, gain, wfour, c256, s256, e)
            xc = _ffn(xc, four_c, ret_c, na_c, mods, g_ffn[i], wo_p, w1b, w3b, w2b, None, ctx=True)
        xl = _ffn(xl, four_l, ret_l, na_l, mods, g_ffn[i], wo_p, w1b, w3b, w2b,
                  g_final if last else None, ctx=False)
    return xl.reshape(B, L, D)
```

```python
import functools

import numpy as np
import jax
import jax.numpy as jnp
from jax import lax
from jax.experimental import pallas as pl
from jax.experimental.pallas import tpu as pltpu

F32 = jnp.float32
BF16 = jnp.bfloat16

D = 1024
B = 2
L = 8192
LC = 256
DEPTH = 2
GRID_W = 64
GRID_H = L // GRID_W
FW = 256
FGW = 64
RH = 4
RD = 96
NH = 6
ND = 64
NPAIR = NH // 2
DFF = 2816
EPS = 1e-6
NEG = -1e30
LOG2E = float(np.log2(np.e))
LANE = 128

C_RQ, C_RK, C_RV, C_RG = 0, 512, 1024, 1536
C_NQ, C_NK, C_NV = 2048, 2432, 2816
NP = 3200
YW = FW + RH * LANE + NH * ND

TM = 512
RET_C = 256
NA_R = 4
NA_KR = NA_R + 8
NA_NB = GRID_H // NA_R
DFT_N1 = 64
DFT_N2 = 128
F1_R = 16
F3_K = 8
VMEM_LIMIT = 56 * 1024 * 1024


def _qk_lane_dims():
    m = np.full((LANE,), -1, np.int64)
    m[0:24] = np.arange(0, 24)
    m[24:48] = np.arange(48, 72)
    m[64:88] = np.arange(24, 48)
    m[88:112] = np.arange(72, 96)
    return m


def _v_lane_dims():
    m = np.full((LANE,), -1, np.int64)
    m[:RD] = np.arange(RD)
    return m


def _proj_sources():
    src = np.full((NP,), -1, np.int64)
    qk, vv = _qk_lane_dims(), _v_lane_dims()
    for dst0, base, lanes in ((C_RQ, 256, qk), (C_RK, 640, qk), (C_RV, 1024, vv), (C_RG, 1408, vv)):
        for h in range(RH):
            ok = lanes >= 0
            src[dst0 + LANE * h + np.nonzero(ok)[0]] = base + RD * h + lanes[ok]
    src[C_NQ:NP] = 1792 + np.arange(NP - C_NQ)
    return src


def _ycat_sources():
    src = np.full((YW,), -1, np.int64)
    src[:FW] = np.arange(FW)
    for h in range(RH):
        src[FW + LANE * h: FW + LANE * h + RD] = FW + RD * h + np.arange(RD)
    src[FW + RH * LANE:] = FW + RH * RD + np.arange(NH * ND)
    return src


def _gain_sources():
    src = np.full((RH * LANE,), -1, np.int64)
    for h in range(RH):
        src[LANE * h: LANE * h + RD] = RD * h + np.arange(RD)
    return src


def _rope_tables():
    pos = np.arange(L)
    prow, pcol = pos // GRID_W, pos % GRID_W
    half = RD // 4
    inv = 10000.0 ** (-np.arange(half, dtype=np.float64) / half)
    ar = prow[:, None] * inv[None, :]
    ac = pcol[:, None] * inv[None, :]
    cos = np.ones((L, LANE), np.float64)
    sin = np.zeros((L, LANE), np.float64)
    for off, sign in ((0, -1.0), (64, 1.0)):
        cos[:, off:off + 24] = np.cos(ar)
        cos[:, off + 24:off + 48] = np.cos(ac)
        sin[:, off:off + 24] = sign * np.sin(ar)
        sin[:, off + 24:off + 48] = sign * np.sin(ac)
    return cos.astype(np.float32), sin.astype(np.float32)


def _dft_tables():
    k1 = np.arange(DFT_N1)
    a = 2 * np.pi * ((k1[:, None] * k1[None, :]) % DFT_N1) / DFT_N1
    a1 = np.kron(np.concatenate([np.cos(a), -np.sin(a)], axis=0), np.eye(F1_R))
    k2 = np.arange(DFT_N2)
    l2 = np.arange(DFT_N2)
    kk = k1[:, None, None] + DFT_N1 * k2[None, :, None]
    ang = 2 * np.pi * ((kk * l2[None, None, :]) % L) / L
    ct, st = np.cos(ang) / np.sqrt(L), np.sin(ang) / np.sqrt(L)
    m3 = np.concatenate([np.concatenate([ct, st], axis=2), np.concatenate([st, -ct], axis=2)], axis=1)
    c = np.arange(FW)
    same = (c[:, None] // FGW) == (c[None, :] // FGW)
    ac = 2 * np.pi * (((c[:, None] % FGW) * (c[None, :] % FGW)) % FGW) / FGW
    e = np.concatenate([np.where(same, np.cos(ac), 0.0), -np.where(same, np.sin(ac), 0.0)], axis=0) / 8.0
    p = np.arange(LC)
    ap = 2 * np.pi * ((p[:, None] * p[None, :]) % LC) / LC
    return tuple(t.astype(np.float32) for t in (a1, m3, e, np.cos(ap) / 16.0, np.sin(ap) / 16.0))


NA_DR = 16


def _na_toeplitz_tables():
    src = np.full((LANE,), -1, np.int64)
    src[0:16] = np.arange(0, 16) + 15
    src[113:128] = np.arange(113, 128) - 113
    src[49:80] = np.arange(49, 80) - 49
    qc = np.arange(GRID_W)[:, None]
    kc = np.arange(LANE)[None, :] % GRID_W
    ws = np.clip(qc - 8, 0, GRID_W - 16)
    mask = np.where((kc >= ws) & (kc < ws + 16), 0.0, NEG).astype(np.float32)
    return src, mask


def _dot(a, b):
    return jnp.dot(a, b, preferred_element_type=F32)


def _dot_nt(a, b):
    return lax.dot_general(a, b, (((1,), (1,)), ((), ())), preferred_element_type=F32)


def _silu(x):
    return x * jax.nn.sigmoid(x)


def _rmsnorm(x, g):
    return x * lax.rsqrt(jnp.mean(x * x, axis=-1, keepdims=True) + EPS) * g


def _ada_kernel(c_ref, w_ref, b_ref, o_ref):
    s = _silu(c_ref[...]).astype(BF16)
    o_ref[0] = _dot(s, w_ref[0].astype(BF16)) + b_ref[0]


def _proj_kernel(*refs, rope):
    if rope:
        x_ref, mod_ref, g_ref, wf_ref, w_ref, cos_ref, sin_ref, f_ref, o_ref = refs
    else:
        x_ref, mod_ref, g_ref, wf_ref, w_ref, f_ref, o_ref = refs
    h = _rmsnorm(x_ref[...], g_ref[...]) * (1.0 + mod_ref[0, 1:2, :]) + mod_ref[0, 0:1, :]
    hb = h.astype(BF16)
    f_ref[...] = _dot(hb, wf_ref[...]).astype(BF16)
    k_scale = RD ** -0.5
    for c0, scale in ((C_RQ, None), (C_RK, k_scale)):
        t4 = _dot(hb, w_ref[:, c0:c0 + RH * LANE])
        for hh in range(RH):
            t = t4[:, LANE * hh:LANE * (hh + 1)]
            if rope:
                t = t * cos_ref[...] + pltpu.roll(t, 64, 1) * sin_ref[...]
            if scale is not None:
                t = t * scale
            o_ref[:, c0 + LANE * hh:c0 + LANE * (hh + 1)] = t.astype(BF16)
    for c0, c1 in ((C_RV, C_RG), (C_RG, C_NQ), (C_NQ, C_NK), (C_NK, C_NV), (C_NV, NP)):
        o_ref[:, c0:c1] = _dot(hb, w_ref[:, c0:c1]).astype(BF16)


def _four1_kernel(a_ref, x_ref, y_ref):
    x = x_ref[0].reshape(DFT_N1 * F1_R, FW)
    y = _dot(a_ref[...], x).astype(BF16)
    y_ref[0] = y.reshape(2 * DFT_N1, F1_R, FW)


def _four3_kernel(m_ref, y_ref, e_ref, wf_ref, o_ref):
    for b in range(B):
        res = []
        for kk in range(F3_K):
            m = m_ref[kk]
            z = _dot(m[:, :DFT_N2], y_ref[b, 0, kk]) + _dot(m[:, DFT_N2:], y_ref[b, 1, kk])
            fr = _dot(z[:DFT_N2].astype(BF16), e_ref[:FW, :]) + _dot(z[DFT_N2:].astype(BF16), e_ref[FW:, :])
            res.append(_dot(fr.astype(BF16), wf_ref[...]))
        o_ref[b] = jnp.stack(res, axis=1)


def _decay_matrix(n, lgf, lgb):
    ii = lax.broadcasted_iota(jnp.int32, (n, n), 0)
    jj = lax.broadcasted_iota(jnp.int32, (n, n), 1)
    diff = (ii - jj).astype(F32)
    fwd = jnp.where(diff >= 0, jnp.exp(jnp.maximum(diff, 0.0) * lgf), 0.0)
    bwd = jnp.where(diff <= 0, jnp.exp(jnp.maximum(-diff, 0.0) * lgb), 0.0)
    return fwd + bwd


def _ret_readout(o, gate, gain):
    ms = jnp.sum(o * o, axis=-1, keepdims=True) * (1.0 / RD)
    return (o * lax.rsqrt(ms + EPS) * gain * _silu(gate.astype(F32))).astype(BF16)


def _ret_kernel(lg_ref, q_ref, k_ref, v_ref, g_ref, s0_ref, gain_ref, o_ref, acc_ref, stf_ref, stb_ref):
    h = pl.program_id(1)
    lgf = lg_ref[0, h]
    lgb = lg_ref[1, h]
    c = RET_C
    n = L // c
    dmat = _decay_matrix(c, lgf, lgb)
    idx = lax.broadcasted_iota(jnp.int32, (c, LANE), 0).astype(F32)
    qdf = jnp.exp((idx + 1.0) * lgf)
    kdf = jnp.exp((c - 1.0 - idx) * lgf)
    qdb = jnp.exp((c - idx) * lgb)
    kdb = jnp.exp(idx * lgb)
    cdf = jnp.exp(jnp.full((1, LANE), float(c), F32) * lgf)
    cdb = jnp.exp(jnp.full((1, LANE), float(c), F32) * lgb)
    gain = gain_ref[...]

    dirs = ((stf_ref, qdf, kdf, cdf), (stb_ref, qdb, kdb, cdb))
    stf_ref[...] = s0_ref[0, 0, 0]
    stb_ref[...] = s0_ref[0, 0, 1]

    def visit(r0, direction, second):
        st_ref, qdec, kdec, cdec = dirs[direction]
        rows = pl.ds(pl.multiple_of(r0, c), c)
        q = q_ref[rows, :]
        k = k_ref[rows, :]
        v = v_ref[rows, :]
        cross = _dot(q, st_ref[...].astype(BF16)) * qdec
        if second:
            o_ref[rows, :] = _ret_readout(acc_ref[rows, :] + cross, g_ref[rows, :], gain)
        else:
            s = _dot_nt(q, k) * dmat
            acc_ref[rows, :] = _dot(s.astype(BF16), v) + cross
        kt = (k.astype(F32) * kdec).T.astype(BF16)
        st_ref[...] = st_ref[...] * cdec + _dot(kt, v)

    def body(second):
        def step(i, carry):
            visit(i * c, 0, second)
            visit((n - 1 - i) * c, 1, second)
            return carry
        return step

    lax.fori_loop(0, n // 2, body(False), 0)
    lax.fori_loop(n // 2, n, body(True), 0)


def _ctx_state_kernel(lg_ref, k_ref, v_ref, o_ref):
    h = pl.program_id(1)
    idx = lax.broadcasted_iota(jnp.int32, (LC, LANE), 0).astype(F32)
    k = k_ref[...].astype(F32)
    v = v_ref[...]
    wf = jnp.exp((LC - 1.0 - idx) * lg_ref[0, h])
    wb = jnp.exp(idx * lg_ref[1, h])
    o_ref[0, 0, 0] = _dot((k * wf).T.astype(BF16), v)
    o_ref[0, 0, 1] = _dot((k * wb).T.astype(BF16), v)


def _pair_attention(q, score_fn, value_fn):
    lane = lax.broadcasted_iota(jnp.int32, q.shape, 1)
    outs = []
    for hh in range(2):
        sel = (lane >= ND * hh) & (lane < ND * (hh + 1))
        qm = jnp.where(sel, q, jnp.zeros_like(q))
        scores = score_fn(qm, hh)
        m = functools.reduce(jnp.maximum, [jnp.max(s, axis=-1, keepdims=True) for s in scores])
        ps = [jnp.exp2(s - m) for s in scores]
        den = functools.reduce(jnp.add, [jnp.sum(p, axis=-1, keepdims=True) for p in ps])
        outs.append(value_fn([p.astype(BF16) for p in ps]) / den)
    return jnp.where(lane < ND, outs[0], outs[1])


def _ctx_mix_kernel(lg_ref, f_ref, p_ref, gain_ref, wf_ref, c_ref, s_ref, e_ref, fo_ref, ro_ref, no_ref):
    f = f_ref[...]
    pr = _dot(c_ref[...], f)
    qi = _dot(s_ref[...], f)
    fr = _dot(pr.astype(BF16), e_ref[:FW, :]) + _dot(qi.astype(BF16), e_ref[FW:, :])
    fo_ref[...] = _dot(fr.astype(BF16), wf_ref[...]).astype(BF16)
    for h in range(RH):
        sl = lambda c0: slice(c0 + LANE * h, c0 + LANE * (h + 1))
        dmat = _decay_matrix(LC, lg_ref[0, h], lg_ref[1, h])
        s = _dot_nt(p_ref[:, sl(C_RQ)], p_ref[:, sl(C_RK)]) * dmat
        o = _dot(s.astype(BF16), p_ref[:, sl(C_RV)])
        ro_ref[:, LANE * h:LANE * (h + 1)] = _ret_readout(o, p_ref[:, sl(C_RG)], gain_ref[:, LANE * h:LANE * (h + 1)])
    for pair in range(NPAIR):
        sl = lambda c0: slice(c0 + LANE * pair, c0 + LANE * (pair + 1))
        k = p_ref[:, sl(C_NK)]
        v = p_ref[:, sl(C_NV)]
        out = _pair_attention(p_ref[:, sl(C_NQ)], lambda qm, hh: [_dot_nt(qm, k)], lambda ps: _dot(ps[0], v))
        no_ref[:, LANE * pair:LANE * (pair + 1)] = out.astype(BF16)


def _na_key_start(r0):
    return min(max(r0 - 4, 0), GRID_H - NA_KR)


def _na_bias_kernel(base_ref, mask_ref, o_ref, tz_ref):
    qc = lax.broadcasted_iota(jnp.int32, (GRID_W, LANE), 0)
    for dr in range(NA_DR - 1):
        t = jnp.broadcast_to(base_ref[0, dr:dr + 1, :], (GRID_W, LANE))
        for bit in range(6):
            t = jnp.where((qc >> bit) & 1 == 1, pltpu.roll(t, 1 << bit, 1), t)
        tz_ref[dr] = t * LOG2E + mask_ref[...]
    tz_ref[NA_DR - 1] = jnp.full((GRID_W, LANE), NEG, F32)
    low_half = lax.broadcasted_iota(jnp.int32, (GRID_W, LANE), 1) < GRID_W
    for var, r0 in enumerate((0, 2 * NA_R, GRID_H - NA_R)):
        ks = _na_key_start(r0)
        for rl in range(NA_R):
            r = r0 + rl
            rs = min(max(r - 4, 0), GRID_H - 8)
            slot = [kr - r + 7 if rs <= kr < rs + 8 else NA_DR - 1 for kr in range(ks, ks + NA_KR)]
            for m in range(NA_KR // 2):
                o_ref[0, var, GRID_W * rl:GRID_W * (rl + 1), LANE * m:LANE * (m + 1)] = jnp.where(
                    low_half, tz_ref[slot[2 * m]], tz_ref[slot[2 * m + 1]])


def _natten_kernel(q_ref, k_ref, v_ref, kc_ref, vc_ref, bias_ref, o_ref):
    nq = NA_R * GRID_W
    nk = NA_KR * GRID_W
    kc = kc_ref[...]
    vc = vc_ref[...]

    def block(i, carry):
        ks = jnp.clip(NA_R * i - 4, 0, GRID_H - NA_KR)
        var = jnp.where(i == 0, 0, jnp.where(i == NA_NB - 1, 2, 1))
        krows = pl.ds(pl.multiple_of(ks * GRID_W, nq), nk)
        qrows = pl.ds(pl.multiple_of(i * nq, nq), nq)
        kw = k_ref[krows, :]
        vw = v_ref[krows, :]

        def scores(qm, hh):
            return [_dot_nt(qm, kw) + bias_ref[hh, var], _dot_nt(qm, kc)]

        def values(ps):
            return _dot(ps[0], vw) + _dot(ps[1], vc)

        o_ref[qrows, :] = _pair_attention(q_ref[qrows, :], scores, values).astype(BF16)
        return carry

    lax.fori_loop(0, NA_NB, block, 0)


def _ffn_kernel(*refs, final):
    if final:
        x_ref, f_ref, r_ref, n_ref, mod_ref, g_ref, wo_ref, w1_ref, w3_ref, w2_ref, gf_ref, o_ref, acc_ref = refs
    else:
        x_ref, f_ref, r_ref, n_ref, mod_ref, g_ref, wo_ref, w1_ref, w3_ref, w2_ref, o_ref, acc_ref = refs
    r0, n0 = FW, FW + RH * LANE
    y = (_dot(f_ref[...].astype(BF16), wo_ref[:r0, :]) + _dot(r_ref[...], wo_ref[r0:n0, :])
         + _dot(n_ref[...], wo_ref[n0:, :]))
    x1 = x_ref[...] + mod_ref[0, 2:3, :] * y
    hb = (_rmsnorm(x1, g_ref[...]) * (1.0 + mod_ref[0, 4:5, :]) + mod_ref[0, 3:4, :]).astype(BF16)
    fc = 256
    for c in range(DFF // fc):
        a = _dot(hb, w1_ref[:, c * fc:(c + 1) * fc])
        b = _dot(hb, w3_ref[:, c * fc:(c + 1) * fc])
        t = _dot((_silu(a) * b).astype(BF16), w2_ref[c * fc:(c + 1) * fc, :])
        if c == 0:
            acc_ref[...] = t
        else:
            acc_ref[...] += t
    x2 = x1 + mod_ref[0, 5:6, :] * acc_ref[...]
    if final:
        x2 = _rmsnorm(x2, gf_ref[...])
    o_ref[...] = x2


def _params(sem):
    return pltpu.CompilerParams(dimension_semantics=sem, vmem_limit_bytes=VMEM_LIMIT)


def _const_spec(shape):
    nd = len(shape)
    return pl.BlockSpec(shape, lambda *_: (0,) * nd, pipeline_mode=pl.Buffered(1))


def _smem_spec():
    return pl.BlockSpec(memory_space=pltpu.SMEM)


def _ada(cvec, w_ada, b_ada):
    tn = 1024
    return pl.pallas_call(
        _ada_kernel,
        grid=(DEPTH, 6 * D // tn),
        in_specs=[pl.BlockSpec((8, D), lambda i, j: (0, 0)),
                  pl.BlockSpec((1, D, tn), lambda i, j: (i, 0, j)),
                  pl.BlockSpec((1, 1, tn), lambda i, j: (i, 0, j))],
        out_specs=pl.BlockSpec((1, 8, tn), lambda i, j: (i, 0, j)),
        out_shape=jax.ShapeDtypeStruct((DEPTH, 8, 6 * D), F32),
        compiler_params=_params(("arbitrary", "arbitrary")),
        name="ada",
    )(cvec, w_ada, b_ada.reshape(DEPTH, 1, 6 * D))


def _proj(x2d, mods, g, wf, w, rope_tabs, *, ctx):
    rows = x2d.shape[0]
    nt = rows // TM
    per_batch = L // TM
    mod_idx = (lambda i: (2, 0, 0)) if ctx else (lambda i: (i // per_batch, 0, 0))
    in_specs = [pl.BlockSpec((TM, D), lambda i: (i, 0)),
                pl.BlockSpec((1, 6, D), mod_idx),
                _const_spec((1, D)),
                _const_spec((D, FW)),
                _const_spec((D, NP))]
    args = [x2d, mods, g.reshape(1, D), wf, w]
    if not ctx:
        in_specs += [pl.BlockSpec((TM, LANE), lambda i: (i % per_batch, 0))] * 2
        args += list(rope_tabs)
    return pl.pallas_call(
        functools.partial(_proj_kernel, rope=not ctx),
        grid=(nt,),
        in_specs=in_specs,
        out_specs=[pl.BlockSpec((TM, FW), lambda i: (i, 0)), pl.BlockSpec((TM, NP), lambda i: (i, 0))],
        out_shape=[jax.ShapeDtypeStruct((rows, FW), BF16), jax.ShapeDtypeStruct((rows, NP), BF16)],
        compiler_params=_params(("arbitrary",)),
        name="proj_ctx" if ctx else "proj",
    )(*args)


def _fourier(f2d, wf, a1, m3, e):
    y = pl.pallas_call(
        _four1_kernel,
        grid=(B, DFT_N2 // F1_R),
        in_specs=[_const_spec((2 * DFT_N1 * F1_R, DFT_N1 * F1_R)),
                  pl.BlockSpec((1, DFT_N1, F1_R, FW), lambda b, j: (b, 0, j, 0))],
        out_specs=pl.BlockSpec((1, 2 * DFT_N1, F1_R, FW), lambda b, j: (b, 0, j, 0)),
        out_shape=jax.ShapeDtypeStruct((B, 2 * DFT_N1, DFT_N2, FW), BF16),
        compiler_params=_params(("arbitrary", "arbitrary")),
        name="four1",
    )(a1, f2d.reshape(B, DFT_N1, DFT_N2, FW))
    out = pl.pallas_call(
        _four3_kernel,
        grid=(DFT_N1 // F3_K,),
        in_specs=[pl.BlockSpec((F3_K, 2 * DFT_N2, 2 * DFT_N2), lambda k: (k, 0, 0)),
                  pl.BlockSpec((B, 2, F3_K, DFT_N2, FW), lambda k: (0, 0, k, 0, 0)),
                  _const_spec((2 * FW, FW)),
                  _const_spec((FW, FW))],
        out_specs=pl.BlockSpec((B, DFT_N2, F3_K, FW), lambda k: (0, 0, k, 0)),
        out_shape=jax.ShapeDtypeStruct((B, DFT_N2, DFT_N1, FW), F32),
        compiler_params=_params(("arbitrary",)),
        name="four3",
    )(m3, y.reshape(B, 2, DFT_N1, DFT_N2, FW), e, wf)
    return out.reshape(B * L, FW)


def _retention(p, s0, lg, gain):
    col = lambda c0: (lambda b, h: (b, c0 // LANE + h))
    return pl.pallas_call(
        _ret_kernel,
        grid=(B, RH),
        in_specs=[_smem_spec(),
                  pl.BlockSpec((L, LANE), col(C_RQ)),
                  pl.BlockSpec((L, LANE), col(C_RK)),
                  pl.BlockSpec((L, LANE), col(C_RV)),
                  pl.BlockSpec((L, LANE), col(C_RG)),
                  pl.BlockSpec((1, 1, 2, LANE, LANE), lambda b, h: (b, h, 0, 0, 0)),
                  pl.BlockSpec((1, LANE), lambda b, h: (0, h))],
        out_specs=pl.BlockSpec((L, LANE), lambda b, h: (b, h)),
        out_shape=jax.ShapeDtypeStruct((B * L, RH * LANE), BF16),
        scratch_shapes=[pltpu.VMEM((L, LANE), F32), pltpu.VMEM((LANE, LANE), F32), pltpu.VMEM((LANE, LANE), F32)],
        compiler_params=_params(("arbitrary", "arbitrary")),
        name="ret",
    )(lg, p, p, p, p, s0, gain)


def _ctx_state(pc, lg):
    col = lambda c0: (lambda b, h: (b, c0 // LANE + h))
    return pl.pallas_call(
        _ctx_state_kernel,
        grid=(B, RH),
        in_specs=[_smem_spec(),
                  pl.BlockSpec((LC, LANE), col(C_RK)),
                  pl.BlockSpec((LC, LANE), col(C_RV))],
        out_specs=pl.BlockSpec((1, 1, 2, LANE, LANE), lambda b, h: (b, h, 0, 0, 0)),
        out_shape=jax.ShapeDtypeStruct((B, RH, 2, LANE, LANE), F32),
        compiler_params=_params(("arbitrary", "arbitrary")),
        name="ctx_state",
    )(lg, pc, pc)


def _ctx_mix(fc, pc, lg, gain, wf, c256, s256, e):
    return pl.pallas_call(
        _ctx_mix_kernel,
        grid=(B,),
        in_specs=[_smem_spec(),
                  pl.BlockSpec((LC, FW), lambda b: (b, 0)),
                  pl.BlockSpec((LC, NP), lambda b: (b, 0)),
                  _const_spec((1, RH * LANE)),
                  _const_spec((FW, FW)),
                  _const_spec((LC, LC)),
                  _const_spec((LC, LC)),
                  _const_spec((2 * FW, FW))],
        out_specs=[pl.BlockSpec((LC, FW), lambda b: (b, 0)),
                   pl.BlockSpec((LC, RH * LANE), lambda b: (b, 0)),
                   pl.BlockSpec((LC, NH * ND), lambda b: (b, 0))],
        out_shape=[jax.ShapeDtypeStruct((B * LC, FW), BF16),
                   jax.ShapeDtypeStruct((B * LC, RH * LANE), BF16),
                   jax.ShapeDtypeStruct((B * LC, NH * ND), BF16)],
        compiler_params=_params(("arbitrary",)),
        name="ctx_mix",
    )(lg, fc, pc, gain, wf, c256, s256, e)


def _na_bias(base, mask):
    nq, nk = NA_R * GRID_W, NA_KR * GRID_W
    return pl.pallas_call(
        _na_bias_kernel,
        grid=(NH,),
        in_specs=[pl.BlockSpec((1, NA_DR, LANE), lambda h: (h, 0, 0)),
                  _const_spec((GRID_W, LANE))],
        out_specs=pl.BlockSpec((1, 3, nq, nk), lambda h: (h, 0, 0, 0)),
        out_shape=jax.ShapeDtypeStruct((NH, 3, nq, nk), F32),
        scratch_shapes=[pltpu.VMEM((NA_DR, GRID_W, LANE), F32)],
        compiler_params=_params(("arbitrary",)),
        name="na_bias",
    )(base, mask)


def _natten(p, pc, bias):
    nq, nk = NA_R * GRID_W, NA_KR * GRID_W
    return pl.pallas_call(
        _natten_kernel,
        grid=(B, NPAIR),
        in_specs=[pl.BlockSpec((L, LANE), lambda b, pr: (b, C_NQ // LANE + pr)),
                  pl.BlockSpec((L, LANE), lambda b, pr: (b, C_NK // LANE + pr)),
                  pl.BlockSpec((L, LANE), lambda b, pr: (b, C_NV // LANE + pr)),
                  pl.BlockSpec((LC, LANE), lambda b, pr: (b, C_NK // LANE + pr)),
                  pl.BlockSpec((LC, LANE), lambda b, pr: (b, C_NV // LANE + pr)),
                  pl.BlockSpec((2, 3, nq, nk), lambda b, pr: (pr, 0, 0, 0))],
        out_specs=pl.BlockSpec((L, LANE), lambda b, pr: (b, pr)),
        out_shape=jax.ShapeDtypeStruct((B * L, NH * ND), BF16),
        compiler_params=_params(("arbitrary", "arbitrary")),
        name="natten",
    )(p, p, p, pc, pc, bias)


def _ffn(x2d, four, ret, na, mods, g, wo, w1, w3, w2, g_final, *, ctx):
    rows = x2d.shape[0]
    per_batch = L // TM
    mod_idx = (lambda i: (2, 0, 0)) if ctx else (lambda i: (i // per_batch, 0, 0))
    final = g_final is not None
    row = lambda w: pl.BlockSpec((TM, w), lambda i: (i, 0))
    in_specs = [row(D), row(FW), row(RH * LANE), row(NH * ND),
                pl.BlockSpec((1, 6, D), mod_idx),
                _const_spec((1, D)),
                _const_spec((YW, D)), _const_spec((D, DFF)), _const_spec((D, DFF)), _const_spec((DFF, D))]
    args = [x2d, four, ret, na, mods, g.reshape(1, D), wo, w1, w3, w2]
    if final:
        in_specs.append(_const_spec((1, D)))
        args.append(g_final.reshape(1, D))
    return pl.pallas_call(
        functools.partial(_ffn_kernel, final=final),
        grid=(rows // TM,),
        in_specs=in_specs,
        out_specs=row(D),
        out_shape=jax.ShapeDtypeStruct((rows, D), F32),
        scratch_shapes=[pltpu.VMEM((TM, D), F32)],
        compiler_params=_params(("arbitrary",)),
        name="ffn_ctx" if ctx else ("ffn_final" if final else "ffn"),
    )(*args)


def _take_padded(a, src, axis):
    pieces = []
    i, n = 0, len(src)
    while i < n:
        j = i + 1
        if src[i] < 0:
            while j < n and src[j] < 0:
                j += 1
            shape = list(a.shape)
            shape[axis] = j - i
            pieces.append(jnp.zeros(shape, a.dtype))
        else:
            while j < n and src[j] == src[j - 1] + 1:
                j += 1
            pieces.append(lax.slice_in_dim(a, int(src[i]), int(src[j - 1]) + 1, axis=axis))
        i = j
    return jnp.concatenate(pieces, axis=axis)


def kernel(x, c, ctx, c_ctx, w_ada, b_ada, g_mix, w_in, ret_decay_logit, ret_norm_g, w_four,
           na_rpb, w_out, g_ffn, w1, w3, w2, g_final):
    src_p = _proj_sources()
    src_y = _ycat_sources()
    src_g = _gain_sources()
    rope_tabs = tuple(jnp.asarray(t) for t in _rope_tables())
    a1, m3, e, c256, s256 = (jnp.asarray(t).astype(BF16) for t in _dft_tables())
    tz_src, tz_mask = _na_toeplitz_tables()
    col_scale = np.ones((1, NP), np.float32)
    col_scale[0, C_NQ:C_NK] = LOG2E * ND ** -0.5

    cvec = jnp.zeros((8, D), F32).at[0:B].set(c).at[B].set(c_ctx)
    mods_all = _ada(cvec, w_ada, b_ada).reshape(DEPTH, 8, 6, D)

    xl = x.reshape(B * L, D)
    xc = ctx.reshape(B * LC, D)
    for i in range(DEPTH):
        last = i == DEPTH - 1
        mods = mods_all[i]
        wf_in = w_in[i][:, :FW].astype(BF16)
        w_in_p = (_take_padded(w_in[i], src_p, 1) * jnp.asarray(col_scale)).astype(BF16)
        wo_p = _take_padded(w_out[i], src_y, 0).astype(BF16)
        gain = _take_padded(ret_norm_g[i], src_g, 0).reshape(1, RH * LANE)
        wfour = w_four[i].astype(BF16)
        lg = jax.nn.log_sigmoid(ret_decay_logit[i].astype(F32))
        tz_base = jnp.pad(_take_padded(na_rpb[i].astype(F32), tz_src, 2), ((0, 0), (0, 1), (0, 0)))
        na_bias = _na_bias(tz_base, jnp.asarray(tz_mask))
        w1b, w3b, w2b = w1[i].astype(BF16), w3[i].astype(BF16), w2[i].astype(BF16)

        f_l, p_l = _proj(xl, mods, g_mix[i], wf_in, w_in_p, rope_tabs, ctx=False)
        f_c, p_c = _proj(xc, mods, g_mix[i], wf_in, w_in_p, None, ctx=True)
        s0 = _ctx_state(p_c, lg)
        four_l = _fourier(f_l, wfour, a1, m3, e)
        ret_l = _retention(p_l, s0, lg, gain)
        na_l = _natten(p_l, p_c, na_bias)
        if not last:
            four_c, ret_c, na_c = _ctx_mix(f_c, p_c, lg, gain, wfour, c256, s256, e)
            xc = _ffn(xc, four_c, ret_c, na_c, mods, g_ffn[i], wo_p, w1b, w3b, w2b, None, ctx=True)
        xl = _ffn(xl, four_l, ret_l, na_l, mods, g_ffn[i], wo_p, w1b, w3b, w2b,
                  g_final if last else None, ctx=False)
    return xl.reshape(B, L, D)
```

```python
import functools

import numpy as np
import jax
import jax.numpy as jnp
from jax import lax
from jax.experimental import pallas as pl
from jax.experimental.pallas import tpu as pltpu

F32 = jnp.float32
BF16 = jnp.bfloat16

D = 1024
B = 2
L = 8192
LC = 256
DEPTH = 2
GRID_W = 64
GRID_H = L // GRID_W
FW = 256
FGW = 64
RH = 4
RD = 96
NH = 6
ND = 64
NPAIR = NH // 2
DFF = 2816
EPS = 1e-6
NEG = -1e30
LOG2E = float(np.log2(np.e))
LANE = 128

C_RQ, C_RK, C_RV, C_RG = 0, 512, 1024, 1536
C_NQ, C_NK, C_NV = 2048, 2432, 2816
NP = 3200
YW = FW + RH * LANE + NH * ND

TM = 512
RET_C = 256
NA_R = 4
NA_KR = NA_R + 8
NA_NB = GRID_H // NA_R
DFT_N1 = 64
DFT_N2 = 128
F1_R = 16
F3_K = 8
VMEM_LIMIT = 56 * 1024 * 1024


def _qk_lane_dims():
    m = np.full((LANE,), -1, np.int64)
    m[0:24] = np.arange(0, 24)
    m[24:48] = np.arange(48, 72)
    m[64:88] = np.arange(24, 48)
    m[88:112] = np.arange(72, 96)
    return m


def _v_lane_dims():
    m = np.full((LANE,), -1, np.int64)
    m[:RD] = np.arange(RD)
    return m


def _proj_sources():
    src = np.full((NP,), -1, np.int64)
    qk, vv = _qk_lane_dims(), _v_lane_dims()
    for dst0, base, lanes in ((C_RQ, 256, qk), (C_RK, 640, qk), (C_RV, 1024, vv), (C_RG, 1408, vv)):
        for h in range(RH):
            ok = lanes >= 0
            src[dst0 + LANE * h + np.nonzero(ok)[0]] = base + RD * h + lanes[ok]
    src[C_NQ:NP] = 1792 + np.arange(NP - C_NQ)
    return src


def _ycat_sources():
    src = np.full((YW,), -1, np.int64)
    src[:FW] = np.arange(FW)
    for h in range(RH):
        src[FW + LANE * h: FW + LANE * h + RD] = FW + RD * h + np.arange(RD)
    src[FW + RH * LANE:] = FW + RH * RD + np.arange(NH * ND)
    return src


def _gain_sources():
    src = np.full((RH * LANE,), -1, np.int64)
    for h in range(RH):
        src[LANE * h: LANE * h + RD] = RD * h + np.arange(RD)
    return src


def _rope_tables():
    pos = np.arange(L)
    prow, pcol = pos // GRID_W, pos % GRID_W
    half = RD // 4
    inv = 10000.0 ** (-np.arange(half, dtype=np.float64) / half)
    ar = prow[:, None] * inv[None, :]
    ac = pcol[:, None] * inv[None, :]
    cos = np.ones((L, LANE), np.float64)
    sin = np.zeros((L, LANE), np.float64)
    for off, sign in ((0, -1.0), (64, 1.0)):
        cos[:, off:off + 24] = np.cos(ar)
        cos[:, off + 24:off + 48] = np.cos(ac)
        sin[:, off:off + 24] = sign * np.sin(ar)
        sin[:, off + 24:off + 48] = sign * np.sin(ac)
    return cos.astype(np.float32), sin.astype(np.float32)


def _dft_tables():
    k1 = np.arange(DFT_N1)
    a = 2 * np.pi * ((k1[:, None] * k1[None, :]) % DFT_N1) / DFT_N1
    a1 = np.kron(np.concatenate([np.cos(a), -np.sin(a)], axis=0), np.eye(F1_R))
    k2 = np.arange(DFT_N2)
    l2 = np.arange(DFT_N2)
    kk = k1[:, None, None] + DFT_N1 * k2[None, :, None]
    ang = 2 * np.pi * ((kk * l2[None, None, :]) % L) / L
    ct, st = np.cos(ang) / np.sqrt(L), np.sin(ang) / np.sqrt(L)
    m3 = np.concatenate([np.concatenate([ct, st], axis=2), np.concatenate([st, -ct], axis=2)], axis=1)
    c = np.arange(FW)
    same = (c[:, None] // FGW) == (c[None, :] // FGW)
    ac = 2 * np.pi * (((c[:, None] % FGW) * (c[None, :] % FGW)) % FGW) / FGW
    e = np.concatenate([np.where(same, np.cos(ac), 0.0), -np.where(same, np.sin(ac), 0.0)], axis=0) / 8.0
    p = np.arange(LC)
    ap = 2 * np.pi * ((p[:, None] * p[None, :]) % LC) / LC
    return tuple(t.astype(np.float32) for t in (a1, m3, e, np.cos(ap) / 16.0, np.sin(ap) / 16.0))


NA_DR = 16


def _na_toeplitz_tables():
    src = np.full((LANE,), -1, np.int64)
    src[0:16] = np.arange(0, 16) + 15
    src[113:128] = np.arange(113, 128) - 113
    src[49:80] = np.arange(49, 80) - 49
    qc = np.arange(GRID_W)[:, None]
    kc = np.arange(LANE)[None, :] % GRID_W
    ws = np.clip(qc - 8, 0, GRID_W - 16)
    mask = np.where((kc >= ws) & (kc < ws + 16), 0.0, NEG).astype(np.float32)
    return src, mask


def _dot(a, b):
    return jnp.dot(a, b, preferred_element_type=F32)


def _dot_nt(a, b):
    return lax.dot_general(a, b, (((1,), (1,)), ((), ())), preferred_element_type=F32)


def _silu(x):
    return x * jax.nn.sigmoid(x)


def _rmsnorm(x, g):
    return x * lax.rsqrt(jnp.mean(x * x, axis=-1, keepdims=True) + EPS) * g


def _ada_kernel(c_ref, w_ref, b_ref, o_ref):
    s = _silu(c_ref[...]).astype(BF16)
    o_ref[0] = _dot(s, w_ref[0].astype(BF16)) + b_ref[0]


def _proj_kernel(*refs, rope):
    if rope:
        x_ref, mod_ref, g_ref, wf_ref, w_ref, cos_ref, sin_ref, f_ref, o_ref = refs
    else:
        x_ref, mod_ref, g_ref, wf_ref, w_ref, f_ref, o_ref = refs
    h = _rmsnorm(x_ref[...], g_ref[...]) * (1.0 + mod_ref[0, 1:2, :]) + mod_ref[0, 0:1, :]
    hb = h.astype(BF16)
    f_ref[...] = _dot(hb, wf_ref[...]).astype(BF16)
    k_scale = RD ** -0.5
    for c0, scale in ((C_RQ, None), (C_RK, k_scale)):
        t4 = _dot(hb, w_ref[:, c0:c0 + RH * LANE])
        for hh in range(RH):
            t = t4[:, LANE * hh:LANE * (hh + 1)]
            if rope:
                t = t * cos_ref[...] + pltpu.roll(t, 64, 1) * sin_ref[...]
            if scale is not None:
                t = t * scale
            o_ref[:, c0 + LANE * hh:c0 + LANE * (hh + 1)] = t.astype(BF16)
    for c0, c1 in ((C_RV, C_RG), (C_RG, C_NQ), (C_NQ, C_NK), (C_NK, C_NV), (C_NV, NP)):
        o_ref[:, c0:c1] = _dot(hb, w_ref[:, c0:c1]).astype(BF16)


def _four1_kernel(a_ref, x_ref, y_ref):
    x = x_ref[0].reshape(DFT_N1 * F1_R, FW)
    y = _dot(a_ref[...], x).astype(BF16)
    y_ref[0] = y.reshape(2 * DFT_N1, F1_R, FW)


def _four3_kernel(m_ref, y_ref, e_ref, wf_ref, o_ref):
    ew = _dot(e_ref[...], wf_ref[...]).astype(BF16)
    for b in range(B):
        res = []
        for kk in range(F3_K):
            y = jnp.concatenate([y_ref[b, 0, kk], y_ref[b, 1, kk]], axis=0)
            z = _dot(m_ref[kk], y).astype(BF16)
            res.append(_dot(jnp.concatenate([z[:DFT_N2], z[DFT_N2:]], axis=1), ew))
        o_ref[b] = jnp.stack(res, axis=1)


def _decay_matrix(n, lgf, lgb):
    ii = lax.broadcasted_iota(jnp.int32, (n, n), 0)
    jj = lax.broadcasted_iota(jnp.int32, (n, n), 1)
    diff = (ii - jj).astype(F32)
    fwd = jnp.where(diff >= 0, jnp.exp(jnp.maximum(diff, 0.0) * lgf), 0.0)
    bwd = jnp.where(diff <= 0, jnp.exp(jnp.maximum(-diff, 0.0) * lgb), 0.0)
    return fwd + bwd


def _ret_readout(o, gate, gain):
    ms = jnp.sum(o * o, axis=-1, keepdims=True) * (1.0 / RD)
    return (o * lax.rsqrt(ms + EPS) * gain * _silu(gate.astype(F32))).astype(BF16)


def _ret_kernel(lg_ref, q_ref, k_ref, v_ref, g_ref, s0_ref, gain_ref, o_ref, acc_ref, stf_ref, stb_ref):
    h = pl.program_id(1)
    lgf = lg_ref[0, h]
    lgb = lg_ref[1, h]
    c = RET_C
    n = L // c
    dmat = _decay_matrix(c, lgf, lgb)
    idx = lax.broadcasted_iota(jnp.int32, (c, LANE), 0).astype(F32)
    qdf = jnp.exp((idx + 1.0) * lgf)
    kdf = jnp.exp((c - 1.0 - idx) * lgf)
    qdb = jnp.exp((c - idx) * lgb)
    kdb = jnp.exp(idx * lgb)
    cdf = jnp.exp(jnp.full((1, LANE), float(c), F32) * lgf)
    cdb = jnp.exp(jnp.full((1, LANE), float(c), F32) * lgb)
    gain = gain_ref[...]

    dirs = ((stf_ref, qdf, kdf, cdf), (stb_ref, qdb, kdb, cdb))
    stf_ref[...] = s0_ref[0, 0, 0]
    stb_ref[...] = s0_ref[0, 0, 1]

    def visit(r0, direction, second):
        st_ref, qdec, kdec, cdec = dirs[direction]
        rows = pl.ds(pl.multiple_of(r0, c), c)
        q = q_ref[rows, :]
        k = k_ref[rows, :]
        v = v_ref[rows, :]
        cross = _dot(q, st_ref[...].astype(BF16)) * qdec
        if second:
            o_ref[rows, :] = _ret_readout(acc_ref[rows, :] + cross, g_ref[rows, :], gain)
        else:
            s = _dot_nt(q, k) * dmat
            acc_ref[rows, :] = _dot(s.astype(BF16), v) + cross
        kt = (k.astype(F32) * kdec).T.astype(BF16)
        st_ref[...] = st_ref[...] * cdec + _dot(kt, v)

    def body(second):
        def step(i, carry):
            visit(i * c, 0, second)
            visit((n - 1 - i) * c, 1, second)
            return carry
        return step

    lax.fori_loop(0, n // 2, body(False), 0, unroll=4)
    lax.fori_loop(n // 2, n, body(True), 0, unroll=4)


def _ctx_state_kernel(lg_ref, k_ref, v_ref, o_ref):
    h = pl.program_id(1)
    idx = lax.broadcasted_iota(jnp.int32, (LC, LANE), 0).astype(F32)
    k = k_ref[...].astype(F32)
    v = v_ref[...]
    wf = jnp.exp((LC - 1.0 - idx) * lg_ref[0, h])
    wb = jnp.exp(idx * lg_ref[1, h])
    o_ref[0, 0, 0] = _dot((k * wf).T.astype(BF16), v)
    o_ref[0, 0, 1] = _dot((k * wb).T.astype(BF16), v)


def _pair_attention(q, score_fn, value_fn):
    lane = lax.broadcasted_iota(jnp.int32, q.shape, 1)
    outs = []
    for hh in range(2):
        sel = (lane >= ND * hh) & (lane < ND * (hh + 1))
        qm = jnp.where(sel, q, jnp.zeros_like(q))
        scores = score_fn(qm, hh)
        m = functools.reduce(jnp.maximum, [jnp.max(s, axis=-1, keepdims=True) for s in scores])
        ps = [jnp.exp2(s - m) for s in scores]
        den = functools.reduce(jnp.add, [jnp.sum(p, axis=-1, keepdims=True) for p in ps])
        outs.append(value_fn([p.astype(BF16) for p in ps]) / den)
    return jnp.where(lane < ND, outs[0], outs[1])


def _ctx_mix_kernel(lg_ref, f_ref, p_ref, gain_ref, wf_ref, c_ref, s_ref, e_ref, fo_ref, ro_ref, no_ref):
    f = f_ref[...]
    pr = _dot(c_ref[...], f)
    qi = _dot(s_ref[...], f)
    fr = _dot(pr.astype(BF16), e_ref[:FW, :]) + _dot(qi.astype(BF16), e_ref[FW:, :])
    fo_ref[...] = _dot(fr.astype(BF16), wf_ref[...]).astype(BF16)
    for h in range(RH):
        sl = lambda c0: slice(c0 + LANE * h, c0 + LANE * (h + 1))
        dmat = _decay_matrix(LC, lg_ref[0, h], lg_ref[1, h])
        s = _dot_nt(p_ref[:, sl(C_RQ)], p_ref[:, sl(C_RK)]) * dmat
        o = _dot(s.astype(BF16), p_ref[:, sl(C_RV)])
        ro_ref[:, LANE * h:LANE * (h + 1)] = _ret_readout(o, p_ref[:, sl(C_RG)], gain_ref[:, LANE * h:LANE * (h + 1)])
    for pair in range(NPAIR):
        sl = lambda c0: slice(c0 + LANE * pair, c0 + LANE * (pair + 1))
        k = p_ref[:, sl(C_NK)]
        v = p_ref[:, sl(C_NV)]
        out = _pair_attention(p_ref[:, sl(C_NQ)], lambda qm, hh: [_dot_nt(qm, k)], lambda ps: _dot(ps[0], v))
        no_ref[:, LANE * pair:LANE * (pair + 1)] = out.astype(BF16)


def _na_key_start(r0):
    return min(max(r0 - 4, 0), GRID_H - NA_KR)


def _na_bias_kernel(base_ref, mask_ref, o_ref, tz_ref):
    qc = lax.broadcasted_iota(jnp.int32, (GRID_W, LANE), 0)
    for dr in range(NA_DR - 1):
        t = jnp.broadcast_to(base_ref[0, dr:dr + 1, :], (GRID_W, LANE))
        for bit in range(6):
            t = jnp.where((qc >> bit) & 1 == 1, pltpu.roll(t, 1 << bit, 1), t)
        tz_ref[dr] = t * LOG2E + mask_ref[...]
    tz_ref[NA_DR - 1] = jnp.full((GRID_W, LANE), NEG, F32)
    low_half = lax.broadcasted_iota(jnp.int32, (GRID_W, LANE), 1) < GRID_W
    for var, r0 in enumerate((0, 2 * NA_R, GRID_H - NA_R)):
        ks = _na_key_start(r0)
        for rl in range(NA_R):
            r = r0 + rl
            rs = min(max(r - 4, 0), GRID_H - 8)
            slot = [kr - r + 7 if rs <= kr < rs + 8 else NA_DR - 1 for kr in range(ks, ks + NA_KR)]
            for m in range(NA_KR // 2):
                o_ref[0, var, GRID_W * rl:GRID_W * (rl + 1), LANE * m:LANE * (m + 1)] = jnp.where(
                    low_half, tz_ref[slot[2 * m]], tz_ref[slot[2 * m + 1]])


def _natten_kernel(q_ref, k_ref, v_ref, kc_ref, vc_ref, bias_ref, o_ref):
    nq = NA_R * GRID_W
    nk = NA_KR * GRID_W
    kc = kc_ref[...]
    vc = vc_ref[...]

    def block(i, carry):
        ks = jnp.clip(NA_R * i - 4, 0, GRID_H - NA_KR)
        var = jnp.where(i == 0, 0, jnp.where(i == NA_NB - 1, 2, 1))
        krows = pl.ds(pl.multiple_of(ks * GRID_W, nq), nk)
        qrows = pl.ds(pl.multiple_of(i * nq, nq), nq)
        kw = k_ref[krows, :]
        vw = v_ref[krows, :]

        def scores(qm, hh):
            return [_dot_nt(qm, kw) + bias_ref[hh, var], _dot_nt(qm, kc)]

        def values(ps):
            return _dot(ps[0], vw) + _dot(ps[1], vc)

        o_ref[qrows, :] = _pair_attention(q_ref[qrows, :], scores, values).astype(BF16)
        return carry

    lax.fori_loop(0, NA_NB, block, 0)


def _ffn_kernel(*refs, final):
    if final:
        x_ref, f_ref, r_ref, n_ref, mod_ref, g_ref, wo_ref, w1_ref, w3_ref, w2_ref, gf_ref, o_ref, acc_ref = refs
    else:
        x_ref, f_ref, r_ref, n_ref, mod_ref, g_ref, wo_ref, w1_ref, w3_ref, w2_ref, o_ref, acc_ref = refs
    r0, n0 = FW, FW + RH * LANE
    y = (_dot(f_ref[...].astype(BF16), wo_ref[:r0, :]) + _dot(r_ref[...], wo_ref[r0:n0, :])
         + _dot(n_ref[...], wo_ref[n0:, :]))
    x1 = x_ref[...] + mod_ref[0, 2:3, :] * y
    hb = (_rmsnorm(x1, g_ref[...]) * (1.0 + mod_ref[0, 4:5, :]) + mod_ref[0, 3:4, :]).astype(BF16)
    fc = 256
    for c in range(DFF // fc):
        a = _dot(hb, w1_ref[:, c * fc:(c + 1) * fc])
        b = _dot(hb, w3_ref[:, c * fc:(c + 1) * fc])
        t = _dot((_silu(a) * b).astype(BF16), w2_ref[c * fc:(c + 1) * fc, :])
        if c == 0:
            acc_ref[...] = t
        else:
            acc_ref[...] += t
    x2 = x1 + mod_ref[0, 5:6, :] * acc_ref[...]
    if final:
        x2 = _rmsnorm(x2, gf_ref[...])
    o_ref[...] = x2


def _params(sem):
    return pltpu.CompilerParams(dimension_semantics=sem, vmem_limit_bytes=VMEM_LIMIT)


def _const_spec(shape):
    nd = len(shape)
    return pl.BlockSpec(shape, lambda *_: (0,) * nd, pipeline_mode=pl.Buffered(1))


def _smem_spec():
    return pl.BlockSpec(memory_space=pltpu.SMEM)


def _ada(cvec, w_ada, b_ada):
    tn = 1024
    return pl.pallas_call(
        _ada_kernel,
        grid=(DEPTH, 6 * D // tn),
        in_specs=[pl.BlockSpec((8, D), lambda i, j: (0, 0)),
                  pl.BlockSpec((1, D, tn), lambda i, j: (i, 0, j)),
                  pl.BlockSpec((1, 1, tn), lambda i, j: (i, 0, j))],
        out_specs=pl.BlockSpec((1, 8, tn), lambda i, j: (i, 0, j)),
        out_shape=jax.ShapeDtypeStruct((DEPTH, 8, 6 * D), F32),
        compiler_params=_params(("arbitrary", "arbitrary")),
        name="ada",
    )(cvec, w_ada, b_ada.reshape(DEPTH, 1, 6 * D))


def _proj(x2d, mods, g, wf, w, rope_tabs, *, ctx):
    rows = x2d.shape[0]
    nt = rows // TM
    per_batch = L // TM
    mod_idx = (lambda i: (2, 0, 0)) if ctx else (lambda i: (i // per_batch, 0, 0))
    in_specs = [pl.BlockSpec((TM, D), lambda i: (i, 0)),
                pl.BlockSpec((1, 6, D), mod_idx),
                _const_spec((1, D)),
                _const_spec((D, FW)),
                _const_spec((D, NP))]
    args = [x2d, mods, g.reshape(1, D), wf, w]
    if not ctx:
        in_specs += [pl.BlockSpec((TM, LANE), lambda i: (i % per_batch, 0))] * 2
        args += list(rope_tabs)
    return pl.pallas_call(
        functools.partial(_proj_kernel, rope=not ctx),
        grid=(nt,),
        in_specs=in_specs,
        out_specs=[pl.BlockSpec((TM, FW), lambda i: (i, 0)), pl.BlockSpec((TM, NP), lambda i: (i, 0))],
        out_shape=[jax.ShapeDtypeStruct((rows, FW), BF16), jax.ShapeDtypeStruct((rows, NP), BF16)],
        compiler_params=_params(("arbitrary",)),
        name="proj_ctx" if ctx else "proj",
    )(*args)


def _fourier(f2d, wf, a1, m3, e):
    y = pl.pallas_call(
        _four1_kernel,
        grid=(B, DFT_N2 // F1_R),
        in_specs=[_const_spec((2 * DFT_N1 * F1_R, DFT_N1 * F1_R)),
                  pl.BlockSpec((1, DFT_N1, F1_R, FW), lambda b, j: (b, 0, j, 0))],
        out_specs=pl.BlockSpec((1, 2 * DFT_N1, F1_R, FW), lambda b, j: (b, 0, j, 0)),
        out_shape=jax.ShapeDtypeStruct((B, 2 * DFT_N1, DFT_N2, FW), BF16),
        compiler_params=_params(("arbitrary", "arbitrary")),
        name="four1",
    )(a1, f2d.reshape(B, DFT_N1, DFT_N2, FW))
    out = pl.pallas_call(
        _four3_kernel,
        grid=(DFT_N1 // F3_K,),
        in_specs=[pl.BlockSpec((F3_K, 2 * DFT_N2, 2 * DFT_N2), lambda k: (k, 0, 0)),
                  pl.BlockSpec((B, 2, F3_K, DFT_N2, FW), lambda k: (0, 0, k, 0, 0)),
                  _const_spec((2 * FW, FW)),
                  _const_spec((FW, FW))],
        out_specs=pl.BlockSpec((B, DFT_N2, F3_K, FW), lambda k: (0, 0, k, 0)),
        out_shape=jax.ShapeDtypeStruct((B, DFT_N2, DFT_N1, FW), F32),
        compiler_params=_params(("arbitrary",)),
        name="four3",
    )(m3, y.reshape(B, 2, DFT_N1, DFT_N2, FW), e, wf)
    return out.reshape(B * L, FW)


def _retention(p, s0, lg, gain):
    col = lambda c0: (lambda b, h: (b, c0 // LANE + h))
    return pl.pallas_call(
        _ret_kernel,
        grid=(B, RH),
        in_specs=[_smem_spec(),
                  pl.BlockSpec((L, LANE), col(C_RQ)),
                  pl.BlockSpec((L, LANE), col(C_RK)),
                  pl.BlockSpec((L, LANE), col(C_RV)),
                  pl.BlockSpec((L, LANE), col(C_RG)),
                  pl.BlockSpec((1, 1, 2, LANE, LANE), lambda b, h: (b, h, 0, 0, 0)),
                  pl.BlockSpec((1, LANE), lambda b, h: (0, h))],
        out_specs=pl.BlockSpec((L, LANE), lambda b, h: (b, h)),
        out_shape=jax.ShapeDtypeStruct((B * L, RH * LANE), BF16),
        scratch_shapes=[pltpu.VMEM((L, LANE), F32), pltpu.VMEM((LANE, LANE), F32), pltpu.VMEM((LANE, LANE), F32)],
        compiler_params=_params(("arbitrary", "arbitrary")),
        name="ret",
    )(lg, p, p, p, p, s0, gain)


def _ctx_state(pc, lg):
    col = lambda c0: (lambda b, h: (b, c0 // LANE + h))
    return pl.pallas_call(
        _ctx_state_kernel,
        grid=(B, RH),
        in_specs=[_smem_spec(),
                  pl.BlockSpec((LC, LANE), col(C_RK)),
                  pl.BlockSpec((LC, LANE), col(C_RV))],
        out_specs=pl.BlockSpec((1, 1, 2, LANE, LANE), lambda b, h: (b, h, 0, 0, 0)),
        out_shape=jax.ShapeDtypeStruct((B, RH, 2, LANE, LANE), F32),
        compiler_params=_params(("arbitrary", "arbitrary")),
        name="ctx_state",
    )(lg, pc, pc)


def _ctx_mix(fc, pc, lg, gain, wf, c256, s256, e):
    return pl.pallas_call(
        _ctx_mix_kernel,
        grid=(B,),
        in_specs=[_smem_spec(),
                  pl.BlockSpec((LC, FW), lambda b: (b, 0)),
                  pl.BlockSpec((LC, NP), lambda b: (b, 0)),
                  _const_spec((1, RH * LANE)),
                  _const_spec((FW, FW)),
                  _const_spec((LC, LC)),
                  _const_spec((LC, LC)),
                  _const_spec((2 * FW, FW))],
        out_specs=[pl.BlockSpec((LC, FW), lambda b: (b, 0)),
                   pl.BlockSpec((LC, RH * LANE), lambda b: (b, 0)),
                   pl.BlockSpec((LC, NH * ND), lambda b: (b, 0))],
        out_shape=[jax.ShapeDtypeStruct((B * LC, FW), BF16),
                   jax.ShapeDtypeStruct((B * LC, RH * LANE), BF16),
                   jax.ShapeDtypeStruct((B * LC, NH * ND), BF16)],
        compiler_params=_params(("arbitrary",)),
        name="ctx_mix",
    )(lg, fc, pc, gain, wf, c256, s256, e)


def _na_bias(base, mask):
    nq, nk = NA_R * GRID_W, NA_KR * GRID_W
    return pl.pallas_call(
        _na_bias_kernel,
        grid=(NH,),
        in_specs=[pl.BlockSpec((1, NA_DR, LANE), lambda h: (h, 0, 0)),
                  _const_spec((GRID_W, LANE))],
        out_specs=pl.BlockSpec((1, 3, nq, nk), lambda h: (h, 0, 0, 0)),
        out_shape=jax.ShapeDtypeStruct((NH, 3, nq, nk), F32),
        scratch_shapes=[pltpu.VMEM((NA_DR, GRID_W, LANE), F32)],
        compiler_params=_params(("arbitrary",)),
        name="na_bias",
    )(base, mask)


def _natten(p, pc, bias):
    nq, nk = NA_R * GRID_W, NA_KR * GRID_W
    return pl.pallas_call(
        _natten_kernel,
        grid=(B, NPAIR),
        in_specs=[pl.BlockSpec((L, LANE), lambda b, pr: (b, C_NQ // LANE + pr)),
                  pl.BlockSpec((L, LANE), lambda b, pr: (b, C_NK // LANE + pr)),
                  pl.BlockSpec((L, LANE), lambda b, pr: (b, C_NV // LANE + pr)),
                  pl.BlockSpec((LC, LANE), lambda b, pr: (b, C_NK // LANE + pr)),
                  pl.BlockSpec((LC, LANE), lambda b, pr: (b, C_NV // LANE + pr)),
                  pl.BlockSpec((2, 3, nq, nk), lambda b, pr: (pr, 0, 0, 0))],
        out_specs=pl.BlockSpec((L, LANE), lambda b, pr: (b, pr)),
        out_shape=jax.ShapeDtypeStruct((B * L, NH * ND), BF16),
        compiler_params=_params(("arbitrary", "arbitrary")),
        name="natten",
    )(p, p, p, pc, pc, bias)


def _ffn(x2d, four, ret, na, mods, g, wo, w1, w3, w2, g_final, *, ctx):
    rows = x2d.shape[0]
    per_batch = L // TM
    mod_idx = (lambda i: (2, 0, 0)) if ctx else (lambda i: (i // per_batch, 0, 0))
    final = g_final is not None
    row = lambda w: pl.BlockSpec((TM, w), lambda i: (i, 0))
    in_specs = [row(D), row(FW), row(RH * LANE), row(NH * ND),
                pl.BlockSpec((1, 6, D), mod_idx),
                _const_spec((1, D)),
                _const_spec((YW, D)), _const_spec((D, DFF)), _const_spec((D, DFF)), _const_spec((DFF, D))]
    args = [x2d, four, ret, na, mods, g.reshape(1, D), wo, w1, w3, w2]
    if final:
        in_specs.append(_const_spec((1, D)))
        args.append(g_final.reshape(1, D))
    return pl.pallas_call(
        functools.partial(_ffn_kernel, final=final),
        grid=(rows // TM,),
        in_specs=in_specs,
        out_specs=row(D),
        out_shape=jax.ShapeDtypeStruct((rows, D), F32),
        scratch_shapes=[pltpu.VMEM((TM, D), F32)],
        compiler_params=_params(("arbitrary",)),
        name="ffn_ctx" if ctx else ("ffn_final" if final else "ffn"),
    )(*args)


def _take_padded(a, src, axis):
    pieces = []
    i, n = 0, len(src)
    while i < n:
        j = i + 1
        if src[i] < 0:
            while j < n and src[j] < 0:
                j += 1
            shape = list(a.shape)
            shape[axis] = j - i
            pieces.append(jnp.zeros(shape, a.dtype))
        else:
            while j < n and src[j] == src[j - 1] + 1:
                j += 1
            pieces.append(lax.slice_in_dim(a, int(src[i]), int(src[j - 1]) + 1, axis=axis))
        i = j
    return jnp.concatenate(pieces, axis=axis)


def kernel(x, c, ctx, c_ctx, w_ada, b_ada, g_mix, w_in, ret_decay_logit, ret_norm_g, w_four,
           na_rpb, w_out, g_ffn, w1, w3, w2, g_final):
    src_p = _proj_sources()
    src_y = _ycat_sources()
    src_g = _gain_sources()
    rope_tabs = tuple(jnp.asarray(t) for t in _rope_tables())
    a1, m3, e, c256, s256 = (jnp.asarray(t).astype(BF16) for t in _dft_tables())
    tz_src, tz_mask = _na_toeplitz_tables()
    col_scale = np.ones((1, 1, w_in.shape[-1]), np.float32)
    col_scale[..., 1792:1792 + NH * ND] = LOG2E * ND ** -0.5
    w_in_b = (w_in * jnp.asarray(col_scale)).astype(BF16)
    w_out_b = w_out.astype(BF16)

    cvec = jnp.zeros((8, D), F32).at[0:B].set(c).at[B].set(c_ctx)
    mods_all = _ada(cvec, w_ada, b_ada).reshape(DEPTH, 8, 6, D)

    xl = x.reshape(B * L, D)
    xc = ctx.reshape(B * LC, D)
    for i in range(DEPTH):
        last = i == DEPTH - 1
        mods = mods_all[i]
        wf_in = w_in_b[i][:, :FW]
        w_in_p = _take_padded(w_in_b[i], src_p, 1)
        wo_p = _take_padded(w_out_b[i], src_y, 0)
        gain = _take_padded(ret_norm_g[i], src_g, 0).reshape(1, RH * LANE)
        wfour = w_four[i].astype(BF16)
        lg = jax.nn.log_sigmoid(ret_decay_logit[i].astype(F32))
        tz_base = jnp.pad(_take_padded(na_rpb[i].astype(F32), tz_src, 2), ((0, 0), (0, 1), (0, 0)))
        na_bias = _na_bias(tz_base, jnp.asarray(tz_mask))
        w1b, w3b, w2b = w1[i].astype(BF16), w3[i].astype(BF16), w2[i].astype(BF16)

        f_l, p_l = _proj(xl, mods, g_mix[i], wf_in, w_in_p, rope_tabs, ctx=False)
        f_c, p_c = _proj(xc, mods, g_mix[i], wf_in, w_in_p, None, ctx=True)
        s0 = _ctx_state(p_c, lg)
        four_l = _fourier(f_l, wfour, a1, m3, e)
        ret_l = _retention(p_l, s0, lg, gain)
        na_l = _natten(p_l, p_c, na_bias)
        if not last:
            four_c, ret_c, na_c = _ctx_mix(f_c, p_c, lg, gain, wfour, c256, s256, e)
            xc = _ffn(xc, four_c, ret_c, na_c, mods, g_ffn[i], wo_p, w1b, w3b, w2b, None, ctx=True)
        xl = _ffn(xl, four_l, ret_l, na_l, mods, g_ffn[i], wo_p, w1b, w3b, w2b,
                  g_final if last else None, ctx=False)
    return xl.reshape(B, L, D)
```

```python
import functools

import numpy as np
import jax
import jax.numpy as jnp
from jax import lax
from jax.experimental import pallas as pl
from jax.experimental.pallas import tpu as pltpu

F32 = jnp.float32
BF16 = jnp.bfloat16

D = 1024
B = 2
L = 8192
LC = 256
DEPTH = 2
GRID_W = 64
GRID_H = L // GRID_W
FW = 256
FGW = 64
RH = 4
RD = 96
NH = 6
ND = 64
NPAIR = NH // 2
DFF = 2816
EPS = 1e-6
NEG = -1e30
LOG2E = float(np.log2(np.e))
LANE = 128

C_RQ, C_RK, C_RV, C_RG = 0, 512, 1024, 1536
C_NQ, C_NK, C_NV = 2048, 2432, 2816
NP = 3200
W_RQ, W_RK, W_RV, W_RG = 256, 640, 1024, 1408
W_NQ, W_NK, W_NV = 1792, 2176, 2560
W_IN = 2944

TM = 512
RET_C = 256
NA_R = 4
NA_KR = NA_R + 8
NA_NB = GRID_H // NA_R
DFT_N1 = 64
DFT_N2 = 128
F1_R = 16
F3_K = 8
VMEM_LIMIT = 56 * 1024 * 1024


def _qk_lane_dims():
    m = np.full((LANE,), -1, np.int64)
    m[0:24] = np.arange(0, 24)
    m[24:48] = np.arange(48, 72)
    m[64:88] = np.arange(24, 48)
    m[88:112] = np.arange(72, 96)
    return m


def _v_lane_dims():
    m = np.full((LANE,), -1, np.int64)
    m[:RD] = np.arange(RD)
    return m


def _head_selectors():
    sels = []
    for lanes in (_qk_lane_dims(), _v_lane_dims()):
        sel = np.zeros((RH * RD, RH * LANE), np.float32)
        ok = np.nonzero(lanes >= 0)[0]
        for h in range(RH):
            sel[RD * h + lanes[ok], LANE * h + ok] = 1.0
        sels.append(sel)
    return sels


def _gain_sources():
    src = np.full((RH * LANE,), -1, np.int64)
    for h in range(RH):
        src[LANE * h: LANE * h + RD] = RD * h + np.arange(RD)
    return src


def _rope_tables():
    pos = np.arange(L)
    prow, pcol = pos // GRID_W, pos % GRID_W
    half = RD // 4
    inv = 10000.0 ** (-np.arange(half, dtype=np.float64) / half)
    ar = prow[:, None] * inv[None, :]
    ac = pcol[:, None] * inv[None, :]
    cos = np.ones((L, LANE), np.float64)
    sin = np.zeros((L, LANE), np.float64)
    for off, sign in ((0, -1.0), (64, 1.0)):
        cos[:, off:off + 24] = np.cos(ar)
        cos[:, off + 24:off + 48] = np.cos(ac)
        sin[:, off:off + 24] = sign * np.sin(ar)
        sin[:, off + 24:off + 48] = sign * np.sin(ac)
    return cos.astype(np.float32), sin.astype(np.float32)


def _dft_tables():
    k1 = np.arange(DFT_N1)
    a = 2 * np.pi * ((k1[:, None] * k1[None, :]) % DFT_N1) / DFT_N1
    a1 = np.kron(np.concatenate([np.cos(a), -np.sin(a)], axis=0), np.eye(F1_R))
    k2 = np.arange(DFT_N2)
    l2 = np.arange(DFT_N2)
    kk = k1[:, None, None] + DFT_N1 * k2[None, :, None]
    ang = 2 * np.pi * ((kk * l2[None, None, :]) % L) / L
    ct, st = np.cos(ang) / np.sqrt(L), np.sin(ang) / np.sqrt(L)
    m3 = np.concatenate([np.concatenate([ct, st], axis=2), np.concatenate([st, -ct], axis=2)], axis=1)
    c = np.arange(FW)
    same = (c[:, None] // FGW) == (c[None, :] // FGW)
    ac = 2 * np.pi * (((c[:, None] % FGW) * (c[None, :] % FGW)) % FGW) / FGW
    e = np.concatenate([np.where(same, np.cos(ac), 0.0), -np.where(same, np.sin(ac), 0.0)], axis=0) / 8.0
    p = np.arange(LC)
    ap = 2 * np.pi * ((p[:, None] * p[None, :]) % LC) / LC
    return tuple(t.astype(np.float32) for t in (a1, m3, e, np.cos(ap) / 16.0, np.sin(ap) / 16.0))


NA_DR = 16


def _na_toeplitz_tables():
    src = np.full((LANE,), -1, np.int64)
    src[0:16] = np.arange(0, 16) + 15
    src[113:128] = np.arange(113, 128) - 113
    src[49:80] = np.arange(49, 80) - 49
    qc = np.arange(GRID_W)[:, None]
    kc = np.arange(LANE)[None, :] % GRID_W
    ws = np.clip(qc - 8, 0, GRID_W - 16)
    mask = np.where((kc >= ws) & (kc < ws + 16), 0.0, NEG).astype(np.float32)
    return src, mask


def _dot(a, b):
    return jnp.dot(a, b, preferred_element_type=F32)


def _dot_nt(a, b):
    return lax.dot_general(a, b, (((1,), (1,)), ((), ())), preferred_element_type=F32)


def _silu(x):
    return x * jax.nn.sigmoid(x)


def _rmsnorm(x, g):
    return x * lax.rsqrt(jnp.mean(x * x, axis=-1, keepdims=True) + EPS) * g


def _ada_kernel(c_ref, w_ref, b_ref, o_ref):
    s = _silu(c_ref[...]).astype(BF16)
    o_ref[0] = _dot(s, w_ref[0].astype(BF16)) + b_ref[0]


def _proj_kernel(*refs, rope, mod_row):
    if rope:
        (x0_ref, xn_ref, mods_ref, g_ref, w_ref, pqk_ref, pv_ref, cos_ref, sin_ref,
         f_ref, o_ref, wret_ref, ha_ref, hb_ref) = refs
    else:
        x0_ref, xn_ref, mods_ref, g_ref, w_ref, pqk_ref, pv_ref, f_ref, o_ref, wret_ref, ha_ref, hb_ref = refs
    s = pl.program_id(0)
    last = pl.num_programs(0) - 1

    def normalise(x_ref, tile, dst_ref):
        mod = mods_ref[mod_row(tile)]
        dst_ref[...] = (_rmsnorm(x_ref[...], g_ref[...]) * (1.0 + mod[1:2, :]) + mod[0:1, :]).astype(BF16)

    @pl.when(s == 0)
    def _():
        for slot, (src0, sel_ref) in enumerate(((W_RQ, pqk_ref), (W_RK, pqk_ref), (W_RV, pv_ref), (W_RG, pv_ref))):
            wret_ref[:, RH * LANE * slot:RH * LANE * (slot + 1)] = _dot(
                w_ref[:, src0:src0 + RH * RD], sel_ref[...]).astype(BF16)
        normalise(x0_ref, 0, ha_ref)

    def step(cur_ref, nxt_ref):
        hb = cur_ref[...]
        f_ref[...] = _dot(hb, w_ref[:, :FW]).astype(BF16)
        k_scale = RD ** -0.5
        for c0, scale in ((C_RQ, None), (C_RK, k_scale)):
            t4 = _dot(hb, wret_ref[:, c0:c0 + RH * LANE])
            for hh in range(RH):
                t = t4[:, LANE * hh:LANE * (hh + 1)]
                if rope:
                    t = t * cos_ref[...] + pltpu.roll(t, 64, 1) * sin_ref[...]
                if scale is not None:
                    t = t * scale
                o_ref[:, c0 + LANE * hh:c0 + LANE * (hh + 1)] = t.astype(BF16)
        for c0 in (C_RV, C_RG):
            o_ref[:, c0:c0 + RH * LANE] = _dot(hb, wret_ref[:, c0:c0 + RH * LANE]).astype(BF16)
        o_ref[:, C_NQ:] = _dot(hb, w_ref[:, W_NQ:]).astype(BF16)
        normalise(xn_ref, jnp.minimum(s + 1, last), nxt_ref)

    @pl.when(s % 2 == 0)
    def _():
        step(ha_ref, hb_ref)

    @pl.when(s % 2 == 1)
    def _():
        step(hb_ref, ha_ref)


def _four1_kernel(a_ref, x_ref, y_ref):
    x = x_ref[0].reshape(DFT_N1 * F1_R, FW)
    y = _dot(a_ref[...], x).astype(BF16)
    y_ref[0] = y.reshape(2 * DFT_N1, F1_R, FW)


def _four3_kernel(m_ref, y_ref, e_ref, wf_ref, o_ref):
    ew = _dot(e_ref[...], wf_ref[...]).astype(BF16)
    for b in range(B):
        res = []
        for kk in range(F3_K):
            y = jnp.concatenate([y_ref[b, 0, kk], y_ref[b, 1, kk]], axis=0)
            z = _dot(m_ref[kk], y).astype(BF16)
            res.append(_dot(jnp.concatenate([z[:DFT_N2], z[DFT_N2:]], axis=1), ew))
        o_ref[b] = jnp.stack(res, axis=1)


def _decay_matrix(n, lgf, lgb):
    ii = lax.broadcasted_iota(jnp.int32, (n, n), 0)
    jj = lax.broadcasted_iota(jnp.int32, (n, n), 1)
    diff = (ii - jj).astype(F32)
    fwd = jnp.where(diff >= 0, jnp.exp(jnp.maximum(diff, 0.0) * lgf), 0.0)
    bwd = jnp.where(diff <= 0, jnp.exp(jnp.maximum(-diff, 0.0) * lgb), 0.0)
    return fwd + bwd


def _ret_readout(o, gate, gain):
    ms = jnp.sum(o * o, axis=-1, keepdims=True) * (1.0 / RD)
    return (o * lax.rsqrt(ms + EPS) * gain * _silu(gate.astype(F32))).astype(BF16)


def _ret_kernel(lg_ref, q_ref, k_ref, v_ref, g_ref, s0_ref, gain_ref, o_ref, acc_ref, stf_ref, stb_ref):
    h = pl.program_id(1)
    lgf = lg_ref[0, h]
    lgb = lg_ref[1, h]
    c = RET_C
    n = L // c
    dmat = _decay_matrix(c, lgf, lgb)
    idx = lax.broadcasted_iota(jnp.int32, (c, LANE), 0).astype(F32)
    qdf = jnp.exp((idx + 1.0) * lgf)
    kdf = jnp.exp((c - 1.0 - idx) * lgf)
    qdb = jnp.exp((c - idx) * lgb)
    kdb = jnp.exp(idx * lgb)
    cdf = jnp.exp(jnp.full((1, LANE), float(c), F32) * lgf)
    cdb = jnp.exp(jnp.full((1, LANE), float(c), F32) * lgb)
    gain = gain_ref[...]

    dirs = ((stf_ref, qdf, kdf, cdf), (stb_ref, qdb, kdb, cdb))
    stf_ref[...] = s0_ref[0, 0, 0]
    stb_ref[...] = s0_ref[0, 0, 1]

    def visit(r0, direction, second):
        st_ref, qdec, kdec, cdec = dirs[direction]
        rows = pl.ds(pl.multiple_of(r0, c), c)
        q = q_ref[rows, :]
        k = k_ref[rows, :]
        v = v_ref[rows, :]
        cross = _dot(q, st_ref[...].astype(BF16)) * qdec
        if second:
            o_ref[rows, :] = _ret_readout(acc_ref[rows, :] + cross, g_ref[rows, :], gain)
        else:
            s = _dot_nt(q, k) * dmat
            acc_ref[rows, :] = _dot(s.astype(BF16), v) + cross
        kt = (k.astype(F32) * kdec).T.astype(BF16)
        st_ref[...] = st_ref[...] * cdec + _dot(kt, v)

    def body(second):
        def step(i, carry):
            visit(i * c, 0, second)
            visit((n - 1 - i) * c, 1, second)
            return carry
        return step

    lax.fori_loop(0, n // 2, body(False), 0, unroll=4)
    lax.fori_loop(n // 2, n, body(True), 0, unroll=4)


def _ctx_state_kernel(lg_ref, k_ref, v_ref, o_ref):
    h = pl.program_id(1)
    idx = lax.broadcasted_iota(jnp.int32, (LC, LANE), 0).astype(F32)
    k = k_ref[...].astype(F32)
    v = v_ref[...]
    wf = jnp.exp((LC - 1.0 - idx) * lg_ref[0, h])
    wb = jnp.exp(idx * lg_ref[1, h])
    o_ref[0, 0, 0] = _dot((k * wf).T.astype(BF16), v)
    o_ref[0, 0, 1] = _dot((k * wb).T.astype(BF16), v)


def _pair_attention(q, score_fn, value_fn):
    lane = lax.broadcasted_iota(jnp.int32, q.shape, 1)
    outs = []
    for hh in range(2):
        sel = (lane >= ND * hh) & (lane < ND * (hh + 1))
        qm = jnp.where(sel, q, jnp.zeros_like(q))
        scores = score_fn(qm, hh)
        m = functools.reduce(jnp.maximum, [jnp.max(s, axis=-1, keepdims=True) for s in scores])
        ps = [jnp.exp2(s - m) for s in scores]
        den = functools.reduce(jnp.add, [jnp.sum(p, axis=-1, keepdims=True) for p in ps])
        outs.append(value_fn([p.astype(BF16) for p in ps]) / den)
    return jnp.where(lane < ND, outs[0], outs[1])


def _ctx_mix_kernel(lg_ref, f_ref, p_ref, gain_ref, wf_ref, c_ref, s_ref, e_ref, fo_ref, ro_ref, no_ref):
    f = f_ref[...]
    pr = _dot(c_ref[...], f)
    qi = _dot(s_ref[...], f)
    fr = _dot(pr.astype(BF16), e_ref[:FW, :]) + _dot(qi.astype(BF16), e_ref[FW:, :])
    fo_ref[...] = _dot(fr.astype(BF16), wf_ref[...]).astype(BF16)
    for h in range(RH):
        sl = lambda c0: slice(c0 + LANE * h, c0 + LANE * (h + 1))
        dmat = _decay_matrix(LC, lg_ref[0, h], lg_ref[1, h])
        s = _dot_nt(p_ref[:, sl(C_RQ)], p_ref[:, sl(C_RK)]) * dmat
        o = _dot(s.astype(BF16), p_ref[:, sl(C_RV)])
        ro_ref[:, LANE * h:LANE * (h + 1)] = _ret_readout(o, p_ref[:, sl(C_RG)], gain_ref[:, LANE * h:LANE * (h + 1)])
    for pair in range(NPAIR):
        sl = lambda c0: slice(c0 + LANE * pair, c0 + LANE * (pair + 1))
        k = p_ref[:, sl(C_NK)]
        v = p_ref[:, sl(C_NV)]
        out = _pair_attention(p_ref[:, sl(C_NQ)], lambda qm, hh: [_dot_nt(qm, k)], lambda ps: _dot(ps[0], v))
        no_ref[:, LANE * pair:LANE * (pair + 1)] = out.astype(BF16)


def _na_key_start(r0):
    return min(max(r0 - 4, 0), GRID_H - NA_KR)


def _na_bias_kernel(base_ref, mask_ref, o_ref, tz_ref):
    qc = lax.broadcasted_iota(jnp.int32, (GRID_W, LANE), 0)
    for dr in range(NA_DR - 1):
        t = jnp.broadcast_to(base_ref[0, dr:dr + 1, :], (GRID_W, LANE))
        for bit in range(6):
            t = jnp.where((qc >> bit) & 1 == 1, pltpu.roll(t, 1 << bit, 1), t)
        tz_ref[dr] = t * LOG2E + mask_ref[...]
    tz_ref[NA_DR - 1] = jnp.full((GRID_W, LANE), NEG, F32)
    low_half = lax.broadcasted_iota(jnp.int32, (GRID_W, LANE), 1) < GRID_W
    for var, r0 in enumerate((0, 2 * NA_R, GRID_H - NA_R)):
        ks = _na_key_start(r0)
        for rl in range(NA_R):
            r = r0 + rl
            rs = min(max(r - 4, 0), GRID_H - 8)
            slot = [kr - r + 7 if rs <= kr < rs + 8 else NA_DR - 1 for kr in range(ks, ks + NA_KR)]
            for m in range(NA_KR // 2):
                o_ref[0, var, GRID_W * rl:GRID_W * (rl + 1), LANE * m:LANE * (m + 1)] = jnp.where(
                    low_half, tz_ref[slot[2 * m]], tz_ref[slot[2 * m + 1]])


def _natten_kernel(q_ref, k_ref, v_ref, kc_ref, vc_ref, bias_ref, o_ref):
    nq = NA_R * GRID_W
    nk = NA_KR * GRID_W
    kc = kc_ref[...]
    vc = vc_ref[...]

    def block(i, carry):
        ks = jnp.clip(NA_R * i - 4, 0, GRID_H - NA_KR)
        var = jnp.where(i == 0, 0, jnp.where(i == NA_NB - 1, 2, 1))
        krows = pl.ds(pl.multiple_of(ks * GRID_W, nq), nk)
        qrows = pl.ds(pl.multiple_of(i * nq, nq), nq)
        kw = k_ref[krows, :]
        vw = v_ref[krows, :]

        def scores(qm, hh):
            return [_dot_nt(qm, kw) + bias_ref[hh, var], _dot_nt(qm, kc)]

        def values(ps):
            return _dot(ps[0], vw) + _dot(ps[1], vc)

        o_ref[qrows, :] = _pair_attention(q_ref[qrows, :], scores, values).astype(BF16)
        return carry

    lax.fori_loop(0, NA_NB, block, 0)


def _ffn_kernel(*refs, final, mod_row):
    (x0_ref, f0_ref, r0_ref, n0_ref, xn_ref, fn_ref, rn_ref, nn_ref, mods_ref, g_ref,
     wo_ref, sel_ref, w1_ref, w3_ref, w2_ref) = refs[:15]
    gf_ref = refs[15] if final else None
    o_ref, acc_ref, wor_ref, xa_ref, ha_ref, xb_ref, hb_ref = refs[-7:]
    s = pl.program_id(0)
    last = pl.num_programs(0) - 1

    def stage_a(x_ref, f_ref, r_ref, n_ref, tile, x1_ref, h_ref):
        mod = mods_ref[mod_row(tile)]
        y = (_dot(f_ref[...].astype(BF16), wo_ref[:FW, :]) + _dot(r_ref[...], wor_ref[...])
             + _dot(n_ref[...], wo_ref[FW + RH * RD:, :]))
        x1 = x_ref[...] + mod[2:3, :] * y
        x1_ref[...] = x1
        h_ref[...] = (_rmsnorm(x1, g_ref[...]) * (1.0 + mod[4:5, :]) + mod[3:4, :]).astype(BF16)

    @pl.when(s == 0)
    def _():
        wor_ref[...] = _dot(sel_ref[...], wo_ref[FW:FW + RH * RD, :]).astype(BF16)
        stage_a(x0_ref, f0_ref, r0_ref, n0_ref, 0, xa_ref, ha_ref)

    def step(x1_ref, h_ref, x1_next_ref, h_next_ref):
        hb = h_ref[...]
        fc = 256
        for c in range(DFF // fc):
            a = _dot(hb, w1_ref[:, c * fc:(c + 1) * fc])
            b = _dot(hb, w3_ref[:, c * fc:(c + 1) * fc])
            t = _dot((_silu(a) * b).astype(BF16), w2_ref[c * fc:(c + 1) * fc, :])
            if c == 0:
                acc_ref[...] = t
            else:
                acc_ref[...] += t
        x2 = x1_ref[...] + mods_ref[mod_row(s)][5:6, :] * acc_ref[...]
        if final:
            x2 = _rmsnorm(x2, gf_ref[...])
        o_ref[...] = x2
        stage_a(xn_ref, fn_ref, rn_ref, nn_ref, jnp.minimum(s + 1, last), x1_next_ref, h_next_ref)

    @pl.when(s % 2 == 0)
    def _():
        step(xa_ref, ha_ref, xb_ref, hb_ref)

    @pl.when(s % 2 == 1)
    def _():
        step(xb_ref, hb_ref, xa_ref, ha_ref)


def _params(sem):
    return pltpu.CompilerParams(dimension_semantics=sem, vmem_limit_bytes=VMEM_LIMIT)


def _const_spec(shape):
    nd = len(shape)
    return pl.BlockSpec(shape, lambda *_: (0,) * nd, pipeline_mode=pl.Buffered(1))


def _smem_spec():
    return pl.BlockSpec(memory_space=pltpu.SMEM)


def _ada(cvec, w_ada, b_ada):
    tn = 1024
    return pl.pallas_call(
        _ada_kernel,
        grid=(DEPTH, 6 * D // tn),
        in_specs=[pl.BlockSpec((8, D), lambda i, j: (0, 0)),
                  pl.BlockSpec((1, D, tn), lambda i, j: (i, 0, j)),
                  pl.BlockSpec((1, 1, tn), lambda i, j: (i, 0, j))],
        out_specs=pl.BlockSpec((1, 8, tn), lambda i, j: (i, 0, j)),
        out_shape=jax.ShapeDtypeStruct((DEPTH, 8, 6 * D), F32),
        compiler_params=_params(("arbitrary", "arbitrary")),
        name="ada",
    )(cvec, w_ada, b_ada.reshape(DEPTH, 1, 6 * D))


def _mod_row(ctx):
    per_batch = L // TM
    return (lambda tile: B) if ctx else (lambda tile: tile // per_batch)


def _proj(x2d, mods, g, w, sel_qk, sel_v, rope_tabs, *, ctx):
    rows = x2d.shape[0]
    nt = rows // TM
    per_batch = L // TM
    in_specs = [_const_spec((TM, D)),
                pl.BlockSpec((TM, D), lambda i: (jnp.minimum(i + 1, nt - 1), 0)),
                _const_spec((8, 6, D)),
                _const_spec((1, D)),
                _const_spec((D, W_IN)),
                _const_spec((RH * RD, RH * LANE)),
                _const_spec((RH * RD, RH * LANE))]
    args = [x2d, x2d, mods, g.reshape(1, D), w, sel_qk, sel_v]
    if not ctx:
        in_specs += [pl.BlockSpec((TM, LANE), lambda i: (i % per_batch, 0))] * 2
        args += list(rope_tabs)
    return pl.pallas_call(
        functools.partial(_proj_kernel, rope=not ctx, mod_row=_mod_row(ctx)),
        grid=(nt,),
        in_specs=in_specs,
        out_specs=[pl.BlockSpec((TM, FW), lambda i: (i, 0)), pl.BlockSpec((TM, NP), lambda i: (i, 0))],
        out_shape=[jax.ShapeDtypeStruct((rows, FW), BF16), jax.ShapeDtypeStruct((rows, NP), BF16)],
        scratch_shapes=[pltpu.VMEM((D, 4 * RH * LANE), BF16), pltpu.VMEM((TM, D), BF16), pltpu.VMEM((TM, D), BF16)],
        compiler_params=_params(("arbitrary",)),
        name="proj_ctx" if ctx else "proj",
    )(*args)


def _fourier(f2d, wf, a1, m3, e):
    y = pl.pallas_call(
        _four1_kernel,
        grid=(B, DFT_N2 // F1_R),
        in_specs=[_const_spec((2 * DFT_N1 * F1_R, DFT_N1 * F1_R)),
                  pl.BlockSpec((1, DFT_N1, F1_R, FW), lambda b, j: (b, 0, j, 0))],
        out_specs=pl.BlockSpec((1, 2 * DFT_N1, F1_R, FW), lambda b, j: (b, 0, j, 0)),
        out_shape=jax.ShapeDtypeStruct((B, 2 * DFT_N1, DFT_N2, FW), BF16),
        compiler_params=_params(("arbitrary", "arbitrary")),
        name="four1",
    )(a1, f2d.reshape(B, DFT_N1, DFT_N2, FW))
    out = pl.pallas_call(
        _four3_kernel,
        grid=(DFT_N1 // F3_K,),
        in_specs=[pl.BlockSpec((F3_K, 2 * DFT_N2, 2 * DFT_N2), lambda k: (k, 0, 0)),
                  pl.BlockSpec((B, 2, F3_K, DFT_N2, FW), lambda k: (0, 0, k, 0, 0)),
                  _const_spec((2 * FW, FW)),
                  _const_spec((FW, FW))],
        out_specs=pl.BlockSpec((B, DFT_N2, F3_K, FW), lambda k: (0, 0, k, 0)),
        out_shape=jax.ShapeDtypeStruct((B, DFT_N2, DFT_N1, FW), F32),
        compiler_params=_params(("arbitrary",)),
        name="four3",
    )(m3, y.reshape(B, 2, DFT_N1, DFT_N2, FW), e, wf)
    return out.reshape(B * L, FW)


def _retention(p, s0, lg, gain):
    col = lambda c0: (lambda b, h: (b, c0 // LANE + h))
    return pl.pallas_call(
        _ret_kernel,
        grid=(B, RH),
        in_specs=[_smem_spec(),
                  pl.BlockSpec((L, LANE), col(C_RQ)),
                  pl.BlockSpec((L, LANE), col(C_RK)),
                  pl.BlockSpec((L, LANE), col(C_RV)),
                  pl.BlockSpec((L, LANE), col(C_RG)),
                  pl.BlockSpec((1, 1, 2, LANE, LANE), lambda b, h: (b, h, 0, 0, 0)),
                  pl.BlockSpec((1, LANE), lambda b, h: (0, h))],
        out_specs=pl.BlockSpec((L, LANE), lambda b, h: (b, h)),
        out_shape=jax.ShapeDtypeStruct((B * L, RH * LANE), BF16),
        scratch_shapes=[pltpu.VMEM((L, LANE), F32), pltpu.VMEM((LANE, LANE), F32), pltpu.VMEM((LANE, LANE), F32)],
        compiler_params=_params(("arbitrary", "arbitrary")),
        name="ret",
    )(lg, p, p, p, p, s0, gain)


def _ctx_state(pc, lg):
    col = lambda c0: (lambda b, h: (b, c0 // LANE + h))
    return pl.pallas_call(
        _ctx_state_kernel,
        grid=(B, RH),
        in_specs=[_smem_spec(),
                  pl.BlockSpec((LC, LANE), col(C_RK)),
                  pl.BlockSpec((LC, LANE), col(C_RV))],
        out_specs=pl.BlockSpec((1, 1, 2, LANE, LANE), lambda b, h: (b, h, 0, 0, 0)),
        out_shape=jax.ShapeDtypeStruct((B, RH, 2, LANE, LANE), F32),
        compiler_params=_params(("arbitrary", "arbitrary")),
        name="ctx_state",
    )(lg, pc, pc)


def _ctx_mix(fc, pc, lg, gain, wf, c256, s256, e):
    return pl.pallas_call(
        _ctx_mix_kernel,
        grid=(B,),
        in_specs=[_smem_spec(),
                  pl.BlockSpec((LC, FW), lambda b: (b, 0)),
                  pl.BlockSpec((LC, NP), lambda b: (b, 0)),
                  _const_spec((1, RH * LANE)),
                  _const_spec((FW, FW)),
                  _const_spec((LC, LC)),
                  _const_spec((LC, LC)),
                  _const_spec((2 * FW, FW))],
        out_specs=[pl.BlockSpec((LC, FW), lambda b: (b, 0)),
                   pl.BlockSpec((LC, RH * LANE), lambda b: (b, 0)),
                   pl.BlockSpec((LC, NH * ND), lambda b: (b, 0))],
        out_shape=[jax.ShapeDtypeStruct((B * LC, FW), BF16),
                   jax.ShapeDtypeStruct((B * LC, RH * LANE), BF16),
                   jax.ShapeDtypeStruct((B * LC, NH * ND), BF16)],
        compiler_params=_params(("arbitrary",)),
        name="ctx_mix",
    )(lg, fc, pc, gain, wf, c256, s256, e)


def _na_bias(base, mask):
    nq, nk = NA_R * GRID_W, NA_KR * GRID_W
    return pl.pallas_call(
        _na_bias_kernel,
        grid=(NH,),
        in_specs=[pl.BlockSpec((1, NA_DR, LANE), lambda h: (h, 0, 0)),
                  _const_spec((GRID_W, LANE))],
        out_specs=pl.BlockSpec((1, 3, nq, nk), lambda h: (h, 0, 0, 0)),
        out_shape=jax.ShapeDtypeStruct((NH, 3, nq, nk), F32),
        scratch_shapes=[pltpu.VMEM((NA_DR, GRID_W, LANE), F32)],
        compiler_params=_params(("arbitrary",)),
        name="na_bias",
    )(base, mask)


def _natten(p, pc, bias):
    nq, nk = NA_R * GRID_W, NA_KR * GRID_W
    return pl.pallas_call(
        _natten_kernel,
        grid=(B, NPAIR),
        in_specs=[pl.BlockSpec((L, LANE), lambda b, pr: (b, C_NQ // LANE + pr)),
                  pl.BlockSpec((L, LANE), lambda b, pr: (b, C_NK // LANE + pr)),
                  pl.BlockSpec((L, LANE), lambda b, pr: (b, C_NV // LANE + pr)),
                  pl.BlockSpec((LC, LANE), lambda b, pr: (b, C_NK // LANE + pr)),
                  pl.BlockSpec((LC, LANE), lambda b, pr: (b, C_NV // LANE + pr)),
                  pl.BlockSpec((2, 3, nq, nk), lambda b, pr: (pr, 0, 0, 0))],
        out_specs=pl.BlockSpec((L, LANE), lambda b, pr: (b, pr)),
        out_shape=jax.ShapeDtypeStruct((B * L, NH * ND), BF16),
        compiler_params=_params(("arbitrary", "arbitrary")),
        name="natten",
    )(p, p, p, pc, pc, bias)


def _ffn(x2d, four, ret, na, mods, g, wo, sel_v_t, w1, w3, w2, g_final, *, ctx):
    rows = x2d.shape[0]
    nt = rows // TM
    final = g_final is not None
    widths = (D, FW, RH * LANE, NH * ND)
    first = [_const_spec((TM, w)) for w in widths]
    nxt = [pl.BlockSpec((TM, w), lambda i: (jnp.minimum(i + 1, nt - 1), 0)) for w in widths]
    in_specs = first + nxt + [_const_spec((8, 6, D)), _const_spec((1, D)),
                              _const_spec((D, D)), _const_spec((RH * LANE, RH * RD)),
                              _const_spec((D, DFF)), _const_spec((D, DFF)), _const_spec((DFF, D))]
    acts = [x2d, four, ret, na]
    args = acts + acts + [mods, g.reshape(1, D), wo, sel_v_t, w1, w3, w2]
    if final:
        in_specs.append(_const_spec((1, D)))
        args.append(g_final.reshape(1, D))
    return pl.pallas_call(
        functools.partial(_ffn_kernel, final=final, mod_row=_mod_row(ctx)),
        grid=(nt,),
        in_specs=in_specs,
        out_specs=pl.BlockSpec((TM, D), lambda i: (i, 0)),
        out_shape=jax.ShapeDtypeStruct((rows, D), F32),
        scratch_shapes=[pltpu.VMEM((TM, D), F32), pltpu.VMEM((RH * LANE, D), BF16),
                        pltpu.VMEM((TM, D), F32), pltpu.VMEM((TM, D), BF16),
                        pltpu.VMEM((TM, D), F32), pltpu.VMEM((TM, D), BF16)],
        compiler_params=_params(("arbitrary",)),
        name="ffn_ctx" if ctx else ("ffn_final" if final else "ffn"),
    )(*args)


def _take_padded(a, src, axis):
    pieces = []
    i, n = 0, len(src)
    while i < n:
        j = i + 1
        if src[i] < 0:
            while j < n and src[j] < 0:
                j += 1
            shape = list(a.shape)
            shape[axis] = j - i
            pieces.append(jnp.zeros(shape, a.dtype))
        else:
            while j < n and src[j] == src[j - 1] + 1:
                j += 1
            pieces.append(lax.slice_in_dim(a, int(src[i]), int(src[j - 1]) + 1, axis=axis))
        i = j
    return jnp.concatenate(pieces, axis=axis)


def kernel(x, c, ctx, c_ctx, w_ada, b_ada, g_mix, w_in, ret_decay_logit, ret_norm_g, w_four,
           na_rpb, w_out, g_ffn, w1, w3, w2, g_final):
    src_g = _gain_sources()
    rope_tabs = tuple(jnp.asarray(t) for t in _rope_tables())
    a1, m3, e, c256, s256 = (jnp.asarray(t).astype(BF16) for t in _dft_tables())
    sel_qk, sel_v = (jnp.asarray(t).astype(BF16) for t in _head_selectors())
    sel_v_t = sel_v.T
    tz_src, tz_mask = _na_toeplitz_tables()
    col_scale = np.ones((1, 1, W_IN), np.float32)
    col_scale[..., W_NQ:W_NK] = LOG2E * ND ** -0.5
    w_in_b = (w_in * jnp.asarray(col_scale)).astype(BF16)
    w_out_b = w_out.astype(BF16)

    cvec = jnp.zeros((8, D), F32).at[0:B].set(c).at[B].set(c_ctx)
    mods_all = _ada(cvec, w_ada, b_ada).reshape(DEPTH, 8, 6, D)

    xl = x.reshape(B * L, D)
    xc = ctx.reshape(B * LC, D)
    for i in range(DEPTH):
        last = i == DEPTH - 1
        mods = mods_all[i]
        gain = _take_padded(ret_norm_g[i], src_g, 0).reshape(1, RH * LANE)
        wfour = w_four[i].astype(BF16)
        lg = jax.nn.log_sigmoid(ret_decay_logit[i].astype(F32))
        tz_base = jnp.pad(_take_padded(na_rpb[i].astype(F32), tz_src, 2), ((0, 0), (0, 1), (0, 0)))
        na_bias = _na_bias(tz_base, jnp.asarray(tz_mask))
        w1b, w3b, w2b = w1[i].astype(BF16), w3[i].astype(BF16), w2[i].astype(BF16)

        f_l, p_l = _proj(xl, mods, g_mix[i], w_in_b[i], sel_qk, sel_v, rope_tabs, ctx=False)
        f_c, p_c = _proj(xc, mods, g_mix[i], w_in_b[i], sel_qk, sel_v, None, ctx=True)
        s0 = _ctx_state(p_c, lg)
        four_l = _fourier(f_l, wfour, a1, m3, e)
        ret_l = _retention(p_l, s0, lg, gain)
        na_l = _natten(p_l, p_c, na_bias)
        if not last:
            four_c, ret_c, na_c = _ctx_mix(f_c, p_c, lg, gain, wfour, c256, s256, e)
            xc = _ffn(xc, four_c, ret_c, na_c, mods, g_ffn[i], w_out_b[i], sel_v_t, w1b, w3b, w2b, None, ctx=True)
        xl = _ffn(xl, four_l, ret_l, na_l, mods, g_ffn[i], w_out_b[i], sel_v_t, w1b, w3b, w2b,
                  g_final if last else None, ctx=False)
    return xl.reshape(B, L, D)
```

```python
import functools

import numpy as np
import jax
import jax.numpy as jnp
from jax import lax
from jax.experimental import pallas as pl
from jax.experimental.pallas import tpu as pltpu

F32 = jnp.float32
BF16 = jnp.bfloat16

D = 1024
B = 2
L = 8192
LC = 256
DEPTH = 2
GRID_W = 64
GRID_H = L // GRID_W
FW = 256
FGW = 64
RH = 4
RD = 96
NH = 6
ND = 64
NPAIR = NH // 2
DFF = 2816
EPS = 1e-6
NEG = -1e30
LOG2E = float(np.log2(np.e))
LANE = 128

C_RQ, C_RK, C_RV, C_RG = 0, 512, 1024, 1536
C_NQ, C_NK, C_NV = 2048, 2432, 2816
NP = 3200
W_RQ, W_RK, W_RV, W_RG = 256, 640, 1024, 1408
W_NQ, W_NK, W_NV = 1792, 2176, 2560
W_IN = 2944

TM = 512
RET_C = 256
NA_R = 4
NA_KR = NA_R + 8
NA_NB = GRID_H // NA_R
DFT_N1 = 64
DFT_N2 = 128
F1_R = 16
F3_K = 8
VMEM_LIMIT = 56 * 1024 * 1024


def _qk_lane_dims():
    m = np.full((LANE,), -1, np.int64)
    m[0:24] = np.arange(0, 24)
    m[24:48] = np.arange(48, 72)
    m[64:88] = np.arange(24, 48)
    m[88:112] = np.arange(72, 96)
    return m


def _v_lane_dims():
    m = np.full((LANE,), -1, np.int64)
    m[:RD] = np.arange(RD)
    return m


def _head_selectors():
    sels = []
    for lanes in (_qk_lane_dims(), _v_lane_dims()):
        sel = np.zeros((RH * RD, RH * LANE), np.float32)
        ok = np.nonzero(lanes >= 0)[0]
        for h in range(RH):
            sel[RD * h + lanes[ok], LANE * h + ok] = 1.0
        sels.append(sel)
    return sels


def _gain_sources():
    src = np.full((RH * LANE,), -1, np.int64)
    for h in range(RH):
        src[LANE * h: LANE * h + RD] = RD * h + np.arange(RD)
    return src


def _rope_tables():
    pos = np.arange(L)
    prow, pcol = pos // GRID_W, pos % GRID_W
    half = RD // 4
    inv = 10000.0 ** (-np.arange(half, dtype=np.float64) / half)
    ar = prow[:, None] * inv[None, :]
    ac = pcol[:, None] * inv[None, :]
    cos = np.ones((L, LANE), np.float64)
    sin = np.zeros((L, LANE), np.float64)
    for off, sign in ((0, -1.0), (64, 1.0)):
        cos[:, off:off + 24] = np.cos(ar)
        cos[:, off + 24:off + 48] = np.cos(ac)
        sin[:, off:off + 24] = sign * np.sin(ar)
        sin[:, off + 24:off + 48] = sign * np.sin(ac)
    return cos.astype(np.float32), sin.astype(np.float32)


def _dft_tables():
    k1 = np.arange(DFT_N1)
    a = 2 * np.pi * ((k1[:, None] * k1[None, :]) % DFT_N1) / DFT_N1
    a1 = np.kron(np.concatenate([np.cos(a), -np.sin(a)], axis=0), np.eye(F1_R))
    k2 = np.arange(DFT_N2)
    l2 = np.arange(DFT_N2)
    kk = k1[:, None, None] + DFT_N1 * k2[None, :, None]
    ang = 2 * np.pi * ((kk * l2[None, None, :]) % L) / L
    ct, st = np.cos(ang) / np.sqrt(L), np.sin(ang) / np.sqrt(L)
    m3 = np.concatenate([np.concatenate([ct, st], axis=2), np.concatenate([st, -ct], axis=2)], axis=1)
    c = np.arange(FW)
    same = (c[:, None] // FGW) == (c[None, :] // FGW)
    ac = 2 * np.pi * (((c[:, None] % FGW) * (c[None, :] % FGW)) % FGW) / FGW
    e = np.concatenate([np.where(same, np.cos(ac), 0.0), -np.where(same, np.sin(ac), 0.0)], axis=0) / 8.0
    p = np.arange(LC)
    ap = 2 * np.pi * ((p[:, None] * p[None, :]) % LC) / LC
    return tuple(t.astype(np.float32) for t in (a1, m3, e, np.cos(ap) / 16.0, np.sin(ap) / 16.0))


NA_DR = 16


def _na_toeplitz_tables():
    src = np.full((LANE,), -1, np.int64)
    src[0:16] = np.arange(0, 16) + 15
    src[113:128] = np.arange(113, 128) - 113
    src[49:80] = np.arange(49, 80) - 49
    qc = np.arange(GRID_W)[:, None]
    kc = np.arange(LANE)[None, :] % GRID_W
    ws = np.clip(qc - 8, 0, GRID_W - 16)
    mask = np.where((kc >= ws) & (kc < ws + 16), 0.0, NEG).astype(np.float32)
    return src, mask


def _dot(a, b):
    return jnp.dot(a, b, preferred_element_type=F32)


def _dot_nt(a, b):
    return lax.dot_general(a, b, (((1,), (1,)), ((), ())), preferred_element_type=F32)


def _silu(x):
    return x * jax.nn.sigmoid(x)


def _rmsnorm(x, g):
    return x * lax.rsqrt(jnp.mean(x * x, axis=-1, keepdims=True) + EPS) * g


def _ada_kernel(c_ref, w_ref, b_ref, o_ref):
    s = _silu(c_ref[...]).astype(BF16)
    o_ref[0] = _dot(s, w_ref[0].astype(BF16)) + b_ref[0]


def _proj_kernel(*refs, rope, mod_row):
    if rope:
        x_ref, mods_ref, g_ref, w_ref, pqk_ref, pv_ref, cos_ref, sin_ref, f_ref, o_ref, wret_ref = refs
    else:
        x_ref, mods_ref, g_ref, w_ref, pqk_ref, pv_ref, f_ref, o_ref, wret_ref = refs
    s = pl.program_id(0)

    @pl.when(s == 0)
    def _():
        for slot, (src0, sel_ref) in enumerate(((W_RQ, pqk_ref), (W_RK, pqk_ref), (W_RV, pv_ref), (W_RG, pv_ref))):
            wret_ref[:, RH * LANE * slot:RH * LANE * (slot + 1)] = _dot(
                w_ref[:, src0:src0 + RH * RD], sel_ref[...]).astype(BF16)

    mod = mods_ref[mod_row(s)]
    hb = (_rmsnorm(x_ref[...], g_ref[...]) * (1.0 + mod[1:2, :]) + mod[0:1, :]).astype(BF16)
    f_ref[...] = _dot(hb, w_ref[:, :FW]).astype(BF16)
    k_scale = RD ** -0.5
    for c0, scale in ((C_RQ, None), (C_RK, k_scale)):
        t4 = _dot(hb, wret_ref[:, c0:c0 + RH * LANE])
        for hh in range(RH):
            t = t4[:, LANE * hh:LANE * (hh + 1)]
            if rope:
                t = t * cos_ref[...] + pltpu.roll(t, 64, 1) * sin_ref[...]
            if scale is not None:
                t = t * scale
            o_ref[:, c0 + LANE * hh:c0 + LANE * (hh + 1)] = t.astype(BF16)
    for c0 in (C_RV, C_RG):
        o_ref[:, c0:c0 + RH * LANE] = _dot(hb, wret_ref[:, c0:c0 + RH * LANE]).astype(BF16)
    o_ref[:, C_NQ:] = _dot(hb, w_ref[:, W_NQ:]).astype(BF16)


def _four1_kernel(a_ref, x_ref, y_ref):
    x = x_ref[0].reshape(DFT_N1 * F1_R, FW)
    y = _dot(a_ref[...], x).astype(BF16)
    y_ref[0] = y.reshape(2 * DFT_N1, F1_R, FW)


def _four3_kernel(m_ref, y_ref, e_ref, wf_ref, o_ref):
    ew = _dot(e_ref[...], wf_ref[...]).astype(BF16)
    for b in range(B):
        res = []
        for kk in range(F3_K):
            y = jnp.concatenate([y_ref[b, 0, kk], y_ref[b, 1, kk]], axis=0)
            z = _dot(m_ref[kk], y).astype(BF16)
            res.append(_dot(jnp.concatenate([z[:DFT_N2], z[DFT_N2:]], axis=1), ew))
        o_ref[b] = jnp.stack(res, axis=1)


def _decay_matrix(n, lgf, lgb):
    ii = lax.broadcasted_iota(jnp.int32, (n, n), 0)
    jj = lax.broadcasted_iota(jnp.int32, (n, n), 1)
    diff = (ii - jj).astype(F32)
    fwd = jnp.where(diff >= 0, jnp.exp(jnp.maximum(diff, 0.0) * lgf), 0.0)
    bwd = jnp.where(diff <= 0, jnp.exp(jnp.maximum(-diff, 0.0) * lgb), 0.0)
    return fwd + bwd


def _ret_readout(o, gate, gain):
    ms = jnp.sum(o * o, axis=-1, keepdims=True) * (1.0 / RD)
    return (o * lax.rsqrt(ms + EPS) * gain * _silu(gate.astype(F32))).astype(BF16)


def _ret_kernel(lg_ref, q_ref, k_ref, v_ref, g_ref, s0_ref, gain_ref, o_ref, acc_ref, stf_ref, stb_ref):
    h = pl.program_id(1)
    lgf = lg_ref[0, h]
    lgb = lg_ref[1, h]
    c = RET_C
    n = L // c
    dmat = _decay_matrix(c, lgf, lgb)
    idx = lax.broadcasted_iota(jnp.int32, (c, LANE), 0).astype(F32)
    qdf = jnp.exp((idx + 1.0) * lgf)
    kdf = jnp.exp((c - 1.0 - idx) * lgf)
    qdb = jnp.exp((c - idx) * lgb)
    kdb = jnp.exp(idx * lgb)
    cdf = jnp.exp(jnp.full((1, LANE), float(c), F32) * lgf)
    cdb = jnp.exp(jnp.full((1, LANE), float(c), F32) * lgb)
    gain = gain_ref[...]

    dirs = ((stf_ref, qdf, kdf, cdf), (stb_ref, qdb, kdb, cdb))
    stf_ref[...] = s0_ref[0, 0, 0]
    stb_ref[...] = s0_ref[0, 0, 1]

    def visit(r0, direction, second):
        st_ref, qdec, kdec, cdec = dirs[direction]
        rows = pl.ds(pl.multiple_of(r0, c), c)
        q = q_ref[rows, :]
        k = k_ref[rows, :]
        v = v_ref[rows, :]
        cross = _dot(q, st_ref[...].astype(BF16)) * qdec
        if second:
            o_ref[rows, :] = _ret_readout(acc_ref[rows, :] + cross, g_ref[rows, :], gain)
        else:
            s = _dot_nt(q, k) * dmat
            acc_ref[rows, :] = _dot(s.astype(BF16), v) + cross
        kt = (k.astype(F32) * kdec).T.astype(BF16)
        st_ref[...] = st_ref[...] * cdec + _dot(kt, v)

    def body(second):
        def step(i, carry):
            visit(i * c, 0, second)
            visit((n - 1 - i) * c, 1, second)
            return carry
        return step

    lax.fori_loop(0, n // 2, body(False), 0, unroll=4)
    lax.fori_loop(n // 2, n, body(True), 0, unroll=4)


def _ctx_state_kernel(lg_ref, k_ref, v_ref, o_ref):
    h = pl.program_id(1)
    idx = lax.broadcasted_iota(jnp.int32, (LC, LANE), 0).astype(F32)
    k = k_ref[...].astype(F32)
    v = v_ref[...]
    wf = jnp.exp((LC - 1.0 - idx) * lg_ref[0, h])
    wb = jnp.exp(idx * lg_ref[1, h])
    o_ref[0, 0, 0] = _dot((k * wf).T.astype(BF16), v)
    o_ref[0, 0, 1] = _dot((k * wb).T.astype(BF16), v)


def _pair_attention(q, score_fn, value_fn):
    lane = lax.broadcasted_iota(jnp.int32, q.shape, 1)
    outs = []
    for hh in range(2):
        sel = (lane >= ND * hh) & (lane < ND * (hh + 1))
        qm = jnp.where(sel, q, jnp.zeros_like(q))
        scores = score_fn(qm, hh)
        m = functools.reduce(jnp.maximum, [jnp.max(s, axis=-1, keepdims=True) for s in scores])
        ps = [jnp.exp2(s - m) for s in scores]
        den = functools.reduce(jnp.add, [jnp.sum(p, axis=-1, keepdims=True) for p in ps])
        outs.append(value_fn([p.astype(BF16) for p in ps]) / den)
    return jnp.where(lane < ND, outs[0], outs[1])


def _ctx_mix_kernel(lg_ref, f_ref, p_ref, gain_ref, wf_ref, c_ref, s_ref, e_ref, fo_ref, ro_ref, no_ref):
    f = f_ref[...]
    pr = _dot(c_ref[...], f)
    qi = _dot(s_ref[...], f)
    fr = _dot(pr.astype(BF16), e_ref[:FW, :]) + _dot(qi.astype(BF16), e_ref[FW:, :])
    fo_ref[...] = _dot(fr.astype(BF16), wf_ref[...]).astype(BF16)
    for h in range(RH):
        sl = lambda c0: slice(c0 + LANE * h, c0 + LANE * (h + 1))
        dmat = _decay_matrix(LC, lg_ref[0, h], lg_ref[1, h])
        s = _dot_nt(p_ref[:, sl(C_RQ)], p_ref[:, sl(C_RK)]) * dmat
        o = _dot(s.astype(BF16), p_ref[:, sl(C_RV)])
        ro_ref[:, LANE * h:LANE * (h + 1)] = _ret_readout(o, p_ref[:, sl(C_RG)], gain_ref[:, LANE * h:LANE * (h + 1)])
    for pair in range(NPAIR):
        sl = lambda c0: slice(c0 + LANE * pair, c0 + LANE * (pair + 1))
        k = p_ref[:, sl(C_NK)]
        v = p_ref[:, sl(C_NV)]
        out = _pair_attention(p_ref[:, sl(C_NQ)], lambda qm, hh: [_dot_nt(qm, k)], lambda ps: _dot(ps[0], v))
        no_ref[:, LANE * pair:LANE * (pair + 1)] = out.astype(BF16)


def _na_key_start(r0):
    return min(max(r0 - 4, 0), GRID_H - NA_KR)


def _na_bias_kernel(base_ref, mask_ref, o_ref, tz_ref):
    qc = lax.broadcasted_iota(jnp.int32, (GRID_W, LANE), 0)
    for dr in range(NA_DR - 1):
        t = jnp.broadcast_to(base_ref[0, dr:dr + 1, :], (GRID_W, LANE))
        for bit in range(6):
            t = jnp.where((qc >> bit) & 1 == 1, pltpu.roll(t, 1 << bit, 1), t)
        tz_ref[dr] = t * LOG2E + mask_ref[...]
    tz_ref[NA_DR - 1] = jnp.full((GRID_W, LANE), NEG, F32)
    low_half = lax.broadcasted_iota(jnp.int32, (GRID_W, LANE), 1) < GRID_W
    for var, r0 in enumerate((0, 2 * NA_R, GRID_H - NA_R)):
        ks = _na_key_start(r0)
        for rl in range(NA_R):
            r = r0 + rl
            rs = min(max(r - 4, 0), GRID_H - 8)
            slot = [kr - r + 7 if rs <= kr < rs + 8 else NA_DR - 1 for kr in range(ks, ks + NA_KR)]
            for m in range(NA_KR // 2):
                o_ref[0, var, GRID_W * rl:GRID_W * (rl + 1), LANE * m:LANE * (m + 1)] = jnp.where(
                    low_half, tz_ref[slot[2 * m]], tz_ref[slot[2 * m + 1]])


def _natten_kernel(q_ref, k_ref, v_ref, kc_ref, vc_ref, bias_ref, o_ref):
    nq = NA_R * GRID_W
    nk = NA_KR * GRID_W
    kc = kc_ref[...]
    vc = vc_ref[...]

    def block(i, carry):
        ks = jnp.clip(NA_R * i - 4, 0, GRID_H - NA_KR)
        var = jnp.where(i == 0, 0, jnp.where(i == NA_NB - 1, 2, 1))
        krows = pl.ds(pl.multiple_of(ks * GRID_W, nq), nk)
        qrows = pl.ds(pl.multiple_of(i * nq, nq), nq)
        kw = k_ref[krows, :]
        vw = v_ref[krows, :]

        def scores(qm, hh):
            return [_dot_nt(qm, kw) + bias_ref[hh, var], _dot_nt(qm, kc)]

        def values(ps):
            return _dot(ps[0], vw) + _dot(ps[1], vc)

        o_ref[qrows, :] = _pair_attention(q_ref[qrows, :], scores, values).astype(BF16)
        return carry

    lax.fori_loop(0, NA_NB, block, 0)


def _ffn_kernel(*refs, final, mod_row):
    x_ref, f_ref, r_ref, n_ref, mods_ref, g_ref, wo_ref, sel_ref, w1_ref, w3_ref, w2_ref = refs[:11]
    gf_ref = refs[11] if final else None
    o_ref, acc_ref, wor_ref = refs[-3:]
    s = pl.program_id(0)

    @pl.when(s == 0)
    def _():
        wor_ref[...] = _dot(sel_ref[...], wo_ref[FW:FW + RH * RD, :]).astype(BF16)

    mod = mods_ref[mod_row(s)]
    y = (_dot(f_ref[...].astype(BF16), wo_ref[:FW, :]) + _dot(r_ref[...], wor_ref[...])
         + _dot(n_ref[...], wo_ref[FW + RH * RD:, :]))
    x1 = x_ref[...] + mod[2:3, :] * y
    hb = (_rmsnorm(x1, g_ref[...]) * (1.0 + mod[4:5, :]) + mod[3:4, :]).astype(BF16)
    fc = 256
    for c in range(DFF // fc):
        a = _dot(hb, w1_ref[:, c * fc:(c + 1) * fc])
        b = _dot(hb, w3_ref[:, c * fc:(c + 1) * fc])
        t = _dot((_silu(a) * b).astype(BF16), w2_ref[c * fc:(c + 1) * fc, :])
        if c == 0:
            acc_ref[...] = t
        else:
            acc_ref[...] += t
    x2 = x1 + mod[5:6, :] * acc_ref[...]
    if final:
        x2 = _rmsnorm(x2, gf_ref[...])
    o_ref[...] = x2


def _params(sem):
    return pltpu.CompilerParams(dimension_semantics=sem, vmem_limit_bytes=VMEM_LIMIT)


def _const_spec(shape):
    nd = len(shape)
    return pl.BlockSpec(shape, lambda *_: (0,) * nd, pipeline_mode=pl.Buffered(1))


def _smem_spec():
    return pl.BlockSpec(memory_space=pltpu.SMEM)


def _ada(cvec, w_ada, b_ada):
    tn = 1024
    return pl.pallas_call(
        _ada_kernel,
        grid=(DEPTH, 6 * D // tn),
        in_specs=[pl.BlockSpec((8, D), lambda i, j: (0, 0)),
                  pl.BlockSpec((1, D, tn), lambda i, j: (i, 0, j)),
                  pl.BlockSpec((1, 1, tn), lambda i, j: (i, 0, j))],
        out_specs=pl.BlockSpec((1, 8, tn), lambda i, j: (i, 0, j)),
        out_shape=jax.ShapeDtypeStruct((DEPTH, 8, 6 * D), F32),
        compiler_params=_params(("arbitrary", "arbitrary")),
        name="ada",
    )(cvec, w_ada, b_ada.reshape(DEPTH, 1, 6 * D))


def _mod_row(ctx):
    per_batch = L // TM
    return (lambda tile: B) if ctx else (lambda tile: tile // per_batch)


def _layer_spec(shape, layer):
    nd = len(shape)
    return pl.BlockSpec((None,) + tuple(shape), lambda *_: (layer,) + (0,) * nd, pipeline_mode=pl.Buffered(1))


def _proj(x2d, mods, g, w_all, layer, sel_qk, sel_v, rope_tabs, *, ctx):
    rows = x2d.shape[0]
    nt = rows // TM
    per_batch = L // TM
    in_specs = [pl.BlockSpec((TM, D), lambda i: (i, 0)),
                _const_spec((8, 6, D)),
                _const_spec((1, D)),
                _layer_spec((D, W_IN), layer),
                _const_spec((RH * RD, RH * LANE)),
                _const_spec((RH * RD, RH * LANE))]
    args = [x2d, mods, g.reshape(1, D), w_all, sel_qk, sel_v]
    if not ctx:
        in_specs += [pl.BlockSpec((TM, LANE), lambda i: (i % per_batch, 0))] * 2
        args += list(rope_tabs)
    return pl.pallas_call(
        functools.partial(_proj_kernel, rope=not ctx, mod_row=_mod_row(ctx)),
        grid=(nt,),
        in_specs=in_specs,
        out_specs=[pl.BlockSpec((TM, FW), lambda i: (i, 0)), pl.BlockSpec((TM, NP), lambda i: (i, 0))],
        out_shape=[jax.ShapeDtypeStruct((rows, FW), BF16), jax.ShapeDtypeStruct((rows, NP), BF16)],
        scratch_shapes=[pltpu.VMEM((D, 4 * RH * LANE), BF16)],
        compiler_params=_params(("arbitrary",)),
        name="proj_ctx" if ctx else "proj",
    )(*args)


def _fourier(f2d, wf, a1, m3, e):
    y = pl.pallas_call(
        _four1_kernel,
        grid=(B, DFT_N2 // F1_R),
        in_specs=[_const_spec((2 * DFT_N1 * F1_R, DFT_N1 * F1_R)),
                  pl.BlockSpec((1, DFT_N1, F1_R, FW), lambda b, j: (b, 0, j, 0))],
        out_specs=pl.BlockSpec((1, 2 * DFT_N1, F1_R, FW), lambda b, j: (b, 0, j, 0)),
        out_shape=jax.ShapeDtypeStruct((B, 2 * DFT_N1, DFT_N2, FW), BF16),
        compiler_params=_params(("arbitrary", "arbitrary")),
        name="four1",
    )(a1, f2d.reshape(B, DFT_N1, DFT_N2, FW))
    out = pl.pallas_call(
        _four3_kernel,
        grid=(DFT_N1 // F3_K,),
        in_specs=[pl.BlockSpec((F3_K, 2 * DFT_N2, 2 * DFT_N2), lambda k: (k, 0, 0)),
                  pl.BlockSpec((B, 2, F3_K, DFT_N2, FW), lambda k: (0, 0, k, 0, 0)),
                  _const_spec((2 * FW, FW)),
                  _const_spec((FW, FW))],
        out_specs=pl.BlockSpec((B, DFT_N2, F3_K, FW), lambda k: (0, 0, k, 0)),
        out_shape=jax.ShapeDtypeStruct((B, DFT_N2, DFT_N1, FW), F32),
        compiler_params=_params(("arbitrary",)),
        name="four3",
    )(m3, y.reshape(B, 2, DFT_N1, DFT_N2, FW), e, wf)
    return out.reshape(B * L, FW)


def _retention(p, s0, lg, gain):
    col = lambda c0: (lambda b, h: (b, c0 // LANE + h))
    return pl.pallas_call(
        _ret_kernel,
        grid=(B, RH),
        in_specs=[_smem_spec(),
                  pl.BlockSpec((L, LANE), col(C_RQ)),
                  pl.BlockSpec((L, LANE), col(C_RK)),
                  pl.BlockSpec((L, LANE), col(C_RV)),
                  pl.BlockSpec((L, LANE), col(C_RG)),
                  pl.BlockSpec((1, 1, 2, LANE, LANE), lambda b, h: (b, h, 0, 0, 0)),
                  pl.BlockSpec((1, LANE), lambda b, h: (0, h))],
        out_specs=pl.BlockSpec((L, LANE), lambda b, h: (b, h)),
        out_shape=jax.ShapeDtypeStruct((B * L, RH * LANE), BF16),
        scratch_shapes=[pltpu.VMEM((L, LANE), F32), pltpu.VMEM((LANE, LANE), F32), pltpu.VMEM((LANE, LANE), F32)],
        compiler_params=_params(("arbitrary", "arbitrary")),
        name="ret",
    )(lg, p, p, p, p, s0, gain)


def _ctx_state(pc, lg):
    col = lambda c0: (lambda b, h: (b, c0 // LANE + h))
    return pl.pallas_call(
        _ctx_state_kernel,
        grid=(B, RH),
        in_specs=[_smem_spec(),
                  pl.BlockSpec((LC, LANE), col(C_RK)),
                  pl.BlockSpec((LC, LANE), col(C_RV))],
        out_specs=pl.BlockSpec((1, 1, 2, LANE, LANE), lambda b, h: (b, h, 0, 0, 0)),
        out_shape=jax.ShapeDtypeStruct((B, RH, 2, LANE, LANE), F32),
        compiler_params=_params(("arbitrary", "arbitrary")),
        name="ctx_state",
    )(lg, pc, pc)


def _ctx_mix(fc, pc, lg, gain, wf, c256, s256, e):
    return pl.pallas_call(
        _ctx_mix_kernel,
        grid=(B,),
        in_specs=[_smem_spec(),
                  pl.BlockSpec((LC, FW), lambda b: (b, 0)),
                  pl.BlockSpec((LC, NP), lambda b: (b, 0)),
                  _const_spec((1, RH * LANE)),
                  _const_spec((FW, FW)),
                  _const_spec((LC, LC)),
                  _const_spec((LC, LC)),
                  _const_spec((2 * FW, FW))],
        out_specs=[pl.BlockSpec((LC, FW), lambda b: (b, 0)),
                   pl.BlockSpec((LC, RH * LANE), lambda b: (b, 0)),
                   pl.BlockSpec((LC, NH * ND), lambda b: (b, 0))],
        out_shape=[jax.ShapeDtypeStruct((B * LC, FW), BF16),
                   jax.ShapeDtypeStruct((B * LC, RH * LANE), BF16),
                   jax.ShapeDtypeStruct((B * LC, NH * ND), BF16)],
        compiler_params=_params(("arbitrary",)),
        name="ctx_mix",
    )(lg, fc, pc, gain, wf, c256, s256, e)


def _na_bias(base, mask):
    nq, nk = NA_R * GRID_W, NA_KR * GRID_W
    return pl.pallas_call(
        _na_bias_kernel,
        grid=(NH,),
        in_specs=[pl.BlockSpec((1, NA_DR, LANE), lambda h: (h, 0, 0)),
                  _const_spec((GRID_W, LANE))],
        out_specs=pl.BlockSpec((1, 3, nq, nk), lambda h: (h, 0, 0, 0)),
        out_shape=jax.ShapeDtypeStruct((NH, 3, nq, nk), F32),
        scratch_shapes=[pltpu.VMEM((NA_DR, GRID_W, LANE), F32)],
        compiler_params=_params(("arbitrary",)),
        name="na_bias",
    )(base, mask)


def _natten(p, pc, bias):
    nq, nk = NA_R * GRID_W, NA_KR * GRID_W
    return pl.pallas_call(
        _natten_kernel,
        grid=(B, NPAIR),
        in_specs=[pl.BlockSpec((L, LANE), lambda b, pr: (b, C_NQ // LANE + pr)),
                  pl.BlockSpec((L, LANE), lambda b, pr: (b, C_NK // LANE + pr)),
                  pl.BlockSpec((L, LANE), lambda b, pr: (b, C_NV // LANE + pr)),
                  pl.BlockSpec((LC, LANE), lambda b, pr: (b, C_NK // LANE + pr)),
                  pl.BlockSpec((LC, LANE), lambda b, pr: (b, C_NV // LANE + pr)),
                  pl.BlockSpec((2, 3, nq, nk), lambda b, pr: (pr, 0, 0, 0))],
        out_specs=pl.BlockSpec((L, LANE), lambda b, pr: (b, pr)),
        out_shape=jax.ShapeDtypeStruct((B * L, NH * ND), BF16),
        compiler_params=_params(("arbitrary", "arbitrary")),
        name="natten",
    )(p, p, p, pc, pc, bias)


def _ffn(x2d, four, ret, na, mods, g, wo_all, sel_v_t, w1_all, w3_all, w2_all, layer, g_final, *, ctx):
    rows = x2d.shape[0]
    final = g_final is not None
    row = lambda w: pl.BlockSpec((TM, w), lambda i: (i, 0))
    in_specs = [row(D), row(FW), row(RH * LANE), row(NH * ND),
                _const_spec((8, 6, D)), _const_spec((1, D)),
                _layer_spec((D, D), layer), _const_spec((RH * LANE, RH * RD)),
                _layer_spec((D, DFF), layer), _layer_spec((D, DFF), layer), _layer_spec((DFF, D), layer)]
    args = [x2d, four, ret, na, mods, g.reshape(1, D), wo_all, sel_v_t, w1_all, w3_all, w2_all]
    if final:
        in_specs.append(_const_spec((1, D)))
        args.append(g_final.reshape(1, D))
    return pl.pallas_call(
        functools.partial(_ffn_kernel, final=final, mod_row=_mod_row(ctx)),
        grid=(rows // TM,),
        in_specs=in_specs,
        out_specs=row(D),
        out_shape=jax.ShapeDtypeStruct((rows, D), F32),
        scratch_shapes=[pltpu.VMEM((TM, D), F32), pltpu.VMEM((RH * LANE, D), BF16)],
        compiler_params=_params(("arbitrary",)),
        name="ffn_ctx" if ctx else ("ffn_final" if final else "ffn"),
    )(*args)


def _take_padded(a, src, axis):
    pieces = []
    i, n = 0, len(src)
    while i < n:
        j = i + 1
        if src[i] < 0:
            while j < n and src[j] < 0:
                j += 1
            shape = list(a.shape)
            shape[axis] = j - i
            pieces.append(jnp.zeros(shape, a.dtype))
        else:
            while j < n and src[j] == src[j - 1] + 1:
                j += 1
            pieces.append(lax.slice_in_dim(a, int(src[i]), int(src[j - 1]) + 1, axis=axis))
        i = j
    return jnp.concatenate(pieces, axis=axis)


def kernel(x, c, ctx, c_ctx, w_ada, b_ada, g_mix, w_in, ret_decay_logit, ret_norm_g, w_four,
           na_rpb, w_out, g_ffn, w1, w3, w2, g_final):
    src_g = _gain_sources()
    rope_tabs = tuple(jnp.asarray(t) for t in _rope_tables())
    a1, m3, e, c256, s256 = (jnp.asarray(t).astype(BF16) for t in _dft_tables())
    sel_qk, sel_v = (jnp.asarray(t).astype(BF16) for t in _head_selectors())
    sel_v_t = sel_v.T
    tz_src, tz_mask = _na_toeplitz_tables()
    col_scale = np.ones((1, 1, W_IN), np.float32)
    col_scale[..., W_NQ:W_NK] = LOG2E * ND ** -0.5
    w_in_b = (w_in * jnp.asarray(col_scale)).astype(BF16)
    w_out_b = w_out.astype(BF16)
    w1b, w3b, w2b = w1.astype(BF16), w3.astype(BF16), w2.astype(BF16)

    cvec = jnp.zeros((8, D), F32).at[0:B].set(c).at[B].set(c_ctx)
    mods_all = _ada(cvec, w_ada, b_ada).reshape(DEPTH, 8, 6, D)

    xl = x.reshape(B * L, D)
    xc = ctx.reshape(B * LC, D)
    for i in range(DEPTH):
        last = i == DEPTH - 1
        mods = mods_all[i]
        gain = _take_padded(ret_norm_g[i], src_g, 0).reshape(1, RH * LANE)
        wfour = w_four[i].astype(BF16)
        lg = jax.nn.log_sigmoid(ret_decay_logit[i].astype(F32))
        tz_base = jnp.pad(_take_padded(na_rpb[i].astype(F32), tz_src, 2), ((0, 0), (0, 1), (0, 0)))
        na_bias = _na_bias(tz_base, jnp.asarray(tz_mask))

        f_l, p_l = _proj(xl, mods, g_mix[i], w_in_b, i, sel_qk, sel_v, rope_tabs, ctx=False)
        f_c, p_c = _proj(xc, mods, g_mix[i], w_in_b, i, sel_qk, sel_v, None, ctx=True)
        s0 = _ctx_state(p_c, lg)
        four_l = _fourier(f_l, wfour, a1, m3, e)
        ret_l = _retention(p_l, s0, lg, gain)
        na_l = _natten(p_l, p_c, na_bias)
        if not last:
            four_c, ret_c, na_c = _ctx_mix(f_c, p_c, lg, gain, wfour, c256, s256, e)
            xc = _ffn(xc, four_c, ret_c, na_c, mods, g_ffn[i], w_out_b, sel_v_t, w1b, w3b, w2b, i, None, ctx=True)
        xl = _ffn(xl, four_l, ret_l, na_l, mods, g_ffn[i], w_out_b, sel_v_t, w1b, w3b, w2b, i,
                  g_final if last else None, ctx=False)
    return xl.reshape(B, L, D)
```

```python
import functools

import numpy as np
import jax
import jax.numpy as jnp
from jax import lax
from jax.experimental import pallas as pl
from jax.experimental.pallas import tpu as pltpu

F32 = jnp.float32
BF16 = jnp.bfloat16

D = 1024
B = 2
L = 8192
LC = 256
DEPTH = 2
GRID_W = 64
GRID_H = L // GRID_W
FW = 256
FGW = 64
RH = 4
RD = 96
NH = 6
ND = 64
NPAIR = NH // 2
DFF = 2816
EPS = 1e-6
NEG = -1e30
LOG2E = float(np.log2(np.e))
LANE = 128

C_RQ, C_RK, C_RV, C_RG = 0, 512, 1024, 1536
C_NQ, C_NK, C_NV = 2048, 2432, 2816
NP = 3200
W_RQ, W_RK, W_RV, W_RG = 256, 640, 1024, 1408
W_NQ, W_NK, W_NV = 1792, 2176, 2560
W_IN = 2944

TM = 512
TM_PROJ = 1024
RET_C = 256
NA_R = 4
NA_KR = NA_R + 8
NA_NB = GRID_H // NA_R
DFT_N1 = 64
DFT_N2 = 128
F1_R = 16
F3_K = 8
VMEM_LIMIT = 56 * 1024 * 1024


def _qk_lane_dims():
    m = np.full((LANE,), -1, np.int64)
    m[0:24] = np.arange(0, 24)
    m[24:48] = np.arange(48, 72)
    m[64:88] = np.arange(24, 48)
    m[88:112] = np.arange(72, 96)
    return m


def _v_lane_dims():
    m = np.full((LANE,), -1, np.int64)
    m[:RD] = np.arange(RD)
    return m


def _head_selectors():
    sels = []
    for lanes in (_qk_lane_dims(), _v_lane_dims()):
        sel = np.zeros((RH * RD, RH * LANE), np.float32)
        ok = np.nonzero(lanes >= 0)[0]
        for h in range(RH):
            sel[RD * h + lanes[ok], LANE * h + ok] = 1.0
        sels.append(sel)
    return sels


def _gain_sources():
    src = np.full((RH * LANE,), -1, np.int64)
    for h in range(RH):
        src[LANE * h: LANE * h + RD] = RD * h + np.arange(RD)
    return src


def _rope_tables():
    pos = np.arange(L)
    prow, pcol = pos // GRID_W, pos % GRID_W
    half = RD // 4
    inv = 10000.0 ** (-np.arange(half, dtype=np.float64) / half)
    ar = prow[:, None] * inv[None, :]
    ac = pcol[:, None] * inv[None, :]
    cos = np.ones((L, LANE), np.float64)
    sin = np.zeros((L, LANE), np.float64)
    for off, sign in ((0, -1.0), (64, 1.0)):
        cos[:, off:off + 24] = np.cos(ar)
        cos[:, off + 24:off + 48] = np.cos(ac)
        sin[:, off:off + 24] = sign * np.sin(ar)
        sin[:, off + 24:off + 48] = sign * np.sin(ac)
    return cos.astype(np.float32), sin.astype(np.float32)


def _dft_tables():
    k1 = np.arange(DFT_N1)
    a = 2 * np.pi * ((k1[:, None] * k1[None, :]) % DFT_N1) / DFT_N1
    a1 = np.kron(np.concatenate([np.cos(a), -np.sin(a)], axis=0), np.eye(F1_R))
    k2 = np.arange(DFT_N2)
    l2 = np.arange(DFT_N2)
    kk = k1[:, None, None] + DFT_N1 * k2[None, :, None]
    ang = 2 * np.pi * ((kk * l2[None, None, :]) % L) / L
    ct, st = np.cos(ang) / np.sqrt(L), np.sin(ang) / np.sqrt(L)
    m3 = np.concatenate([np.concatenate([ct, st], axis=2), np.concatenate([st, -ct], axis=2)], axis=1)
    c = np.arange(FW)
    same = (c[:, None] // FGW) == (c[None, :] // FGW)
    ac = 2 * np.pi * (((c[:, None] % FGW) * (c[None, :] % FGW)) % FGW) / FGW
    e = np.concatenate([np.where(same, np.cos(ac), 0.0), -np.where(same, np.sin(ac), 0.0)], axis=0) / 8.0
    p = np.arange(LC)
    ap = 2 * np.pi * ((p[:, None] * p[None, :]) % LC) / LC
    return tuple(t.astype(np.float32) for t in (a1, m3, e, np.cos(ap) / 16.0, np.sin(ap) / 16.0))


NA_DR = 16


def _na_toeplitz_tables():
    src = np.full((LANE,), -1, np.int64)
    src[0:16] = np.arange(0, 16) + 15
    src[113:128] = np.arange(113, 128) - 113
    src[49:80] = np.arange(49, 80) - 49
    qc = np.arange(GRID_W)[:, None]
    kc = np.arange(LANE)[None, :] % GRID_W
    ws = np.clip(qc - 8, 0, GRID_W - 16)
    mask = np.where((kc >= ws) & (kc < ws + 16), 0.0, NEG).astype(np.float32)
    return src, mask


def _dot(a, b):
    return jnp.dot(a, b, preferred_element_type=F32)


def _dot_nt(a, b):
    return lax.dot_general(a, b, (((1,), (1,)), ((), ())), preferred_element_type=F32)


def _silu(x):
    return x * jax.nn.sigmoid(x)


def _rmsnorm(x, g):
    return x * lax.rsqrt(jnp.mean(x * x, axis=-1, keepdims=True) + EPS) * g


def _ada_kernel(c_ref, w_ref, b_ref, o_ref):
    s = _silu(c_ref[...]).astype(BF16)
    o_ref[0] = _dot(s, w_ref[0].astype(BF16)) + b_ref[0]


def _proj_kernel(*refs, rope, mod_row):
    if rope:
        x_ref, mods_ref, g_ref, w_ref, pqk_ref, pv_ref, cos_ref, sin_ref, f_ref, o_ref, wret_ref = refs
    else:
        x_ref, mods_ref, g_ref, w_ref, pqk_ref, pv_ref, f_ref, o_ref, wret_ref = refs
    s = pl.program_id(0)

    @pl.when(s == 0)
    def _():
        for slot, (src0, sel_ref) in enumerate(((W_RQ, pqk_ref), (W_RK, pqk_ref), (W_RV, pv_ref), (W_RG, pv_ref))):
            wret_ref[:, RH * LANE * slot:RH * LANE * (slot + 1)] = _dot(
                w_ref[:, src0:src0 + RH * RD], sel_ref[...]).astype(BF16)

    mod = mods_ref[mod_row(s)]
    hb = (_rmsnorm(x_ref[...], g_ref[...]) * (1.0 + mod[1:2, :]) + mod[0:1, :]).astype(BF16)
    f_ref[...] = _dot(hb, w_ref[:, :FW]).astype(BF16)
    k_scale = RD ** -0.5
    for c0, scale in ((C_RQ, None), (C_RK, k_scale)):
        t4 = _dot(hb, wret_ref[:, c0:c0 + RH * LANE])
        for hh in range(RH):
            t = t4[:, LANE * hh:LANE * (hh + 1)]
            if rope:
                t = t * cos_ref[...] + pltpu.roll(t, 64, 1) * sin_ref[...]
            if scale is not None:
                t = t * scale
            o_ref[:, c0 + LANE * hh:c0 + LANE * (hh + 1)] = t.astype(BF16)
    for c0 in (C_RV, C_RG):
        o_ref[:, c0:c0 + RH * LANE] = _dot(hb, wret_ref[:, c0:c0 + RH * LANE]).astype(BF16)
    o_ref[:, C_NQ:] = _dot(hb, w_ref[:, W_NQ:]).astype(BF16)


def _four1_kernel(a_ref, x_ref, y_ref):
    x = x_ref[0].reshape(DFT_N1 * F1_R, FW)
    y = _dot(a_ref[...], x).astype(BF16)
    y_ref[0] = y.reshape(2 * DFT_N1, F1_R, FW)


def _four3_kernel(m_ref, y_ref, e_ref, wf_ref, o_ref):
    ew = _dot(e_ref[...], wf_ref[...]).astype(BF16)
    for b in range(B):
        res = []
        for kk in range(F3_K):
            y = jnp.concatenate([y_ref[b, 0, kk], y_ref[b, 1, kk]], axis=0)
            z = _dot(m_ref[kk], y).astype(BF16)
            res.append(_dot(jnp.concatenate([z[:DFT_N2], z[DFT_N2:]], axis=1), ew))
        o_ref[b] = jnp.stack(res, axis=1)


def _decay_matrix(n, lgf, lgb):
    ii = lax.broadcasted_iota(jnp.int32, (n, n), 0)
    jj = lax.broadcasted_iota(jnp.int32, (n, n), 1)
    diff = (ii - jj).astype(F32)
    fwd = jnp.where(diff >= 0, jnp.exp(jnp.maximum(diff, 0.0) * lgf), 0.0)
    bwd = jnp.where(diff <= 0, jnp.exp(jnp.maximum(-diff, 0.0) * lgb), 0.0)
    return fwd + bwd


def _ret_readout(o, gate, gain):
    ms = jnp.sum(o * o, axis=-1, keepdims=True) * (1.0 / RD)
    return (o * lax.rsqrt(ms + EPS) * gain * _silu(gate.astype(F32))).astype(BF16)


def _ret_kernel(lg_ref, q_ref, k_ref, v_ref, g_ref, s0_ref, gain_ref, o_ref, acc_ref, stf_ref, stb_ref):
    h = pl.program_id(1)
    lgf = lg_ref[0, h]
    lgb = lg_ref[1, h]
    c = RET_C
    n = L // c
    dmat = _decay_matrix(c, lgf, lgb)
    idx = lax.broadcasted_iota(jnp.int32, (c, LANE), 0).astype(F32)
    qdf = jnp.exp((idx + 1.0) * lgf)
    kdf = jnp.exp((c - 1.0 - idx) * lgf)
    qdb = jnp.exp((c - idx) * lgb)
    kdb = jnp.exp(idx * lgb)
    cdf = jnp.exp(jnp.full((1, LANE), float(c), F32) * lgf)
    cdb = jnp.exp(jnp.full((1, LANE), float(c), F32) * lgb)
    gain = gain_ref[...]

    dirs = ((stf_ref, qdf, kdf, cdf), (stb_ref, qdb, kdb, cdb))
    stf_ref[...] = s0_ref[0, 0, 0]
    stb_ref[...] = s0_ref[0, 0, 1]

    def visit(r0, direction, second):
        st_ref, qdec, kdec, cdec = dirs[direction]
        rows = pl.ds(pl.multiple_of(r0, c), c)
        q = q_ref[rows, :]
        k = k_ref[rows, :]
        v = v_ref[rows, :]
        cross = _dot(q, st_ref[...].astype(BF16)) * qdec
        if second:
            o_ref[rows, :] = _ret_readout(acc_ref[rows, :] + cross, g_ref[rows, :], gain)
        else:
            s = _dot_nt(q, k) * dmat
            acc_ref[rows, :] = _dot(s.astype(BF16), v) + cross
        kt = (k.astype(F32) * kdec).T.astype(BF16)
        st_ref[...] = st_ref[...] * cdec + _dot(kt, v)

    def body(second):
        def step(i, carry):
            visit(i * c, 0, second)
            visit((n - 1 - i) * c, 1, second)
            return carry
        return step

    lax.fori_loop(0, n // 2, body(False), 0, unroll=4)
    lax.fori_loop(n // 2, n, body(True), 0, unroll=4)


def _ctx_state_kernel(lg_ref, k_ref, v_ref, o_ref):
    h = pl.program_id(1)
    idx = lax.broadcasted_iota(jnp.int32, (LC, LANE), 0).astype(F32)
    k = k_ref[...].astype(F32)
    v = v_ref[...]
    wf = jnp.exp((LC - 1.0 - idx) * lg_ref[0, h])
    wb = jnp.exp(idx * lg_ref[1, h])
    o_ref[0, 0, 0] = _dot((k * wf).T.astype(BF16), v)
    o_ref[0, 0, 1] = _dot((k * wb).T.astype(BF16), v)


def _pair_attention(q, score_fn, value_fn):
    lane = lax.broadcasted_iota(jnp.int32, q.shape, 1)
    outs = []
    for hh in range(2):
        sel = (lane >= ND * hh) & (lane < ND * (hh + 1))
        qm = jnp.where(sel, q, jnp.zeros_like(q))
        scores = score_fn(qm, hh)
        m = functools.reduce(jnp.maximum, [jnp.max(s, axis=-1, keepdims=True) for s in scores])
        ps = [jnp.exp2(s - m) for s in scores]
        den = functools.reduce(jnp.add, [jnp.sum(p, axis=-1, keepdims=True) for p in ps])
        outs.append(value_fn([p.astype(BF16) for p in ps]) / den)
    return jnp.where(lane < ND, outs[0], outs[1])


def _ctx_mix_kernel(lg_ref, f_ref, p_ref, gain_ref, wf_ref, c_ref, s_ref, e_ref, fo_ref, ro_ref, no_ref):
    f = f_ref[...]
    pr = _dot(c_ref[...], f)
    qi = _dot(s_ref[...], f)
    fr = _dot(pr.astype(BF16), e_ref[:FW, :]) + _dot(qi.astype(BF16), e_ref[FW:, :])
    fo_ref[...] = _dot(fr.astype(BF16), wf_ref[...]).astype(BF16)
    for h in range(RH):
        sl = lambda c0: slice(c0 + LANE * h, c0 + LANE * (h + 1))
        dmat = _decay_matrix(LC, lg_ref[0, h], lg_ref[1, h])
        s = _dot_nt(p_ref[:, sl(C_RQ)], p_ref[:, sl(C_RK)]) * dmat
        o = _dot(s.astype(BF16), p_ref[:, sl(C_RV)])
        ro_ref[:, LANE * h:LANE * (h + 1)] = _ret_readout(o, p_ref[:, sl(C_RG)], gain_ref[:, LANE * h:LANE * (h + 1)])
    for pair in range(NPAIR):
        sl = lambda c0: slice(c0 + LANE * pair, c0 + LANE * (pair + 1))
        k = p_ref[:, sl(C_NK)]
        v = p_ref[:, sl(C_NV)]
        out = _pair_attention(p_ref[:, sl(C_NQ)], lambda qm, hh: [_dot_nt(qm, k)], lambda ps: _dot(ps[0], v))
        no_ref[:, LANE * pair:LANE * (pair + 1)] = out.astype(BF16)


def _na_key_start(r0):
    return min(max(r0 - 4, 0), GRID_H - NA_KR)


def _na_bias_kernel(base_ref, mask_ref, o_ref, tz_ref):
    qc = lax.broadcasted_iota(jnp.int32, (GRID_W, LANE), 0)
    for dr in range(NA_DR - 1):
        t = jnp.broadcast_to(base_ref[0, dr:dr + 1, :], (GRID_W, LANE))
        for bit in range(6):
            t = jnp.where((qc >> bit) & 1 == 1, pltpu.roll(t, 1 << bit, 1), t)
        tz_ref[dr] = t * LOG2E + mask_ref[...]
    tz_ref[NA_DR - 1] = jnp.full((GRID_W, LANE), NEG, F32)
    low_half = lax.broadcasted_iota(jnp.int32, (GRID_W, LANE), 1) < GRID_W
    for var, r0 in enumerate((0, 2 * NA_R, GRID_H - NA_R)):
        ks = _na_key_start(r0)
        for rl in range(NA_R):
            r = r0 + rl
            rs = min(max(r - 4, 0), GRID_H - 8)
            slot = [kr - r + 7 if rs <= kr < rs + 8 else NA_DR - 1 for kr in range(ks, ks + NA_KR)]
            for m in range(NA_KR // 2):
                o_ref[0, var, GRID_W * rl:GRID_W * (rl + 1), LANE * m:LANE * (m + 1)] = jnp.where(
                    low_half, tz_ref[slot[2 * m]], tz_ref[slot[2 * m + 1]])


def _natten_kernel(q_ref, k_ref, v_ref, kc_ref, vc_ref, bias_ref, o_ref, pa_ref, da_ref, pb_ref, db_ref):
    nq = NA_R * GRID_W
    nk = NA_KR * GRID_W
    kc = kc_ref[...]
    vc = vc_ref[...]
    lane = lax.broadcasted_iota(jnp.int32, (nq, LANE), 1)

    def rows(i):
        ks = jnp.clip(NA_R * i - 4, 0, GRID_H - NA_KR)
        return pl.ds(pl.multiple_of(ks * GRID_W, nq), nk), pl.ds(pl.multiple_of(i * nq, nq), nq)

    def probabilities(i, p_ref, d_ref):
        i = jnp.minimum(i, NA_NB - 1)
        var = jnp.where(i == 0, 0, jnp.where(i == NA_NB - 1, 2, 1))
        krows, qrows = rows(i)
        q = q_ref[qrows, :]
        kw = k_ref[krows, :]
        for hh in range(2):
            qm = jnp.where((lane >= ND * hh) & (lane < ND * (hh + 1)), q, jnp.zeros_like(q))
            s_win = _dot_nt(qm, kw) + bias_ref[hh, var]
            s_ctx = _dot_nt(qm, kc)
            m = jnp.maximum(jnp.max(s_win, axis=-1, keepdims=True), jnp.max(s_ctx, axis=-1, keepdims=True))
            p_win = jnp.exp2(s_win - m)
            p_ctx = jnp.exp2(s_ctx - m)
            den = jnp.sum(p_win, axis=-1, keepdims=True) + jnp.sum(p_ctx, axis=-1, keepdims=True)
            p_ref[hh, :, :nk] = p_win.astype(BF16)
            p_ref[hh, :, nk:] = p_ctx.astype(BF16)
            d_ref[hh] = jnp.broadcast_to(den, (nq, LANE))

    def values(i, p_ref, d_ref):
        krows, qrows = rows(i)
        vals = jnp.concatenate([v_ref[krows, :], vc], axis=0)
        outs = [_dot(p_ref[hh], vals) / d_ref[hh] for hh in range(2)]
        o_ref[qrows, :] = jnp.where(lane < ND, outs[0], outs[1]).astype(BF16)

    probabilities(0, pa_ref, da_ref)

    def step(j, carry):
        values(2 * j, pa_ref, da_ref)
        probabilities(2 * j + 1, pb_ref, db_ref)
        values(2 * j + 1, pb_ref, db_ref)
        probabilities(2 * j + 2, pa_ref, da_ref)
        return carry

    lax.fori_loop(0, NA_NB // 2, step, 0)


def _ffn_kernel(*refs, final, mod_row):
    x_ref, f_ref, r_ref, n_ref, mods_ref, g_ref, wo_ref, sel_ref, w1_ref, w3_ref, w2_ref = refs[:11]
    gf_ref = refs[11] if final else None
    o_ref, acc_ref, wor_ref = refs[-3:]
    s = pl.program_id(0)

    @pl.when(s == 0)
    def _():
        wor_ref[...] = _dot(sel_ref[...], wo_ref[FW:FW + RH * RD, :]).astype(BF16)

    mod = mods_ref[mod_row(s)]
    y = (_dot(f_ref[...].astype(BF16), wo_ref[:FW, :]) + _dot(r_ref[...], wor_ref[...])
         + _dot(n_ref[...], wo_ref[FW + RH * RD:, :]))
    x1 = x_ref[...] + mod[2:3, :] * y
    hb = (_rmsnorm(x1, g_ref[...]) * (1.0 + mod[4:5, :]) + mod[3:4, :]).astype(BF16)
    fc = 256
    for c in range(DFF // fc):
        a = _dot(hb, w1_ref[:, c * fc:(c + 1) * fc])
        b = _dot(hb, w3_ref[:, c * fc:(c + 1) * fc])
        t = _dot((_silu(a) * b).astype(BF16), w2_ref[c * fc:(c + 1) * fc, :])
        if c == 0:
            acc_ref[...] = t
        else:
            acc_ref[...] += t
    x2 = x1 + mod[5:6, :] * acc_ref[...]
    if final:
        x2 = _rmsnorm(x2, gf_ref[...])
    o_ref[...] = x2


def _params(sem):
    return pltpu.CompilerParams(dimension_semantics=sem, vmem_limit_bytes=VMEM_LIMIT)


def _const_spec(shape):
    nd = len(shape)
    return pl.BlockSpec(shape, lambda *_: (0,) * nd, pipeline_mode=pl.Buffered(1))


def _smem_spec():
    return pl.BlockSpec(memory_space=pltpu.SMEM)


def _ada(cvec, w_ada, b_ada):
    tn = 1024
    return pl.pallas_call(
        _ada_kernel,
        grid=(DEPTH, 6 * D // tn),
        in_specs=[pl.BlockSpec((8, D), lambda i, j: (0, 0)),
                  pl.BlockSpec((1, D, tn), lambda i, j: (i, 0, j)),
                  pl.BlockSpec((1, 1, tn), lambda i, j: (i, 0, j))],
        out_specs=pl.BlockSpec((1, 8, tn), lambda i, j: (i, 0, j)),
        out_shape=jax.ShapeDtypeStruct((DEPTH, 8, 6 * D), F32),
        compiler_params=_params(("arbitrary", "arbitrary")),
        name="ada",
    )(cvec, w_ada, b_ada.reshape(DEPTH, 1, 6 * D))


def _mod_row(ctx, tm):
    per_batch = L // tm
    return (lambda tile: B) if ctx else (lambda tile: tile // per_batch)


def _layer_spec(shape, layer):
    nd = len(shape)
    return pl.BlockSpec((None,) + tuple(shape), lambda *_: (layer,) + (0,) * nd, pipeline_mode=pl.Buffered(1))


def _proj(x2d, mods, g, w_all, layer, sel_qk, sel_v, rope_tabs, *, ctx):
    rows = x2d.shape[0]
    tm = min(TM_PROJ, rows)
    per_batch = L // tm
    in_specs = [pl.BlockSpec((tm, D), lambda i: (i, 0)),
                _const_spec((8, 6, D)),
                _const_spec((1, D)),
                _layer_spec((D, W_IN), layer),
                _const_spec((RH * RD, RH * LANE)),
                _const_spec((RH * RD, RH * LANE))]
    args = [x2d, mods, g.reshape(1, D), w_all, sel_qk, sel_v]
    if not ctx:
        in_specs += [pl.BlockSpec((tm, LANE), lambda i: (i % per_batch, 0))] * 2
        args += list(rope_tabs)
    return pl.pallas_call(
        functools.partial(_proj_kernel, rope=not ctx, mod_row=_mod_row(ctx, tm)),
        grid=(rows // tm,),
        in_specs=in_specs,
        out_specs=[pl.BlockSpec((tm, FW), lambda i: (i, 0)), pl.BlockSpec((tm, NP), lambda i: (i, 0))],
        out_shape=[jax.ShapeDtypeStruct((rows, FW), BF16), jax.ShapeDtypeStruct((rows, NP), BF16)],
        scratch_shapes=[pltpu.VMEM((D, 4 * RH * LANE), BF16)],
        compiler_params=_params(("arbitrary",)),
        name="proj_ctx" if ctx else "proj",
    )(*args)


def _fourier(f2d, wf, a1, m3, e):
    y = pl.pallas_call(
        _four1_kernel,
        grid=(B, DFT_N2 // F1_R),
        in_specs=[_const_spec((2 * DFT_N1 * F1_R, DFT_N1 * F1_R)),
                  pl.BlockSpec((1, DFT_N1, F1_R, FW), lambda b, j: (b, 0, j, 0))],
        out_specs=pl.BlockSpec((1, 2 * DFT_N1, F1_R, FW), lambda b, j: (b, 0, j, 0)),
        out_shape=jax.ShapeDtypeStruct((B, 2 * DFT_N1, DFT_N2, FW), BF16),
        compiler_params=_params(("arbitrary", "arbitrary")),
        name="four1",
    )(a1, f2d.reshape(B, DFT_N1, DFT_N2, FW))
    out = pl.pallas_call(
        _four3_kernel,
        grid=(DFT_N1 // F3_K,),
        in_specs=[pl.BlockSpec((F3_K, 2 * DFT_N2, 2 * DFT_N2), lambda k: (k, 0, 0)),
                  pl.BlockSpec((B, 2, F3_K, DFT_N2, FW), lambda k: (0, 0, k, 0, 0)),
                  _const_spec((2 * FW, FW)),
                  _const_spec((FW, FW))],
        out_specs=pl.BlockSpec((B, DFT_N2, F3_K, FW), lambda k: (0, 0, k, 0)),
        out_shape=jax.ShapeDtypeStruct((B, DFT_N2, DFT_N1, FW), F32),
        compiler_params=_params(("arbitrary",)),
        name="four3",
    )(m3, y.reshape(B, 2, DFT_N1, DFT_N2, FW), e, wf)
    return out.reshape(B * L, FW)


def _retention(p, s0, lg, gain):
    col = lambda c0: (lambda b, h: (b, c0 // LANE + h))
    return pl.pallas_call(
        _ret_kernel,
        grid=(B, RH),
        in_specs=[_smem_spec(),
                  pl.BlockSpec((L, LANE), col(C_RQ)),
                  pl.BlockSpec((L, LANE), col(C_RK)),
                  pl.BlockSpec((L, LANE), col(C_RV)),
                  pl.BlockSpec((L, LANE), col(C_RG)),
                  pl.BlockSpec((1, 1, 2, LANE, LANE), lambda b, h: (b, h, 0, 0, 0)),
                  pl.BlockSpec((1, LANE), lambda b, h: (0, h))],
        out_specs=pl.BlockSpec((L, LANE), lambda b, h: (b, h)),
        out_shape=jax.ShapeDtypeStruct((B * L, RH * LANE), BF16),
        scratch_shapes=[pltpu.VMEM((L, LANE), F32), pltpu.VMEM((LANE, LANE), F32), pltpu.VMEM((LANE, LANE), F32)],
        compiler_params=_params(("arbitrary", "arbitrary")),
        name="ret",
    )(lg, p, p, p, p, s0, gain)


def _ctx_state(pc, lg):
    col = lambda c0: (lambda b, h: (b, c0 // LANE + h))
    return pl.pallas_call(
        _ctx_state_kernel,
        grid=(B, RH),
        in_specs=[_smem_spec(),
                  pl.BlockSpec((LC, LANE), col(C_RK)),
                  pl.BlockSpec((LC, LANE), col(C_RV))],
        out_specs=pl.BlockSpec((1, 1, 2, LANE, LANE), lambda b, h: (b, h, 0, 0, 0)),
        out_shape=jax.ShapeDtypeStruct((B, RH, 2, LANE, LANE), F32),
        compiler_params=_params(("arbitrary", "arbitrary")),
        name="ctx_state",
    )(lg, pc, pc)


def _ctx_mix(fc, pc, lg, gain, wf, c256, s256, e):
    return pl.pallas_call(
        _ctx_mix_kernel,
        grid=(B,),
        in_specs=[_smem_spec(),
                  pl.BlockSpec((LC, FW), lambda b: (b, 0)),
                  pl.BlockSpec((LC, NP), lambda b: (b, 0)),
                  _const_spec((1, RH * LANE)),
                  _const_spec((FW, FW)),
                  _const_spec((LC, LC)),
                  _const_spec((LC, LC)),
                  _const_spec((2 * FW, FW))],
        out_specs=[pl.BlockSpec((LC, FW), lambda b: (b, 0)),
                   pl.BlockSpec((LC, RH * LANE), lambda b: (b, 0)),
                   pl.BlockSpec((LC, NH * ND), lambda b: (b, 0))],
        out_shape=[jax.ShapeDtypeStruct((B * LC, FW), BF16),
                   jax.ShapeDtypeStruct((B * LC, RH * LANE), BF16),
                   jax.ShapeDtypeStruct((B * LC, NH * ND), BF16)],
        compiler_params=_params(("arbitrary",)),
        name="ctx_mix",
    )(lg, fc, pc, gain, wf, c256, s256, e)


def _na_bias(base, mask):
    nq, nk = NA_R * GRID_W, NA_KR * GRID_W
    return pl.pallas_call(
        _na_bias_kernel,
        grid=(NH,),
        in_specs=[pl.BlockSpec((1, NA_DR, LANE), lambda h: (h, 0, 0)),
                  _const_spec((GRID_W, LANE))],
        out_specs=pl.BlockSpec((1, 3, nq, nk), lambda h: (h, 0, 0, 0)),
        out_shape=jax.ShapeDtypeStruct((NH, 3, nq, nk), F32),
        scratch_shapes=[pltpu.VMEM((NA_DR, GRID_W, LANE), F32)],
        compiler_params=_params(("arbitrary",)),
        name="na_bias",
    )(base, mask)


def _natten(p, pc, bias):
    nq, nk = NA_R * GRID_W, NA_KR * GRID_W
    return pl.pallas_call(
        _natten_kernel,
        grid=(B, NPAIR),
        in_specs=[pl.BlockSpec((L, LANE), lambda b, pr: (b, C_NQ // LANE + pr)),
                  pl.BlockSpec((L, LANE), lambda b, pr: (b, C_NK // LANE + pr)),
                  pl.BlockSpec((L, LANE), lambda b, pr: (b, C_NV // LANE + pr)),
                  pl.BlockSpec((LC, LANE), lambda b, pr: (b, C_NK // LANE + pr)),
                  pl.BlockSpec((LC, LANE), lambda b, pr: (b, C_NV // LANE + pr)),
                  pl.BlockSpec((2, 3, nq, nk), lambda b, pr: (pr, 0, 0, 0))],
        out_specs=pl.BlockSpec((L, LANE), lambda b, pr: (b, pr)),
        out_shape=jax.ShapeDtypeStruct((B * L, NH * ND), BF16),
        scratch_shapes=[pltpu.VMEM((2, nq, nk + LC), BF16), pltpu.VMEM((2, nq, LANE), F32)] * 2,
        compiler_params=_params(("arbitrary", "arbitrary")),
        name="natten",
    )(p, p, p, pc, pc, bias)


def _ffn(x2d, four, ret, na, mods, g, wo_all, sel_v_t, w1_all, w3_all, w2_all, layer, g_final, *, ctx):
    rows = x2d.shape[0]
    final = g_final is not None
    row = lambda w: pl.BlockSpec((TM, w), lambda i: (i, 0))
    in_specs = [row(D), row(FW), row(RH * LANE), row(NH * ND),
                _const_spec((8, 6, D)), _const_spec((1, D)),
                _layer_spec((D, D), layer), _const_spec((RH * LANE, RH * RD)),
                _layer_spec((D, DFF), layer), _layer_spec((D, DFF), layer), _layer_spec((DFF, D), layer)]
    args = [x2d, four, ret, na, mods, g.reshape(1, D), wo_all, sel_v_t, w1_all, w3_all, w2_all]
    if final:
        in_specs.append(_const_spec((1, D)))
        args.append(g_final.reshape(1, D))
    return pl.pallas_call(
        functools.partial(_ffn_kernel, final=final, mod_row=_mod_row(ctx, TM)),
        grid=(rows // TM,),
        in_specs=in_specs,
        out_specs=row(D),
        out_shape=jax.ShapeDtypeStruct((rows, D), F32),
        scratch_shapes=[pltpu.VMEM((TM, D), F32), pltpu.VMEM((RH * LANE, D), BF16)],
        compiler_params=_params(("arbitrary",)),
        name="ffn_ctx" if ctx else ("ffn_final" if final else "ffn"),
    )(*args)


def _take_padded(a, src, axis):
    pieces = []
    i, n = 0, len(src)
    while i < n:
        j = i + 1
        if src[i] < 0:
            while j < n and src[j] < 0:
                j += 1
            shape = list(a.shape)
            shape[axis] = j - i
            pieces.append(jnp.zeros(shape, a.dtype))
        else:
            while j < n and src[j] == src[j - 1] + 1:
                j += 1
            pieces.append(lax.slice_in_dim(a, int(src[i]), int(src[j - 1]) + 1, axis=axis))
        i = j
    return jnp.concatenate(pieces, axis=axis)


def kernel(x, c, ctx, c_ctx, w_ada, b_ada, g_mix, w_in, ret_decay_logit, ret_norm_g, w_four,
           na_rpb, w_out, g_ffn, w1, w3, w2, g_final):
    src_g = _gain_sources()
    rope_tabs = tuple(jnp.asarray(t) for t in _rope_tables())
    a1, m3, e, c256, s256 = (jnp.asarray(t).astype(BF16) for t in _dft_tables())
    sel_qk, sel_v = (jnp.asarray(t).astype(BF16) for t in _head_selectors())
    sel_v_t = sel_v.T
    tz_src, tz_mask = _na_toeplitz_tables()
    col_scale = np.ones((1, 1, W_IN), np.float32)
    col_scale[..., W_NQ:W_NK] = LOG2E * ND ** -0.5
    w_in_b = (w_in * jnp.asarray(col_scale)).astype(BF16)
    w_out_b = w_out.astype(BF16)
    w1b, w3b, w2b = w1.astype(BF16), w3.astype(BF16), w2.astype(BF16)

    cvec = jnp.zeros((8, D), F32).at[0:B].set(c).at[B].set(c_ctx)
    mods_all = _ada(cvec, w_ada, b_ada).reshape(DEPTH, 8, 6, D)

    xl = x.reshape(B * L, D)
    xc = ctx.reshape(B * LC, D)
    for i in range(DEPTH):
        last = i == DEPTH - 1
        mods = mods_all[i]
        gain = _take_padded(ret_norm_g[i], src_g, 0).reshape(1, RH * LANE)
        wfour = w_four[i].astype(BF16)
        lg = jax.nn.log_sigmoid(ret_decay_logit[i].astype(F32))
        tz_base = jnp.pad(_take_padded(na_rpb[i].astype(F32), tz_src, 2), ((0, 0), (0, 1), (0, 0)))
        na_bias = _na_bias(tz_base, jnp.asarray(tz_mask))

        f_l, p_l = _proj(xl, mods, g_mix[i], w_in_b, i, sel_qk, sel_v, rope_tabs, ctx=False)
        f_c, p_c = _proj(xc, mods, g_mix[i], w_in_b, i, sel_qk, sel_v, None, ctx=True)
        s0 = _ctx_state(p_c, lg)
        four_l = _fourier(f_l, wfour, a1, m3, e)
        ret_l = _retention(p_l, s0, lg, gain)
        na_l = _natten(p_l, p_c, na_bias)
        if not last:
            four_c, ret_c, na_c = _ctx_mix(f_c, p_c, lg, gain, wfour, c256, s256, e)
            xc = _ffn(xc, four_c, ret_c, na_c, mods, g_ffn[i], w_out_b, sel_v_t, w1b, w3b, w2b, i, None, ctx=True)
        xl = _ffn(xl, four_l, ret_l, na_l, mods, g_ffn[i], w_out_b, sel_v_t, w1b, w3b, w2b, i,
                  g_final if last else None, ctx=False)
    return xl.reshape(B, L, D)
```

```python
import functools

import numpy as np
import jax
import jax.numpy as jnp
from jax import lax
from jax.experimental import pallas as pl
from jax.experimental.pallas import tpu as pltpu

F32 = jnp.float32
BF16 = jnp.bfloat16

D = 1024
B = 2
L = 8192
LC = 256
DEPTH = 2
GRID_W = 64
GRID_H = L // GRID_W
FW = 256
FGW = 64
RH = 4
RD = 96
NH = 6
ND = 64
NPAIR = NH // 2
DFF = 2816
EPS = 1e-6
NEG = -1e30
LOG2E = float(np.log2(np.e))
LANE = 128

C_RQ, C_RK, C_RV, C_RG = 0, 512, 1024, 1536
C_NQ, C_NK, C_NV = 2048, 2432, 2816
NP = 3200
W_RQ, W_RK, W_RV, W_RG = 256, 640, 1024, 1408
W_NQ, W_NK, W_NV = 1792, 2176, 2560
W_IN = 2944

TM = 512
TM_PROJ = 1024
RET_C = 256
NA_R = 4
NA_KR = NA_R + 8
NA_NB = GRID_H // NA_R
DFT_N1 = 64
DFT_N2 = 128
F1_R = 16
F3_K = 8
VMEM_LIMIT = 56 * 1024 * 1024


def _qk_lane_dims():
    m = np.full((LANE,), -1, np.int64)
    m[0:24] = np.arange(0, 24)
    m[24:48] = np.arange(48, 72)
    m[64:88] = np.arange(24, 48)
    m[88:112] = np.arange(72, 96)
    return m


def _v_lane_dims():
    m = np.full((LANE,), -1, np.int64)
    m[:RD] = np.arange(RD)
    return m


def _head_selectors():
    sels = []
    for lanes in (_qk_lane_dims(), _v_lane_dims()):
        sel = np.zeros((RH * RD, RH * LANE), np.float32)
        ok = np.nonzero(lanes >= 0)[0]
        for h in range(RH):
            sel[RD * h + lanes[ok], LANE * h + ok] = 1.0
        sels.append(sel)
    return sels


def _gain_sources():
    src = np.full((RH * LANE,), -1, np.int64)
    for h in range(RH):
        src[LANE * h: LANE * h + RD] = RD * h + np.arange(RD)
    return src


def _rope_tables():
    pos = np.arange(L)
    prow, pcol = pos // GRID_W, pos % GRID_W
    half = RD // 4
    inv = 10000.0 ** (-np.arange(half, dtype=np.float64) / half)
    ar = prow[:, None] * inv[None, :]
    ac = pcol[:, None] * inv[None, :]
    cos = np.ones((L, LANE), np.float64)
    sin = np.zeros((L, LANE), np.float64)
    for off, sign in ((0, -1.0), (64, 1.0)):
        cos[:, off:off + 24] = np.cos(ar)
        cos[:, off + 24:off + 48] = np.cos(ac)
        sin[:, off:off + 24] = sign * np.sin(ar)
        sin[:, off + 24:off + 48] = sign * np.sin(ac)
    return cos.astype(np.float32), sin.astype(np.float32)


def _dft_tables():
    k1 = np.arange(DFT_N1)
    a = 2 * np.pi * ((k1[:, None] * k1[None, :]) % DFT_N1) / DFT_N1
    a1 = np.kron(np.concatenate([np.cos(a), -np.sin(a)], axis=0), np.eye(F1_R))
    k2 = np.arange(DFT_N2)
    l2 = np.arange(DFT_N2)
    kk = k1[:, None, None] + DFT_N1 * k2[None, :, None]
    ang = 2 * np.pi * ((kk * l2[None, None, :]) % L) / L
    ct, st = np.cos(ang) / np.sqrt(L), np.sin(ang) / np.sqrt(L)
    m3 = np.concatenate([np.concatenate([ct, st], axis=2), np.concatenate([st, -ct], axis=2)], axis=1)
    c = np.arange(FW)
    same = (c[:, None] // FGW) == (c[None, :] // FGW)
    ac = 2 * np.pi * (((c[:, None] % FGW) * (c[None, :] % FGW)) % FGW) / FGW
    e = np.concatenate([np.where(same, np.cos(ac), 0.0), -np.where(same, np.sin(ac), 0.0)], axis=0) / 8.0
    p = np.arange(LC)
    ap = 2 * np.pi * ((p[:, None] * p[None, :]) % LC) / LC
    return tuple(t.astype(np.float32) for t in (a1, m3, e, np.cos(ap) / 16.0, np.sin(ap) / 16.0))


NA_DR = 16


def _na_toeplitz_tables():
    src = np.full((LANE,), -1, np.int64)
    src[0:16] = np.arange(0, 16) + 15
    src[113:128] = np.arange(113, 128) - 113
    src[49:80] = np.arange(49, 80) - 49
    qc = np.arange(GRID_W)[:, None]
    kc = np.arange(LANE)[None, :] % GRID_W
    ws = np.clip(qc - 8, 0, GRID_W - 16)
    mask = np.where((kc >= ws) & (kc < ws + 16), 0.0, NEG).astype(np.float32)
    return src, mask


def _dot(a, b):
    return jnp.dot(a, b, preferred_element_type=F32)


def _dot_nt(a, b):
    return lax.dot_general(a, b, (((1,), (1,)), ((), ())), preferred_element_type=F32)


def _silu(x):
    return x * jax.nn.sigmoid(x)


def _rmsnorm(x, g):
    return x * lax.rsqrt(jnp.mean(x * x, axis=-1, keepdims=True) + EPS) * g


def _ada_kernel(c_ref, w_ref, b_ref, o_ref):
    s = _silu(c_ref[...]).astype(BF16)
    o_ref[0] = _dot(s, w_ref[0].astype(BF16)) + b_ref[0]


def _proj_kernel(*refs, rope, mod_row):
    if rope:
        x_ref, mods_ref, g_ref, w_ref, pqk_ref, pv_ref, cos_ref, sin_ref, f_ref, o_ref, wret_ref = refs
    else:
        x_ref, mods_ref, g_ref, w_ref, pqk_ref, pv_ref, f_ref, o_ref, wret_ref = refs
    s = pl.program_id(0)

    @pl.when(s == 0)
    def _():
        for slot, (src0, sel_ref) in enumerate(((W_RQ, pqk_ref), (W_RK, pqk_ref), (W_RV, pv_ref), (W_RG, pv_ref))):
            wret_ref[:, RH * LANE * slot:RH * LANE * (slot + 1)] = _dot(
                w_ref[:, src0:src0 + RH * RD], sel_ref[...]).astype(BF16)

    mod = mods_ref[mod_row(s)]
    hb = (_rmsnorm(x_ref[...], g_ref[...]) * (1.0 + mod[1:2, :]) + mod[0:1, :]).astype(BF16)
    f_ref[...] = _dot(hb, w_ref[:, :FW]).astype(BF16)
    k_scale = RD ** -0.5
    for c0, scale in ((C_RQ, None), (C_RK, k_scale)):
        t4 = _dot(hb, wret_ref[:, c0:c0 + RH * LANE])
        for hh in range(RH):
            t = t4[:, LANE * hh:LANE * (hh + 1)]
            if rope:
                t = t * cos_ref[...] + pltpu.roll(t, 64, 1) * sin_ref[...]
            if scale is not None:
                t = t * scale
            o_ref[:, c0 + LANE * hh:c0 + LANE * (hh + 1)] = t.astype(BF16)
    for c0 in (C_RV, C_RG):
        o_ref[:, c0:c0 + RH * LANE] = _dot(hb, wret_ref[:, c0:c0 + RH * LANE]).astype(BF16)
    o_ref[:, C_NQ:] = _dot(hb, w_ref[:, W_NQ:]).astype(BF16)


def _four1_kernel(a_ref, x_ref, y_ref):
    x = x_ref[0].reshape(DFT_N1 * F1_R, FW)
    y = _dot(a_ref[...], x).astype(BF16)
    y_ref[0] = y.reshape(2 * DFT_N1, F1_R, FW)


def _four3_kernel(m_ref, y_ref, e_ref, wf_ref, o_ref):
    ew = _dot(e_ref[...], wf_ref[...]).astype(BF16)
    for b in range(B):
        res = []
        for kk in range(F3_K):
            y = jnp.concatenate([y_ref[b, 0, kk], y_ref[b, 1, kk]], axis=0)
            z = _dot(m_ref[kk], y).astype(BF16)
            res.append(_dot(jnp.concatenate([z[:DFT_N2], z[DFT_N2:]], axis=1), ew))
        o_ref[b] = jnp.stack(res, axis=1)


def _decay_matrix(n, lgf, lgb):
    ii = lax.broadcasted_iota(jnp.int32, (n, n), 0)
    jj = lax.broadcasted_iota(jnp.int32, (n, n), 1)
    diff = (ii - jj).astype(F32)
    fwd = jnp.where(diff >= 0, jnp.exp(jnp.maximum(diff, 0.0) * lgf), 0.0)
    bwd = jnp.where(diff <= 0, jnp.exp(jnp.maximum(-diff, 0.0) * lgb), 0.0)
    return fwd + bwd


def _ret_readout(o, gate, gain):
    ms = jnp.sum(o * o, axis=-1, keepdims=True) * (1.0 / RD)
    return (o * lax.rsqrt(ms + EPS) * gain * _silu(gate.astype(F32))).astype(BF16)


def _ret_kernel(lg_ref, q_ref, k_ref, v_ref, g_ref, s0_ref, gain_ref, o_ref, acc_ref, stf_ref, stb_ref):
    h = pl.program_id(1)
    lgf = lg_ref[0, h]
    lgb = lg_ref[1, h]
    c = RET_C
    n = L // c
    dmat = _decay_matrix(c, lgf, lgb)
    idx = lax.broadcasted_iota(jnp.int32, (c, LANE), 0).astype(F32)
    qdf = jnp.exp((idx + 1.0) * lgf)
    kdf = jnp.exp((c - 1.0 - idx) * lgf)
    qdb = jnp.exp((c - idx) * lgb)
    kdb = jnp.exp(idx * lgb)
    cdf = jnp.exp(jnp.full((1, LANE), float(c), F32) * lgf)
    cdb = jnp.exp(jnp.full((1, LANE), float(c), F32) * lgb)
    gain = gain_ref[...]

    dirs = ((stf_ref, qdf, kdf, cdf), (stb_ref, qdb, kdb, cdb))
    stf_ref[...] = s0_ref[0, 0, 0]
    stb_ref[...] = s0_ref[0, 0, 1]

    def visit(r0, direction, second):
        st_ref, qdec, kdec, cdec = dirs[direction]
        rows = pl.ds(pl.multiple_of(r0, c), c)
        q = q_ref[rows, :]
        k = k_ref[rows, :]
        v = v_ref[rows, :]
        cross = _dot(q, st_ref[...].astype(BF16)) * qdec
        if second:
            o_ref[rows, :] = _ret_readout(acc_ref[rows, :] + cross, g_ref[rows, :], gain)
        else:
            s = _dot_nt(q, k) * dmat
            acc_ref[rows, :] = _dot(s.astype(BF16), v) + cross
        kt = (k.astype(F32) * kdec).T.astype(BF16)
        st_ref[...] = st_ref[...] * cdec + _dot(kt, v)

    def body(second):
        def step(i, carry):
            visit(i * c, 0, second)
            visit((n - 1 - i) * c, 1, second)
            return carry
        return step

    lax.fori_loop(0, n // 2, body(False), 0, unroll=4)
    lax.fori_loop(n // 2, n, body(True), 0, unroll=4)


def _ctx_state_kernel(lg_ref, k_ref, v_ref, o_ref):
    h = pl.program_id(1)
    idx = lax.broadcasted_iota(jnp.int32, (LC, LANE), 0).astype(F32)
    k = k_ref[...].astype(F32)
    v = v_ref[...]
    wf = jnp.exp((LC - 1.0 - idx) * lg_ref[0, h])
    wb = jnp.exp(idx * lg_ref[1, h])
    o_ref[0, 0, 0] = _dot((k * wf).T.astype(BF16), v)
    o_ref[0, 0, 1] = _dot((k * wb).T.astype(BF16), v)


def _pair_attention(q, score_fn, value_fn):
    lane = lax.broadcasted_iota(jnp.int32, q.shape, 1)
    outs = []
    for hh in range(2):
        sel = (lane >= ND * hh) & (lane < ND * (hh + 1))
        qm = jnp.where(sel, q, jnp.zeros_like(q))
        scores = score_fn(qm, hh)
        m = functools.reduce(jnp.maximum, [jnp.max(s, axis=-1, keepdims=True) for s in scores])
        ps = [jnp.exp2(s - m) for s in scores]
        den = functools.reduce(jnp.add, [jnp.sum(p, axis=-1, keepdims=True) for p in ps])
        outs.append(value_fn([p.astype(BF16) for p in ps]) / den)
    return jnp.where(lane < ND, outs[0], outs[1])


def _ctx_mix_kernel(lg_ref, f_ref, p_ref, gain_ref, wf_ref, c_ref, s_ref, e_ref, fo_ref, ro_ref, no_ref):
    f = f_ref[...]
    pr = _dot(c_ref[...], f)
    qi = _dot(s_ref[...], f)
    fr = _dot(pr.astype(BF16), e_ref[:FW, :]) + _dot(qi.astype(BF16), e_ref[FW:, :])
    fo_ref[...] = _dot(fr.astype(BF16), wf_ref[...]).astype(BF16)
    for h in range(RH):
        sl = lambda c0: slice(c0 + LANE * h, c0 + LANE * (h + 1))
        dmat = _decay_matrix(LC, lg_ref[0, h], lg_ref[1, h])
        s = _dot_nt(p_ref[:, sl(C_RQ)], p_ref[:, sl(C_RK)]) * dmat
        o = _dot(s.astype(BF16), p_ref[:, sl(C_RV)])
        ro_ref[:, LANE * h:LANE * (h + 1)] = _ret_readout(o, p_ref[:, sl(C_RG)], gain_ref[:, LANE * h:LANE * (h + 1)])
    for pair in range(NPAIR):
        sl = lambda c0: slice(c0 + LANE * pair, c0 + LANE * (pair + 1))
        k = p_ref[:, sl(C_NK)]
        v = p_ref[:, sl(C_NV)]
        out = _pair_attention(p_ref[:, sl(C_NQ)], lambda qm, hh: [_dot_nt(qm, k)], lambda ps: _dot(ps[0], v))
        no_ref[:, LANE * pair:LANE * (pair + 1)] = out.astype(BF16)


def _na_key_start(r0):
    return min(max(r0 - 4, 0), GRID_H - NA_KR)


def _na_bias_kernel(base_ref, mask_ref, o_ref, tz_ref):
    qc = lax.broadcasted_iota(jnp.int32, (GRID_W, LANE), 0)
    for dr in range(NA_DR - 1):
        t = jnp.broadcast_to(base_ref[0, dr:dr + 1, :], (GRID_W, LANE))
        for bit in range(6):
            t = jnp.where((qc >> bit) & 1 == 1, pltpu.roll(t, 1 << bit, 1), t)
        tz_ref[dr] = t * LOG2E + mask_ref[...]
    tz_ref[NA_DR - 1] = jnp.full((GRID_W, LANE), NEG, F32)
    low_half = lax.broadcasted_iota(jnp.int32, (GRID_W, LANE), 1) < GRID_W
    for var, r0 in enumerate((0, 2 * NA_R, GRID_H - NA_R)):
        ks = _na_key_start(r0)
        for rl in range(NA_R):
            r = r0 + rl
            rs = min(max(r - 4, 0), GRID_H - 8)
            slot = [kr - r + 7 if rs <= kr < rs + 8 else NA_DR - 1 for kr in range(ks, ks + NA_KR)]
            for m in range(NA_KR // 2):
                o_ref[0, var, GRID_W * rl:GRID_W * (rl + 1), LANE * m:LANE * (m + 1)] = jnp.where(
                    low_half, tz_ref[slot[2 * m]], tz_ref[slot[2 * m + 1]])


def _natten_kernel(q_ref, k_ref, v_ref, kc_ref, vc_ref, bias_ref, o_ref, pa_ref, pb_ref):
    nq = NA_R * GRID_W
    nk = NA_KR * GRID_W
    kc = kc_ref[...]
    vc = vc_ref[...]
    lane = lax.broadcasted_iota(jnp.int32, (nq, LANE), 1)

    def rows(i):
        ks = jnp.clip(NA_R * i - 4, 0, GRID_H - NA_KR)
        return pl.ds(pl.multiple_of(ks * GRID_W, nq), nk), pl.ds(pl.multiple_of(i * nq, nq), nq)

    def probabilities(i, p_ref):
        i = jnp.minimum(i, NA_NB - 1)
        var = jnp.where(i == 0, 0, jnp.where(i == NA_NB - 1, 2, 1))
        krows, qrows = rows(i)
        q = q_ref[qrows, :]
        kw = k_ref[krows, :]
        for hh in range(2):
            qm = jnp.where((lane >= ND * hh) & (lane < ND * (hh + 1)), q, jnp.zeros_like(q))
            s_win = _dot_nt(qm, kw) + bias_ref[hh, var]
            s_ctx = _dot_nt(qm, kc)
            m = jnp.maximum(jnp.max(s_win, axis=-1, keepdims=True), jnp.max(s_ctx, axis=-1, keepdims=True))
            p_ref[hh, :, :nk] = jnp.exp2(s_win - m).astype(BF16)
            p_ref[hh, :, nk:] = jnp.exp2(s_ctx - m).astype(BF16)

    def values(i, p_ref):
        krows, qrows = rows(i)
        vals = jnp.concatenate([v_ref[krows, :], vc], axis=0)
        vals = jnp.concatenate([vals, jnp.ones_like(vals)], axis=1)
        o2 = [_dot(p_ref[hh], vals) for hh in range(2)]
        outs = [o[:, :LANE] / o[:, LANE:] for o in o2]
        o_ref[qrows, :] = jnp.where(lane < ND, outs[0], outs[1]).astype(BF16)

    probabilities(0, pa_ref)

    def step(j, carry):
        values(2 * j, pa_ref)
        probabilities(2 * j + 1, pb_ref)
        values(2 * j + 1, pb_ref)
        probabilities(2 * j + 2, pa_ref)
        return carry

    lax.fori_loop(0, NA_NB // 2, step, 0)


def _ffn_kernel(*refs, final, mod_row):
    x_ref, f_ref, r_ref, n_ref, mods_ref, g_ref, wo_ref, sel_ref, w1_ref, w3_ref, w2_ref = refs[:11]
    gf_ref = refs[11] if final else None
    o_ref, acc_ref, wor_ref = refs[-3:]
    s = pl.program_id(0)

    @pl.when(s == 0)
    def _():
        wor_ref[...] = _dot(sel_ref[...], wo_ref[FW:FW + RH * RD, :]).astype(BF16)

    mod = mods_ref[mod_row(s)]
    y = (_dot(f_ref[...].astype(BF16), wo_ref[:FW, :]) + _dot(r_ref[...], wor_ref[...])
         + _dot(n_ref[...], wo_ref[FW + RH * RD:, :]))
    x1 = x_ref[...] + mod[2:3, :] * y
    hb = (_rmsnorm(x1, g_ref[...]) * (1.0 + mod[4:5, :]) + mod[3:4, :]).astype(BF16)
    fc = 256
    for c in range(DFF // fc):
        a = _dot(hb, w1_ref[:, c * fc:(c + 1) * fc])
        b = _dot(hb, w3_ref[:, c * fc:(c + 1) * fc])
        t = _dot((_silu(a) * b).astype(BF16), w2_ref[c * fc:(c + 1) * fc, :])
        if c == 0:
            acc_ref[...] = t
        else:
            acc_ref[...] += t
    x2 = x1 + mod[5:6, :] * acc_ref[...]
    if final:
        x2 = _rmsnorm(x2, gf_ref[...])
    o_ref[...] = x2


def _params(sem):
    return pltpu.CompilerParams(dimension_semantics=sem, vmem_limit_bytes=VMEM_LIMIT)


def _const_spec(shape):
    nd = len(shape)
    return pl.BlockSpec(shape, lambda *_: (0,) * nd, pipeline_mode=pl.Buffered(1))


def _smem_spec():
    return pl.BlockSpec(memory_space=pltpu.SMEM)


def _ada(cvec, w_ada, b_ada):
    tn = 1024
    return pl.pallas_call(
        _ada_kernel,
        grid=(DEPTH, 6 * D // tn),
        in_specs=[pl.BlockSpec((8, D), lambda i, j: (0, 0)),
                  pl.BlockSpec((1, D, tn), lambda i, j: (i, 0, j)),
                  pl.BlockSpec((1, 1, tn), lambda i, j: (i, 0, j))],
        out_specs=pl.BlockSpec((1, 8, tn), lambda i, j: (i, 0, j)),
        out_shape=jax.ShapeDtypeStruct((DEPTH, 8, 6 * D), F32),
        compiler_params=_params(("arbitrary", "arbitrary")),
        name="ada",
    )(cvec, w_ada, b_ada.reshape(DEPTH, 1, 6 * D))


def _mod_row(ctx, tm):
    per_batch = L // tm
    return (lambda tile: B) if ctx else (lambda tile: tile // per_batch)


def _layer_spec(shape, layer):
    nd = len(shape)
    return pl.BlockSpec((None,) + tuple(shape), lambda *_: (layer,) + (0,) * nd, pipeline_mode=pl.Buffered(1))


def _proj(x2d, mods, g, w_all, layer, sel_qk, sel_v, rope_tabs, *, ctx):
    rows = x2d.shape[0]
    tm = min(TM_PROJ, rows)
    per_batch = L // tm
    in_specs = [pl.BlockSpec((tm, D), lambda i: (i, 0)),
                _const_spec((8, 6, D)),
                _const_spec((1, D)),
                _layer_spec((D, W_IN), layer),
                _const_spec((RH * RD, RH * LANE)),
                _const_spec((RH * RD, RH * LANE))]
    args = [x2d, mods, g.reshape(1, D), w_all, sel_qk, sel_v]
    if not ctx:
        in_specs += [pl.BlockSpec((tm, LANE), lambda i: (i % per_batch, 0))] * 2
        args += list(rope_tabs)
    return pl.pallas_call(
        functools.partial(_proj_kernel, rope=not ctx, mod_row=_mod_row(ctx, tm)),
        grid=(rows // tm,),
        in_specs=in_specs,
        out_specs=[pl.BlockSpec((tm, FW), lambda i: (i, 0)), pl.BlockSpec((tm, NP), lambda i: (i, 0))],
        out_shape=[jax.ShapeDtypeStruct((rows, FW), BF16), jax.ShapeDtypeStruct((rows, NP), BF16)],
        scratch_shapes=[pltpu.VMEM((D, 4 * RH * LANE), BF16)],
        compiler_params=_params(("arbitrary",)),
        name="proj_ctx" if ctx else "proj",
    )(*args)


def _fourier(f2d, wf, a1, m3, e):
    y = pl.pallas_call(
        _four1_kernel,
        grid=(B, DFT_N2 // F1_R),
        in_specs=[_const_spec((2 * DFT_N1 * F1_R, DFT_N1 * F1_R)),
                  pl.BlockSpec((1, DFT_N1, F1_R, FW), lambda b, j: (b, 0, j, 0))],
        out_specs=pl.BlockSpec((1, 2 * DFT_N1, F1_R, FW), lambda b, j: (b, 0, j, 0)),
        out_shape=jax.ShapeDtypeStruct((B, 2 * DFT_N1, DFT_N2, FW), BF16),
        compiler_params=_params(("arbitrary", "arbitrary")),
        name="four1",
    )(a1, f2d.reshape(B, DFT_N1, DFT_N2, FW))
    out = pl.pallas_call(
        _four3_kernel,
        grid=(DFT_N1 // F3_K,),
        in_specs=[pl.BlockSpec((F3_K, 2 * DFT_N2, 2 * DFT_N2), lambda k: (k, 0, 0)),
                  pl.BlockSpec((B, 2, F3_K, DFT_N2, FW), lambda k: (0, 0, k, 0, 0)),
                  _const_spec((2 * FW, FW)),
                  _const_spec((FW, FW))],
        out_specs=pl.BlockSpec((B, DFT_N2, F3_K, FW), lambda k: (0, 0, k, 0)),
        out_shape=jax.ShapeDtypeStruct((B, DFT_N2, DFT_N1, FW), F32),
        compiler_params=_params(("arbitrary",)),
        name="four3",
    )(m3, y.reshape(B, 2, DFT_N1, DFT_N2, FW), e, wf)
    return out.reshape(B * L, FW)


def _retention(p, s0, lg, gain):
    col = lambda c0: (lambda b, h: (b, c0 // LANE + h))
    return pl.pallas_call(
        _ret_kernel,
        grid=(B, RH),
        in_specs=[_smem_spec(),
                  pl.BlockSpec((L, LANE), col(C_RQ)),
                  pl.BlockSpec((L, LANE), col(C_RK)),
                  pl.BlockSpec((L, LANE), col(C_RV)),
                  pl.BlockSpec((L, LANE), col(C_RG)),
                  pl.BlockSpec((1, 1, 2, LANE, LANE), lambda b, h: (b, h, 0, 0, 0)),
                  pl.BlockSpec((1, LANE), lambda b, h: (0, h))],
        out_specs=pl.BlockSpec((L, LANE), lambda b, h: (b, h)),
        out_shape=jax.ShapeDtypeStruct((B * L, RH * LANE), BF16),
        scratch_shapes=[pltpu.VMEM((L, LANE), F32), pltpu.VMEM((LANE, LANE), F32), pltpu.VMEM((LANE, LANE), F32)],
        compiler_params=_params(("arbitrary", "arbitrary")),
        name="ret",
    )(lg, p, p, p, p, s0, gain)


def _ctx_state(pc, lg):
    col = lambda c0: (lambda b, h: (b, c0 // LANE + h))
    return pl.pallas_call(
        _ctx_state_kernel,
        grid=(B, RH),
        in_specs=[_smem_spec(),
                  pl.BlockSpec((LC, LANE), col(C_RK)),
                  pl.BlockSpec((LC, LANE), col(C_RV))],
        out_specs=pl.BlockSpec((1, 1, 2, LANE, LANE), lambda b, h: (b, h, 0, 0, 0)),
        out_shape=jax.ShapeDtypeStruct((B, RH, 2, LANE, LANE), F32),
        compiler_params=_params(("arbitrary", "arbitrary")),
        name="ctx_state",
    )(lg, pc, pc)


def _ctx_mix(fc, pc, lg, gain, wf, c256, s256, e):
    return pl.pallas_call(
        _ctx_mix_kernel,
        grid=(B,),
        in_specs=[_smem_spec(),
                  pl.BlockSpec((LC, FW), lambda b: (b, 0)),
                  pl.BlockSpec((LC, NP), lambda b: (b, 0)),
                  _const_spec((1, RH * LANE)),
                  _const_spec((FW, FW)),
                  _const_spec((LC, LC)),
                  _const_spec((LC, LC)),
                  _const_spec((2 * FW, FW))],
        out_specs=[pl.BlockSpec((LC, FW), lambda b: (b, 0)),
                   pl.BlockSpec((LC, RH * LANE), lambda b: (b, 0)),
                   pl.BlockSpec((LC, NH * ND), lambda b: (b, 0))],
        out_shape=[jax.ShapeDtypeStruct((B * LC, FW), BF16),
                   jax.ShapeDtypeStruct((B * LC, RH * LANE), BF16),
                   jax.ShapeDtypeStruct((B * LC, NH * ND), BF16)],
        compiler_params=_params(("arbitrary",)),
        name="ctx_mix",
    )(lg, fc, pc, gain, wf, c256, s256, e)


def _na_bias(base, mask):
    nq, nk = NA_R * GRID_W, NA_KR * GRID_W
    return pl.pallas_call(
        _na_bias_kernel,
        grid=(NH,),
        in_specs=[pl.BlockSpec((1, NA_DR, LANE), lambda h: (h, 0, 0)),
                  _const_spec((GRID_W, LANE))],
        out_specs=pl.BlockSpec((1, 3, nq, nk), lambda h: (h, 0, 0, 0)),
        out_shape=jax.ShapeDtypeStruct((NH, 3, nq, nk), F32),
        scratch_shapes=[pltpu.VMEM((NA_DR, GRID_W, LANE), F32)],
        compiler_params=_params(("arbitrary",)),
        name="na_bias",
    )(base, mask)


def _natten(p, pc, bias):
    nq, nk = NA_R * GRID_W, NA_KR * GRID_W
    return pl.pallas_call(
        _natten_kernel,
        grid=(B, NPAIR),
        in_specs=[pl.BlockSpec((L, LANE), lambda b, pr: (b, C_NQ // LANE + pr)),
                  pl.BlockSpec((L, LANE), lambda b, pr: (b, C_NK // LANE + pr)),
                  pl.BlockSpec((L, LANE), lambda b, pr: (b, C_NV // LANE + pr)),
                  pl.BlockSpec((LC, LANE), lambda b, pr: (b, C_NK // LANE + pr)),
                  pl.BlockSpec((LC, LANE), lambda b, pr: (b, C_NV // LANE + pr)),
                  pl.BlockSpec((2, 3, nq, nk), lambda b, pr: (pr, 0, 0, 0))],
        out_specs=pl.BlockSpec((L, LANE), lambda b, pr: (b, pr)),
        out_shape=jax.ShapeDtypeStruct((B * L, NH * ND), BF16),
        scratch_shapes=[pltpu.VMEM((2, nq, nk + LC), BF16)] * 2,
        compiler_params=_params(("arbitrary", "arbitrary")),
        name="natten",
    )(p, p, p, pc, pc, bias)


def _ffn(x2d, four, ret, na, mods, g, wo_all, sel_v_t, w1_all, w3_all, w2_all, layer, g_final, *, ctx):
    rows = x2d.shape[0]
    final = g_final is not None
    row = lambda w: pl.BlockSpec((TM, w), lambda i: (i, 0))
    in_specs = [row(D), row(FW), row(RH * LANE), row(NH * ND),
                _const_spec((8, 6, D)), _const_spec((1, D)),
                _layer_spec((D, D), layer), _const_spec((RH * LANE, RH * RD)),
                _layer_spec((D, DFF), layer), _layer_spec((D, DFF), layer), _layer_spec((DFF, D), layer)]
    args = [x2d, four, ret, na, mods, g.reshape(1, D), wo_all, sel_v_t, w1_all, w3_all, w2_all]
    if final:
        in_specs.append(_const_spec((1, D)))
        args.append(g_final.reshape(1, D))
    return pl.pallas_call(
        functools.partial(_ffn_kernel, final=final, mod_row=_mod_row(ctx, TM)),
        grid=(rows // TM,),
        in_specs=in_specs,
        out_specs=row(D),
        out_shape=jax.ShapeDtypeStruct((rows, D), F32),
        scratch_shapes=[pltpu.VMEM((TM, D), F32), pltpu.VMEM((RH * LANE, D), BF16)],
        compiler_params=_params(("arbitrary",)),
        name="ffn_ctx" if ctx else ("ffn_final" if final else "ffn"),
    )(*args)


def _take_padded(a, src, axis):
    pieces = []
    i, n = 0, len(src)
    while i < n:
        j = i + 1
        if src[i] < 0:
            while j < n and src[j] < 0:
                j += 1
            shape = list(a.shape)
            shape[axis] = j - i
            pieces.append(jnp.zeros(shape, a.dtype))
        else:
            while j < n and src[j] == src[j - 1] + 1:
                j += 1
            pieces.append(lax.slice_in_dim(a, int(src[i]), int(src[j - 1]) + 1, axis=axis))
        i = j
    return jnp.concatenate(pieces, axis=axis)


def kernel(x, c, ctx, c_ctx, w_ada, b_ada, g_mix, w_in, ret_decay_logit, ret_norm_g, w_four,
           na_rpb, w_out, g_ffn, w1, w3, w2, g_final):
    src_g = _gain_sources()
    rope_tabs = tuple(jnp.asarray(t) for t in _rope_tables())
    a1, m3, e, c256, s256 = (jnp.asarray(t).astype(BF16) for t in _dft_tables())
    sel_qk, sel_v = (jnp.asarray(t).astype(BF16) for t in _head_selectors())
    sel_v_t = sel_v.T
    tz_src, tz_mask = _na_toeplitz_tables()
    col_scale = np.ones((1, 1, W_IN), np.float32)
    col_scale[..., W_NQ:W_NK] = LOG2E * ND ** -0.5
    w_in_b = (w_in * jnp.asarray(col_scale)).astype(BF16)
    w_out_b = w_out.astype(BF16)
    w1b, w3b, w2b = w1.astype(BF16), w3.astype(BF16), w2.astype(BF16)

    cvec = jnp.zeros((8, D), F32).at[0:B].set(c).at[B].set(c_ctx)
    mods_all = _ada(cvec, w_ada, b_ada).reshape(DEPTH, 8, 6, D)

    xl = x.reshape(B * L, D)
    xc = ctx.reshape(B * LC, D)
    for i in range(DEPTH):
        last = i == DEPTH - 1
        mods = mods_all[i]
        gain = _take_padded(ret_norm_g[i], src_g, 0).reshape(1, RH * LANE)
        wfour = w_four[i].astype(BF16)
        lg = jax.nn.log_sigmoid(ret_decay_logit[i].astype(F32))
        tz_base = jnp.pad(_take_padded(na_rpb[i].astype(F32), tz_src, 2), ((0, 0), (0, 1), (0, 0)))
        na_bias = _na_bias(tz_base, jnp.asarray(tz_mask))

        f_l, p_l = _proj(xl, mods, g_mix[i], w_in_b, i, sel_qk, sel_v, rope_tabs, ctx=False)
        f_c, p_c = _proj(xc, mods, g_mix[i], w_in_b, i, sel_qk, sel_v, None, ctx=True)
        s0 = _ctx_state(p_c, lg)
        four_l = _fourier(f_l, wfour, a1, m3, e)
        ret_l = _retention(p_l, s0, lg, gain)
        na_l = _natten(p_l, p_c, na_bias)
        if not last:
            four_c, ret_c, na_c = _ctx_mix(f_c, p_c, lg, gain, wfour, c256, s256, e)
            xc = _ffn(xc, four_c, ret_c, na_c, mods, g_ffn[i], w_out_b, sel_v_t, w1b, w3b, w2b, i, None, ctx=True)
        xl = _ffn(xl, four_l, ret_l, na_l, mods, g_ffn[i], w_out_b, sel_v_t, w1b, w3b, w2b, i,
                  g_final if last else None, ctx=False)
    return xl.reshape(B, L, D)
```

```python
import functools

import numpy as np
import jax
import jax.numpy as jnp
from jax import lax
from jax.experimental import pallas as pl
from jax.experimental.pallas import tpu as pltpu

F32 = jnp.float32
BF16 = jnp.bfloat16

D = 1024
B = 2
L = 8192
LC = 256
DEPTH = 2
GRID_W = 64
GRID_H = L // GRID_W
FW = 256
FGW = 64
RH = 4
RD = 96
NH = 6
ND = 64
NPAIR = NH // 2
DFF = 2816
EPS = 1e-6
NEG = -1e30
LOG2E = float(np.log2(np.e))
LANE = 128

C_RQ, C_RK, C_RV, C_RG = 0, 512, 1024, 1536
C_NQ, C_NK, C_NV = 2048, 2432, 2816
NP = 3200
W_RQ, W_RK, W_RV, W_RG = 256, 640, 1024, 1408
W_NQ, W_NK, W_NV = 1792, 2176, 2560
W_IN = 2944

TM = 512
TM_PROJ = 1024
RET_C = 256
NA_R = 4
NA_KR = NA_R + 8
NA_NB = GRID_H // NA_R
DFT_N1 = 64
DFT_N2 = 128
F1_R = 16
F3_K = 8
VMEM_LIMIT = 56 * 1024 * 1024


def _qk_lane_dims():
    m = np.full((LANE,), -1, np.int64)
    m[0:24] = np.arange(0, 24)
    m[24:48] = np.arange(48, 72)
    m[64:88] = np.arange(24, 48)
    m[88:112] = np.arange(72, 96)
    return m


def _v_lane_dims():
    m = np.full((LANE,), -1, np.int64)
    m[:RD] = np.arange(RD)
    return m


def _head_selectors():
    sels = []
    for lanes in (_qk_lane_dims(), _v_lane_dims()):
        sel = np.zeros((RH * RD, RH * LANE), np.float32)
        ok = np.nonzero(lanes >= 0)[0]
        for h in range(RH):
            sel[RD * h + lanes[ok], LANE * h + ok] = 1.0
        sels.append(sel)
    return sels


def _gain_sources():
    src = np.full((RH * LANE,), -1, np.int64)
    for h in range(RH):
        src[LANE * h: LANE * h + RD] = RD * h + np.arange(RD)
    return src


def _rope_tables():
    pos = np.arange(L)
    prow, pcol = pos // GRID_W, pos % GRID_W
    half = RD // 4
    inv = 10000.0 ** (-np.arange(half, dtype=np.float64) / half)
    ar = prow[:, None] * inv[None, :]
    ac = pcol[:, None] * inv[None, :]
    cos = np.ones((L, LANE), np.float64)
    sin = np.zeros((L, LANE), np.float64)
    for off, sign in ((0, -1.0), (64, 1.0)):
        cos[:, off:off + 24] = np.cos(ar)
        cos[:, off + 24:off + 48] = np.cos(ac)
        sin[:, off:off + 24] = sign * np.sin(ar)
        sin[:, off + 24:off + 48] = sign * np.sin(ac)
    return cos.astype(np.float32), sin.astype(np.float32)


def _dft_tables():
    k1 = np.arange(DFT_N1)
    a = 2 * np.pi * ((k1[:, None] * k1[None, :]) % DFT_N1) / DFT_N1
    a1 = np.kron(np.concatenate([np.cos(a), -np.sin(a)], axis=0), np.eye(F1_R))
    k2 = np.arange(DFT_N2)
    l2 = np.arange(DFT_N2)
    kk = k1[:, None, None] + DFT_N1 * k2[None, :, None]
    ang = 2 * np.pi * ((kk * l2[None, None, :]) % L) / L
    ct, st = np.cos(ang) / np.sqrt(L), np.sin(ang) / np.sqrt(L)
    m3 = np.concatenate([np.concatenate([ct, st], axis=2), np.concatenate([st, -ct], axis=2)], axis=1)
    c = np.arange(FW)
    same = (c[:, None] // FGW) == (c[None, :] // FGW)
    ac = 2 * np.pi * (((c[:, None] % FGW) * (c[None, :] % FGW)) % FGW) / FGW
    e = np.concatenate([np.where(same, np.cos(ac), 0.0), -np.where(same, np.sin(ac), 0.0)], axis=0) / 8.0
    p = np.arange(LC)
    ap = 2 * np.pi * ((p[:, None] * p[None, :]) % LC) / LC
    return tuple(t.astype(np.float32) for t in (a1, m3, e, np.cos(ap) / 16.0, np.sin(ap) / 16.0))


NA_DR = 16


def _na_toeplitz_tables():
    src = np.full((LANE,), -1, np.int64)
    src[0:16] = np.arange(0, 16) + 15
    src[113:128] = np.arange(113, 128) - 113
    src[49:80] = np.arange(49, 80) - 49
    qc = np.arange(GRID_W)[:, None]
    kc = np.arange(LANE)[None, :] % GRID_W
    ws = np.clip(qc - 8, 0, GRID_W - 16)
    mask = np.where((kc >= ws) & (kc < ws + 16), 0.0, NEG).astype(np.float32)
    return src, mask


def _dot(a, b):
    return jnp.dot(a, b, preferred_element_type=F32)


def _dot_nt(a, b):
    return lax.dot_general(a, b, (((1,), (1,)), ((), ())), preferred_element_type=F32)


def _silu(x):
    return x * jax.nn.sigmoid(x)


def _rmsnorm(x, g):
    return x * lax.rsqrt(jnp.mean(x * x, axis=-1, keepdims=True) + EPS) * g


def _ada_kernel(c_ref, w_ref, b_ref, o_ref):
    s = _silu(c_ref[...]).astype(BF16)
    o_ref[0] = _dot(s, w_ref[0].astype(BF16)) + b_ref[0]


def _proj_kernel(*refs, rope, mod_row):
    if rope:
        x_ref, mods_ref, g_ref, w_ref, pqk_ref, pv_ref, cos_ref, sin_ref, f_ref, o_ref, wret_ref = refs
    else:
        x_ref, mods_ref, g_ref, w_ref, pqk_ref, pv_ref, f_ref, o_ref, wret_ref = refs
    s = pl.program_id(0)

    @pl.when(s == 0)
    def _():
        for slot, (src0, sel_ref) in enumerate(((W_RQ, pqk_ref), (W_RK, pqk_ref), (W_RV, pv_ref), (W_RG, pv_ref))):
            wret_ref[:, RH * LANE * slot:RH * LANE * (slot + 1)] = _dot(
                w_ref[:, src0:src0 + RH * RD], sel_ref[...]).astype(BF16)

    mod = mods_ref[mod_row(s)]
    hb = (_rmsnorm(x_ref[...], g_ref[...]) * (1.0 + mod[1:2, :]) + mod[0:1, :]).astype(BF16)
    f_ref[...] = _dot(hb, w_ref[:, :FW]).astype(BF16)
    k_scale = RD ** -0.5
    for c0, scale in ((C_RQ, None), (C_RK, k_scale)):
        t4 = _dot(hb, wret_ref[:, c0:c0 + RH * LANE])
        for hh in range(RH):
            t = t4[:, LANE * hh:LANE * (hh + 1)]
            if rope:
                t = t * cos_ref[...] + pltpu.roll(t, 64, 1) * sin_ref[...]
            if scale is not None:
                t = t * scale
            o_ref[:, c0 + LANE * hh:c0 + LANE * (hh + 1)] = t.astype(BF16)
    o_ref[:, C_RV:C_RG] = _dot(hb, wret_ref[:, C_RV:C_RG]).astype(BF16)
    o_ref[:, C_RG:C_NQ] = _silu(_dot(hb, wret_ref[:, C_RG:C_NQ])).astype(BF16)
    o_ref[:, C_NQ:] = _dot(hb, w_ref[:, W_NQ:]).astype(BF16)


def _four1_kernel(a_ref, x_ref, y_ref):
    x = x_ref[0].reshape(DFT_N1 * F1_R, FW)
    y = _dot(a_ref[...], x).astype(BF16)
    y_ref[0] = y.reshape(2 * DFT_N1, F1_R, FW)


def _four3_kernel(m_ref, y_ref, e_ref, wf_ref, o_ref):
    ew = _dot(e_ref[...], wf_ref[...]).astype(BF16)
    for b in range(B):
        res = []
        for kk in range(F3_K):
            y = jnp.concatenate([y_ref[b, 0, kk], y_ref[b, 1, kk]], axis=0)
            z = _dot(m_ref[kk], y).astype(BF16)
            res.append(_dot(jnp.concatenate([z[:DFT_N2], z[DFT_N2:]], axis=1), ew))
        o_ref[b] = jnp.stack(res, axis=1)


def _decay_matrix(n, lgf, lgb):
    ii = lax.broadcasted_iota(jnp.int32, (n, n), 0)
    jj = lax.broadcasted_iota(jnp.int32, (n, n), 1)
    diff = (ii - jj).astype(F32)
    fwd = jnp.where(diff >= 0, jnp.exp(jnp.maximum(diff, 0.0) * lgf), 0.0)
    bwd = jnp.where(diff <= 0, jnp.exp(jnp.maximum(-diff, 0.0) * lgb), 0.0)
    return fwd + bwd


def _ret_readout(o, gate, gain):
    ms = jnp.sum(o * o, axis=-1, keepdims=True) * (1.0 / RD)
    return (o * lax.rsqrt(ms + EPS) * gain * gate.astype(F32)).astype(BF16)


def _ret_kernel(lg_ref, q_ref, k_ref, v_ref, g_ref, s0_ref, gain_ref, o_ref, acc_ref, stf_ref, stb_ref):
    h = pl.program_id(1)
    lgf = lg_ref[0, h]
    lgb = lg_ref[1, h]
    c = RET_C
    n = L // c
    dmat = _decay_matrix(c, lgf, lgb)
    idx = lax.broadcasted_iota(jnp.int32, (c, LANE), 0).astype(F32)
    qdf = jnp.exp((idx + 1.0) * lgf)
    kdf = jnp.exp((c - 1.0 - idx) * lgf)
    qdb = jnp.exp((c - idx) * lgb)
    kdb = jnp.exp(idx * lgb)
    cdf = jnp.exp(jnp.full((1, LANE), float(c), F32) * lgf)
    cdb = jnp.exp(jnp.full((1, LANE), float(c), F32) * lgb)
    gain = gain_ref[...]

    dirs = ((stf_ref, qdf, kdf, cdf), (stb_ref, qdb, kdb, cdb))
    stf_ref[...] = s0_ref[0, 0, 0]
    stb_ref[...] = s0_ref[0, 0, 1]

    def visit(r0, direction, second):
        st_ref, qdec, kdec, cdec = dirs[direction]
        rows = pl.ds(pl.multiple_of(r0, c), c)
        q = q_ref[rows, :]
        k = k_ref[rows, :]
        v = v_ref[rows, :]
        cross = _dot(q, st_ref[...].astype(BF16)) * qdec
        if second:
            o_ref[rows, :] = _ret_readout(acc_ref[rows, :] + cross, g_ref[rows, :], gain)
        else:
            s = _dot_nt(q, k) * dmat
            acc_ref[rows, :] = _dot(s.astype(BF16), v) + cross
        kt = (k.astype(F32) * kdec).T.astype(BF16)
        st_ref[...] = st_ref[...] * cdec + _dot(kt, v)

    def body(second):
        def step(i, carry):
            visit(i * c, 0, second)
            visit((n - 1 - i) * c, 1, second)
            return carry
        return step

    lax.fori_loop(0, n // 2, body(False), 0, unroll=4)
    lax.fori_loop(n // 2, n, body(True), 0, unroll=4)


def _ctx_state_kernel(lg_ref, k_ref, v_ref, o_ref):
    h = pl.program_id(1)
    idx = lax.broadcasted_iota(jnp.int32, (LC, LANE), 0).astype(F32)
    k = k_ref[...].astype(F32)
    v = v_ref[...]
    wf = jnp.exp((LC - 1.0 - idx) * lg_ref[0, h])
    wb = jnp.exp(idx * lg_ref[1, h])
    o_ref[0, 0, 0] = _dot((k * wf).T.astype(BF16), v)
    o_ref[0, 0, 1] = _dot((k * wb).T.astype(BF16), v)


def _pair_attention(q, score_fn, value_fn):
    lane = lax.broadcasted_iota(jnp.int32, q.shape, 1)
    outs = []
    for hh in range(2):
        sel = (lane >= ND * hh) & (lane < ND * (hh + 1))
        qm = jnp.where(sel, q, jnp.zeros_like(q))
        scores = score_fn(qm, hh)
        m = functools.reduce(jnp.maximum, [jnp.max(s, axis=-1, keepdims=True) for s in scores])
        ps = [jnp.exp2(s - m) for s in scores]
        den = functools.reduce(jnp.add, [jnp.sum(p, axis=-1, keepdims=True) for p in ps])
        outs.append(value_fn([p.astype(BF16) for p in ps]) / den)
    return jnp.where(lane < ND, outs[0], outs[1])


def _ctx_mix_kernel(lg_ref, f_ref, p_ref, gain_ref, wf_ref, c_ref, s_ref, e_ref, fo_ref, ro_ref, no_ref):
    f = f_ref[...]
    pr = _dot(c_ref[...], f)
    qi = _dot(s_ref[...], f)
    fr = _dot(pr.astype(BF16), e_ref[:FW, :]) + _dot(qi.astype(BF16), e_ref[FW:, :])
    fo_ref[...] = _dot(fr.astype(BF16), wf_ref[...]).astype(BF16)
    for h in range(RH):
        sl = lambda c0: slice(c0 + LANE * h, c0 + LANE * (h + 1))
        dmat = _decay_matrix(LC, lg_ref[0, h], lg_ref[1, h])
        s = _dot_nt(p_ref[:, sl(C_RQ)], p_ref[:, sl(C_RK)]) * dmat
        o = _dot(s.astype(BF16), p_ref[:, sl(C_RV)])
        ro_ref[:, LANE * h:LANE * (h + 1)] = _ret_readout(o, p_ref[:, sl(C_RG)], gain_ref[:, LANE * h:LANE * (h + 1)])
    for pair in range(NPAIR):
        sl = lambda c0: slice(c0 + LANE * pair, c0 + LANE * (pair + 1))
        k = p_ref[:, sl(C_NK)]
        v = p_ref[:, sl(C_NV)]
        out = _pair_attention(p_ref[:, sl(C_NQ)], lambda qm, hh: [_dot_nt(qm, k)], lambda ps: _dot(ps[0], v))
        no_ref[:, LANE * pair:LANE * (pair + 1)] = out.astype(BF16)


def _na_key_start(r0):
    return min(max(r0 - 4, 0), GRID_H - NA_KR)


def _na_bias_kernel(base_ref, mask_ref, o_ref, tz_ref):
    qc = lax.broadcasted_iota(jnp.int32, (GRID_W, LANE), 0)
    for dr in range(NA_DR - 1):
        t = jnp.broadcast_to(base_ref[0, dr:dr + 1, :], (GRID_W, LANE))
        for bit in range(6):
            t = jnp.where((qc >> bit) & 1 == 1, pltpu.roll(t, 1 << bit, 1), t)
        tz_ref[dr] = t * LOG2E + mask_ref[...]
    tz_ref[NA_DR - 1] = jnp.full((GRID_W, LANE), NEG, F32)
    low_half = lax.broadcasted_iota(jnp.int32, (GRID_W, LANE), 1) < GRID_W
    for var, r0 in enumerate((0, 2 * NA_R, GRID_H - NA_R)):
        ks = _na_key_start(r0)
        for rl in range(NA_R):
            r = r0 + rl
            rs = min(max(r - 4, 0), GRID_H - 8)
            slot = [kr - r + 7 if rs <= kr < rs + 8 else NA_DR - 1 for kr in range(ks, ks + NA_KR)]
            for m in range(NA_KR // 2):
                o_ref[0, var, GRID_W * rl:GRID_W * (rl + 1), LANE * m:LANE * (m + 1)] = jnp.where(
                    low_half, tz_ref[slot[2 * m]], tz_ref[slot[2 * m + 1]])


def _natten_kernel(q_ref, k_ref, v_ref, kc_ref, vc_ref, bias_ref, o_ref, pa_ref, pb_ref):
    nq = NA_R * GRID_W
    nk = NA_KR * GRID_W
    kc = kc_ref[...]
    vc = vc_ref[...]
    lane = lax.broadcasted_iota(jnp.int32, (nq, LANE), 1)

    def rows(i):
        ks = jnp.clip(NA_R * i - 4, 0, GRID_H - NA_KR)
        return pl.ds(pl.multiple_of(ks * GRID_W, nq), nk), pl.ds(pl.multiple_of(i * nq, nq), nq)

    def probabilities(i, p_ref):
        i = jnp.minimum(i, NA_NB - 1)
        var = jnp.where(i == 0, 0, jnp.where(i == NA_NB - 1, 2, 1))
        krows, qrows = rows(i)
        q = q_ref[qrows, :]
        kw = k_ref[krows, :]
        for hh in range(2):
            qm = jnp.where((lane >= ND * hh) & (lane < ND * (hh + 1)), q, jnp.zeros_like(q))
            s_win = _dot_nt(qm, kw) + bias_ref[hh, var]
            s_ctx = _dot_nt(qm, kc)
            m = jnp.maximum(jnp.max(s_win, axis=-1, keepdims=True), jnp.max(s_ctx, axis=-1, keepdims=True))
            p_ref[hh, :, :nk] = jnp.exp2(s_win - m).astype(BF16)
            p_ref[hh, :, nk:] = jnp.exp2(s_ctx - m).astype(BF16)

    def values(i, p_ref):
        krows, qrows = rows(i)
        vals = jnp.concatenate([v_ref[krows, :], vc], axis=0)
        vals = jnp.concatenate([vals, jnp.ones_like(vals)], axis=1)
        o2 = [_dot(p_ref[hh], vals) for hh in range(2)]
        outs = [o[:, :LANE] / o[:, LANE:] for o in o2]
        o_ref[qrows, :] = jnp.where(lane < ND, outs[0], outs[1]).astype(BF16)

    probabilities(0, pa_ref)

    def step(j, carry):
        values(2 * j, pa_ref)
        probabilities(2 * j + 1, pb_ref)
        values(2 * j + 1, pb_ref)
        probabilities(2 * j + 2, pa_ref)
        return carry

    lax.fori_loop(0, NA_NB // 2, step, 0)


def _ffn_kernel(*refs, final, mod_row):
    x_ref, f_ref, r_ref, n_ref, mods_ref, g_ref, wo_ref, sel_ref, w1_ref, w3_ref, w2_ref = refs[:11]
    gf_ref = refs[11] if final else None
    o_ref, acc_ref, wor_ref = refs[-3:]
    s = pl.program_id(0)

    @pl.when(s == 0)
    def _():
        wor_ref[...] = _dot(sel_ref[...], wo_ref[FW:FW + RH * RD, :]).astype(BF16)

    mod = mods_ref[mod_row(s)]
    y = (_dot(f_ref[...].astype(BF16), wo_ref[:FW, :]) + _dot(r_ref[...], wor_ref[...])
         + _dot(n_ref[...], wo_ref[FW + RH * RD:, :]))
    x1 = x_ref[...] + mod[2:3, :] * y
    hb = (_rmsnorm(x1, g_ref[...]) * (1.0 + mod[4:5, :]) + mod[3:4, :]).astype(BF16)
    fc = 256
    for c in range(DFF // fc):
        a = _dot(hb, w1_ref[:, c * fc:(c + 1) * fc])
        b = _dot(hb, w3_ref[:, c * fc:(c + 1) * fc])
        acc_ref[:, c * fc:(c + 1) * fc] = (_silu(a) * b).astype(BF16)
    x2 = x1 + mod[5:6, :] * _dot(acc_ref[...], w2_ref[...])
    if final:
        x2 = _rmsnorm(x2, gf_ref[...])
    o_ref[...] = x2


def _params(sem):
    return pltpu.CompilerParams(dimension_semantics=sem, vmem_limit_bytes=VMEM_LIMIT)


def _const_spec(shape):
    nd = len(shape)
    return pl.BlockSpec(shape, lambda *_: (0,) * nd, pipeline_mode=pl.Buffered(1))


def _smem_spec():
    return pl.BlockSpec(memory_space=pltpu.SMEM)


def _ada(cvec, w_ada, b_ada):
    tn = 1024
    return pl.pallas_call(
        _ada_kernel,
        grid=(DEPTH, 6 * D // tn),
        in_specs=[pl.BlockSpec((8, D), lambda i, j: (0, 0)),
                  pl.BlockSpec((1, D, tn), lambda i, j: (i, 0, j)),
                  pl.BlockSpec((1, 1, tn), lambda i, j: (i, 0, j))],
        out_specs=pl.BlockSpec((1, 8, tn), lambda i, j: (i, 0, j)),
        out_shape=jax.ShapeDtypeStruct((DEPTH, 8, 6 * D), F32),
        compiler_params=_params(("arbitrary", "arbitrary")),
        name="ada",
    )(cvec, w_ada, b_ada.reshape(DEPTH, 1, 6 * D))


def _mod_row(ctx, tm):
    per_batch = L // tm
    return (lambda tile: B) if ctx else (lambda tile: tile // per_batch)


def _layer_spec(shape, layer):
    nd = len(shape)
    return pl.BlockSpec((None,) + tuple(shape), lambda *_: (layer,) + (0,) * nd, pipeline_mode=pl.Buffered(1))


def _proj(x2d, mods, g, w_all, layer, sel_qk, sel_v, rope_tabs, *, ctx):
    rows = x2d.shape[0]
    tm = min(TM_PROJ, rows)
    per_batch = L // tm
    in_specs = [pl.BlockSpec((tm, D), lambda i: (i, 0)),
                _const_spec((8, 6, D)),
                _const_spec((1, D)),
                _layer_spec((D, W_IN), layer),
                _const_spec((RH * RD, RH * LANE)),
                _const_spec((RH * RD, RH * LANE))]
    args = [x2d, mods, g.reshape(1, D), w_all, sel_qk, sel_v]
    if not ctx:
        in_specs += [pl.BlockSpec((tm, LANE), lambda i: (i % per_batch, 0))] * 2
        args += list(rope_tabs)
    return pl.pallas_call(
        functools.partial(_proj_kernel, rope=not ctx, mod_row=_mod_row(ctx, tm)),
        grid=(rows // tm,),
        in_specs=in_specs,
        out_specs=[pl.BlockSpec((tm, FW), lambda i: (i, 0)), pl.BlockSpec((tm, NP), lambda i: (i, 0))],
        out_shape=[jax.ShapeDtypeStruct((rows, FW), BF16), jax.ShapeDtypeStruct((rows, NP), BF16)],
        scratch_shapes=[pltpu.VMEM((D, 4 * RH * LANE), BF16)],
        compiler_params=_params(("arbitrary",)),
        name="proj_ctx" if ctx else "proj",
    )(*args)


def _fourier(f2d, wf, a1, m3, e):
    y = pl.pallas_call(
        _four1_kernel,
        grid=(B, DFT_N2 // F1_R),
        in_specs=[_const_spec((2 * DFT_N1 * F1_R, DFT_N1 * F1_R)),
                  pl.BlockSpec((1, DFT_N1, F1_R, FW), lambda b, j: (b, 0, j, 0))],
        out_specs=pl.BlockSpec((1, 2 * DFT_N1, F1_R, FW), lambda b, j: (b, 0, j, 0)),
        out_shape=jax.ShapeDtypeStruct((B, 2 * DFT_N1, DFT_N2, FW), BF16),
        compiler_params=_params(("arbitrary", "arbitrary")),
        name="four1",
    )(a1, f2d.reshape(B, DFT_N1, DFT_N2, FW))
    out = pl.pallas_call(
        _four3_kernel,
        grid=(DFT_N1 // F3_K,),
        in_specs=[pl.BlockSpec((F3_K, 2 * DFT_N2, 2 * DFT_N2), lambda k: (k, 0, 0)),
                  pl.BlockSpec((B, 2, F3_K, DFT_N2, FW), lambda k: (0, 0, k, 0, 0)),
                  _const_spec((2 * FW, FW)),
                  _const_spec((FW, FW))],
        out_specs=pl.BlockSpec((B, DFT_N2, F3_K, FW), lambda k: (0, 0, k, 0)),
        out_shape=jax.ShapeDtypeStruct((B, DFT_N2, DFT_N1, FW), F32),
        compiler_params=_params(("arbitrary",)),
        name="four3",
    )(m3, y.reshape(B, 2, DFT_N1, DFT_N2, FW), e, wf)
    return out.reshape(B * L, FW)


def _retention(p, s0, lg, gain):
    col = lambda c0: (lambda b, h: (b, c0 // LANE + h))
    return pl.pallas_call(
        _ret_kernel,
        grid=(B, RH),
        in_specs=[_smem_spec(),
                  pl.BlockSpec((L, LANE), col(C_RQ)),
                  pl.BlockSpec((L, LANE), col(C_RK)),
                  pl.BlockSpec((L, LANE), col(C_RV)),
                  pl.BlockSpec((L, LANE), col(C_RG)),
                  pl.BlockSpec((1, 1, 2, LANE, LANE), lambda b, h: (b, h, 0, 0, 0)),
                  pl.BlockSpec((1, LANE), lambda b, h: (0, h))],
        out_specs=pl.BlockSpec((L, LANE), lambda b, h: (b, h)),
        out_shape=jax.ShapeDtypeStruct((B * L, RH * LANE), BF16),
        scratch_shapes=[pltpu.VMEM((L, LANE), F32), pltpu.VMEM((LANE, LANE), F32), pltpu.VMEM((LANE, LANE), F32)],
        compiler_params=_params(("arbitrary", "arbitrary")),
        name="ret",
    )(lg, p, p, p, p, s0, gain)


def _ctx_state(pc, lg):
    col = lambda c0: (lambda b, h: (b, c0 // LANE + h))
    return pl.pallas_call(
        _ctx_state_kernel,
        grid=(B, RH),
        in_specs=[_smem_spec(),
                  pl.BlockSpec((LC, LANE), col(C_RK)),
                  pl.BlockSpec((LC, LANE), col(C_RV))],
        out_specs=pl.BlockSpec((1, 1, 2, LANE, LANE), lambda b, h: (b, h, 0, 0, 0)),
        out_shape=jax.ShapeDtypeStruct((B, RH, 2, LANE, LANE), F32),
        compiler_params=_params(("arbitrary", "arbitrary")),
        name="ctx_state",
    )(lg, pc, pc)


def _ctx_mix(fc, pc, lg, gain, wf, c256, s256, e):
    return pl.pallas_call(
        _ctx_mix_kernel,
        grid=(B,),
        in_specs=[_smem_spec(),
                  pl.BlockSpec((LC, FW), lambda b: (b, 0)),
                  pl.BlockSpec((LC, NP), lambda b: (b, 0)),
                  _const_spec((1, RH * LANE)),
                  _const_spec((FW, FW)),
                  _const_spec((LC, LC)),
                  _const_spec((LC, LC)),
                  _const_spec((2 * FW, FW))],
        out_specs=[pl.BlockSpec((LC, FW), lambda b: (b, 0)),
                   pl.BlockSpec((LC, RH * LANE), lambda b: (b, 0)),
                   pl.BlockSpec((LC, NH * ND), lambda b: (b, 0))],
        out_shape=[jax.ShapeDtypeStruct((B * LC, FW), BF16),
                   jax.ShapeDtypeStruct((B * LC, RH * LANE), BF16),
                   jax.ShapeDtypeStruct((B * LC, NH * ND), BF16)],
        compiler_params=_params(("arbitrary",)),
        name="ctx_mix",
    )(lg, fc, pc, gain, wf, c256, s256, e)


def _na_bias(base, mask):
    nq, nk = NA_R * GRID_W, NA_KR * GRID_W
    return pl.pallas_call(
        _na_bias_kernel,
        grid=(NH,),
        in_specs=[pl.BlockSpec((1, NA_DR, LANE), lambda h: (h, 0, 0)),
                  _const_spec((GRID_W, LANE))],
        out_specs=pl.BlockSpec((1, 3, nq, nk), lambda h: (h, 0, 0, 0)),
        out_shape=jax.ShapeDtypeStruct((NH, 3, nq, nk), F32),
        scratch_shapes=[pltpu.VMEM((NA_DR, GRID_W, LANE), F32)],
        compiler_params=_params(("arbitrary",)),
        name="na_bias",
    )(base, mask)


def _natten(p, pc, bias):
    nq, nk = NA_R * GRID_W, NA_KR * GRID_W
    return pl.pallas_call(
        _natten_kernel,
        grid=(B, NPAIR),
        in_specs=[pl.BlockSpec((L, LANE), lambda b, pr: (b, C_NQ // LANE + pr)),
                  pl.BlockSpec((L, LANE), lambda b, pr: (b, C_NK // LANE + pr)),
                  pl.BlockSpec((L, LANE), lambda b, pr: (b, C_NV // LANE + pr)),
                  pl.BlockSpec((LC, LANE), lambda b, pr: (b, C_NK // LANE + pr)),
                  pl.BlockSpec((LC, LANE), lambda b, pr: (b, C_NV // LANE + pr)),
                  pl.BlockSpec((2, 3, nq, nk), lambda b, pr: (pr, 0, 0, 0))],
        out_specs=pl.BlockSpec((L, LANE), lambda b, pr: (b, pr)),
        out_shape=jax.ShapeDtypeStruct((B * L, NH * ND), BF16),
        scratch_shapes=[pltpu.VMEM((2, nq, nk + LC), BF16)] * 2,
        compiler_params=_params(("arbitrary", "arbitrary")),
        name="natten",
    )(p, p, p, pc, pc, bias)


def _ffn(x2d, four, ret, na, mods, g, wo_all, sel_v_t, w1_all, w3_all, w2_all, layer, g_final, *, ctx):
    rows = x2d.shape[0]
    final = g_final is not None
    row = lambda w: pl.BlockSpec((TM, w), lambda i: (i, 0))
    in_specs = [row(D), row(FW), row(RH * LANE), row(NH * ND),
                _const_spec((8, 6, D)), _const_spec((1, D)),
                _layer_spec((D, D), layer), _const_spec((RH * LANE, RH * RD)),
                _layer_spec((D, DFF), layer), _layer_spec((D, DFF), layer), _layer_spec((DFF, D), layer)]
    args = [x2d, four, ret, na, mods, g.reshape(1, D), wo_all, sel_v_t, w1_all, w3_all, w2_all]
    if final:
        in_specs.append(_const_spec((1, D)))
        args.append(g_final.reshape(1, D))
    return pl.pallas_call(
        functools.partial(_ffn_kernel, final=final, mod_row=_mod_row(ctx, TM)),
        grid=(rows // TM,),
        in_specs=in_specs,
        out_specs=row(D),
        out_shape=jax.ShapeDtypeStruct((rows, D), F32),
        scratch_shapes=[pltpu.VMEM((TM, DFF), BF16), pltpu.VMEM((RH * LANE, D), BF16)],
        compiler_params=_params(("arbitrary",)),
        name="ffn_ctx" if ctx else ("ffn_final" if final else "ffn"),
    )(*args)


def _take_padded(a, src, axis):
    pieces = []
    i, n = 0, len(src)
    while i < n:
        j = i + 1
        if src[i] < 0:
            while j < n and src[j] < 0:
                j += 1
            shape = list(a.shape)
            shape[axis] = j - i
            pieces.append(jnp.zeros(shape, a.dtype))
        else:
            while j < n and src[j] == src[j - 1] + 1:
                j += 1
            pieces.append(lax.slice_in_dim(a, int(src[i]), int(src[j - 1]) + 1, axis=axis))
        i = j
    return jnp.concatenate(pieces, axis=axis)


def kernel(x, c, ctx, c_ctx, w_ada, b_ada, g_mix, w_in, ret_decay_logit, ret_norm_g, w_four,
           na_rpb, w_out, g_ffn, w1, w3, w2, g_final):
    src_g = _gain_sources()
    rope_tabs = tuple(jnp.asarray(t) for t in _rope_tables())
    a1, m3, e, c256, s256 = (jnp.asarray(t).astype(BF16) for t in _dft_tables())
    sel_qk, sel_v = (jnp.asarray(t).astype(BF16) for t in _head_selectors())
    sel_v_t = sel_v.T
    tz_src, tz_mask = _na_toeplitz_tables()
    col_scale = np.ones((1, 1, W_IN), np.float32)
    col_scale[..., W_NQ:W_NK] = LOG2E * ND ** -0.5
    w_in_b = (w_in * jnp.asarray(col_scale)).astype(BF16)
    w_out_b = w_out.astype(BF16)
    w1b, w3b, w2b = w1.astype(BF16), w3.astype(BF16), w2.astype(BF16)

    cvec = jnp.zeros((8, D), F32).at[0:B].set(c).at[B].set(c_ctx)
    mods_all = _ada(cvec, w_ada, b_ada).reshape(DEPTH, 8, 6, D)

    xl = x.reshape(B * L, D)
    xc = ctx.reshape(B * LC, D)
    for i in range(DEPTH):
        last = i == DEPTH - 1
        mods = mods_all[i]
        gain = _take_padded(ret_norm_g[i], src_g, 0).reshape(1, RH * LANE)
        wfour = w_four[i].astype(BF16)
        lg = jax.nn.log_sigmoid(ret_decay_logit[i].astype(F32))
        tz_base = jnp.pad(_take_padded(na_rpb[i].astype(F32), tz_src, 2), ((0, 0), (0, 1), (0, 0)))
        na_bias = _na_bias(tz_base, jnp.asarray(tz_mask))

        f_l, p_l = _proj(xl, mods, g_mix[i], w_in_b, i, sel_qk, sel_v, rope_tabs, ctx=False)
        f_c, p_c = _proj(xc, mods, g_mix[i], w_in_b, i, sel_qk, sel_v, None, ctx=True)
        s0 = _ctx_state(p_c, lg)
        four_l = _fourier(f_l, wfour, a1, m3, e)
        ret_l = _retention(p_l, s0, lg, gain)
        na_l = _natten(p_l, p_c, na_bias)
        if not last:
            four_c, ret_c, na_c = _ctx_mix(f_c, p_c, lg, gain, wfour, c256, s256, e)
            xc = _ffn(xc, four_c, ret_c, na_c, mods, g_ffn[i], w_out_b, sel_v_t, w1b, w3b, w2b, i, None, ctx=True)
        xl = _ffn(xl, four_l, ret_l, na_l, mods, g_ffn[i], w_out_b, sel_v_t, w1b, w3b, w2b, i,
                  g_final if last else None, ctx=False)
    return xl.reshape(B, L, D)
```

```python
import functools

import numpy as np
import jax
import jax.numpy as jnp
from jax import lax
from jax.experimental import pallas as pl
from jax.experimental.pallas import tpu as pltpu

F32 = jnp.float32
BF16 = jnp.bfloat16

D = 1024
B = 2
L = 8192
LC = 256
DEPTH = 2
GRID_W = 64
GRID_H = L // GRID_W
FW = 256
FGW = 64
RH = 4
RD = 96
NH = 6
ND = 64
NPAIR = NH // 2
DFF = 2816
EPS = 1e-6
NEG = -1e30
LOG2E = float(np.log2(np.e))
LANE = 128

C_RQ, C_RK, C_RV, C_RG = 0, 512, 1024, 1536
C_NQ, C_NK, C_NV = 2048, 2432, 2816
NP = 3200
W_RQ, W_RK, W_RV, W_RG = 256, 640, 1024, 1408
W_NQ, W_NK, W_NV = 1792, 2176, 2560
W_IN = 2944

TM = 512
TM_PROJ = 1024
RET_C = 256
NA_R = 4
NA_KR = NA_R + 8
NA_NB = GRID_H // NA_R
DFT_N1 = 64
DFT_N2 = 128
F1_R = 16
F3_K = 8
VMEM_LIMIT = 56 * 1024 * 1024


def _qk_lane_dims():
    m = np.full((LANE,), -1, np.int64)
    m[0:24] = np.arange(0, 24)
    m[24:48] = np.arange(48, 72)
    m[64:88] = np.arange(24, 48)
    m[88:112] = np.arange(72, 96)
    return m


def _v_lane_dims():
    m = np.full((LANE,), -1, np.int64)
    m[:RD] = np.arange(RD)
    return m


def _head_selectors():
    sels = []
    for lanes in (_qk_lane_dims(), _v_lane_dims()):
        sel = np.zeros((RH * RD, RH * LANE), np.float32)
        ok = np.nonzero(lanes >= 0)[0]
        for h in range(RH):
            sel[RD * h + lanes[ok], LANE * h + ok] = 1.0
        sels.append(sel)
    return sels


def _gain_sources():
    src = np.full((RH * LANE,), -1, np.int64)
    for h in range(RH):
        src[LANE * h: LANE * h + RD] = RD * h + np.arange(RD)
    return src


def _rope_tables():
    pos = np.arange(L)
    prow, pcol = pos // GRID_W, pos % GRID_W
    half = RD // 4
    inv = 10000.0 ** (-np.arange(half, dtype=np.float64) / half)
    ar = prow[:, None] * inv[None, :]
    ac = pcol[:, None] * inv[None, :]
    cos = np.ones((L, LANE), np.float64)
    sin = np.zeros((L, LANE), np.float64)
    for off, sign in ((0, -1.0), (64, 1.0)):
        cos[:, off:off + 24] = np.cos(ar)
        cos[:, off + 24:off + 48] = np.cos(ac)
        sin[:, off:off + 24] = sign * np.sin(ar)
        sin[:, off + 24:off + 48] = sign * np.sin(ac)
    return cos.astype(np.float32), sin.astype(np.float32)


def _dft_tables():
    k1 = np.arange(DFT_N1)
    a = 2 * np.pi * ((k1[:, None] * k1[None, :]) % DFT_N1) / DFT_N1
    a1 = np.kron(np.concatenate([np.cos(a), -np.sin(a)], axis=0), np.eye(F1_R))
    k2 = np.arange(DFT_N2)
    l2 = np.arange(DFT_N2)
    kk = k1[:, None, None] + DFT_N1 * k2[None, :, None]
    ang = 2 * np.pi * ((kk * l2[None, None, :]) % L) / L
    ct, st = np.cos(ang) / np.sqrt(L), np.sin(ang) / np.sqrt(L)
    m3 = np.concatenate([np.concatenate([ct, st], axis=2), np.concatenate([st, -ct], axis=2)], axis=1)
    c = np.arange(FW)
    same = (c[:, None] // FGW) == (c[None, :] // FGW)
    ac = 2 * np.pi * (((c[:, None] % FGW) * (c[None, :] % FGW)) % FGW) / FGW
    e = np.concatenate([np.where(same, np.cos(ac), 0.0), -np.where(same, np.sin(ac), 0.0)], axis=0) / 8.0
    p = np.arange(LC)
    ap = 2 * np.pi * ((p[:, None] * p[None, :]) % LC) / LC
    return tuple(t.astype(np.float32) for t in (a1, m3, e, np.cos(ap) / 16.0, np.sin(ap) / 16.0))


NA_DR = 16


def _na_toeplitz_tables():
    src = np.full((LANE,), -1, np.int64)
    src[0:16] = np.arange(0, 16) + 15
    src[113:128] = np.arange(113, 128) - 113
    src[49:80] = np.arange(49, 80) - 49
    qc = np.arange(GRID_W)[:, None]
    kc = np.arange(LANE)[None, :] % GRID_W
    ws = np.clip(qc - 8, 0, GRID_W - 16)
    mask = np.where((kc >= ws) & (kc < ws + 16), 0.0, NEG).astype(np.float32)
    return src, mask


def _dot(a, b):
    return jnp.dot(a, b, preferred_element_type=F32)


def _dot_nt(a, b):
    return lax.dot_general(a, b, (((1,), (1,)), ((), ())), preferred_element_type=F32)


def _silu(x):
    return x * jax.nn.sigmoid(x)


def _rmsnorm(x, g):
    return x * lax.rsqrt(jnp.mean(x * x, axis=-1, keepdims=True) + EPS) * g


def _ada_kernel(c_ref, w_ref, b_ref, o_ref):
    s = _silu(c_ref[...]).astype(BF16)
    o_ref[0] = _dot(s, w_ref[0].astype(BF16)) + b_ref[0]


def _proj_kernel(*refs, rope, mod_row):
    if rope:
        x_ref, mods_ref, g_ref, w_ref, pqk_ref, pv_ref, cos_ref, sin_ref, f_ref, o_ref, wret_ref = refs
    else:
        x_ref, mods_ref, g_ref, w_ref, pqk_ref, pv_ref, lg_ref, f_ref, o_ref, s0_ref, wret_ref = refs
    s = pl.program_id(0)

    @pl.when(s == 0)
    def _():
        for slot, (src0, sel_ref) in enumerate(((W_RQ, pqk_ref), (W_RK, pqk_ref), (W_RV, pv_ref), (W_RG, pv_ref))):
            wret_ref[:, RH * LANE * slot:RH * LANE * (slot + 1)] = _dot(
                w_ref[:, src0:src0 + RH * RD], sel_ref[...]).astype(BF16)

    mod = mods_ref[mod_row(s)]
    hb = (_rmsnorm(x_ref[...], g_ref[...]) * (1.0 + mod[1:2, :]) + mod[0:1, :]).astype(BF16)
    f_ref[...] = _dot(hb, w_ref[:, :FW]).astype(BF16)
    k_scale = RD ** -0.5
    for c0, scale in ((C_RQ, None), (C_RK, k_scale)):
        t4 = _dot(hb, wret_ref[:, c0:c0 + RH * LANE])
        for hh in range(RH):
            t = t4[:, LANE * hh:LANE * (hh + 1)]
            if rope:
                t = t * cos_ref[...] + pltpu.roll(t, 64, 1) * sin_ref[...]
            if scale is not None:
                t = t * scale
            o_ref[:, c0 + LANE * hh:c0 + LANE * (hh + 1)] = t.astype(BF16)
    o_ref[:, C_RV:C_RG] = _dot(hb, wret_ref[:, C_RV:C_RG]).astype(BF16)
    o_ref[:, C_RG:C_NQ] = _silu(_dot(hb, wret_ref[:, C_RG:C_NQ])).astype(BF16)
    o_ref[:, C_NQ:] = _dot(hb, w_ref[:, W_NQ:]).astype(BF16)
    if not rope:
        idx = lax.broadcasted_iota(jnp.int32, (LC, LANE), 0).astype(F32)
        for b in range(B):
            for h in range(RH):
                k = o_ref[LC * b:LC * (b + 1), C_RK + LANE * h:C_RK + LANE * (h + 1)].astype(F32)
                v = o_ref[LC * b:LC * (b + 1), C_RV + LANE * h:C_RV + LANE * (h + 1)]
                wf = jnp.exp((LC - 1.0 - idx) * lg_ref[0, h])
                wb = jnp.exp(idx * lg_ref[1, h])
                s0_ref[b, h, 0] = _dot((k * wf).T.astype(BF16), v)
                s0_ref[b, h, 1] = _dot((k * wb).T.astype(BF16), v)


def _four1_kernel(a_ref, x_ref, y_ref):
    x = x_ref[0].reshape(DFT_N1 * F1_R, FW)
    y = _dot(a_ref[...], x).astype(BF16)
    y_ref[0] = y.reshape(2 * DFT_N1, F1_R, FW)


def _four3_kernel(m_ref, y_ref, e_ref, wf_ref, o_ref):
    ew = _dot(e_ref[...], wf_ref[...]).astype(BF16)
    for b in range(B):
        res = []
        for kk in range(F3_K):
            y = jnp.concatenate([y_ref[b, 0, kk], y_ref[b, 1, kk]], axis=0)
            z = _dot(m_ref[kk], y).astype(BF16)
            res.append(_dot(jnp.concatenate([z[:DFT_N2], z[DFT_N2:]], axis=1), ew))
        o_ref[b] = jnp.stack(res, axis=1)


def _decay_matrix(n, lgf, lgb):
    ii = lax.broadcasted_iota(jnp.int32, (n, n), 0)
    jj = lax.broadcasted_iota(jnp.int32, (n, n), 1)
    diff = (ii - jj).astype(F32)
    fwd = jnp.where(diff >= 0, jnp.exp(jnp.maximum(diff, 0.0) * lgf), 0.0)
    bwd = jnp.where(diff <= 0, jnp.exp(jnp.maximum(-diff, 0.0) * lgb), 0.0)
    return fwd + bwd


def _ret_readout(o, gate, gain):
    ms = jnp.sum(o * o, axis=-1, keepdims=True) * (1.0 / RD)
    return (o * lax.rsqrt(ms + EPS) * gain * gate.astype(F32)).astype(BF16)


def _ret_kernel(lg_ref, q_ref, k_ref, v_ref, g_ref, s0_ref, gain_ref, o_ref, acc_ref, stf_ref, stb_ref):
    h = pl.program_id(1)
    lgf = lg_ref[0, h]
    lgb = lg_ref[1, h]
    c = RET_C
    n = L // c
    dmat = _decay_matrix(c, lgf, lgb)
    idx = lax.broadcasted_iota(jnp.int32, (c, LANE), 0).astype(F32)
    qdf = jnp.exp((idx + 1.0) * lgf)
    kdf = jnp.exp((c - 1.0 - idx) * lgf)
    qdb = jnp.exp((c - idx) * lgb)
    kdb = jnp.exp(idx * lgb)
    cdf = jnp.exp(jnp.full((1, LANE), float(c), F32) * lgf)
    cdb = jnp.exp(jnp.full((1, LANE), float(c), F32) * lgb)
    gain = gain_ref[...]

    dirs = ((stf_ref, qdf, kdf, cdf), (stb_ref, qdb, kdb, cdb))
    stf_ref[...] = s0_ref[0, 0, 0]
    stb_ref[...] = s0_ref[0, 0, 1]

    def visit(r0, direction, second):
        st_ref, qdec, kdec, cdec = dirs[direction]
        rows = pl.ds(pl.multiple_of(r0, c), c)
        q = q_ref[rows, :]
        k = k_ref[rows, :]
        v = v_ref[rows, :]
        cross = _dot(q, st_ref[...].astype(BF16)) * qdec
        if second:
            o_ref[rows, :] = _ret_readout(acc_ref[rows, :] + cross, g_ref[rows, :], gain)
        else:
            s = _dot_nt(q, k) * dmat
            acc_ref[rows, :] = _dot(s.astype(BF16), v) + cross
        kt = (k.astype(F32) * kdec).T.astype(BF16)
        st_ref[...] = st_ref[...] * cdec + _dot(kt, v)

    def body(second):
        def step(i, carry):
            visit(i * c, 0, second)
            visit((n - 1 - i) * c, 1, second)
            return carry
        return step

    lax.fori_loop(0, n // 2, body(False), 0, unroll=4)
    lax.fori_loop(n // 2, n, body(True), 0, unroll=4)


def _pair_attention(q, score_fn, value_fn):
    lane = lax.broadcasted_iota(jnp.int32, q.shape, 1)
    outs = []
    for hh in range(2):
        sel = (lane >= ND * hh) & (lane < ND * (hh + 1))
        qm = jnp.where(sel, q, jnp.zeros_like(q))
        scores = score_fn(qm, hh)
        m = functools.reduce(jnp.maximum, [jnp.max(s, axis=-1, keepdims=True) for s in scores])
        ps = [jnp.exp2(s - m) for s in scores]
        den = functools.reduce(jnp.add, [jnp.sum(p, axis=-1, keepdims=True) for p in ps])
        outs.append(value_fn([p.astype(BF16) for p in ps]) / den)
    return jnp.where(lane < ND, outs[0], outs[1])


def _ctx_mix_kernel(lg_ref, f_ref, p_ref, gain_ref, wf_ref, c_ref, s_ref, e_ref, fo_ref, ro_ref, no_ref):
    f = f_ref[...]
    pr = _dot(c_ref[...], f)
    qi = _dot(s_ref[...], f)
    fr = _dot(pr.astype(BF16), e_ref[:FW, :]) + _dot(qi.astype(BF16), e_ref[FW:, :])
    fo_ref[...] = _dot(fr.astype(BF16), wf_ref[...]).astype(BF16)
    for h in range(RH):
        sl = lambda c0: slice(c0 + LANE * h, c0 + LANE * (h + 1))
        dmat = _decay_matrix(LC, lg_ref[0, h], lg_ref[1, h])
        s = _dot_nt(p_ref[:, sl(C_RQ)], p_ref[:, sl(C_RK)]) * dmat
        o = _dot(s.astype(BF16), p_ref[:, sl(C_RV)])
        ro_ref[:, LANE * h:LANE * (h + 1)] = _ret_readout(o, p_ref[:, sl(C_RG)], gain_ref[:, LANE * h:LANE * (h + 1)])
    for pair in range(NPAIR):
        sl = lambda c0: slice(c0 + LANE * pair, c0 + LANE * (pair + 1))
        k = p_ref[:, sl(C_NK)]
        v = p_ref[:, sl(C_NV)]
        out = _pair_attention(p_ref[:, sl(C_NQ)], lambda qm, hh: [_dot_nt(qm, k)], lambda ps: _dot(ps[0], v))
        no_ref[:, LANE * pair:LANE * (pair + 1)] = out.astype(BF16)


def _na_key_start(r0):
    return min(max(r0 - 4, 0), GRID_H - NA_KR)


def _na_bias_kernel(base_ref, mask_ref, o_ref, tz_ref):
    qc = lax.broadcasted_iota(jnp.int32, (GRID_W, LANE), 0)
    for dr in range(NA_DR - 1):
        t = jnp.broadcast_to(base_ref[0, dr:dr + 1, :], (GRID_W, LANE))
        for bit in range(6):
            t = jnp.where((qc >> bit) & 1 == 1, pltpu.roll(t, 1 << bit, 1), t)
        tz_ref[dr] = t * LOG2E + mask_ref[...]
    tz_ref[NA_DR - 1] = jnp.full((GRID_W, LANE), NEG, F32)
    low_half = lax.broadcasted_iota(jnp.int32, (GRID_W, LANE), 1) < GRID_W
    for var, r0 in enumerate((0, 2 * NA_R, GRID_H - NA_R)):
        ks = _na_key_start(r0)
        for rl in range(NA_R):
            r = r0 + rl
            rs = min(max(r - 4, 0), GRID_H - 8)
            slot = [kr - r + 7 if rs <= kr < rs + 8 else NA_DR - 1 for kr in range(ks, ks + NA_KR)]
            for m in range(NA_KR // 2):
                o_ref[0, var, GRID_W * rl:GRID_W * (rl + 1), LANE * m:LANE * (m + 1)] = jnp.where(
                    low_half, tz_ref[slot[2 * m]], tz_ref[slot[2 * m + 1]])


def _natten_kernel(q_ref, k_ref, v_ref, kc_ref, vc_ref, bias_ref, o_ref, pa_ref, pb_ref):
    nq = NA_R * GRID_W
    nk = NA_KR * GRID_W
    kc = kc_ref[...]
    vc = vc_ref[...]
    lane = lax.broadcasted_iota(jnp.int32, (nq, LANE), 1)

    def rows(i):
        ks = jnp.clip(NA_R * i - 4, 0, GRID_H - NA_KR)
        return pl.ds(pl.multiple_of(ks * GRID_W, nq), nk), pl.ds(pl.multiple_of(i * nq, nq), nq)

    def probabilities(i, p_ref):
        i = jnp.minimum(i, NA_NB - 1)
        var = jnp.where(i == 0, 0, jnp.where(i == NA_NB - 1, 2, 1))
        krows, qrows = rows(i)
        q = q_ref[qrows, :]
        kw = k_ref[krows, :]
        for hh in range(2):
            qm = jnp.where((lane >= ND * hh) & (lane < ND * (hh + 1)), q, jnp.zeros_like(q))
            s_win = _dot_nt(qm, kw) + bias_ref[hh, var]
            s_ctx = _dot_nt(qm, kc)
            m = jnp.maximum(jnp.max(s_win, axis=-1, keepdims=True), jnp.max(s_ctx, axis=-1, keepdims=True))
            p_ref[hh, :, :nk] = jnp.exp2(s_win - m).astype(BF16)
            p_ref[hh, :, nk:] = jnp.exp2(s_ctx - m).astype(BF16)

    def values(i, p_ref):
        krows, qrows = rows(i)
        vals = jnp.concatenate([v_ref[krows, :], vc], axis=0)
        vals = jnp.concatenate([vals, jnp.ones_like(vals)], axis=1)
        o2 = [_dot(p_ref[hh], vals) for hh in range(2)]
        outs = [o[:, :LANE] / o[:, LANE:] for o in o2]
        o_ref[qrows, :] = jnp.where(lane < ND, outs[0], outs[1]).astype(BF16)

    probabilities(0, pa_ref)

    def step(j, carry):
        values(2 * j, pa_ref)
        probabilities(2 * j + 1, pb_ref)
        values(2 * j + 1, pb_ref)
        probabilities(2 * j + 2, pa_ref)
        return carry

    lax.fori_loop(0, NA_NB // 2, step, 0)


def _ffn_kernel(*refs, final, mod_row):
    x_ref, f_ref, r_ref, n_ref, mods_ref, g_ref, wo_ref, sel_ref, w1_ref, w3_ref, w2_ref = refs[:11]
    gf_ref = refs[11] if final else None
    o_ref, acc_ref, wor_ref = refs[-3:]
    s = pl.program_id(0)

    @pl.when(s == 0)
    def _():
        wor_ref[...] = _dot(sel_ref[...], wo_ref[FW:FW + RH * RD, :]).astype(BF16)

    mod = mods_ref[mod_row(s)]
    y = (_dot(f_ref[...].astype(BF16), wo_ref[:FW, :]) + _dot(r_ref[...], wor_ref[...])
         + _dot(n_ref[...], wo_ref[FW + RH * RD:, :]))
    x1 = x_ref[...] + mod[2:3, :] * y
    hb = (_rmsnorm(x1, g_ref[...]) * (1.0 + mod[4:5, :]) + mod[3:4, :]).astype(BF16)
    fc = 256
    for c in range(DFF // fc):
        a = _dot(hb, w1_ref[:, c * fc:(c + 1) * fc])
        b = _dot(hb, w3_ref[:, c * fc:(c + 1) * fc])
        acc_ref[:, c * fc:(c + 1) * fc] = (_silu(a) * b).astype(BF16)
    x2 = x1 + mod[5:6, :] * _dot(acc_ref[...], w2_ref[...])
    if final:
        x2 = _rmsnorm(x2, gf_ref[...])
    o_ref[...] = x2


def _params(sem):
    return pltpu.CompilerParams(dimension_semantics=sem, vmem_limit_bytes=VMEM_LIMIT)


def _const_spec(shape):
    nd = len(shape)
    return pl.BlockSpec(shape, lambda *_: (0,) * nd, pipeline_mode=pl.Buffered(1))


def _smem_spec():
    return pl.BlockSpec(memory_space=pltpu.SMEM)


def _ada(cvec, w_ada, b_ada):
    tn = 1024
    return pl.pallas_call(
        _ada_kernel,
        grid=(DEPTH, 6 * D // tn),
        in_specs=[pl.BlockSpec((8, D), lambda i, j: (0, 0)),
                  pl.BlockSpec((1, D, tn), lambda i, j: (i, 0, j)),
                  pl.BlockSpec((1, 1, tn), lambda i, j: (i, 0, j))],
        out_specs=pl.BlockSpec((1, 8, tn), lambda i, j: (i, 0, j)),
        out_shape=jax.ShapeDtypeStruct((DEPTH, 8, 6 * D), F32),
        compiler_params=_params(("arbitrary", "arbitrary")),
        name="ada",
    )(cvec, w_ada, b_ada.reshape(DEPTH, 1, 6 * D))


def _mod_row(ctx, tm):
    per_batch = L // tm
    return (lambda tile: B) if ctx else (lambda tile: tile // per_batch)


def _layer_spec(shape, layer):
    nd = len(shape)
    return pl.BlockSpec((None,) + tuple(shape), lambda *_: (layer,) + (0,) * nd, pipeline_mode=pl.Buffered(1))


def _proj(x2d, mods, g, w_all, layer, sel_qk, sel_v, extra, *, ctx):
    rows = x2d.shape[0]
    tm = min(TM_PROJ, rows)
    per_batch = L // tm
    in_specs = [pl.BlockSpec((tm, D), lambda i: (i, 0)),
                _const_spec((8, 6, D)),
                _const_spec((1, D)),
                _layer_spec((D, W_IN), layer),
                _const_spec((RH * RD, RH * LANE)),
                _const_spec((RH * RD, RH * LANE))]
    args = [x2d, mods, g.reshape(1, D), w_all, sel_qk, sel_v]
    out_specs = [pl.BlockSpec((tm, FW), lambda i: (i, 0)), pl.BlockSpec((tm, NP), lambda i: (i, 0))]
    out_shape = [jax.ShapeDtypeStruct((rows, FW), BF16), jax.ShapeDtypeStruct((rows, NP), BF16)]
    if ctx:
        assert rows == tm == B * LC
        in_specs.append(_smem_spec())
        args.append(extra)
        out_specs.append(pl.BlockSpec((B, RH, 2, LANE, LANE), lambda i: (0, 0, 0, 0, 0)))
        out_shape.append(jax.ShapeDtypeStruct((B, RH, 2, LANE, LANE), F32))
    else:
        in_specs += [pl.BlockSpec((tm, LANE), lambda i: (i % per_batch, 0))] * 2
        args += list(extra)
    return pl.pallas_call(
        functools.partial(_proj_kernel, rope=not ctx, mod_row=_mod_row(ctx, tm)),
        grid=(rows // tm,),
        in_specs=in_specs,
        out_specs=out_specs,
        out_shape=out_shape,
        scratch_shapes=[pltpu.VMEM((D, 4 * RH * LANE), BF16)],
        compiler_params=_params(("arbitrary",)),
        name="proj_ctx" if ctx else "proj",
    )(*args)


def _fourier(f2d, wf, a1, m3, e):
    y = pl.pallas_call(
        _four1_kernel,
        grid=(B, DFT_N2 // F1_R),
        in_specs=[_const_spec((2 * DFT_N1 * F1_R, DFT_N1 * F1_R)),
                  pl.BlockSpec((1, DFT_N1, F1_R, FW), lambda b, j: (b, 0, j, 0))],
        out_specs=pl.BlockSpec((1, 2 * DFT_N1, F1_R, FW), lambda b, j: (b, 0, j, 0)),
        out_shape=jax.ShapeDtypeStruct((B, 2 * DFT_N1, DFT_N2, FW), BF16),
        compiler_params=_params(("arbitrary", "arbitrary")),
        name="four1",
    )(a1, f2d.reshape(B, DFT_N1, DFT_N2, FW))
    out = pl.pallas_call(
        _four3_kernel,
        grid=(DFT_N1 // F3_K,),
        in_specs=[pl.BlockSpec((F3_K, 2 * DFT_N2, 2 * DFT_N2), lambda k: (k, 0, 0)),
                  pl.BlockSpec((B, 2, F3_K, DFT_N2, FW), lambda k: (0, 0, k, 0, 0)),
                  _const_spec((2 * FW, FW)),
                  _const_spec((FW, FW))],
        out_specs=pl.BlockSpec((B, DFT_N2, F3_K, FW), lambda k: (0, 0, k, 0)),
        out_shape=jax.ShapeDtypeStruct((B, DFT_N2, DFT_N1, FW), F32),
        compiler_params=_params(("arbitrary",)),
        name="four3",
    )(m3, y.reshape(B, 2, DFT_N1, DFT_N2, FW), e, wf)
    return out.reshape(B * L, FW)


def _retention(p, s0, lg, gain):
    col = lambda c0: (lambda b, h: (b, c0 // LANE + h))
    return pl.pallas_call(
        _ret_kernel,
        grid=(B, RH),
        in_specs=[_smem_spec(),
                  pl.BlockSpec((L, LANE), col(C_RQ)),
                  pl.BlockSpec((L, LANE), col(C_RK)),
                  pl.BlockSpec((L, LANE), col(C_RV)),
                  pl.BlockSpec((L, LANE), col(C_RG)),
                  pl.BlockSpec((1, 1, 2, LANE, LANE), lambda b, h: (b, h, 0, 0, 0)),
                  pl.BlockSpec((1, LANE), lambda b, h: (0, h))],
        out_specs=pl.BlockSpec((L, LANE), lambda b, h: (b, h)),
        out_shape=jax.ShapeDtypeStruct((B * L, RH * LANE), BF16),
        scratch_shapes=[pltpu.VMEM((L, LANE), F32), pltpu.VMEM((LANE, LANE), F32), pltpu.VMEM((LANE, LANE), F32)],
        compiler_params=_params(("arbitrary", "arbitrary")),
        name="ret",
    )(lg, p, p, p, p, s0, gain)


def _ctx_mix(fc, pc, lg, gain, wf, c256, s256, e):
    return pl.pallas_call(
        _ctx_mix_kernel,
        grid=(B,),
        in_specs=[_smem_spec(),
                  pl.BlockSpec((LC, FW), lambda b: (b, 0)),
                  pl.BlockSpec((LC, NP), lambda b: (b, 0)),
                  _const_spec((1, RH * LANE)),
                  _const_spec((FW, FW)),
                  _const_spec((LC, LC)),
                  _const_spec((LC, LC)),
                  _const_spec((2 * FW, FW))],
        out_specs=[pl.BlockSpec((LC, FW), lambda b: (b, 0)),
                   pl.BlockSpec((LC, RH * LANE), lambda b: (b, 0)),
                   pl.BlockSpec((LC, NH * ND), lambda b: (b, 0))],
        out_shape=[jax.ShapeDtypeStruct((B * LC, FW), BF16),
                   jax.ShapeDtypeStruct((B * LC, RH * LANE), BF16),
                   jax.ShapeDtypeStruct((B * LC, NH * ND), BF16)],
        compiler_params=_params(("arbitrary",)),
        name="ctx_mix",
    )(lg, fc, pc, gain, wf, c256, s256, e)


def _na_bias(base, mask):
    nq, nk = NA_R * GRID_W, NA_KR * GRID_W
    return pl.pallas_call(
        _na_bias_kernel,
        grid=(base.shape[0],),
        in_specs=[pl.BlockSpec((1, NA_DR, LANE), lambda h: (h, 0, 0)),
                  _const_spec((GRID_W, LANE))],
        out_specs=pl.BlockSpec((1, 3, nq, nk), lambda h: (h, 0, 0, 0)),
        out_shape=jax.ShapeDtypeStruct((base.shape[0], 3, nq, nk), F32),
        scratch_shapes=[pltpu.VMEM((NA_DR, GRID_W, LANE), F32)],
        compiler_params=_params(("arbitrary",)),
        name="na_bias",
    )(base, mask)


def _natten(p, pc, bias, layer):
    nq, nk = NA_R * GRID_W, NA_KR * GRID_W
    return pl.pallas_call(
        _natten_kernel,
        grid=(B, NPAIR),
        in_specs=[pl.BlockSpec((L, LANE), lambda b, pr: (b, C_NQ // LANE + pr)),
                  pl.BlockSpec((L, LANE), lambda b, pr: (b, C_NK // LANE + pr)),
                  pl.BlockSpec((L, LANE), lambda b, pr: (b, C_NV // LANE + pr)),
                  pl.BlockSpec((LC, LANE), lambda b, pr: (b, C_NK // LANE + pr)),
                  pl.BlockSpec((LC, LANE), lambda b, pr: (b, C_NV // LANE + pr)),
                  pl.BlockSpec((2, 3, nq, nk), lambda b, pr: (NPAIR * layer + pr, 0, 0, 0))],
        out_specs=pl.BlockSpec((L, LANE), lambda b, pr: (b, pr)),
        out_shape=jax.ShapeDtypeStruct((B * L, NH * ND), BF16),
        scratch_shapes=[pltpu.VMEM((2, nq, nk + LC), BF16)] * 2,
        compiler_params=_params(("arbitrary", "arbitrary")),
        name="natten",
    )(p, p, p, pc, pc, bias)


def _ffn(x2d, four, ret, na, mods, g, wo_all, sel_v_t, w1_all, w3_all, w2_all, layer, g_final, *, ctx):
    rows = x2d.shape[0]
    final = g_final is not None
    row = lambda w: pl.BlockSpec((TM, w), lambda i: (i, 0))
    in_specs = [row(D), row(FW), row(RH * LANE), row(NH * ND),
                _const_spec((8, 6, D)), _const_spec((1, D)),
                _layer_spec((D, D), layer), _const_spec((RH * LANE, RH * RD)),
                _layer_spec((D, DFF), layer), _layer_spec((D, DFF), layer), _layer_spec((DFF, D), layer)]
    args = [x2d, four, ret, na, mods, g.reshape(1, D), wo_all, sel_v_t, w1_all, w3_all, w2_all]
    if final:
        in_specs.append(_const_spec((1, D)))
        args.append(g_final.reshape(1, D))
    return pl.pallas_call(
        functools.partial(_ffn_kernel, final=final, mod_row=_mod_row(ctx, TM)),
        grid=(rows // TM,),
        in_specs=in_specs,
        out_specs=row(D),
        out_shape=jax.ShapeDtypeStruct((rows, D), F32),
        scratch_shapes=[pltpu.VMEM((TM, DFF), BF16), pltpu.VMEM((RH * LANE, D), BF16)],
        compiler_params=_params(("arbitrary",)),
        name="ffn_ctx" if ctx else ("ffn_final" if final else "ffn"),
    )(*args)


def _take_padded(a, src, axis):
    pieces = []
    i, n = 0, len(src)
    while i < n:
        j = i + 1
        if src[i] < 0:
            while j < n and src[j] < 0:
                j += 1
            shape = list(a.shape)
            shape[axis] = j - i
            pieces.append(jnp.zeros(shape, a.dtype))
        else:
            while j < n and src[j] == src[j - 1] + 1:
                j += 1
            pieces.append(lax.slice_in_dim(a, int(src[i]), int(src[j - 1]) + 1, axis=axis))
        i = j
    return jnp.concatenate(pieces, axis=axis)


def kernel(x, c, ctx, c_ctx, w_ada, b_ada, g_mix, w_in, ret_decay_logit, ret_norm_g, w_four,
           na_rpb, w_out, g_ffn, w1, w3, w2, g_final):
    src_g = _gain_sources()
    rope_tabs = tuple(jnp.asarray(t) for t in _rope_tables())
    a1, m3, e, c256, s256 = (jnp.asarray(t).astype(BF16) for t in _dft_tables())
    sel_qk, sel_v = (jnp.asarray(t).astype(BF16) for t in _head_selectors())
    sel_v_t = sel_v.T
    tz_src, tz_mask = _na_toeplitz_tables()
    col_scale = np.ones((1, 1, W_IN), np.float32)
    col_scale[..., W_NQ:W_NK] = LOG2E * ND ** -0.5
    w_in_b = (w_in * jnp.asarray(col_scale)).astype(BF16)
    w_out_b = w_out.astype(BF16)
    w1b, w3b, w2b = w1.astype(BF16), w3.astype(BF16), w2.astype(BF16)

    tz_base = jnp.pad(_take_padded(na_rpb.astype(F32).reshape(DEPTH * NH, 15, 31), tz_src, 2), ((0, 0), (0, 1), (0, 0)))
    na_bias = _na_bias(tz_base, jnp.asarray(tz_mask))

    cvec = jnp.zeros((8, D), F32).at[0:B].set(c).at[B].set(c_ctx)
    mods_all = _ada(cvec, w_ada, b_ada).reshape(DEPTH, 8, 6, D)

    xl = x.reshape(B * L, D)
    xc = ctx.reshape(B * LC, D)
    for i in range(DEPTH):
        last = i == DEPTH - 1
        mods = mods_all[i]
        gain = _take_padded(ret_norm_g[i], src_g, 0).reshape(1, RH * LANE)
        wfour = w_four[i].astype(BF16)
        lg = jax.nn.log_sigmoid(ret_decay_logit[i].astype(F32))

        f_l, p_l = _proj(xl, mods, g_mix[i], w_in_b, i, sel_qk, sel_v, rope_tabs, ctx=False)
        f_c, p_c, s0 = _proj(xc, mods, g_mix[i], w_in_b, i, sel_qk, sel_v, lg, ctx=True)
        four_l = _fourier(f_l, wfour, a1, m3, e)
        ret_l = _retention(p_l, s0, lg, gain)
        na_l = _natten(p_l, p_c, na_bias, i)
        if not last:
            four_c, ret_c, na_c = _ctx_mix(f_c, p_c, lg, gain, wfour, c256, s256, e)
            xc = _ffn(xc, four_c, ret_c, na_c, mods, g_ffn[i], w_out_b, sel_v_t, w1b, w3b, w2b, i, None, ctx=True)
        xl = _ffn(xl, four_l, ret_l, na_l, mods, g_ffn[i], w_out_b, sel_v_t, w1b, w3b, w2b, i,
                  g_final if last else None, ctx=False)
    return xl.reshape(B, L, D)
```

```python
import functools

import numpy as np
import jax
import jax.numpy as jnp
from jax import lax
from jax.experimental import pallas as pl
from jax.experimental.pallas import tpu as pltpu

F32 = jnp.float32
BF16 = jnp.bfloat16

D = 1024
B = 2
L = 8192
LC = 256
DEPTH = 2
GRID_W = 64
GRID_H = L // GRID_W
FW = 256
FGW = 64
RH = 4
RD = 96
NH = 6
ND = 64
NPAIR = NH // 2
DFF = 2816
EPS = 1e-6
NEG = -1e30
LOG2E = float(np.log2(np.e))
LANE = 128

C_RQ, C_RK, C_RV, C_RG = 0, 512, 1024, 1536
C_NQ, C_NK, C_NV = 2048, 2432, 2816
NP = 3200
W_RQ, W_RK, W_RV, W_RG = 256, 640, 1024, 1408
W_NQ, W_NK, W_NV = 1792, 2176, 2560
W_IN = 2944

TM = 512
TM_PROJ = 1024
RET_C = 256
NA_R = 4
NA_KR = NA_R + 8
NA_NB = GRID_H // NA_R
DFT_N1 = 64
DFT_N2 = 128
F1_R = 16
F3_K = 8
VMEM_LIMIT = 56 * 1024 * 1024


def _qk_lane_dims():
    m = np.full((LANE,), -1, np.int64)
    m[0:24] = np.arange(0, 24)
    m[24:48] = np.arange(48, 72)
    m[64:88] = np.arange(24, 48)
    m[88:112] = np.arange(72, 96)
    return m


def _v_lane_dims():
    m = np.full((LANE,), -1, np.int64)
    m[:RD] = np.arange(RD)
    return m


def _head_selectors():
    sels = []
    for lanes in (_qk_lane_dims(), _v_lane_dims()):
        sel = np.zeros((RH * RD, RH * LANE), np.float32)
        ok = np.nonzero(lanes >= 0)[0]
        for h in range(RH):
            sel[RD * h + lanes[ok], LANE * h + ok] = 1.0
        sels.append(sel)
    return sels


def _gain_sources():
    src = np.full((RH * LANE,), -1, np.int64)
    for h in range(RH):
        src[LANE * h: LANE * h + RD] = RD * h + np.arange(RD)
    return src


def _rope_tables():
    pos = np.arange(L)
    prow, pcol = pos // GRID_W, pos % GRID_W
    half = RD // 4
    inv = 10000.0 ** (-np.arange(half, dtype=np.float64) / half)
    ar = prow[:, None] * inv[None, :]
    ac = pcol[:, None] * inv[None, :]
    cos = np.ones((L, LANE), np.float64)
    sin = np.zeros((L, LANE), np.float64)
    for off, sign in ((0, -1.0), (64, 1.0)):
        cos[:, off:off + 24] = np.cos(ar)
        cos[:, off + 24:off + 48] = np.cos(ac)
        sin[:, off:off + 24] = sign * np.sin(ar)
        sin[:, off + 24:off + 48] = sign * np.sin(ac)
    return cos.astype(np.float32), sin.astype(np.float32)


def _dft_tables():
    k1 = np.arange(DFT_N1)
    a = 2 * np.pi * ((k1[:, None] * k1[None, :]) % DFT_N1) / DFT_N1
    a1 = np.kron(np.concatenate([np.cos(a), -np.sin(a)], axis=0), np.eye(F1_R))
    k2 = np.arange(DFT_N2)
    l2 = np.arange(DFT_N2)
    kk = k1[:, None, None] + DFT_N1 * k2[None, :, None]
    ang = 2 * np.pi * ((kk * l2[None, None, :]) % L) / L
    ct, st = np.cos(ang) / np.sqrt(L), np.sin(ang) / np.sqrt(L)
    m3 = np.concatenate([np.concatenate([ct, st], axis=2), np.concatenate([st, -ct], axis=2)], axis=1)
    c = np.arange(FW)
    same = (c[:, None] // FGW) == (c[None, :] // FGW)
    ac = 2 * np.pi * (((c[:, None] % FGW) * (c[None, :] % FGW)) % FGW) / FGW
    e = np.concatenate([np.where(same, np.cos(ac), 0.0), -np.where(same, np.sin(ac), 0.0)], axis=0) / 8.0
    p = np.arange(LC)
    ap = 2 * np.pi * ((p[:, None] * p[None, :]) % LC) / LC
    return tuple(t.astype(np.float32) for t in (a1, m3, e, np.cos(ap) / 16.0, np.sin(ap) / 16.0))


NA_DR = 16


def _na_toeplitz_tables():
    src = np.full((LANE,), -1, np.int64)
    src[0:16] = np.arange(0, 16) + 15
    src[113:128] = np.arange(113, 128) - 113
    src[49:80] = np.arange(49, 80) - 49
    qc = np.arange(GRID_W)[:, None]
    kc = np.arange(LANE)[None, :] % GRID_W
    ws = np.clip(qc - 8, 0, GRID_W - 16)
    mask = np.where((kc >= ws) & (kc < ws + 16), 0.0, NEG).astype(np.float32)
    return src, mask


def _dot(a, b):
    return jnp.dot(a, b, preferred_element_type=F32)


def _dot_nt(a, b):
    return lax.dot_general(a, b, (((1,), (1,)), ((), ())), preferred_element_type=F32)


def _silu(x):
    return x * jax.nn.sigmoid(x)


def _rmsnorm(x, g):
    return x * lax.rsqrt(jnp.mean(x * x, axis=-1, keepdims=True) + EPS) * g


def _ada_kernel(c_ref, w_ref, b_ref, o_ref):
    s = _silu(c_ref[...]).astype(BF16)
    o_ref[0] = _dot(s, w_ref[0].astype(BF16)) + b_ref[0]


def _proj_kernel(*refs, rope, mod_row):
    if rope:
        x_ref, mods_ref, g_ref, w_ref, pqk_ref, pv_ref, cos_ref, sin_ref, f_ref, o_ref, wret_ref = refs
    else:
        x_ref, mods_ref, g_ref, w_ref, pqk_ref, pv_ref, lg_ref, f_ref, o_ref, s0_ref, wret_ref = refs
    s = pl.program_id(0)

    @pl.when(s == 0)
    def _():
        for slot, (src0, sel_ref) in enumerate(((W_RQ, pqk_ref), (W_RK, pqk_ref), (W_RV, pv_ref), (W_RG, pv_ref))):
            wret_ref[:, RH * LANE * slot:RH * LANE * (slot + 1)] = _dot(
                w_ref[:, src0:src0 + RH * RD], sel_ref[...]).astype(BF16)

    mod = mods_ref[mod_row(s)]
    hb = (_rmsnorm(x_ref[...], g_ref[...]) * (1.0 + mod[1:2, :]) + mod[0:1, :]).astype(BF16)
    f_ref[...] = _dot(hb, w_ref[:, :FW]).astype(BF16)
    k_scale = RD ** -0.5
    for c0, scale in ((C_RQ, None), (C_RK, k_scale)):
        t4 = _dot(hb, wret_ref[:, c0:c0 + RH * LANE])
        for hh in range(RH):
            t = t4[:, LANE * hh:LANE * (hh + 1)]
            if rope:
                t = t * cos_ref[...] + pltpu.roll(t, 64, 1) * sin_ref[...]
            if scale is not None:
                t = t * scale
            o_ref[:, c0 + LANE * hh:c0 + LANE * (hh + 1)] = t.astype(BF16)
    o_ref[:, C_RV:C_RG] = _dot(hb, wret_ref[:, C_RV:C_RG]).astype(BF16)
    o_ref[:, C_RG:C_NQ] = _silu(_dot(hb, wret_ref[:, C_RG:C_NQ])).astype(BF16)
    o_ref[:, C_NQ:] = _dot(hb, w_ref[:, W_NQ:]).astype(BF16)
    if not rope:
        idx = lax.broadcasted_iota(jnp.int32, (LC, LANE), 0).astype(F32)
        for b in range(B):
            for h in range(RH):
                k = o_ref[LC * b:LC * (b + 1), C_RK + LANE * h:C_RK + LANE * (h + 1)].astype(F32)
                v = o_ref[LC * b:LC * (b + 1), C_RV + LANE * h:C_RV + LANE * (h + 1)]
                wf = jnp.exp((LC - 1.0 - idx) * lg_ref[0, h])
                wb = jnp.exp(idx * lg_ref[1, h])
                s0_ref[b, h, 0] = _dot((k * wf).T.astype(BF16), v)
                s0_ref[b, h, 1] = _dot((k * wb).T.astype(BF16), v)


def _four1_kernel(a_ref, x_ref, y_ref):
    x = x_ref[0].reshape(DFT_N1 * F1_R, FW)
    y = _dot(a_ref[...], x).astype(BF16)
    y_ref[0] = y.reshape(2 * DFT_N1, F1_R, FW)


def _four3_kernel(m_ref, y_ref, e_ref, wf_ref, o_ref):
    ew = _dot(e_ref[...], wf_ref[...]).astype(BF16)
    for b in range(B):
        res = []
        for kk in range(F3_K):
            y = jnp.concatenate([y_ref[b, 0, kk], y_ref[b, 1, kk]], axis=0)
            z = _dot(m_ref[kk], y).astype(BF16)
            res.append(_dot(jnp.concatenate([z[:DFT_N2], z[DFT_N2:]], axis=1), ew))
        o_ref[b] = jnp.stack(res, axis=1)


def _decay_matrix(n, lgf, lgb):
    ii = lax.broadcasted_iota(jnp.int32, (n, n), 0)
    jj = lax.broadcasted_iota(jnp.int32, (n, n), 1)
    diff = (ii - jj).astype(F32)
    fwd = jnp.where(diff >= 0, jnp.exp(jnp.maximum(diff, 0.0) * lgf), 0.0)
    bwd = jnp.where(diff <= 0, jnp.exp(jnp.maximum(-diff, 0.0) * lgb), 0.0)
    return fwd + bwd


def _ret_readout(o, gate, gain):
    ms = jnp.sum(o * o, axis=-1, keepdims=True) * (1.0 / RD)
    return (o * lax.rsqrt(ms + EPS) * gain * gate.astype(F32)).astype(BF16)


def _ret_kernel(lg_ref, q_ref, k_ref, v_ref, g_ref, s0_ref, gain_ref, o_ref, acc_ref, stf_ref, stb_ref):
    h = pl.program_id(1)
    lgf = lg_ref[0, h]
    lgb = lg_ref[1, h]
    c = RET_C
    n = L // c
    dmat = _decay_matrix(c, lgf, lgb)
    idx = lax.broadcasted_iota(jnp.int32, (c, LANE), 0).astype(F32)
    qdf = jnp.exp((idx + 1.0) * lgf)
    kdf = jnp.exp((c - 1.0 - idx) * lgf)
    qdb = jnp.exp((c - idx) * lgb)
    kdb = jnp.exp(idx * lgb)
    cdf = jnp.exp(jnp.full((1, LANE), float(c), F32) * lgf)
    cdb = jnp.exp(jnp.full((1, LANE), float(c), F32) * lgb)
    gain = gain_ref[...]

    dirs = ((stf_ref, qdf, kdf, cdf), (stb_ref, qdb, kdb, cdb))
    stf_ref[...] = s0_ref[0, 0, 0]
    stb_ref[...] = s0_ref[0, 0, 1]

    def visit(r0, direction, second):
        st_ref, qdec, kdec, cdec = dirs[direction]
        rows = pl.ds(pl.multiple_of(r0, c), c)
        q = q_ref[rows, :]
        k = k_ref[rows, :]
        v = v_ref[rows, :]
        cross = _dot(q, st_ref[...].astype(BF16)) * qdec
        if second:
            o_ref[rows, :] = _ret_readout(acc_ref[rows, :] + cross, g_ref[rows, :], gain)
        else:
            s = _dot_nt(q, k) * dmat
            acc_ref[rows, :] = _dot(s.astype(BF16), v) + cross
        kt = (k.astype(F32) * kdec).T.astype(BF16)
        st_ref[...] = st_ref[...] * cdec + _dot(kt, v)

    def body(second):
        def step(i, carry):
            visit(i * c, 0, second)
            visit((n - 1 - i) * c, 1, second)
            return carry
        return step

    lax.fori_loop(0, n // 2, body(False), 0, unroll=4)
    lax.fori_loop(n // 2, n, body(True), 0, unroll=4)


def _pair_attention(q, score_fn, value_fn):
    lane = lax.broadcasted_iota(jnp.int32, q.shape, 1)
    outs = []
    for hh in range(2):
        sel = (lane >= ND * hh) & (lane < ND * (hh + 1))
        qm = jnp.where(sel, q, jnp.zeros_like(q))
        scores = score_fn(qm, hh)
        m = functools.reduce(jnp.maximum, [jnp.max(s, axis=-1, keepdims=True) for s in scores])
        ps = [jnp.exp2(s - m) for s in scores]
        den = functools.reduce(jnp.add, [jnp.sum(p, axis=-1, keepdims=True) for p in ps])
        outs.append(value_fn([p.astype(BF16) for p in ps]) / den)
    return jnp.where(lane < ND, outs[0], outs[1])


def _ctx_mix_kernel(lg_ref, f_ref, p_ref, gain_ref, wf_ref, c_ref, s_ref, e_ref, fo_ref, ro_ref, no_ref):
    f = f_ref[...]
    pr = _dot(c_ref[...], f)
    qi = _dot(s_ref[...], f)
    fr = _dot(pr.astype(BF16), e_ref[:FW, :]) + _dot(qi.astype(BF16), e_ref[FW:, :])
    fo_ref[...] = _dot(fr.astype(BF16), wf_ref[...]).astype(BF16)
    for h in range(RH):
        sl = lambda c0: slice(c0 + LANE * h, c0 + LANE * (h + 1))
        dmat = _decay_matrix(LC, lg_ref[0, h], lg_ref[1, h])
        s = _dot_nt(p_ref[:, sl(C_RQ)], p_ref[:, sl(C_RK)]) * dmat
        o = _dot(s.astype(BF16), p_ref[:, sl(C_RV)])
        ro_ref[:, LANE * h:LANE * (h + 1)] = _ret_readout(o, p_ref[:, sl(C_RG)], gain_ref[:, LANE * h:LANE * (h + 1)])
    for pair in range(NPAIR):
        sl = lambda c0: slice(c0 + LANE * pair, c0 + LANE * (pair + 1))
        k = p_ref[:, sl(C_NK)]
        v = p_ref[:, sl(C_NV)]
        out = _pair_attention(p_ref[:, sl(C_NQ)], lambda qm, hh: [_dot_nt(qm, k)], lambda ps: _dot(ps[0], v))
        no_ref[:, LANE * pair:LANE * (pair + 1)] = out.astype(BF16)


def _na_key_start(r0):
    return min(max(r0 - 4, 0), GRID_H - NA_KR)


def _na_bias_kernel(base_ref, mask_ref, o_ref, tz_ref):
    for dr in range(NA_DR - 1):
        t = jnp.broadcast_to(base_ref[0, dr:dr + 1, :], (GRID_W, LANE))
        t = pltpu.roll(t, 0, 1, stride=1, stride_axis=0)
        tz_ref[dr] = t * LOG2E + mask_ref[...]
    tz_ref[NA_DR - 1] = jnp.full((GRID_W, LANE), NEG, F32)
    low_half = lax.broadcasted_iota(jnp.int32, (GRID_W, LANE), 1) < GRID_W
    for var, r0 in enumerate((0, 2 * NA_R, GRID_H - NA_R)):
        ks = _na_key_start(r0)
        for rl in range(NA_R):
            r = r0 + rl
            rs = min(max(r - 4, 0), GRID_H - 8)
            slot = [kr - r + 7 if rs <= kr < rs + 8 else NA_DR - 1 for kr in range(ks, ks + NA_KR)]
            for m in range(NA_KR // 2):
                o_ref[0, var, GRID_W * rl:GRID_W * (rl + 1), LANE * m:LANE * (m + 1)] = jnp.where(
                    low_half, tz_ref[slot[2 * m]], tz_ref[slot[2 * m + 1]])


def _natten_kernel(q_ref, k_ref, v_ref, kc_ref, vc_ref, bias_ref, o_ref, pa_ref, pb_ref):
    nq = NA_R * GRID_W
    nk = NA_KR * GRID_W
    kc = kc_ref[...]
    vc = vc_ref[...]
    lane = lax.broadcasted_iota(jnp.int32, (nq, LANE), 1)

    def rows(i):
        ks = jnp.clip(NA_R * i - 4, 0, GRID_H - NA_KR)
        return pl.ds(pl.multiple_of(ks * GRID_W, nq), nk), pl.ds(pl.multiple_of(i * nq, nq), nq)

    def probabilities(i, p_ref):
        i = jnp.minimum(i, NA_NB - 1)
        var = jnp.where(i == 0, 0, jnp.where(i == NA_NB - 1, 2, 1))
        krows, qrows = rows(i)
        q = q_ref[qrows, :]
        kw = k_ref[krows, :]
        for hh in range(2):
            qm = jnp.where((lane >= ND * hh) & (lane < ND * (hh + 1)), q, jnp.zeros_like(q))
            s_win = _dot_nt(qm, kw) + bias_ref[hh, var]
            s_ctx = _dot_nt(qm, kc)
            m = jnp.maximum(jnp.max(s_win, axis=-1, keepdims=True), jnp.max(s_ctx, axis=-1, keepdims=True))
            p_ref[hh, :, :nk] = jnp.exp2(s_win - m).astype(BF16)
            p_ref[hh, :, nk:] = jnp.exp2(s_ctx - m).astype(BF16)

    def values(i, p_ref):
        krows, qrows = rows(i)
        vals = jnp.concatenate([v_ref[krows, :], vc], axis=0)
        vals = jnp.concatenate([vals, jnp.ones_like(vals)], axis=1)
        o2 = [_dot(p_ref[hh], vals) for hh in range(2)]
        outs = [o[:, :LANE] / o[:, LANE:] for o in o2]
        o_ref[qrows, :] = jnp.where(lane < ND, outs[0], outs[1]).astype(BF16)

    probabilities(0, pa_ref)

    def step(j, carry):
        values(2 * j, pa_ref)
        probabilities(2 * j + 1, pb_ref)
        values(2 * j + 1, pb_ref)
        probabilities(2 * j + 2, pa_ref)
        return carry

    lax.fori_loop(0, NA_NB // 2, step, 0)


def _ffn_kernel(*refs, final, mod_row):
    x_ref, f_ref, r_ref, n_ref, mods_ref, g_ref, wo_ref, sel_ref, w1_ref, w3_ref, w2_ref = refs[:11]
    gf_ref = refs[11] if final else None
    o_ref, acc_ref, wor_ref = refs[-3:]
    s = pl.program_id(0)

    @pl.when(s == 0)
    def _():
        wor_ref[...] = _dot(sel_ref[...], wo_ref[FW:FW + RH * RD, :]).astype(BF16)

    mod = mods_ref[mod_row(s)]
    y = (_dot(f_ref[...].astype(BF16), wo_ref[:FW, :]) + _dot(r_ref[...], wor_ref[...])
         + _dot(n_ref[...], wo_ref[FW + RH * RD:, :]))
    x1 = x_ref[...] + mod[2:3, :] * y
    hb = (_rmsnorm(x1, g_ref[...]) * (1.0 + mod[4:5, :]) + mod[3:4, :]).astype(BF16)
    fc = 256
    for c in range(DFF // fc):
        a = _dot(hb, w1_ref[:, c * fc:(c + 1) * fc])
        b = _dot(hb, w3_ref[:, c * fc:(c + 1) * fc])
        acc_ref[:, c * fc:(c + 1) * fc] = (_silu(a) * b).astype(BF16)
    x2 = x1 + mod[5:6, :] * _dot(acc_ref[...], w2_ref[...])
    if final:
        x2 = _rmsnorm(x2, gf_ref[...])
    o_ref[...] = x2


def _params(sem):
    return pltpu.CompilerParams(dimension_semantics=sem, vmem_limit_bytes=VMEM_LIMIT)


def _const_spec(shape):
    nd = len(shape)
    return pl.BlockSpec(shape, lambda *_: (0,) * nd, pipeline_mode=pl.Buffered(1))


def _smem_spec():
    return pl.BlockSpec(memory_space=pltpu.SMEM)


def _ada(cvec, w_ada, b_ada):
    tn = 1024
    return pl.pallas_call(
        _ada_kernel,
        grid=(DEPTH, 6 * D // tn),
        in_specs=[pl.BlockSpec((8, D), lambda i, j: (0, 0)),
                  pl.BlockSpec((1, D, tn), lambda i, j: (i, 0, j)),
                  pl.BlockSpec((1, 1, tn), lambda i, j: (i, 0, j))],
        out_specs=pl.BlockSpec((1, 8, tn), lambda i, j: (i, 0, j)),
        out_shape=jax.ShapeDtypeStruct((DEPTH, 8, 6 * D), F32),
        compiler_params=_params(("arbitrary", "arbitrary")),
        name="ada",
    )(cvec, w_ada, b_ada.reshape(DEPTH, 1, 6 * D))


def _mod_row(ctx, tm):
    per_batch = L // tm
    return (lambda tile: B) if ctx else (lambda tile: tile // per_batch)


def _layer_spec(shape, layer):
    nd = len(shape)
    return pl.BlockSpec((None,) + tuple(shape), lambda *_: (layer,) + (0,) * nd, pipeline_mode=pl.Buffered(1))


def _proj(x2d, mods, g, w_all, layer, sel_qk, sel_v, extra, *, ctx):
    rows = x2d.shape[0]
    tm = min(TM_PROJ, rows)
    per_batch = L // tm
    in_specs = [pl.BlockSpec((tm, D), lambda i: (i, 0)),
                _const_spec((8, 6, D)),
                _const_spec((1, D)),
                _layer_spec((D, W_IN), layer),
                _const_spec((RH * RD, RH * LANE)),
                _const_spec((RH * RD, RH * LANE))]
    args = [x2d, mods, g.reshape(1, D), w_all, sel_qk, sel_v]
    out_specs = [pl.BlockSpec((tm, FW), lambda i: (i, 0)), pl.BlockSpec((tm, NP), lambda i: (i, 0))]
    out_shape = [jax.ShapeDtypeStruct((rows, FW), BF16), jax.ShapeDtypeStruct((rows, NP), BF16)]
    if ctx:
        assert rows == tm == B * LC
        in_specs.append(_smem_spec())
        args.append(extra)
        out_specs.append(pl.BlockSpec((B, RH, 2, LANE, LANE), lambda i: (0, 0, 0, 0, 0)))
        out_shape.append(jax.ShapeDtypeStruct((B, RH, 2, LANE, LANE), F32))
    else:
        in_specs += [pl.BlockSpec((tm, LANE), lambda i: (i % per_batch, 0))] * 2
        args += list(extra)
    return pl.pallas_call(
        functools.partial(_proj_kernel, rope=not ctx, mod_row=_mod_row(ctx, tm)),
        grid=(rows // tm,),
        in_specs=in_specs,
        out_specs=out_specs,
        out_shape=out_shape,
        scratch_shapes=[pltpu.VMEM((D, 4 * RH * LANE), BF16)],
        compiler_params=_params(("arbitrary",)),
        name="proj_ctx" if ctx else "proj",
    )(*args)


def _fourier(f2d, wf, a1, m3, e):
    y = pl.pallas_call(
        _four1_kernel,
        grid=(B, DFT_N2 // F1_R),
        in_specs=[_const_spec((2 * DFT_N1 * F1_R, DFT_N1 * F1_R)),
                  pl.BlockSpec((1, DFT_N1, F1_R, FW), lambda b, j: (b, 0, j, 0))],
        out_specs=pl.BlockSpec((1, 2 * DFT_N1, F1_R, FW), lambda b, j: (b, 0, j, 0)),
        out_shape=jax.ShapeDtypeStruct((B, 2 * DFT_N1, DFT_N2, FW), BF16),
        compiler_params=_params(("arbitrary", "arbitrary")),
        name="four1",
    )(a1, f2d.reshape(B, DFT_N1, DFT_N2, FW))
    out = pl.pallas_call(
        _four3_kernel,
        grid=(DFT_N1 // F3_K,),
        in_specs=[pl.BlockSpec((F3_K, 2 * DFT_N2, 2 * DFT_N2), lambda k: (k, 0, 0)),
                  pl.BlockSpec((B, 2, F3_K, DFT_N2, FW), lambda k: (0, 0, k, 0, 0)),
                  _const_spec((2 * FW, FW)),
                  _const_spec((FW, FW))],
        out_specs=pl.BlockSpec((B, DFT_N2, F3_K, FW), lambda k: (0, 0, k, 0)),
        out_shape=jax.ShapeDtypeStruct((B, DFT_N2, DFT_N1, FW), F32),
        compiler_params=_params(("arbitrary",)),
        name="four3",
    )(m3, y.reshape(B, 2, DFT_N1, DFT_N2, FW), e, wf)
    return out.reshape(B * L, FW)


def _retention(p, s0, lg, gain):
    col = lambda c0: (lambda b, h: (b, c0 // LANE + h))
    return pl.pallas_call(
        _ret_kernel,
        grid=(B, RH),
        in_specs=[_smem_spec(),
                  pl.BlockSpec((L, LANE), col(C_RQ)),
                  pl.BlockSpec((L, LANE), col(C_RK)),
                  pl.BlockSpec((L, LANE), col(C_RV)),
                  pl.BlockSpec((L, LANE), col(C_RG)),
                  pl.BlockSpec((1, 1, 2, LANE, LANE), lambda b, h: (b, h, 0, 0, 0)),
                  pl.BlockSpec((1, LANE), lambda b, h: (0, h))],
        out_specs=pl.BlockSpec((L, LANE), lambda b, h: (b, h)),
        out_shape=jax.ShapeDtypeStruct((B * L, RH * LANE), BF16),
        scratch_shapes=[pltpu.VMEM((L, LANE), F32), pltpu.VMEM((LANE, LANE), F32), pltpu.VMEM((LANE, LANE), F32)],
        compiler_params=_params(("arbitrary", "arbitrary")),
        name="ret",
    )(lg, p, p, p, p, s0, gain)


def _ctx_mix(fc, pc, lg, gain, wf, c256, s256, e):
    return pl.pallas_call(
        _ctx_mix_kernel,
        grid=(B,),
        in_specs=[_smem_spec(),
                  pl.BlockSpec((LC, FW), lambda b: (b, 0)),
                  pl.BlockSpec((LC, NP), lambda b: (b, 0)),
                  _const_spec((1, RH * LANE)),
                  _const_spec((FW, FW)),
                  _const_spec((LC, LC)),
                  _const_spec((LC, LC)),
                  _const_spec((2 * FW, FW))],
        out_specs=[pl.BlockSpec((LC, FW), lambda b: (b, 0)),
                   pl.BlockSpec((LC, RH * LANE), lambda b: (b, 0)),
                   pl.BlockSpec((LC, NH * ND), lambda b: (b, 0))],
        out_shape=[jax.ShapeDtypeStruct((B * LC, FW), BF16),
                   jax.ShapeDtypeStruct((B * LC, RH * LANE), BF16),
                   jax.ShapeDtypeStruct((B * LC, NH * ND), BF16)],
        compiler_params=_params(("arbitrary",)),
        name="ctx_mix",
    )(lg, fc, pc, gain, wf, c256, s256, e)


def _na_bias(base, mask):
    nq, nk = NA_R * GRID_W, NA_KR * GRID_W
    return pl.pallas_call(
        _na_bias_kernel,
        grid=(base.shape[0],),
        in_specs=[pl.BlockSpec((1, NA_DR, LANE), lambda h: (h, 0, 0)),
                  _const_spec((GRID_W, LANE))],
        out_specs=pl.BlockSpec((1, 3, nq, nk), lambda h: (h, 0, 0, 0)),
        out_shape=jax.ShapeDtypeStruct((base.shape[0], 3, nq, nk), F32),
        scratch_shapes=[pltpu.VMEM((NA_DR, GRID_W, LANE), F32)],
        compiler_params=_params(("arbitrary",)),
        name="na_bias",
    )(base, mask)


def _natten(p, pc, bias, layer):
    nq, nk = NA_R * GRID_W, NA_KR * GRID_W
    return pl.pallas_call(
        _natten_kernel,
        grid=(B, NPAIR),
        in_specs=[pl.BlockSpec((L, LANE), lambda b, pr: (b, C_NQ // LANE + pr)),
                  pl.BlockSpec((L, LANE), lambda b, pr: (b, C_NK // LANE + pr)),
                  pl.BlockSpec((L, LANE), lambda b, pr: (b, C_NV // LANE + pr)),
                  pl.BlockSpec((LC, LANE), lambda b, pr: (b, C_NK // LANE + pr)),
                  pl.BlockSpec((LC, LANE), lambda b, pr: (b, C_NV // LANE + pr)),
                  pl.BlockSpec((2, 3, nq, nk), lambda b, pr: (NPAIR * layer + pr, 0, 0, 0))],
        out_specs=pl.BlockSpec((L, LANE), lambda b, pr: (b, pr)),
        out_shape=jax.ShapeDtypeStruct((B * L, NH * ND), BF16),
        scratch_shapes=[pltpu.VMEM((2, nq, nk + LC), BF16)] * 2,
        compiler_params=_params(("arbitrary", "arbitrary")),
        name="natten",
    )(p, p, p, pc, pc, bias)


def _ffn(x2d, four, ret, na, mods, g, wo_all, sel_v_t, w1_all, w3_all, w2_all, layer, g_final, *, ctx):
    rows = x2d.shape[0]
    final = g_final is not None
    row = lambda w: pl.BlockSpec((TM, w), lambda i: (i, 0))
    in_specs = [row(D), row(FW), row(RH * LANE), row(NH * ND),
                _const_spec((8, 6, D)), _const_spec((1, D)),
                _layer_spec((D, D), layer), _const_spec((RH * LANE, RH * RD)),
                _layer_spec((D, DFF), layer), _layer_spec((D, DFF), layer), _layer_spec((DFF, D), layer)]
    args = [x2d, four, ret, na, mods, g.reshape(1, D), wo_all, sel_v_t, w1_all, w3_all, w2_all]
    if final:
        in_specs.append(_const_spec((1, D)))
        args.append(g_final.reshape(1, D))
    return pl.pallas_call(
        functools.partial(_ffn_kernel, final=final, mod_row=_mod_row(ctx, TM)),
        grid=(rows // TM,),
        in_specs=in_specs,
        out_specs=row(D),
        out_shape=jax.ShapeDtypeStruct((rows, D), F32),
        scratch_shapes=[pltpu.VMEM((TM, DFF), BF16), pltpu.VMEM((RH * LANE, D), BF16)],
        compiler_params=_params(("arbitrary",)),
        name="ffn_ctx" if ctx else ("ffn_final" if final else "ffn"),
    )(*args)


def _take_padded(a, src, axis):
    pieces = []
    i, n = 0, len(src)
    while i < n:
        j = i + 1
        if src[i] < 0:
            while j < n and src[j] < 0:
                j += 1
            shape = list(a.shape)
            shape[axis] = j - i
            pieces.append(jnp.zeros(shape, a.dtype))
        else:
            while j < n and src[j] == src[j - 1] + 1:
                j += 1
            pieces.append(lax.slice_in_dim(a, int(src[i]), int(src[j - 1]) + 1, axis=axis))
        i = j
    return jnp.concatenate(pieces, axis=axis)


def kernel(x, c, ctx, c_ctx, w_ada, b_ada, g_mix, w_in, ret_decay_logit, ret_norm_g, w_four,
           na_rpb, w_out, g_ffn, w1, w3, w2, g_final):
    src_g = _gain_sources()
    rope_tabs = tuple(jnp.asarray(t) for t in _rope_tables())
    a1, m3, e, c256, s256 = (jnp.asarray(t).astype(BF16) for t in _dft_tables())
    sel_qk, sel_v = (jnp.asarray(t).astype(BF16) for t in _head_selectors())
    sel_v_t = sel_v.T
    tz_src, tz_mask = _na_toeplitz_tables()
    col_scale = np.ones((1, 1, W_IN), np.float32)
    col_scale[..., W_NQ:W_NK] = LOG2E * ND ** -0.5
    w_in_b = (w_in * jnp.asarray(col_scale)).astype(BF16)
    w_out_b = w_out.astype(BF16)
    w1b, w3b, w2b = w1.astype(BF16), w3.astype(BF16), w2.astype(BF16)

    tz_base = jnp.pad(_take_padded(na_rpb.astype(F32).reshape(DEPTH * NH, 15, 31), tz_src, 2), ((0, 0), (0, 1), (0, 0)))
    na_bias = _na_bias(tz_base, jnp.asarray(tz_mask))

    cvec = jnp.zeros((8, D), F32).at[0:B].set(c).at[B].set(c_ctx)
    mods_all = _ada(cvec, w_ada, b_ada).reshape(DEPTH, 8, 6, D)

    xl = x.reshape(B * L, D)
    xc = ctx.reshape(B * LC, D)
    for i in range(DEPTH):
        last = i == DEPTH - 1
        mods = mods_all[i]
        gain = _take_padded(ret_norm_g[i], src_g, 0).reshape(1, RH * LANE)
        wfour = w_four[i].astype(BF16)
        lg = jax.nn.log_sigmoid(ret_decay_logit[i].astype(F32))

        f_l, p_l = _proj(xl, mods, g_mix[i], w_in_b, i, sel_qk, sel_v, rope_tabs, ctx=False)
        f_c, p_c, s0 = _proj(xc, mods, g_mix[i], w_in_b, i, sel_qk, sel_v, lg, ctx=True)
        four_l = _fourier(f_l, wfour, a1, m3, e)
        ret_l = _retention(p_l, s0, lg, gain)
        na_l = _natten(p_l, p_c, na_bias, i)
        if not last:
            four_c, ret_c, na_c = _ctx_mix(f_c, p_c, lg, gain, wfour, c256, s256, e)
            xc = _ffn(xc, four_c, ret_c, na_c, mods, g_ffn[i], w_out_b, sel_v_t, w1b, w3b, w2b, i, None, ctx=True)
        xl = _ffn(xl, four_l, ret_l, na_l, mods, g_ffn[i], w_out_b, sel_v_t, w1b, w3b, w2b, i,
                  g_final if last else None, ctx=False)
    return xl.reshape(B, L, D)
```

```python
import functools

import numpy as np
import jax
import jax.numpy as jnp
from jax import lax
from jax.experimental import pallas as pl
from jax.experimental.pallas import tpu as pltpu

F32 = jnp.float32
BF16 = jnp.bfloat16

D = 1024
B = 2
L = 8192
LC = 256
DEPTH = 2
GRID_W = 64
GRID_H = L // GRID_W
FW = 256
FGW = 64
RH = 4
RD = 96
NH = 6
ND = 64
NPAIR = NH // 2
DFF = 2816
EPS = 1e-6
NEG = -1e30
LOG2E = float(np.log2(np.e))
LANE = 128

C_RQ, C_RK, C_RV, C_RG = 0, 512, 1024, 1536
C_NQ, C_NK, C_NV = 2048, 2432, 2816
NP = 3200
W_RQ, W_RK, W_RV, W_RG = 256, 640, 1024, 1408
W_NQ, W_NK, W_NV = 1792, 2176, 2560
W_IN = 2944

TM = 512
TM_PROJ = 1024
RET_C = 256
NA_R = 4
NA_KR = NA_R + 8
NA_NB = GRID_H // NA_R
DFT_N1 = 64
DFT_N2 = 128
F1_R = 16
F3_K = 8
VMEM_LIMIT = 56 * 1024 * 1024


def _qk_lane_dims():
    m = np.full((LANE,), -1, np.int64)
    m[0:24] = np.arange(0, 24)
    m[24:48] = np.arange(48, 72)
    m[64:88] = np.arange(24, 48)
    m[88:112] = np.arange(72, 96)
    return m


def _v_lane_dims():
    m = np.full((LANE,), -1, np.int64)
    m[:RD] = np.arange(RD)
    return m


def _head_selectors():
    sels = []
    for lanes in (_qk_lane_dims(), _v_lane_dims()):
        sel = np.zeros((RH * RD, RH * LANE), np.float32)
        ok = np.nonzero(lanes >= 0)[0]
        for h in range(RH):
            sel[RD * h + lanes[ok], LANE * h + ok] = 1.0
        sels.append(sel)
    return sels


def _gain_sources():
    src = np.full((RH * LANE,), -1, np.int64)
    for h in range(RH):
        src[LANE * h: LANE * h + RD] = RD * h + np.arange(RD)
    return src


def _rope_tables():
    pos = np.arange(L)
    prow, pcol = pos // GRID_W, pos % GRID_W
    half = RD // 4
    inv = 10000.0 ** (-np.arange(half, dtype=np.float64) / half)
    ar = prow[:, None] * inv[None, :]
    ac = pcol[:, None] * inv[None, :]
    cos = np.ones((L, LANE), np.float64)
    sin = np.zeros((L, LANE), np.float64)
    for off, sign in ((0, -1.0), (64, 1.0)):
        cos[:, off:off + 24] = np.cos(ar)
        cos[:, off + 24:off + 48] = np.cos(ac)
        sin[:, off:off + 24] = sign * np.sin(ar)
        sin[:, off + 24:off + 48] = sign * np.sin(ac)
    return cos.astype(np.float32), sin.astype(np.float32)


def _dft_tables():
    k1 = np.arange(DFT_N1)
    a = 2 * np.pi * ((k1[:, None] * k1[None, :]) % DFT_N1) / DFT_N1
    a1 = np.kron(np.concatenate([np.cos(a), -np.sin(a)], axis=0), np.eye(F1_R))
    k2 = np.arange(DFT_N2)
    l2 = np.arange(DFT_N2)
    kk = k1[:, None, None] + DFT_N1 * k2[None, :, None]
    ang = 2 * np.pi * ((kk * l2[None, None, :]) % L) / L
    ct, st = np.cos(ang) / np.sqrt(L), np.sin(ang) / np.sqrt(L)
    m3 = np.concatenate([np.concatenate([ct, st], axis=2), np.concatenate([st, -ct], axis=2)], axis=1)
    c = np.arange(FW)
    same = (c[:, None] // FGW) == (c[None, :] // FGW)
    ac = 2 * np.pi * (((c[:, None] % FGW) * (c[None, :] % FGW)) % FGW) / FGW
    e = np.concatenate([np.where(same, np.cos(ac), 0.0), -np.where(same, np.sin(ac), 0.0)], axis=0) / 8.0
    p = np.arange(LC)
    ap = 2 * np.pi * ((p[:, None] * p[None, :]) % LC) / LC
    return tuple(t.astype(np.float32) for t in (a1, m3, e, np.cos(ap) / 16.0, np.sin(ap) / 16.0))


NA_DR = 16


def _na_toeplitz_tables():
    src = np.full((LANE,), -1, np.int64)
    src[0:16] = np.arange(0, 16) + 15
    src[113:128] = np.arange(113, 128) - 113
    src[49:80] = np.arange(49, 80) - 49
    qc = np.arange(GRID_W)[:, None]
    kc = np.arange(LANE)[None, :] % GRID_W
    ws = np.clip(qc - 8, 0, GRID_W - 16)
    mask = np.where((kc >= ws) & (kc < ws + 16), 0.0, NEG).astype(np.float32)
    return src, mask


def _dot(a, b):
    return jnp.dot(a, b, preferred_element_type=F32)


def _dot_nt(a, b):
    return lax.dot_general(a, b, (((1,), (1,)), ((), ())), preferred_element_type=F32)


def _silu(x):
    return x * jax.nn.sigmoid(x)


def _rmsnorm(x, g):
    return x * lax.rsqrt(jnp.mean(x * x, axis=-1, keepdims=True) + EPS) * g


def _ada_kernel(c_ref, w_ref, b_ref, o_ref):
    s = _silu(c_ref[...]).astype(BF16)
    o_ref[0] = _dot(s, w_ref[0].astype(BF16)) + b_ref[0]


def _proj_kernel(*refs, rope, mod_row):
    if rope:
        x_ref, mods_ref, g_ref, w_ref, pqk_ref, pv_ref, cos_ref, sin_ref, f_ref, o_ref, wret_ref = refs
    else:
        x_ref, mods_ref, g_ref, w_ref, pqk_ref, pv_ref, lg_ref, f_ref, o_ref, s0_ref, wret_ref = refs
    s = pl.program_id(0)

    @pl.when(s == 0)
    def _():
        for slot, (src0, sel_ref) in enumerate(((W_RQ, pqk_ref), (W_RK, pqk_ref), (W_RV, pv_ref), (W_RG, pv_ref))):
            wret_ref[:, RH * LANE * slot:RH * LANE * (slot + 1)] = _dot(
                w_ref[:, src0:src0 + RH * RD], sel_ref[...]).astype(BF16)

    mod = mods_ref[mod_row(s)]
    hb = (_rmsnorm(x_ref[...], g_ref[...]) * (1.0 + mod[1:2, :]) + mod[0:1, :]).astype(BF16)
    f_ref[...] = _dot(hb, w_ref[:, :FW]).astype(BF16)
    k_scale = RD ** -0.5
    for c0, scale in ((C_RQ, None), (C_RK, k_scale)):
        t4 = _dot(hb, wret_ref[:, c0:c0 + RH * LANE])
        for hh in range(RH):
            t = t4[:, LANE * hh:LANE * (hh + 1)]
            if rope:
                t = t * cos_ref[...] + pltpu.roll(t, 64, 1) * sin_ref[...]
            if scale is not None:
                t = t * scale
            o_ref[:, c0 + LANE * hh:c0 + LANE * (hh + 1)] = t.astype(BF16)
    o_ref[:, C_RV:C_RG] = _dot(hb, wret_ref[:, C_RV:C_RG]).astype(BF16)
    o_ref[:, C_RG:C_NQ] = _silu(_dot(hb, wret_ref[:, C_RG:C_NQ])).astype(BF16)
    o_ref[:, C_NQ:] = _dot(hb, w_ref[:, W_NQ:]).astype(BF16)
    if not rope:
        idx = lax.broadcasted_iota(jnp.int32, (LC, LANE), 0).astype(F32)
        for b in range(B):
            for h in range(RH):
                k = o_ref[LC * b:LC * (b + 1), C_RK + LANE * h:C_RK + LANE * (h + 1)].astype(F32)
                v = o_ref[LC * b:LC * (b + 1), C_RV + LANE * h:C_RV + LANE * (h + 1)]
                wf = jnp.exp((LC - 1.0 - idx) * lg_ref[0, h])
                wb = jnp.exp(idx * lg_ref[1, h])
                s0_ref[b, h, 0] = _dot((k * wf).T.astype(BF16), v)
                s0_ref[b, h, 1] = _dot((k * wb).T.astype(BF16), v)


def _four1_kernel(a_ref, x_ref, y_ref):
    x = x_ref[0].reshape(DFT_N1 * F1_R, FW)
    y = _dot(a_ref[...], x).astype(BF16)
    y_ref[0] = y.reshape(2 * DFT_N1, F1_R, FW)


def _four3_kernel(m_ref, y_ref, e_ref, wf_ref, o_ref):
    ew = _dot(e_ref[...], wf_ref[...]).astype(BF16)
    for b in range(B):
        for kk in range(F3_K):
            y = jnp.concatenate([y_ref[b, 0, kk], y_ref[b, 1, kk]], axis=0)
            z = _dot(m_ref[kk], y).astype(BF16)
            o_ref[b, kk] = _dot(jnp.concatenate([z[:DFT_N2], z[DFT_N2:]], axis=1), ew)


def _decay_matrix(n, lgf, lgb):
    ii = lax.broadcasted_iota(jnp.int32, (n, n), 0)
    jj = lax.broadcasted_iota(jnp.int32, (n, n), 1)
    diff = (ii - jj).astype(F32)
    fwd = jnp.where(diff >= 0, jnp.exp(jnp.maximum(diff, 0.0) * lgf), 0.0)
    bwd = jnp.where(diff <= 0, jnp.exp(jnp.maximum(-diff, 0.0) * lgb), 0.0)
    return fwd + bwd


def _ret_readout(o, gate, gain):
    ms = jnp.sum(o * o, axis=-1, keepdims=True) * (1.0 / RD)
    return (o * lax.rsqrt(ms + EPS) * gain * gate.astype(F32)).astype(BF16)


def _ret_kernel(lg_ref, q_ref, k_ref, v_ref, g_ref, s0_ref, gain_ref, o_ref, acc_ref, stf_ref, stb_ref):
    h = pl.program_id(1)
    lgf = lg_ref[0, h]
    lgb = lg_ref[1, h]
    c = RET_C
    n = L // c
    dmat = _decay_matrix(c, lgf, lgb)
    idx = lax.broadcasted_iota(jnp.int32, (c, LANE), 0).astype(F32)
    qdf = jnp.exp((idx + 1.0) * lgf)
    kdf = jnp.exp((c - 1.0 - idx) * lgf)
    qdb = jnp.exp((c - idx) * lgb)
    kdb = jnp.exp(idx * lgb)
    cdf = jnp.exp(jnp.full((1, LANE), float(c), F32) * lgf)
    cdb = jnp.exp(jnp.full((1, LANE), float(c), F32) * lgb)
    gain = gain_ref[...]

    dirs = ((stf_ref, qdf, kdf, cdf), (stb_ref, qdb, kdb, cdb))
    stf_ref[...] = s0_ref[0, 0, 0]
    stb_ref[...] = s0_ref[0, 0, 1]

    def visit(r0, direction, second):
        st_ref, qdec, kdec, cdec = dirs[direction]
        rows = pl.ds(pl.multiple_of(r0, c), c)
        q = q_ref[rows, :]
        k = k_ref[rows, :]
        v = v_ref[rows, :]
        cross = _dot(q, st_ref[...].astype(BF16)) * qdec
        if second:
            o_ref[rows, :] = _ret_readout(acc_ref[rows, :] + cross, g_ref[rows, :], gain)
        else:
            s = _dot_nt(q, k) * dmat
            acc_ref[rows, :] = _dot(s.astype(BF16), v) + cross
        kt = (k.astype(F32) * kdec).T.astype(BF16)
        st_ref[...] = st_ref[...] * cdec + _dot(kt, v)

    def body(second):
        def step(i, carry):
            visit(i * c, 0, second)
            visit((n - 1 - i) * c, 1, second)
            return carry
        return step

    lax.fori_loop(0, n // 2, body(False), 0, unroll=4)
    lax.fori_loop(n // 2, n, body(True), 0, unroll=4)


def _pair_attention(q, score_fn, value_fn):
    lane = lax.broadcasted_iota(jnp.int32, q.shape, 1)
    outs = []
    for hh in range(2):
        sel = (lane >= ND * hh) & (lane < ND * (hh + 1))
        qm = jnp.where(sel, q, jnp.zeros_like(q))
        scores = score_fn(qm, hh)
        m = functools.reduce(jnp.maximum, [jnp.max(s, axis=-1, keepdims=True) for s in scores])
        ps = [jnp.exp2(s - m) for s in scores]
        den = functools.reduce(jnp.add, [jnp.sum(p, axis=-1, keepdims=True) for p in ps])
        outs.append(value_fn([p.astype(BF16) for p in ps]) / den)
    return jnp.where(lane < ND, outs[0], outs[1])


def _ctx_mix_kernel(lg_ref, f_ref, p_ref, gain_ref, wf_ref, c_ref, s_ref, e_ref, fo_ref, ro_ref, no_ref):
    f = f_ref[...]
    pr = _dot(c_ref[...], f)
    qi = _dot(s_ref[...], f)
    fr = _dot(pr.astype(BF16), e_ref[:FW, :]) + _dot(qi.astype(BF16), e_ref[FW:, :])
    fo_ref[...] = _dot(fr.astype(BF16), wf_ref[...]).astype(BF16)
    for h in range(RH):
        sl = lambda c0: slice(c0 + LANE * h, c0 + LANE * (h + 1))
        dmat = _decay_matrix(LC, lg_ref[0, h], lg_ref[1, h])
        s = _dot_nt(p_ref[:, sl(C_RQ)], p_ref[:, sl(C_RK)]) * dmat
        o = _dot(s.astype(BF16), p_ref[:, sl(C_RV)])
        ro_ref[:, LANE * h:LANE * (h + 1)] = _ret_readout(o, p_ref[:, sl(C_RG)], gain_ref[:, LANE * h:LANE * (h + 1)])
    for pair in range(NPAIR):
        sl = lambda c0: slice(c0 + LANE * pair, c0 + LANE * (pair + 1))
        k = p_ref[:, sl(C_NK)]
        v = p_ref[:, sl(C_NV)]
        out = _pair_attention(p_ref[:, sl(C_NQ)], lambda qm, hh: [_dot_nt(qm, k)], lambda ps: _dot(ps[0], v))
        no_ref[:, LANE * pair:LANE * (pair + 1)] = out.astype(BF16)


def _na_key_start(r0):
    return min(max(r0 - 4, 0), GRID_H - NA_KR)


def _na_bias_kernel(base_ref, mask_ref, o_ref, tz_ref):
    for dr in range(NA_DR - 1):
        t = jnp.broadcast_to(base_ref[0, dr:dr + 1, :], (GRID_W, LANE))
        t = pltpu.roll(t, 0, 1, stride=1, stride_axis=0)
        tz_ref[dr] = t * LOG2E + mask_ref[...]
    tz_ref[NA_DR - 1] = jnp.full((GRID_W, LANE), NEG, F32)
    low_half = lax.broadcasted_iota(jnp.int32, (GRID_W, LANE), 1) < GRID_W
    for var, r0 in enumerate((0, 2 * NA_R, GRID_H - NA_R)):
        ks = _na_key_start(r0)
        for rl in range(NA_R):
            r = r0 + rl
            rs = min(max(r - 4, 0), GRID_H - 8)
            slot = [kr - r + 7 if rs <= kr < rs + 8 else NA_DR - 1 for kr in range(ks, ks + NA_KR)]
            for m in range(NA_KR // 2):
                o_ref[0, var, GRID_W * rl:GRID_W * (rl + 1), LANE * m:LANE * (m + 1)] = jnp.where(
                    low_half, tz_ref[slot[2 * m]], tz_ref[slot[2 * m + 1]])


def _natten_kernel(q_ref, k_ref, v_ref, kc_ref, vc_ref, bias_ref, o_ref, pa_ref, pb_ref):
    nq = NA_R * GRID_W
    nk = NA_KR * GRID_W
    kc = kc_ref[...]
    vc = vc_ref[...]
    lane = lax.broadcasted_iota(jnp.int32, (nq, LANE), 1)

    def rows(i):
        ks = jnp.clip(NA_R * i - 4, 0, GRID_H - NA_KR)
        return pl.ds(pl.multiple_of(ks * GRID_W, nq), nk), pl.ds(pl.multiple_of(i * nq, nq), nq)

    def probabilities(i, p_ref):
        i = jnp.minimum(i, NA_NB - 1)
        var = jnp.where(i == 0, 0, jnp.where(i == NA_NB - 1, 2, 1))
        krows, qrows = rows(i)
        q = q_ref[qrows, :]
        kw = k_ref[krows, :]
        for hh in range(2):
            qm = jnp.where((lane >= ND * hh) & (lane < ND * (hh + 1)), q, jnp.zeros_like(q))
            s_win = _dot_nt(qm, kw) + bias_ref[hh, var]
            s_ctx = _dot_nt(qm, kc)
            m = jnp.maximum(jnp.max(s_win, axis=-1, keepdims=True), jnp.max(s_ctx, axis=-1, keepdims=True))
            p_ref[hh, :, :nk] = jnp.exp2(s_win - m).astype(BF16)
            p_ref[hh, :, nk:] = jnp.exp2(s_ctx - m).astype(BF16)

    def values(i, p_ref):
        krows, qrows = rows(i)
        vals = jnp.concatenate([v_ref[krows, :], vc], axis=0)
        vals = jnp.concatenate([vals, jnp.ones_like(vals)], axis=1)
        o2 = [_dot(p_ref[hh], vals) for hh in range(2)]
        outs = [o[:, :LANE] / o[:, LANE:] for o in o2]
        o_ref[qrows, :] = jnp.where(lane < ND, outs[0], outs[1]).astype(BF16)

    probabilities(0, pa_ref)

    def step(j, carry):
        values(2 * j, pa_ref)
        probabilities(2 * j + 1, pb_ref)
        values(2 * j + 1, pb_ref)
        probabilities(2 * j + 2, pa_ref)
        return carry

    lax.fori_loop(0, NA_NB // 2, step, 0)


def _ffn_kernel(*refs, final, mod_row):
    x_ref, f_ref, r_ref, n_ref, mods_ref, g_ref, wo_ref, sel_ref, w1_ref, w3_ref, w2_ref = refs[:11]
    gf_ref = refs[11] if final else None
    o_ref, acc_ref, wor_ref = refs[-3:]
    s = pl.program_id(0)

    @pl.when(s == 0)
    def _():
        wor_ref[...] = _dot(sel_ref[...], wo_ref[FW:FW + RH * RD, :]).astype(BF16)

    mod = mods_ref[mod_row(s)]
    if f_ref.ndim == 3:
        f = jnp.concatenate([f_ref[:, r, :] for r in range(f_ref.shape[1])], axis=0)
    else:
        f = f_ref[...]
    y = (_dot(f.astype(BF16), wo_ref[:FW, :]) + _dot(r_ref[...], wor_ref[...])
         + _dot(n_ref[...], wo_ref[FW + RH * RD:, :]))
    x1 = x_ref[...] + mod[2:3, :] * y
    hb = (_rmsnorm(x1, g_ref[...]) * (1.0 + mod[4:5, :]) + mod[3:4, :]).astype(BF16)
    fc = 256
    for c in range(DFF // fc):
        a = _dot(hb, w1_ref[:, c * fc:(c + 1) * fc])
        b = _dot(hb, w3_ref[:, c * fc:(c + 1) * fc])
        acc_ref[:, c * fc:(c + 1) * fc] = (_silu(a) * b).astype(BF16)
    x2 = x1 + mod[5:6, :] * _dot(acc_ref[...], w2_ref[...])
    if final:
        x2 = _rmsnorm(x2, gf_ref[...])
    o_ref[...] = x2


def _params(sem):
    return pltpu.CompilerParams(dimension_semantics=sem, vmem_limit_bytes=VMEM_LIMIT)


def _const_spec(shape):
    nd = len(shape)
    return pl.BlockSpec(shape, lambda *_: (0,) * nd, pipeline_mode=pl.Buffered(1))


def _smem_spec():
    return pl.BlockSpec(memory_space=pltpu.SMEM)


def _ada(cvec, w_ada, b_ada):
    tn = 1024
    return pl.pallas_call(
        _ada_kernel,
        grid=(DEPTH, 6 * D // tn),
        in_specs=[pl.BlockSpec((8, D), lambda i, j: (0, 0)),
                  pl.BlockSpec((1, D, tn), lambda i, j: (i, 0, j)),
                  pl.BlockSpec((1, 1, tn), lambda i, j: (i, 0, j))],
        out_specs=pl.BlockSpec((1, 8, tn), lambda i, j: (i, 0, j)),
        out_shape=jax.ShapeDtypeStruct((DEPTH, 8, 6 * D), F32),
        compiler_params=_params(("arbitrary", "arbitrary")),
        name="ada",
    )(cvec, w_ada, b_ada.reshape(DEPTH, 1, 6 * D))


def _mod_row(ctx, tm):
    per_batch = L // tm
    return (lambda tile: B) if ctx else (lambda tile: tile // per_batch)


def _layer_spec(shape, layer):
    nd = len(shape)
    return pl.BlockSpec((None,) + tuple(shape), lambda *_: (layer,) + (0,) * nd, pipeline_mode=pl.Buffered(1))


def _proj(x2d, mods, g, w_all, layer, sel_qk, sel_v, extra, *, ctx):
    rows = x2d.shape[0]
    tm = min(TM_PROJ, rows)
    per_batch = L // tm
    in_specs = [pl.BlockSpec((tm, D), lambda i: (i, 0)),
                _const_spec((8, 6, D)),
                _const_spec((1, D)),
                _layer_spec((D, W_IN), layer),
                _const_spec((RH * RD, RH * LANE)),
                _const_spec((RH * RD, RH * LANE))]
    args = [x2d, mods, g.reshape(1, D), w_all, sel_qk, sel_v]
    out_specs = [pl.BlockSpec((tm, FW), lambda i: (i, 0)), pl.BlockSpec((tm, NP), lambda i: (i, 0))]
    out_shape = [jax.ShapeDtypeStruct((rows, FW), BF16), jax.ShapeDtypeStruct((rows, NP), BF16)]
    if ctx:
        assert rows == tm == B * LC
        in_specs.append(_smem_spec())
        args.append(extra)
        out_specs.append(pl.BlockSpec((B, RH, 2, LANE, LANE), lambda i: (0, 0, 0, 0, 0)))
        out_shape.append(jax.ShapeDtypeStruct((B, RH, 2, LANE, LANE), F32))
    else:
        in_specs += [pl.BlockSpec((tm, LANE), lambda i: (i % per_batch, 0))] * 2
        args += list(extra)
    return pl.pallas_call(
        functools.partial(_proj_kernel, rope=not ctx, mod_row=_mod_row(ctx, tm)),
        grid=(rows // tm,),
        in_specs=in_specs,
        out_specs=out_specs,
        out_shape=out_shape,
        scratch_shapes=[pltpu.VMEM((D, 4 * RH * LANE), BF16)],
        compiler_params=_params(("arbitrary",)),
        name="proj_ctx" if ctx else "proj",
    )(*args)


def _fourier(f2d, wf, a1, m3, e):
    y = pl.pallas_call(
        _four1_kernel,
        grid=(B, DFT_N2 // F1_R),
        in_specs=[_const_spec((2 * DFT_N1 * F1_R, DFT_N1 * F1_R)),
                  pl.BlockSpec((1, DFT_N1, F1_R, FW), lambda b, j: (b, 0, j, 0))],
        out_specs=pl.BlockSpec((1, 2 * DFT_N1, F1_R, FW), lambda b, j: (b, 0, j, 0)),
        out_shape=jax.ShapeDtypeStruct((B, 2 * DFT_N1, DFT_N2, FW), BF16),
        compiler_params=_params(("arbitrary", "arbitrary")),
        name="four1",
    )(a1, f2d.reshape(B, DFT_N1, DFT_N2, FW))
    out = pl.pallas_call(
        _four3_kernel,
        grid=(DFT_N1 // F3_K,),
        in_specs=[pl.BlockSpec((F3_K, 2 * DFT_N2, 2 * DFT_N2), lambda k: (k, 0, 0)),
                  pl.BlockSpec((B, 2, F3_K, DFT_N2, FW), lambda k: (0, 0, k, 0, 0)),
                  _const_spec((2 * FW, FW)),
                  _const_spec((FW, FW))],
        out_specs=pl.BlockSpec((B, F3_K, DFT_N2, FW), lambda k: (0, k, 0, 0)),
        out_shape=jax.ShapeDtypeStruct((B, DFT_N1, DFT_N2, FW), F32),
        compiler_params=_params(("arbitrary",)),
        name="four3",
    )(m3, y.reshape(B, 2, DFT_N1, DFT_N2, FW), e, wf)
    return out.reshape(B * DFT_N1, DFT_N2, FW)


def _retention(p, s0, lg, gain):
    col = lambda c0: (lambda b, h: (b, c0 // LANE + h))
    return pl.pallas_call(
        _ret_kernel,
        grid=(B, RH),
        in_specs=[_smem_spec(),
                  pl.BlockSpec((L, LANE), col(C_RQ)),
                  pl.BlockSpec((L, LANE), col(C_RK)),
                  pl.BlockSpec((L, LANE), col(C_RV)),
                  pl.BlockSpec((L, LANE), col(C_RG)),
                  pl.BlockSpec((1, 1, 2, LANE, LANE), lambda b, h: (b, h, 0, 0, 0)),
                  pl.BlockSpec((1, LANE), lambda b, h: (0, h))],
        out_specs=pl.BlockSpec((L, LANE), lambda b, h: (b, h)),
        out_shape=jax.ShapeDtypeStruct((B * L, RH * LANE), BF16),
        scratch_shapes=[pltpu.VMEM((L, LANE), F32), pltpu.VMEM((LANE, LANE), F32), pltpu.VMEM((LANE, LANE), F32)],
        compiler_params=_params(("arbitrary", "arbitrary")),
        name="ret",
    )(lg, p, p, p, p, s0, gain)


def _ctx_mix(fc, pc, lg, gain, wf, c256, s256, e):
    return pl.pallas_call(
        _ctx_mix_kernel,
        grid=(B,),
        in_specs=[_smem_spec(),
                  pl.BlockSpec((LC, FW), lambda b: (b, 0)),
                  pl.BlockSpec((LC, NP), lambda b: (b, 0)),
                  _const_spec((1, RH * LANE)),
                  _const_spec((FW, FW)),
                  _const_spec((LC, LC)),
                  _const_spec((LC, LC)),
                  _const_spec((2 * FW, FW))],
        out_specs=[pl.BlockSpec((LC, FW), lambda b: (b, 0)),
                   pl.BlockSpec((LC, RH * LANE), lambda b: (b, 0)),
                   pl.BlockSpec((LC, NH * ND), lambda b: (b, 0))],
        out_shape=[jax.ShapeDtypeStruct((B * LC, FW), BF16),
                   jax.ShapeDtypeStruct((B * LC, RH * LANE), BF16),
                   jax.ShapeDtypeStruct((B * LC, NH * ND), BF16)],
        compiler_params=_params(("arbitrary",)),
        name="ctx_mix",
    )(lg, fc, pc, gain, wf, c256, s256, e)


def _na_bias(base, mask):
    nq, nk = NA_R * GRID_W, NA_KR * GRID_W
    return pl.pallas_call(
        _na_bias_kernel,
        grid=(base.shape[0],),
        in_specs=[pl.BlockSpec((1, NA_DR, LANE), lambda h: (h, 0, 0)),
                  _const_spec((GRID_W, LANE))],
        out_specs=pl.BlockSpec((1, 3, nq, nk), lambda h: (h, 0, 0, 0)),
        out_shape=jax.ShapeDtypeStruct((base.shape[0], 3, nq, nk), F32),
        scratch_shapes=[pltpu.VMEM((NA_DR, GRID_W, LANE), F32)],
        compiler_params=_params(("arbitrary",)),
        name="na_bias",
    )(base, mask)


def _natten(p, pc, bias, layer):
    nq, nk = NA_R * GRID_W, NA_KR * GRID_W
    return pl.pallas_call(
        _natten_kernel,
        grid=(B, NPAIR),
        in_specs=[pl.BlockSpec((L, LANE), lambda b, pr: (b, C_NQ // LANE + pr)),
                  pl.BlockSpec((L, LANE), lambda b, pr: (b, C_NK // LANE + pr)),
                  pl.BlockSpec((L, LANE), lambda b, pr: (b, C_NV // LANE + pr)),
                  pl.BlockSpec((LC, LANE), lambda b, pr: (b, C_NK // LANE + pr)),
                  pl.BlockSpec((LC, LANE), lambda b, pr: (b, C_NV // LANE + pr)),
                  pl.BlockSpec((2, 3, nq, nk), lambda b, pr: (NPAIR * layer + pr, 0, 0, 0))],
        out_specs=pl.BlockSpec((L, LANE), lambda b, pr: (b, pr)),
        out_shape=jax.ShapeDtypeStruct((B * L, NH * ND), BF16),
        scratch_shapes=[pltpu.VMEM((2, nq, nk + LC), BF16)] * 2,
        compiler_params=_params(("arbitrary", "arbitrary")),
        name="natten",
    )(p, p, p, pc, pc, bias)


def _ffn(x2d, four, ret, na, mods, g, wo_all, sel_v_t, w1_all, w3_all, w2_all, layer, g_final, *, ctx):
    rows = x2d.shape[0]
    final = g_final is not None
    row = lambda w: pl.BlockSpec((TM, w), lambda i: (i, 0))
    if four.ndim == 3:
        per_batch = L // TM
        four_spec = pl.BlockSpec((DFT_N1, TM // DFT_N1, FW), lambda i: (i // per_batch, i % per_batch, 0))
    else:
        four_spec = row(FW)
    in_specs = [row(D), four_spec, row(RH * LANE), row(NH * ND),
                _const_spec((8, 6, D)), _const_spec((1, D)),
                _layer_spec((D, D), layer), _const_spec((RH * LANE, RH * RD)),
                _layer_spec((D, DFF), layer), _layer_spec((D, DFF), layer), _layer_spec((DFF, D), layer)]
    args = [x2d, four, ret, na, mods, g.reshape(1, D), wo_all, sel_v_t, w1_all, w3_all, w2_all]
    if final:
        in_specs.append(_const_spec((1, D)))
        args.append(g_final.reshape(1, D))
    return pl.pallas_call(
        functools.partial(_ffn_kernel, final=final, mod_row=_mod_row(ctx, TM)),
        grid=(rows // TM,),
        in_specs=in_specs,
        out_specs=row(D),
        out_shape=jax.ShapeDtypeStruct((rows, D), F32),
        scratch_shapes=[pltpu.VMEM((TM, DFF), BF16), pltpu.VMEM((RH * LANE, D), BF16)],
        compiler_params=_params(("arbitrary",)),
        name="ffn_ctx" if ctx else ("ffn_final" if final else "ffn"),
    )(*args)


def _take_padded(a, src, axis):
    pieces = []
    i, n = 0, len(src)
    while i < n:
        j = i + 1
        if src[i] < 0:
            while j < n and src[j] < 0:
                j += 1
            shape = list(a.shape)
            shape[axis] = j - i
            pieces.append(jnp.zeros(shape, a.dtype))
        else:
            while j < n and src[j] == src[j - 1] + 1:
                j += 1
            pieces.append(lax.slice_in_dim(a, int(src[i]), int(src[j - 1]) + 1, axis=axis))
        i = j
    return jnp.concatenate(pieces, axis=axis)


def kernel(x, c, ctx, c_ctx, w_ada, b_ada, g_mix, w_in, ret_decay_logit, ret_norm_g, w_four,
           na_rpb, w_out, g_ffn, w1, w3, w2, g_final):
    src_g = _gain_sources()
    rope_tabs = tuple(jnp.asarray(t) for t in _rope_tables())
    a1, m3, e, c256, s256 = (jnp.asarray(t).astype(BF16) for t in _dft_tables())
    sel_qk, sel_v = (jnp.asarray(t).astype(BF16) for t in _head_selectors())
    sel_v_t = sel_v.T
    tz_src, tz_mask = _na_toeplitz_tables()
    col_scale = np.ones((1, 1, W_IN), np.float32)
    col_scale[..., W_NQ:W_NK] = LOG2E * ND ** -0.5
    w_in_b = (w_in * jnp.asarray(col_scale)).astype(BF16)
    w_out_b = w_out.astype(BF16)
    w1b, w3b, w2b = w1.astype(BF16), w3.astype(BF16), w2.astype(BF16)

    tz_base = jnp.pad(_take_padded(na_rpb.astype(F32).reshape(DEPTH * NH, 15, 31), tz_src, 2), ((0, 0), (0, 1), (0, 0)))
    na_bias = _na_bias(tz_base, jnp.asarray(tz_mask))

    cvec = jnp.zeros((8, D), F32).at[0:B].set(c).at[B].set(c_ctx)
    mods_all = _ada(cvec, w_ada, b_ada).reshape(DEPTH, 8, 6, D)

    xl = x.reshape(B * L, D)
    xc = ctx.reshape(B * LC, D)
    for i in range(DEPTH):
        last = i == DEPTH - 1
        mods = mods_all[i]
        gain = _take_padded(ret_norm_g[i], src_g, 0).reshape(1, RH * LANE)
        wfour = w_four[i].astype(BF16)
        lg = jax.nn.log_sigmoid(ret_decay_logit[i].astype(F32))

        f_l, p_l = _proj(xl, mods, g_mix[i], w_in_b, i, sel_qk, sel_v, rope_tabs, ctx=False)
        f_c, p_c, s0 = _proj(xc, mods, g_mix[i], w_in_b, i, sel_qk, sel_v, lg, ctx=True)
        four_l = _fourier(f_l, wfour, a1, m3, e)
        ret_l = _retention(p_l, s0, lg, gain)
        na_l = _natten(p_l, p_c, na_bias, i)
        if not last:
            four_c, ret_c, na_c = _ctx_mix(f_c, p_c, lg, gain, wfour, c256, s256, e)
            xc = _ffn(xc, four_c, ret_c, na_c, mods, g_ffn[i], w_out_b, sel_v_t, w1b, w3b, w2b, i, None, ctx=True)
        xl = _ffn(xl, four_l, ret_l, na_l, mods, g_ffn[i], w_out_b, sel_v_t, w1b, w3b, w2b, i,
                  g_final if last else None, ctx=False)
    return xl.reshape(B, L, D)
```

```python
import functools

import numpy as np
import jax
import jax.numpy as jnp
from jax import lax
from jax.experimental import pallas as pl
from jax.experimental.pallas import tpu as pltpu

F32 = jnp.float32
BF16 = jnp.bfloat16

D = 1024
B = 2
L = 8192
LC = 256
DEPTH = 2
GRID_W = 64
GRID_H = L // GRID_W
FW = 256
FGW = 64
RH = 4
RD = 96
NH = 6
ND = 64
NPAIR = NH // 2
DFF = 2816
EPS = 1e-6
NEG = -1e30
LOG2E = float(np.log2(np.e))
LANE = 128

C_RQ, C_RK, C_RV, C_RG = 0, 512, 1024, 1536
C_NQ, C_NK, C_NV = 2048, 2432, 2816
NP = 3200
W_RQ, W_RK, W_RV, W_RG = 256, 640, 1024, 1408
W_NQ, W_NK, W_NV = 1792, 2176, 2560
W_IN = 2944

TM = 1024
TM_PROJ = 1024
RET_C = 256
NA_R = 4
NA_KR = NA_R + 8
NA_NB = GRID_H // NA_R
DFT_N1 = 64
DFT_N2 = 128
F1_R = 16
F3_K = 8
VMEM_LIMIT = 56 * 1024 * 1024


def _qk_lane_dims():
    m = np.full((LANE,), -1, np.int64)
    m[0:24] = np.arange(0, 24)
    m[24:48] = np.arange(48, 72)
    m[64:88] = np.arange(24, 48)
    m[88:112] = np.arange(72, 96)
    return m


def _v_lane_dims():
    m = np.full((LANE,), -1, np.int64)
    m[:RD] = np.arange(RD)
    return m


def _head_selectors():
    sels = []
    for lanes in (_qk_lane_dims(), _v_lane_dims()):
        sel = np.zeros((RH * RD, RH * LANE), np.float32)
        ok = np.nonzero(lanes >= 0)[0]
        for h in range(RH):
            sel[RD * h + lanes[ok], LANE * h + ok] = 1.0
        sels.append(sel)
    return sels


def _gain_sources():
    src = np.full((RH * LANE,), -1, np.int64)
    for h in range(RH):
        src[LANE * h: LANE * h + RD] = RD * h + np.arange(RD)
    return src


def _rope_tables():
    pos = np.arange(L)
    prow, pcol = pos // GRID_W, pos % GRID_W
    half = RD // 4
    inv = 10000.0 ** (-np.arange(half, dtype=np.float64) / half)
    ar = prow[:, None] * inv[None, :]
    ac = pcol[:, None] * inv[None, :]
    cos = np.ones((L, LANE), np.float64)
    sin = np.zeros((L, LANE), np.float64)
    for off, sign in ((0, -1.0), (64, 1.0)):
        cos[:, off:off + 24] = np.cos(ar)
        cos[:, off + 24:off + 48] = np.cos(ac)
        sin[:, off:off + 24] = sign * np.sin(ar)
        sin[:, off + 24:off + 48] = sign * np.sin(ac)
    return cos.astype(np.float32), sin.astype(np.float32)


def _dft_tables():
    k1 = np.arange(DFT_N1)
    a = 2 * np.pi * ((k1[:, None] * k1[None, :]) % DFT_N1) / DFT_N1
    a1 = np.kron(np.concatenate([np.cos(a), -np.sin(a)], axis=0), np.eye(F1_R))
    k2 = np.arange(DFT_N2)
    l2 = np.arange(DFT_N2)
    kk = k1[:, None, None] + DFT_N1 * k2[None, :, None]
    ang = 2 * np.pi * ((kk * l2[None, None, :]) % L) / L
    ct, st = np.cos(ang) / np.sqrt(L), np.sin(ang) / np.sqrt(L)
    m3 = np.concatenate([np.concatenate([ct, st], axis=2), np.concatenate([st, -ct], axis=2)], axis=1)
    c = np.arange(FW)
    same = (c[:, None] // FGW) == (c[None, :] // FGW)
    ac = 2 * np.pi * (((c[:, None] % FGW) * (c[None, :] % FGW)) % FGW) / FGW
    e = np.concatenate([np.where(same, np.cos(ac), 0.0), -np.where(same, np.sin(ac), 0.0)], axis=0) / 8.0
    p = np.arange(LC)
    ap = 2 * np.pi * ((p[:, None] * p[None, :]) % LC) / LC
    return tuple(t.astype(np.float32) for t in (a1, m3, e, np.cos(ap) / 16.0, np.sin(ap) / 16.0))


NA_DR = 16


def _na_toeplitz_tables():
    src = np.full((LANE,), -1, np.int64)
    src[0:16] = np.arange(0, 16) + 15
    src[113:128] = np.arange(113, 128) - 113
    src[49:80] = np.arange(49, 80) - 49
    qc = np.arange(GRID_W)[:, None]
    kc = np.arange(LANE)[None, :] % GRID_W
    ws = np.clip(qc - 8, 0, GRID_W - 16)
    mask = np.where((kc >= ws) & (kc < ws + 16), 0.0, NEG).astype(np.float32)
    return src, mask


def _dot(a, b):
    return jnp.dot(a, b, preferred_element_type=F32)


def _dot_nt(a, b):
    return lax.dot_general(a, b, (((1,), (1,)), ((), ())), preferred_element_type=F32)


def _silu(x):
    return x * jax.nn.sigmoid(x)


def _rmsnorm(x, g):
    return x * lax.rsqrt(jnp.mean(x * x, axis=-1, keepdims=True) + EPS) * g


def _ada_kernel(c_ref, w_ref, b_ref, o_ref):
    s = _silu(c_ref[...]).astype(BF16)
    o_ref[0] = _dot(s, w_ref[0].astype(BF16)) + b_ref[0]


def _proj_kernel(*refs, rope, mod_row):
    if rope:
        x_ref, mods_ref, g_ref, w_ref, pqk_ref, pv_ref, cos_ref, sin_ref, f_ref, o_ref, wret_ref = refs
    else:
        x_ref, mods_ref, g_ref, w_ref, pqk_ref, pv_ref, lg_ref, f_ref, o_ref, s0_ref, wret_ref = refs
    s = pl.program_id(0)

    @pl.when(s == 0)
    def _():
        for slot, (src0, sel_ref) in enumerate(((W_RQ, pqk_ref), (W_RK, pqk_ref), (W_RV, pv_ref), (W_RG, pv_ref))):
            wret_ref[:, RH * LANE * slot:RH * LANE * (slot + 1)] = _dot(
                w_ref[:, src0:src0 + RH * RD], sel_ref[...]).astype(BF16)

    mod = mods_ref[mod_row(s)]
    hb = (_rmsnorm(x_ref[...], g_ref[...]) * (1.0 + mod[1:2, :]) + mod[0:1, :]).astype(BF16)
    f_ref[...] = _dot(hb, w_ref[:, :FW]).astype(BF16)
    k_scale = RD ** -0.5
    for c0, scale in ((C_RQ, None), (C_RK, k_scale)):
        t4 = _dot(hb, wret_ref[:, c0:c0 + RH * LANE])
        for hh in range(RH):
            t = t4[:, LANE * hh:LANE * (hh + 1)]
            if rope:
                t = t * cos_ref[...] + pltpu.roll(t, 64, 1) * sin_ref[...]
            if scale is not None:
                t = t * scale
            o_ref[:, c0 + LANE * hh:c0 + LANE * (hh + 1)] = t.astype(BF16)
    o_ref[:, C_RV:C_RG] = _dot(hb, wret_ref[:, C_RV:C_RG]).astype(BF16)
    o_ref[:, C_RG:C_NQ] = _silu(_dot(hb, wret_ref[:, C_RG:C_NQ])).astype(BF16)
    o_ref[:, C_NQ:] = _dot(hb, w_ref[:, W_NQ:]).astype(BF16)
    if not rope:
        idx = lax.broadcasted_iota(jnp.int32, (LC, LANE), 0).astype(F32)
        for b in range(B):
            for h in range(RH):
                k = o_ref[LC * b:LC * (b + 1), C_RK + LANE * h:C_RK + LANE * (h + 1)].astype(F32)
                v = o_ref[LC * b:LC * (b + 1), C_RV + LANE * h:C_RV + LANE * (h + 1)]
                wf = jnp.exp((LC - 1.0 - idx) * lg_ref[0, h])
                wb = jnp.exp(idx * lg_ref[1, h])
                s0_ref[b, h, 0] = _dot((k * wf).T.astype(BF16), v)
                s0_ref[b, h, 1] = _dot((k * wb).T.astype(BF16), v)


def _four1_kernel(a_ref, x_ref, y_ref):
    x = x_ref[0].reshape(DFT_N1 * F1_R, FW)
    y = _dot(a_ref[...], x).astype(BF16)
    y_ref[0] = y.reshape(2 * DFT_N1, F1_R, FW)


def _four3_kernel(m_ref, y_ref, e_ref, wf_ref, o_ref):
    ew = _dot(e_ref[...], wf_ref[...]).astype(BF16)
    for b in range(B):
        for kk in range(F3_K):
            y = jnp.concatenate([y_ref[b, 0, kk], y_ref[b, 1, kk]], axis=0)
            z = _dot(m_ref[kk], y).astype(BF16)
            o_ref[b, kk] = _dot(jnp.concatenate([z[:DFT_N2], z[DFT_N2:]], axis=1), ew)


def _decay_matrix(n, lgf, lgb):
    ii = lax.broadcasted_iota(jnp.int32, (n, n), 0)
    jj = lax.broadcasted_iota(jnp.int32, (n, n), 1)
    diff = (ii - jj).astype(F32)
    fwd = jnp.where(diff >= 0, jnp.exp(jnp.maximum(diff, 0.0) * lgf), 0.0)
    bwd = jnp.where(diff <= 0, jnp.exp(jnp.maximum(-diff, 0.0) * lgb), 0.0)
    return fwd + bwd


def _ret_readout(o, gate, gain):
    ms = jnp.sum(o * o, axis=-1, keepdims=True) * (1.0 / RD)
    return (o * lax.rsqrt(ms + EPS) * gain * gate.astype(F32)).astype(BF16)


def _ret_kernel(lg_ref, q_ref, k_ref, v_ref, g_ref, s0_ref, gain_ref, o_ref, acc_ref, stf_ref, stb_ref):
    h = pl.program_id(1)
    lgf = lg_ref[0, h]
    lgb = lg_ref[1, h]
    c = RET_C
    n = L // c
    dmat = _decay_matrix(c, lgf, lgb)
    idx = lax.broadcasted_iota(jnp.int32, (c, LANE), 0).astype(F32)
    qdf = jnp.exp((idx + 1.0) * lgf)
    kdf = jnp.exp((c - 1.0 - idx) * lgf)
    qdb = jnp.exp((c - idx) * lgb)
    kdb = jnp.exp(idx * lgb)
    cdf = jnp.exp(jnp.full((1, LANE), float(c), F32) * lgf)
    cdb = jnp.exp(jnp.full((1, LANE), float(c), F32) * lgb)
    gain = gain_ref[...]

    dirs = ((stf_ref, qdf, kdf, cdf), (stb_ref, qdb, kdb, cdb))
    stf_ref[...] = s0_ref[0, 0, 0]
    stb_ref[...] = s0_ref[0, 0, 1]

    def visit(r0, direction, second):
        st_ref, qdec, kdec, cdec = dirs[direction]
        rows = pl.ds(pl.multiple_of(r0, c), c)
        q = q_ref[rows, :]
        k = k_ref[rows, :]
        v = v_ref[rows, :]
        cross = _dot(q, st_ref[...].astype(BF16)) * qdec
        if second:
            o_ref[rows, :] = _ret_readout(acc_ref[rows, :] + cross, g_ref[rows, :], gain)
        else:
            s = _dot_nt(q, k) * dmat
            acc_ref[rows, :] = _dot(s.astype(BF16), v) + cross
        kt = (k.astype(F32) * kdec).T.astype(BF16)
        st_ref[...] = st_ref[...] * cdec + _dot(kt, v)

    def body(second):
        def step(i, carry):
            visit(i * c, 0, second)
            visit((n - 1 - i) * c, 1, second)
            return carry
        return step

    lax.fori_loop(0, n // 2, body(False), 0, unroll=4)
    lax.fori_loop(n // 2, n, body(True), 0, unroll=4)


def _pair_attention(q, score_fn, value_fn):
    lane = lax.broadcasted_iota(jnp.int32, q.shape, 1)
    outs = []
    for hh in range(2):
        sel = (lane >= ND * hh) & (lane < ND * (hh + 1))
        qm = jnp.where(sel, q, jnp.zeros_like(q))
        scores = score_fn(qm, hh)
        m = functools.reduce(jnp.maximum, [jnp.max(s, axis=-1, keepdims=True) for s in scores])
        ps = [jnp.exp2(s - m) for s in scores]
        den = functools.reduce(jnp.add, [jnp.sum(p, axis=-1, keepdims=True) for p in ps])
        outs.append(value_fn([p.astype(BF16) for p in ps]) / den)
    return jnp.where(lane < ND, outs[0], outs[1])


def _ctx_mix_kernel(lg_ref, f_ref, p_ref, gain_ref, wf_ref, c_ref, s_ref, e_ref, fo_ref, ro_ref, no_ref):
    f = f_ref[...]
    pr = _dot(c_ref[...], f)
    qi = _dot(s_ref[...], f)
    fr = _dot(pr.astype(BF16), e_ref[:FW, :]) + _dot(qi.astype(BF16), e_ref[FW:, :])
    fo_ref[...] = _dot(fr.astype(BF16), wf_ref[...]).astype(BF16)
    for h in range(RH):
        sl = lambda c0: slice(c0 + LANE * h, c0 + LANE * (h + 1))
        dmat = _decay_matrix(LC, lg_ref[0, h], lg_ref[1, h])
        s = _dot_nt(p_ref[:, sl(C_RQ)], p_ref[:, sl(C_RK)]) * dmat
        o = _dot(s.astype(BF16), p_ref[:, sl(C_RV)])
        ro_ref[:, LANE * h:LANE * (h + 1)] = _ret_readout(o, p_ref[:, sl(C_RG)], gain_ref[:, LANE * h:LANE * (h + 1)])
    for pair in range(NPAIR):
        sl = lambda c0: slice(c0 + LANE * pair, c0 + LANE * (pair + 1))
        k = p_ref[:, sl(C_NK)]
        v = p_ref[:, sl(C_NV)]
        out = _pair_attention(p_ref[:, sl(C_NQ)], lambda qm, hh: [_dot_nt(qm, k)], lambda ps: _dot(ps[0], v))
        no_ref[:, LANE * pair:LANE * (pair + 1)] = out.astype(BF16)


def _na_key_start(r0):
    return min(max(r0 - 4, 0), GRID_H - NA_KR)


def _na_bias_kernel(base_ref, mask_ref, o_ref, tz_ref):
    for dr in range(NA_DR - 1):
        t = jnp.broadcast_to(base_ref[0, dr:dr + 1, :], (GRID_W, LANE))
        t = pltpu.roll(t, 0, 1, stride=1, stride_axis=0)
        tz_ref[dr] = t * LOG2E + mask_ref[...]
    tz_ref[NA_DR - 1] = jnp.full((GRID_W, LANE), NEG, F32)
    low_half = lax.broadcasted_iota(jnp.int32, (GRID_W, LANE), 1) < GRID_W
    for var, r0 in enumerate((0, 2 * NA_R, GRID_H - NA_R)):
        ks = _na_key_start(r0)
        for rl in range(NA_R):
            r = r0 + rl
            rs = min(max(r - 4, 0), GRID_H - 8)
            slot = [kr - r + 7 if rs <= kr < rs + 8 else NA_DR - 1 for kr in range(ks, ks + NA_KR)]
            for m in range(NA_KR // 2):
                o_ref[0, var, GRID_W * rl:GRID_W * (rl + 1), LANE * m:LANE * (m + 1)] = jnp.where(
                    low_half, tz_ref[slot[2 * m]], tz_ref[slot[2 * m + 1]])


def _natten_kernel(q_ref, k_ref, v_ref, kc_ref, vc_ref, bias_ref, o_ref, pa_ref, pb_ref):
    nq = NA_R * GRID_W
    nk = NA_KR * GRID_W
    kc = kc_ref[...]
    vc = vc_ref[...]
    lane = lax.broadcasted_iota(jnp.int32, (nq, LANE), 1)

    def rows(i):
        ks = jnp.clip(NA_R * i - 4, 0, GRID_H - NA_KR)
        return pl.ds(pl.multiple_of(ks * GRID_W, nq), nk), pl.ds(pl.multiple_of(i * nq, nq), nq)

    def probabilities(i, p_ref):
        i = jnp.minimum(i, NA_NB - 1)
        var = jnp.where(i == 0, 0, jnp.where(i == NA_NB - 1, 2, 1))
        krows, qrows = rows(i)
        q = q_ref[qrows, :]
        kw = k_ref[krows, :]
        for hh in range(2):
            qm = jnp.where((lane >= ND * hh) & (lane < ND * (hh + 1)), q, jnp.zeros_like(q))
            s_win = _dot_nt(qm, kw) + bias_ref[hh, var]
            s_ctx = _dot_nt(qm, kc)
            m = jnp.maximum(jnp.max(s_win, axis=-1, keepdims=True), jnp.max(s_ctx, axis=-1, keepdims=True))
            p_ref[hh, :, :nk] = jnp.exp2(s_win - m).astype(BF16)
            p_ref[hh, :, nk:] = jnp.exp2(s_ctx - m).astype(BF16)

    def values(i, p_ref):
        krows, qrows = rows(i)
        vals = jnp.concatenate([v_ref[krows, :], vc], axis=0)
        vals = jnp.concatenate([vals, jnp.ones_like(vals)], axis=1)
        o2 = [_dot(p_ref[hh], vals) for hh in range(2)]
        outs = [o[:, :LANE] / o[:, LANE:] for o in o2]
        o_ref[qrows, :] = jnp.where(lane < ND, outs[0], outs[1]).astype(BF16)

    probabilities(0, pa_ref)

    def step(j, carry):
        values(2 * j, pa_ref)
        probabilities(2 * j + 1, pb_ref)
        values(2 * j + 1, pb_ref)
        probabilities(2 * j + 2, pa_ref)
        return carry

    lax.fori_loop(0, NA_NB // 2, step, 0)


def _ffn_kernel(*refs, final, mod_row):
    x_ref, f_ref, r_ref, n_ref, mods_ref, g_ref, wo_ref, sel_ref, w1_ref, w3_ref, w2_ref = refs[:11]
    gf_ref = refs[11] if final else None
    o_ref, acc_ref, wor_ref = refs[-3:]
    s = pl.program_id(0)

    @pl.when(s == 0)
    def _():
        wor_ref[...] = _dot(sel_ref[...], wo_ref[FW:FW + RH * RD, :]).astype(BF16)

    mod = mods_ref[mod_row(s)]
    if f_ref.ndim == 3:
        f = jnp.concatenate([f_ref[:, r, :] for r in range(f_ref.shape[1])], axis=0)
    else:
        f = f_ref[...]
    y = (_dot(f.astype(BF16), wo_ref[:FW, :]) + _dot(r_ref[...], wor_ref[...])
         + _dot(n_ref[...], wo_ref[FW + RH * RD:, :]))
    x1 = x_ref[...] + mod[2:3, :] * y
    hb = (_rmsnorm(x1, g_ref[...]) * (1.0 + mod[4:5, :]) + mod[3:4, :]).astype(BF16)
    fc = 256
    for c in range(DFF // fc):
        a = _dot(hb, w1_ref[:, c * fc:(c + 1) * fc])
        b = _dot(hb, w3_ref[:, c * fc:(c + 1) * fc])
        acc_ref[:, c * fc:(c + 1) * fc] = (_silu(a) * b).astype(BF16)
    x2 = x1 + mod[5:6, :] * _dot(acc_ref[...], w2_ref[...])
    if final:
        x2 = _rmsnorm(x2, gf_ref[...])
    o_ref[...] = x2


def _params(sem):
    return pltpu.CompilerParams(dimension_semantics=sem, vmem_limit_bytes=VMEM_LIMIT)


def _const_spec(shape):
    nd = len(shape)
    return pl.BlockSpec(shape, lambda *_: (0,) * nd, pipeline_mode=pl.Buffered(1))


def _smem_spec():
    return pl.BlockSpec(memory_space=pltpu.SMEM)


def _ada(cvec, w_ada, b_ada):
    tn = 2048
    return pl.pallas_call(
        _ada_kernel,
        grid=(DEPTH, 6 * D // tn),
        in_specs=[pl.BlockSpec((8, D), lambda i, j: (0, 0)),
                  pl.BlockSpec((1, D, tn), lambda i, j: (i, 0, j)),
                  pl.BlockSpec((1, 1, tn), lambda i, j: (i, 0, j))],
        out_specs=pl.BlockSpec((1, 8, tn), lambda i, j: (i, 0, j)),
        out_shape=jax.ShapeDtypeStruct((DEPTH, 8, 6 * D), F32),
        compiler_params=_params(("arbitrary", "arbitrary")),
        name="ada",
    )(cvec, w_ada, b_ada.reshape(DEPTH, 1, 6 * D))


def _mod_row(ctx, tm):
    per_batch = L // tm
    return (lambda tile: B) if ctx else (lambda tile: tile // per_batch)


def _layer_spec(shape, layer):
    nd = len(shape)
    return pl.BlockSpec((None,) + tuple(shape), lambda *_: (layer,) + (0,) * nd, pipeline_mode=pl.Buffered(1))


def _proj(x2d, mods, g, w_all, layer, sel_qk, sel_v, extra, *, ctx):
    rows = x2d.shape[0]
    tm = min(TM_PROJ, rows)
    per_batch = L // tm
    in_specs = [pl.BlockSpec((tm, D), lambda i: (i, 0)),
                _const_spec((8, 6, D)),
                _const_spec((1, D)),
                _layer_spec((D, W_IN), layer),
                _const_spec((RH * RD, RH * LANE)),
                _const_spec((RH * RD, RH * LANE))]
    args = [x2d, mods, g.reshape(1, D), w_all, sel_qk, sel_v]
    out_specs = [pl.BlockSpec((tm, FW), lambda i: (i, 0)), pl.BlockSpec((tm, NP), lambda i: (i, 0))]
    out_shape = [jax.ShapeDtypeStruct((rows, FW), BF16), jax.ShapeDtypeStruct((rows, NP), BF16)]
    if ctx:
        assert rows == tm == B * LC
        in_specs.append(_smem_spec())
        args.append(extra)
        out_specs.append(pl.BlockSpec((B, RH, 2, LANE, LANE), lambda i: (0, 0, 0, 0, 0)))
        out_shape.append(jax.ShapeDtypeStruct((B, RH, 2, LANE, LANE), F32))
    else:
        in_specs += [pl.BlockSpec((tm, LANE), lambda i: (i % per_batch, 0))] * 2
        args += list(extra)
    return pl.pallas_call(
        functools.partial(_proj_kernel, rope=not ctx, mod_row=_mod_row(ctx, tm)),
        grid=(rows // tm,),
        in_specs=in_specs,
        out_specs=out_specs,
        out_shape=out_shape,
        scratch_shapes=[pltpu.VMEM((D, 4 * RH * LANE), BF16)],
        compiler_params=_params(("arbitrary",)),
        name="proj_ctx" if ctx else "proj",
    )(*args)


def _fourier(f2d, wf, a1, m3, e):
    y = pl.pallas_call(
        _four1_kernel,
        grid=(B, DFT_N2 // F1_R),
        in_specs=[_const_spec((2 * DFT_N1 * F1_R, DFT_N1 * F1_R)),
                  pl.BlockSpec((1, DFT_N1, F1_R, FW), lambda b, j: (b, 0, j, 0))],
        out_specs=pl.BlockSpec((1, 2 * DFT_N1, F1_R, FW), lambda b, j: (b, 0, j, 0)),
        out_shape=jax.ShapeDtypeStruct((B, 2 * DFT_N1, DFT_N2, FW), BF16),
        compiler_params=_params(("arbitrary", "arbitrary")),
        name="four1",
    )(a1, f2d.reshape(B, DFT_N1, DFT_N2, FW))
    out = pl.pallas_call(
        _four3_kernel,
        grid=(DFT_N1 // F3_K,),
        in_specs=[pl.BlockSpec((F3_K, 2 * DFT_N2, 2 * DFT_N2), lambda k: (k, 0, 0)),
                  pl.BlockSpec((B, 2, F3_K, DFT_N2, FW), lambda k: (0, 0, k, 0, 0)),
                  _const_spec((2 * FW, FW)),
                  _const_spec((FW, FW))],
        out_specs=pl.BlockSpec((B, F3_K, DFT_N2, FW), lambda k: (0, k, 0, 0)),
        out_shape=jax.ShapeDtypeStruct((B, DFT_N1, DFT_N2, FW), F32),
        compiler_params=_params(("arbitrary",)),
        name="four3",
    )(m3, y.reshape(B, 2, DFT_N1, DFT_N2, FW), e, wf)
    return out.reshape(B * DFT_N1, DFT_N2, FW)


def _retention(p, s0, lg, gain):
    col = lambda c0: (lambda b, h: (b, c0 // LANE + h))
    return pl.pallas_call(
        _ret_kernel,
        grid=(B, RH),
        in_specs=[_smem_spec(),
                  pl.BlockSpec((L, LANE), col(C_RQ)),
                  pl.BlockSpec((L, LANE), col(C_RK)),
                  pl.BlockSpec((L, LANE), col(C_RV)),
                  pl.BlockSpec((L, LANE), col(C_RG)),
                  pl.BlockSpec((1, 1, 2, LANE, LANE), lambda b, h: (b, h, 0, 0, 0)),
                  pl.BlockSpec((1, LANE), lambda b, h: (0, h))],
        out_specs=pl.BlockSpec((L, LANE), lambda b, h: (b, h)),
        out_shape=jax.ShapeDtypeStruct((B * L, RH * LANE), BF16),
        scratch_shapes=[pltpu.VMEM((L, LANE), F32), pltpu.VMEM((LANE, LANE), F32), pltpu.VMEM((LANE, LANE), F32)],
        compiler_params=_params(("arbitrary", "arbitrary")),
        name="ret",
    )(lg, p, p, p, p, s0, gain)


def _ctx_mix(fc, pc, lg, gain, wf, c256, s256, e):
    return pl.pallas_call(
        _ctx_mix_kernel,
        grid=(B,),
        in_specs=[_smem_spec(),
                  pl.BlockSpec((LC, FW), lambda b: (b, 0)),
                  pl.BlockSpec((LC, NP), lambda b: (b, 0)),
                  _const_spec((1, RH * LANE)),
                  _const_spec((FW, FW)),
                  _const_spec((LC, LC)),
                  _const_spec((LC, LC)),
                  _const_spec((2 * FW, FW))],
        out_specs=[pl.BlockSpec((LC, FW), lambda b: (b, 0)),
                   pl.BlockSpec((LC, RH * LANE), lambda b: (b, 0)),
                   pl.BlockSpec((LC, NH * ND), lambda b: (b, 0))],
        out_shape=[jax.ShapeDtypeStruct((B * LC, FW), BF16),
                   jax.ShapeDtypeStruct((B * LC, RH * LANE), BF16),
                   jax.ShapeDtypeStruct((B * LC, NH * ND), BF16)],
        compiler_params=_params(("arbitrary",)),
        name="ctx_mix",
    )(lg, fc, pc, gain, wf, c256, s256, e)


def _na_bias(base, mask):
    nq, nk = NA_R * GRID_W, NA_KR * GRID_W
    return pl.pallas_call(
        _na_bias_kernel,
        grid=(base.shape[0],),
        in_specs=[pl.BlockSpec((1, NA_DR, LANE), lambda h: (h, 0, 0)),
                  _const_spec((GRID_W, LANE))],
        out_specs=pl.BlockSpec((1, 3, nq, nk), lambda h: (h, 0, 0, 0)),
        out_shape=jax.ShapeDtypeStruct((base.shape[0], 3, nq, nk), F32),
        scratch_shapes=[pltpu.VMEM((NA_DR, GRID_W, LANE), F32)],
        compiler_params=_params(("arbitrary",)),
        name="na_bias",
    )(base, mask)


def _natten(p, pc, bias, layer):
    nq, nk = NA_R * GRID_W, NA_KR * GRID_W
    return pl.pallas_call(
        _natten_kernel,
        grid=(B, NPAIR),
        in_specs=[pl.BlockSpec((L, LANE), lambda b, pr: (b, C_NQ // LANE + pr)),
                  pl.BlockSpec((L, LANE), lambda b, pr: (b, C_NK // LANE + pr)),
                  pl.BlockSpec((L, LANE), lambda b, pr: (b, C_NV // LANE + pr)),
                  pl.BlockSpec((LC, LANE), lambda b, pr: (b, C_NK // LANE + pr)),
                  pl.BlockSpec((LC, LANE), lambda b, pr: (b, C_NV // LANE + pr)),
                  pl.BlockSpec((2, 3, nq, nk), lambda b, pr: (NPAIR * layer + pr, 0, 0, 0))],
        out_specs=pl.BlockSpec((L, LANE), lambda b, pr: (b, pr)),
        out_shape=jax.ShapeDtypeStruct((B * L, NH * ND), BF16),
        scratch_shapes=[pltpu.VMEM((2, nq, nk + LC), BF16)] * 2,
        compiler_params=_params(("arbitrary", "arbitrary")),
        name="natten",
    )(p, p, p, pc, pc, bias)


def _ffn(x2d, four, ret, na, mods, g, wo_all, sel_v_t, w1_all, w3_all, w2_all, layer, g_final, *, ctx):
    rows = x2d.shape[0]
    tm = min(TM, rows)
    final = g_final is not None
    row = lambda w: pl.BlockSpec((tm, w), lambda i: (i, 0))
    if four.ndim == 3:
        per_batch = L // tm
        four_spec = pl.BlockSpec((DFT_N1, tm // DFT_N1, FW), lambda i: (i // per_batch, i % per_batch, 0))
    else:
        four_spec = row(FW)
    in_specs = [row(D), four_spec, row(RH * LANE), row(NH * ND),
                _const_spec((8, 6, D)), _const_spec((1, D)),
                _layer_spec((D, D), layer), _const_spec((RH * LANE, RH * RD)),
                _layer_spec((D, DFF), layer), _layer_spec((D, DFF), layer), _layer_spec((DFF, D), layer)]
    args = [x2d, four, ret, na, mods, g.reshape(1, D), wo_all, sel_v_t, w1_all, w3_all, w2_all]
    if final:
        in_specs.append(_const_spec((1, D)))
        args.append(g_final.reshape(1, D))
    return pl.pallas_call(
        functools.partial(_ffn_kernel, final=final, mod_row=_mod_row(ctx, tm)),
        grid=(rows // tm,),
        in_specs=in_specs,
        out_specs=row(D),
        out_shape=jax.ShapeDtypeStruct((rows, D), F32),
        scratch_shapes=[pltpu.VMEM((tm, DFF), BF16), pltpu.VMEM((RH * LANE, D), BF16)],
        compiler_params=_params(("arbitrary",)),
        name="ffn_ctx" if ctx else ("ffn_final" if final else "ffn"),
    )(*args)


def _take_padded(a, src, axis):
    pieces = []
    i, n = 0, len(src)
    while i < n:
        j = i + 1
        if src[i] < 0:
            while j < n and src[j] < 0:
                j += 1
            shape = list(a.shape)
            shape[axis] = j - i
            pieces.append(jnp.zeros(shape, a.dtype))
        else:
            while j < n and src[j] == src[j - 1] + 1:
                j += 1
            pieces.append(lax.slice_in_dim(a, int(src[i]), int(src[j - 1]) + 1, axis=axis))
        i = j
    return jnp.concatenate(pieces, axis=axis)


def kernel(x, c, ctx, c_ctx, w_ada, b_ada, g_mix, w_in, ret_decay_logit, ret_norm_g, w_four,
           na_rpb, w_out, g_ffn, w1, w3, w2, g_final):
    src_g = _gain_sources()
    rope_tabs = tuple(jnp.asarray(t) for t in _rope_tables())
    a1, m3, e, c256, s256 = (jnp.asarray(t).astype(BF16) for t in _dft_tables())
    sel_qk, sel_v = (jnp.asarray(t).astype(BF16) for t in _head_selectors())
    sel_v_t = sel_v.T
    tz_src, tz_mask = _na_toeplitz_tables()
    col_scale = np.ones((1, 1, W_IN), np.float32)
    col_scale[..., W_NQ:W_NK] = LOG2E * ND ** -0.5
    w_in_b = (w_in * jnp.asarray(col_scale)).astype(BF16)
    w_out_b = w_out.astype(BF16)
    w1b, w3b, w2b = w1.astype(BF16), w3.astype(BF16), w2.astype(BF16)

    tz_base = jnp.pad(_take_padded(na_rpb.astype(F32).reshape(DEPTH * NH, 15, 31), tz_src, 2), ((0, 0), (0, 1), (0, 0)))
    na_bias = _na_bias(tz_base, jnp.asarray(tz_mask))

    cvec = jnp.zeros((8, D), F32).at[0:B].set(c).at[B].set(c_ctx)
    mods_all = _ada(cvec, w_ada, b_ada).reshape(DEPTH, 8, 6, D)

    xl = x.reshape(B * L, D)
    xc = ctx.reshape(B * LC, D)
    for i in range(DEPTH):
        last = i == DEPTH - 1
        mods = mods_all[i]
        gain = _take_padded(ret_norm_g[i], src_g, 0).reshape(1, RH * LANE)
        wfour = w_four[i].astype(BF16)
        lg = jax.nn.log_sigmoid(ret_decay_logit[i].astype(F32))

        f_l, p_l = _proj(xl, mods, g_mix[i], w_in_b, i, sel_qk, sel_v, rope_tabs, ctx=False)
        f_c, p_c, s0 = _proj(xc, mods, g_mix[i], w_in_b, i, sel_qk, sel_v, lg, ctx=True)
        four_l = _fourier(f_l, wfour, a1, m3, e)
        ret_l = _retention(p_l, s0, lg, gain)
        na_l = _natten(p_l, p_c, na_bias, i)
        if not last:
            four_c, ret_c, na_c = _ctx_mix(f_c, p_c, lg, gain, wfour, c256, s256, e)
            xc = _ffn(xc, four_c, ret_c, na_c, mods, g_ffn[i], w_out_b, sel_v_t, w1b, w3b, w2b, i, None, ctx=True)
        xl = _ffn(xl, four_l, ret_l, na_l, mods, g_ffn[i], w_out_b, sel_v_t, w1b, w3b, w2b, i,
                  g_final if last else None, ctx=False)
    return xl.reshape(B, L, D)
```

```python
import functools

import numpy as np
import jax
import jax.numpy as jnp
from jax import lax
from jax.experimental import pallas as pl
from jax.experimental.pallas import tpu as pltpu

F32 = jnp.float32
BF16 = jnp.bfloat16

D = 1024
B = 2
L = 8192
LC = 256
DEPTH = 2
GRID_W = 64
GRID_H = L // GRID_W
FW = 256
FGW = 64
RH = 4
RD = 96
NH = 6
ND = 64
NPAIR = NH // 2
DFF = 2816
EPS = 1e-6
NEG = -1e30
LOG2E = float(np.log2(np.e))
LANE = 128

C_RQ, C_RK, C_RV, C_RG = 0, 512, 1024, 1536
C_NQ, C_NK, C_NV = 2048, 2432, 2816
NP = 3200
W_RQ, W_RK, W_RV, W_RG = 256, 640, 1024, 1408
W_NQ, W_NK, W_NV = 1792, 2176, 2560
W_IN = 2944

TM = 512
TM_PROJ = 1024
RET_C = 256
NA_R = 4
NA_KR = NA_R + 8
NA_NB = GRID_H // NA_R
DFT_N1 = 64
DFT_N2 = 128
F1_R = 16
F3_K = 8
VMEM_LIMIT = 56 * 1024 * 1024


def _qk_lane_dims():
    m = np.full((LANE,), -1, np.int64)
    m[0:24] = np.arange(0, 24)
    m[24:48] = np.arange(48, 72)
    m[64:88] = np.arange(24, 48)
    m[88:112] = np.arange(72, 96)
    return m


def _v_lane_dims():
    m = np.full((LANE,), -1, np.int64)
    m[:RD] = np.arange(RD)
    return m


def _head_selectors():
    sels = []
    for lanes in (_qk_lane_dims(), _v_lane_dims()):
        sel = np.zeros((RH * RD, RH * LANE), np.float32)
        ok = np.nonzero(lanes >= 0)[0]
        for h in range(RH):
            sel[RD * h + lanes[ok], LANE * h + ok] = 1.0
        sels.append(sel)
    return sels


def _gain_sources():
    src = np.full((RH * LANE,), -1, np.int64)
    for h in range(RH):
        src[LANE * h: LANE * h + RD] = RD * h + np.arange(RD)
    return src


def _rope_tables():
    pos = np.arange(L)
    prow, pcol = pos // GRID_W, pos % GRID_W
    half = RD // 4
    inv = 10000.0 ** (-np.arange(half, dtype=np.float64) / half)
    ar = prow[:, None] * inv[None, :]
    ac = pcol[:, None] * inv[None, :]
    cos = np.ones((L, LANE), np.float64)
    sin = np.zeros((L, LANE), np.float64)
    for off, sign in ((0, -1.0), (64, 1.0)):
        cos[:, off:off + 24] = np.cos(ar)
        cos[:, off + 24:off + 48] = np.cos(ac)
        sin[:, off:off + 24] = sign * np.sin(ar)
        sin[:, off + 24:off + 48] = sign * np.sin(ac)
    return cos.astype(np.float32), sin.astype(np.float32)


def _dft_tables():
    k1 = np.arange(DFT_N1)
    a = 2 * np.pi * ((k1[:, None] * k1[None, :]) % DFT_N1) / DFT_N1
    a1 = np.kron(np.concatenate([np.cos(a), -np.sin(a)], axis=0), np.eye(F1_R))
    k2 = np.arange(DFT_N2)
    l2 = np.arange(DFT_N2)
    kk = k1[:, None, None] + DFT_N1 * k2[None, :, None]
    ang = 2 * np.pi * ((kk * l2[None, None, :]) % L) / L
    ct, st = np.cos(ang) / np.sqrt(L), np.sin(ang) / np.sqrt(L)
    m3 = np.concatenate([np.concatenate([ct, st], axis=2), np.concatenate([st, -ct], axis=2)], axis=1)
    c = np.arange(FW)
    same = (c[:, None] // FGW) == (c[None, :] // FGW)
    ac = 2 * np.pi * (((c[:, None] % FGW) * (c[None, :] % FGW)) % FGW) / FGW
    e = np.concatenate([np.where(same, np.cos(ac), 0.0), -np.where(same, np.sin(ac), 0.0)], axis=0) / 8.0
    p = np.arange(LC)
    ap = 2 * np.pi * ((p[:, None] * p[None, :]) % LC) / LC
    return tuple(t.astype(np.float32) for t in (a1, m3, e, np.cos(ap) / 16.0, np.sin(ap) / 16.0))


NA_DR = 16


def _na_toeplitz_tables():
    src = np.full((LANE,), -1, np.int64)
    src[0:16] = np.arange(0, 16) + 15
    src[113:128] = np.arange(113, 128) - 113
    src[49:80] = np.arange(49, 80) - 49
    qc = np.arange(GRID_W)[:, None]
    kc = np.arange(LANE)[None, :] % GRID_W
    ws = np.clip(qc - 8, 0, GRID_W - 16)
    mask = np.where((kc >= ws) & (kc < ws + 16), 0.0, NEG).astype(np.float32)
    return src, mask


def _dot(a, b):
    return jnp.dot(a, b, preferred_element_type=F32)


def _dot_nt(a, b):
    return lax.dot_general(a, b, (((1,), (1,)), ((), ())), preferred_element_type=F32)


def _silu(x):
    return x * jax.nn.sigmoid(x)


def _rmsnorm(x, g):
    return x * lax.rsqrt(jnp.mean(x * x, axis=-1, keepdims=True) + EPS) * g


def _ada_kernel(c_ref, w_ref, b_ref, o_ref):
    s = _silu(c_ref[...]).astype(BF16)
    o_ref[0] = _dot(s, w_ref[0].astype(BF16)) + b_ref[0]


def _proj_kernel(*refs, rope, mod_row):
    if rope:
        x_ref, mods_ref, g_ref, w_ref, pqk_ref, pv_ref, cos_ref, sin_ref, f_ref, o_ref, wret_ref = refs
    else:
        x_ref, mods_ref, g_ref, w_ref, pqk_ref, pv_ref, lg_ref, f_ref, o_ref, s0_ref, wret_ref = refs
    s = pl.program_id(0)

    @pl.when(s == 0)
    def _():
        for slot, (src0, sel_ref) in enumerate(((W_RQ, pqk_ref), (W_RK, pqk_ref), (W_RV, pv_ref), (W_RG, pv_ref))):
            wret_ref[:, RH * LANE * slot:RH * LANE * (slot + 1)] = _dot(
                w_ref[:, src0:src0 + RH * RD], sel_ref[...]).astype(BF16)

    mod = mods_ref[mod_row(s)]
    hb = (_rmsnorm(x_ref[...], g_ref[...]) * (1.0 + mod[1:2, :]) + mod[0:1, :]).astype(BF16)
    f_ref[...] = _dot(hb, w_ref[:, :FW]).astype(BF16)
    k_scale = RD ** -0.5
    for c0, scale in ((C_RQ, None), (C_RK, k_scale)):
        t4 = _dot(hb, wret_ref[:, c0:c0 + RH * LANE])
        for hh in range(RH):
            t = t4[:, LANE * hh:LANE * (hh + 1)]
            if rope:
                t = t * cos_ref[...] + pltpu.roll(t, 64, 1) * sin_ref[...]
            if scale is not None:
                t = t * scale
            o_ref[:, c0 + LANE * hh:c0 + LANE * (hh + 1)] = t.astype(BF16)
    o_ref[:, C_RV:C_RG] = _dot(hb, wret_ref[:, C_RV:C_RG]).astype(BF16)
    o_ref[:, C_RG:C_NQ] = _silu(_dot(hb, wret_ref[:, C_RG:C_NQ])).astype(BF16)
    o_ref[:, C_NQ:] = _dot(hb, w_ref[:, W_NQ:]).astype(BF16)
    if not rope:
        idx = lax.broadcasted_iota(jnp.int32, (LC, LANE), 0).astype(F32)
        for b in range(B):
            for h in range(RH):
                k = o_ref[LC * b:LC * (b + 1), C_RK + LANE * h:C_RK + LANE * (h + 1)].astype(F32)
                v = o_ref[LC * b:LC * (b + 1), C_RV + LANE * h:C_RV + LANE * (h + 1)]
                wf = jnp.exp((LC - 1.0 - idx) * lg_ref[0, h])
                wb = jnp.exp(idx * lg_ref[1, h])
                s0_ref[b, h, 0] = _dot((k * wf).T.astype(BF16), v)
                s0_ref[b, h, 1] = _dot((k * wb).T.astype(BF16), v)


def _four1_kernel(a_ref, x_ref, y_ref):
    x = x_ref[0].reshape(DFT_N1 * F1_R, FW)
    y = _dot(a_ref[...], x).astype(BF16)
    y_ref[0] = y.reshape(2 * DFT_N1, F1_R, FW)


def _four3_kernel(m_ref, y_ref, e_ref, wf_ref, o_ref):
    ew = _dot(e_ref[...], wf_ref[...]).astype(BF16)
    for b in range(B):
        for kk in range(F3_K):
            y = jnp.concatenate([y_ref[b, 0, kk], y_ref[b, 1, kk]], axis=0)
            z = _dot(m_ref[kk], y).astype(BF16)
            o_ref[b, kk] = _dot(jnp.concatenate([z[:DFT_N2], z[DFT_N2:]], axis=1), ew)


def _decay_matrix(n, lgf, lgb):
    ii = lax.broadcasted_iota(jnp.int32, (n, n), 0)
    jj = lax.broadcasted_iota(jnp.int32, (n, n), 1)
    diff = (ii - jj).astype(F32)
    fwd = jnp.where(diff >= 0, jnp.exp(jnp.maximum(diff, 0.0) * lgf), 0.0)
    bwd = jnp.where(diff <= 0, jnp.exp(jnp.maximum(-diff, 0.0) * lgb), 0.0)
    return fwd + bwd


def _ret_readout(o, gate, gain):
    ms = jnp.sum(o * o, axis=-1, keepdims=True) * (1.0 / RD)
    return (o * lax.rsqrt(ms + EPS) * gain * gate.astype(F32)).astype(BF16)


def _ret_kernel(lg_ref, q_ref, k_ref, v_ref, g_ref, s0_ref, gain_ref, o_ref, acc_ref, stf_ref, stb_ref):
    h = pl.program_id(1)
    lgf = lg_ref[0, h]
    lgb = lg_ref[1, h]
    c = RET_C
    n = L // c
    dmat = _decay_matrix(c, lgf, lgb)
    idx = lax.broadcasted_iota(jnp.int32, (c, LANE), 0).astype(F32)
    qdf = jnp.exp((idx + 1.0) * lgf)
    kdf = jnp.exp((c - 1.0 - idx) * lgf)
    qdb = jnp.exp((c - idx) * lgb)
    kdb = jnp.exp(idx * lgb)
    cdf = jnp.exp(jnp.full((1, LANE), float(c), F32) * lgf)
    cdb = jnp.exp(jnp.full((1, LANE), float(c), F32) * lgb)
    gain = gain_ref[...]

    dirs = ((stf_ref, qdf, kdf, cdf), (stb_ref, qdb, kdb, cdb))
    stf_ref[...] = s0_ref[0, 0, 0]
    stb_ref[...] = s0_ref[0, 0, 1]

    def visit(r0, direction, second):
        st_ref, qdec, kdec, cdec = dirs[direction]
        rows = pl.ds(pl.multiple_of(r0, c), c)
        q = q_ref[rows, :]
        k = k_ref[rows, :]
        v = v_ref[rows, :]
        cross = _dot(q, st_ref[...].astype(BF16)) * qdec
        if second:
            o_ref[rows, :] = _ret_readout(acc_ref[rows, :] + cross, g_ref[rows, :], gain)
        else:
            s = _dot_nt(q, k) * dmat
            acc_ref[rows, :] = _dot(s.astype(BF16), v) + cross
        kt = (k.astype(F32) * kdec).T.astype(BF16)
        st_ref[...] = st_ref[...] * cdec + _dot(kt, v)

    def body(second):
        def step(i, carry):
            visit(i * c, 0, second)
            visit((n - 1 - i) * c, 1, second)
            return carry
        return step

    lax.fori_loop(0, n // 2, body(False), 0, unroll=4)
    lax.fori_loop(n // 2, n, body(True), 0, unroll=4)


def _pair_attention(q, score_fn, value_fn):
    lane = lax.broadcasted_iota(jnp.int32, q.shape, 1)
    outs = []
    for hh in range(2):
        sel = (lane >= ND * hh) & (lane < ND * (hh + 1))
        qm = jnp.where(sel, q, jnp.zeros_like(q))
        scores = score_fn(qm, hh)
        m = functools.reduce(jnp.maximum, [jnp.max(s, axis=-1, keepdims=True) for s in scores])
        ps = [jnp.exp2(s - m) for s in scores]
        den = functools.reduce(jnp.add, [jnp.sum(p, axis=-1, keepdims=True) for p in ps])
        outs.append(value_fn([p.astype(BF16) for p in ps]) / den)
    return jnp.where(lane < ND, outs[0], outs[1])


def _ctx_mix_kernel(lg_ref, f_ref, p_ref, gain_ref, wf_ref, c_ref, s_ref, e_ref, fo_ref, ro_ref, no_ref):
    f = f_ref[...]
    pr = _dot(c_ref[...], f)
    qi = _dot(s_ref[...], f)
    fr = _dot(pr.astype(BF16), e_ref[:FW, :]) + _dot(qi.astype(BF16), e_ref[FW:, :])
    fo_ref[...] = _dot(fr.astype(BF16), wf_ref[...]).astype(BF16)
    for h in range(RH):
        sl = lambda c0: slice(c0 + LANE * h, c0 + LANE * (h + 1))
        dmat = _decay_matrix(LC, lg_ref[0, h], lg_ref[1, h])
        s = _dot_nt(p_ref[:, sl(C_RQ)], p_ref[:, sl(C_RK)]) * dmat
        o = _dot(s.astype(BF16), p_ref[:, sl(C_RV)])
        ro_ref[:, LANE * h:LANE * (h + 1)] = _ret_readout(o, p_ref[:, sl(C_RG)], gain_ref[:, LANE * h:LANE * (h + 1)])
    for pair in range(NPAIR):
        sl = lambda c0: slice(c0 + LANE * pair, c0 + LANE * (pair + 1))
        k = p_ref[:, sl(C_NK)]
        v = p_ref[:, sl(C_NV)]
        out = _pair_attention(p_ref[:, sl(C_NQ)], lambda qm, hh: [_dot_nt(qm, k)], lambda ps: _dot(ps[0], v))
        no_ref[:, LANE * pair:LANE * (pair + 1)] = out.astype(BF16)


def _na_key_start(r0):
    return min(max(r0 - 4, 0), GRID_H - NA_KR)


def _na_bias_kernel(base_ref, mask_ref, o_ref, tz_ref):
    for dr in range(NA_DR - 1):
        t = jnp.broadcast_to(base_ref[0, dr:dr + 1, :], (GRID_W, LANE))
        t = pltpu.roll(t, 0, 1, stride=1, stride_axis=0)
        tz_ref[dr] = t * LOG2E + mask_ref[...]
    tz_ref[NA_DR - 1] = jnp.full((GRID_W, LANE), NEG, F32)
    low_half = lax.broadcasted_iota(jnp.int32, (GRID_W, LANE), 1) < GRID_W
    for var, r0 in enumerate((0, 2 * NA_R, GRID_H - NA_R)):
        ks = _na_key_start(r0)
        for rl in range(NA_R):
            r = r0 + rl
            rs = min(max(r - 4, 0), GRID_H - 8)
            slot = [kr - r + 7 if rs <= kr < rs + 8 else NA_DR - 1 for kr in range(ks, ks + NA_KR)]
            for m in range(NA_KR // 2):
                o_ref[0, var, GRID_W * rl:GRID_W * (rl + 1), LANE * m:LANE * (m + 1)] = jnp.where(
                    low_half, tz_ref[slot[2 * m]], tz_ref[slot[2 * m + 1]])


def _natten_kernel(q_ref, k_ref, v_ref, kc_ref, vc_ref, bias_ref, o_ref, pa_ref, pb_ref):
    nq = NA_R * GRID_W
    nk = NA_KR * GRID_W
    kc = kc_ref[...]
    vc = vc_ref[...]
    lane = lax.broadcasted_iota(jnp.int32, (nq, LANE), 1)

    def rows(i):
        ks = jnp.clip(NA_R * i - 4, 0, GRID_H - NA_KR)
        return pl.ds(pl.multiple_of(ks * GRID_W, nq), nk), pl.ds(pl.multiple_of(i * nq, nq), nq)

    def probabilities(i, p_ref):
        i = jnp.minimum(i, NA_NB - 1)
        var = jnp.where(i == 0, 0, jnp.where(i == NA_NB - 1, 2, 1))
        krows, qrows = rows(i)
        q = q_ref[qrows, :]
        kw = k_ref[krows, :]
        for hh in range(2):
            qm = jnp.where((lane >= ND * hh) & (lane < ND * (hh + 1)), q, jnp.zeros_like(q))
            s_win = _dot_nt(qm, kw) + bias_ref[hh, var]
            s_ctx = _dot_nt(qm, kc)
            m = jnp.maximum(jnp.max(s_win, axis=-1, keepdims=True), jnp.max(s_ctx, axis=-1, keepdims=True))
            p_ref[hh, :, :nk] = jnp.exp2(s_win - m).astype(BF16)
            p_ref[hh, :, nk:] = jnp.exp2(s_ctx - m).astype(BF16)

    def values(i, p_ref):
        krows, qrows = rows(i)
        vals = jnp.concatenate([v_ref[krows, :], vc], axis=0)
        vals = jnp.concatenate([vals, jnp.ones_like(vals)], axis=1)
        o2 = [_dot(p_ref[hh], vals) for hh in range(2)]
        outs = [o[:, :LANE] / o[:, LANE:] for o in o2]
        o_ref[qrows, :] = jnp.where(lane < ND, outs[0], outs[1]).astype(BF16)

    probabilities(0, pa_ref)

    def step(j, carry):
        values(2 * j, pa_ref)
        probabilities(2 * j + 1, pb_ref)
        values(2 * j + 1, pb_ref)
        probabilities(2 * j + 2, pa_ref)
        return carry

    lax.fori_loop(0, NA_NB // 2, step, 0)


def _ffn_kernel(*refs, final, ctx_tile, mod_row):
    refs = list(refs)
    x_ref, f_ref, r_ref, n_ref = refs[:4]
    del refs[:4]
    if ctx_tile:
        xc_ref, fc_ref, rc_ref, nc_ref = refs[:4]
        del refs[:4]
    mods_ref, g_ref, wo_ref, sel_ref, w1_ref, w3_ref, w2_ref = refs[:7]
    del refs[:7]
    gf_ref = refs.pop(0) if final else None
    o_ref = refs.pop(0)
    oc_ref = refs.pop(0) if ctx_tile else None
    acc_ref, wor_ref = refs
    s = pl.program_id(0)

    @pl.when(s == 0)
    def _():
        wor_ref[...] = _dot(sel_ref[...], wo_ref[FW:FW + RH * RD, :]).astype(BF16)

    def tile(x, f, r, n, mod, out_ref):
        y = _dot(f.astype(BF16), wo_ref[:FW, :]) + _dot(r, wor_ref[...]) + _dot(n, wo_ref[FW + RH * RD:, :])
        x1 = x + mod[2:3, :] * y
        hb = (_rmsnorm(x1, g_ref[...]) * (1.0 + mod[4:5, :]) + mod[3:4, :]).astype(BF16)
        fc = 256
        for c in range(DFF // fc):
            a = _dot(hb, w1_ref[:, c * fc:(c + 1) * fc])
            b = _dot(hb, w3_ref[:, c * fc:(c + 1) * fc])
            acc_ref[:, c * fc:(c + 1) * fc] = (_silu(a) * b).astype(BF16)
        x2 = x1 + mod[5:6, :] * _dot(acc_ref[...], w2_ref[...])
        if final:
            x2 = _rmsnorm(x2, gf_ref[...])
        out_ref[...] = x2

    def latent_tile(t):
        f = jnp.concatenate([f_ref[:, r, :] for r in range(f_ref.shape[1])], axis=0)
        tile(x_ref[...], f, r_ref[...], n_ref[...], mods_ref[mod_row(t)], o_ref)

    if ctx_tile:
        @pl.when(s == 0)
        def _():
            tile(xc_ref[...], fc_ref[...], rc_ref[...], nc_ref[...], mods_ref[B], oc_ref)

        @pl.when(s > 0)
        def _():
            latent_tile(s - 1)
    else:
        latent_tile(s)


def _params(sem):
    return pltpu.CompilerParams(dimension_semantics=sem, vmem_limit_bytes=VMEM_LIMIT)


def _const_spec(shape):
    nd = len(shape)
    return pl.BlockSpec(shape, lambda *_: (0,) * nd, pipeline_mode=pl.Buffered(1))


def _smem_spec():
    return pl.BlockSpec(memory_space=pltpu.SMEM)


def _ada(cvec, w_ada, b_ada):
    tn = 2048
    return pl.pallas_call(
        _ada_kernel,
        grid=(DEPTH, 6 * D // tn),
        in_specs=[pl.BlockSpec((8, D), lambda i, j: (0, 0)),
                  pl.BlockSpec((1, D, tn), lambda i, j: (i, 0, j)),
                  pl.BlockSpec((1, 1, tn), lambda i, j: (i, 0, j))],
        out_specs=pl.BlockSpec((1, 8, tn), lambda i, j: (i, 0, j)),
        out_shape=jax.ShapeDtypeStruct((DEPTH, 8, 6 * D), F32),
        compiler_params=_params(("arbitrary", "arbitrary")),
        name="ada",
    )(cvec, w_ada, b_ada.reshape(DEPTH, 1, 6 * D))


def _mod_row(ctx, tm):
    per_batch = L // tm
    return (lambda tile: B) if ctx else (lambda tile: tile // per_batch)


def _layer_spec(shape, layer):
    nd = len(shape)
    return pl.BlockSpec((None,) + tuple(shape), lambda *_: (layer,) + (0,) * nd, pipeline_mode=pl.Buffered(1))


def _proj(x2d, mods, g, w_all, layer, sel_qk, sel_v, extra, *, ctx):
    rows = x2d.shape[0]
    tm = min(TM_PROJ, rows)
    per_batch = L // tm
    in_specs = [pl.BlockSpec((tm, D), lambda i: (i, 0)),
                _const_spec((8, 6, D)),
                _const_spec((1, D)),
                _layer_spec((D, W_IN), layer),
                _const_spec((RH * RD, RH * LANE)),
                _const_spec((RH * RD, RH * LANE))]
    args = [x2d, mods, g.reshape(1, D), w_all, sel_qk, sel_v]
    out_specs = [pl.BlockSpec((tm, FW), lambda i: (i, 0)), pl.BlockSpec((tm, NP), lambda i: (i, 0))]
    out_shape = [jax.ShapeDtypeStruct((rows, FW), BF16), jax.ShapeDtypeStruct((rows, NP), BF16)]
    if ctx:
        assert rows == tm == B * LC
        in_specs.append(_smem_spec())
        args.append(extra)
        out_specs.append(pl.BlockSpec((B, RH, 2, LANE, LANE), lambda i: (0, 0, 0, 0, 0)))
        out_shape.append(jax.ShapeDtypeStruct((B, RH, 2, LANE, LANE), F32))
    else:
        in_specs += [pl.BlockSpec((tm, LANE), lambda i: (i % per_batch, 0))] * 2
        args += list(extra)
    return pl.pallas_call(
        functools.partial(_proj_kernel, rope=not ctx, mod_row=_mod_row(ctx, tm)),
        grid=(rows // tm,),
        in_specs=in_specs,
        out_specs=out_specs,
        out_shape=out_shape,
        scratch_shapes=[pltpu.VMEM((D, 4 * RH * LANE), BF16)],
        compiler_params=_params(("arbitrary",)),
        name="proj_ctx" if ctx else "proj",
    )(*args)


def _fourier(f2d, wf, a1, m3, e):
    y = pl.pallas_call(
        _four1_kernel,
        grid=(B, DFT_N2 // F1_R),
        in_specs=[_const_spec((2 * DFT_N1 * F1_R, DFT_N1 * F1_R)),
                  pl.BlockSpec((1, DFT_N1, F1_R, FW), lambda b, j: (b, 0, j, 0))],
        out_specs=pl.BlockSpec((1, 2 * DFT_N1, F1_R, FW), lambda b, j: (b, 0, j, 0)),
        out_shape=jax.ShapeDtypeStruct((B, 2 * DFT_N1, DFT_N2, FW), BF16),
        compiler_params=_params(("arbitrary", "arbitrary")),
        name="four1",
    )(a1, f2d.reshape(B, DFT_N1, DFT_N2, FW))
    out = pl.pallas_call(
        _four3_kernel,
        grid=(DFT_N1 // F3_K,),
        in_specs=[pl.BlockSpec((F3_K, 2 * DFT_N2, 2 * DFT_N2), lambda k: (k, 0, 0)),
                  pl.BlockSpec((B, 2, F3_K, DFT_N2, FW), lambda k: (0, 0, k, 0, 0)),
                  _const_spec((2 * FW, FW)),
                  _const_spec((FW, FW))],
        out_specs=pl.BlockSpec((B, F3_K, DFT_N2, FW), lambda k: (0, k, 0, 0)),
        out_shape=jax.ShapeDtypeStruct((B, DFT_N1, DFT_N2, FW), F32),
        compiler_params=_params(("arbitrary",)),
        name="four3",
    )(m3, y.reshape(B, 2, DFT_N1, DFT_N2, FW), e, wf)
    return out.reshape(B * DFT_N1, DFT_N2, FW)


def _retention(p, s0, lg, gain):
    col = lambda c0: (lambda b, h: (b, c0 // LANE + h))
    return pl.pallas_call(
        _ret_kernel,
        grid=(B, RH),
        in_specs=[_smem_spec(),
                  pl.BlockSpec((L, LANE), col(C_RQ)),
                  pl.BlockSpec((L, LANE), col(C_RK)),
                  pl.BlockSpec((L, LANE), col(C_RV)),
                  pl.BlockSpec((L, LANE), col(C_RG)),
                  pl.BlockSpec((1, 1, 2, LANE, LANE), lambda b, h: (b, h, 0, 0, 0)),
                  pl.BlockSpec((1, LANE), lambda b, h: (0, h))],
        out_specs=pl.BlockSpec((L, LANE), lambda b, h: (b, h)),
        out_shape=jax.ShapeDtypeStruct((B * L, RH * LANE), BF16),
        scratch_shapes=[pltpu.VMEM((L, LANE), F32), pltpu.VMEM((LANE, LANE), F32), pltpu.VMEM((LANE, LANE), F32)],
        compiler_params=_params(("arbitrary", "arbitrary")),
        name="ret",
    )(lg, p, p, p, p, s0, gain)


def _ctx_mix(fc, pc, lg, gain, wf, c256, s256, e):
    return pl.pallas_call(
        _ctx_mix_kernel,
        grid=(B,),
        in_specs=[_smem_spec(),
                  pl.BlockSpec((LC, FW), lambda b: (b, 0)),
                  pl.BlockSpec((LC, NP), lambda b: (b, 0)),
                  _const_spec((1, RH * LANE)),
                  _const_spec((FW, FW)),
                  _const_spec((LC, LC)),
                  _const_spec((LC, LC)),
                  _const_spec((2 * FW, FW))],
        out_specs=[pl.BlockSpec((LC, FW), lambda b: (b, 0)),
                   pl.BlockSpec((LC, RH * LANE), lambda b: (b, 0)),
                   pl.BlockSpec((LC, NH * ND), lambda b: (b, 0))],
        out_shape=[jax.ShapeDtypeStruct((B * LC, FW), BF16),
                   jax.ShapeDtypeStruct((B * LC, RH * LANE), BF16),
                   jax.ShapeDtypeStruct((B * LC, NH * ND), BF16)],
        compiler_params=_params(("arbitrary",)),
        name="ctx_mix",
    )(lg, fc, pc, gain, wf, c256, s256, e)


def _na_bias(base, mask):
    nq, nk = NA_R * GRID_W, NA_KR * GRID_W
    return pl.pallas_call(
        _na_bias_kernel,
        grid=(base.shape[0],),
        in_specs=[pl.BlockSpec((1, NA_DR, LANE), lambda h: (h, 0, 0)),
                  _const_spec((GRID_W, LANE))],
        out_specs=pl.BlockSpec((1, 3, nq, nk), lambda h: (h, 0, 0, 0)),
        out_shape=jax.ShapeDtypeStruct((base.shape[0], 3, nq, nk), F32),
        scratch_shapes=[pltpu.VMEM((NA_DR, GRID_W, LANE), F32)],
        compiler_params=_params(("arbitrary",)),
        name="na_bias",
    )(base, mask)


def _natten(p, pc, bias, layer):
    nq, nk = NA_R * GRID_W, NA_KR * GRID_W
    return pl.pallas_call(
        _natten_kernel,
        grid=(B, NPAIR),
        in_specs=[pl.BlockSpec((L, LANE), lambda b, pr: (b, C_NQ // LANE + pr)),
                  pl.BlockSpec((L, LANE), lambda b, pr: (b, C_NK // LANE + pr)),
                  pl.BlockSpec((L, LANE), lambda b, pr: (b, C_NV // LANE + pr)),
                  pl.BlockSpec((LC, LANE), lambda b, pr: (b, C_NK // LANE + pr)),
                  pl.BlockSpec((LC, LANE), lambda b, pr: (b, C_NV // LANE + pr)),
                  pl.BlockSpec((2, 3, nq, nk), lambda b, pr: (NPAIR * layer + pr, 0, 0, 0))],
        out_specs=pl.BlockSpec((L, LANE), lambda b, pr: (b, pr)),
        out_shape=jax.ShapeDtypeStruct((B * L, NH * ND), BF16),
        scratch_shapes=[pltpu.VMEM((2, nq, nk + LC), BF16)] * 2,
        compiler_params=_params(("arbitrary", "arbitrary")),
        name="natten",
    )(p, p, p, pc, pc, bias)


def _ffn(x2d, four, ret, na, ctx_acts, mods, g, wo_all, sel_v_t, w1_all, w3_all, w2_all, layer, g_final):
    rows = x2d.shape[0]
    final = g_final is not None
    ctx_tile = ctx_acts is not None
    off = 1 if ctx_tile else 0
    per_batch = L // TM
    tile = lambda i: jnp.maximum(i - off, 0)
    row = lambda w: pl.BlockSpec((TM, w), lambda i: (tile(i), 0))
    four_spec = pl.BlockSpec((DFT_N1, TM // DFT_N1, FW), lambda i: (tile(i) // per_batch, tile(i) % per_batch, 0))
    in_specs = [row(D), four_spec, row(RH * LANE), row(NH * ND)]
    args = [x2d, four, ret, na]
    if ctx_tile:
        assert all(a.shape[0] == TM for a in ctx_acts)
        in_specs += [_const_spec(a.shape) for a in ctx_acts]
        args += list(ctx_acts)
    in_specs += [_const_spec((8, 6, D)), _const_spec((1, D)),
                 _layer_spec((D, D), layer), _const_spec((RH * LANE, RH * RD)),
                 _layer_spec((D, DFF), layer), _layer_spec((D, DFF), layer), _layer_spec((DFF, D), layer)]
    args += [mods, g.reshape(1, D), wo_all, sel_v_t, w1_all, w3_all, w2_all]
    if final:
        in_specs.append(_const_spec((1, D)))
        args.append(g_final.reshape(1, D))
    out_specs = [row(D)]
    out_shape = [jax.ShapeDtypeStruct((rows, D), F32)]
    if ctx_tile:
        out_specs.append(pl.BlockSpec((TM, D), lambda i: (0, 0)))
        out_shape.append(jax.ShapeDtypeStruct((TM, D), F32))
    outs = pl.pallas_call(
        functools.partial(_ffn_kernel, final=final, ctx_tile=ctx_tile, mod_row=_mod_row(False, TM)),
        grid=(rows // TM + off,),
        in_specs=in_specs,
        out_specs=out_specs,
        out_shape=out_shape,
        scratch_shapes=[pltpu.VMEM((TM, DFF), BF16), pltpu.VMEM((RH * LANE, D), BF16)],
        compiler_params=_params(("arbitrary",)),
        name="ffn_final" if final else "ffn",
    )(*args)
    return outs if ctx_tile else outs[0]


def _take_padded(a, src, axis):
    pieces = []
    i, n = 0, len(src)
    while i < n:
        j = i + 1
        if src[i] < 0:
            while j < n and src[j] < 0:
                j += 1
            shape = list(a.shape)
            shape[axis] = j - i
            pieces.append(jnp.zeros(shape, a.dtype))
        else:
            while j < n and src[j] == src[j - 1] + 1:
                j += 1
            pieces.append(lax.slice_in_dim(a, int(src[i]), int(src[j - 1]) + 1, axis=axis))
        i = j
    return jnp.concatenate(pieces, axis=axis)


def kernel(x, c, ctx, c_ctx, w_ada, b_ada, g_mix, w_in, ret_decay_logit, ret_norm_g, w_four,
           na_rpb, w_out, g_ffn, w1, w3, w2, g_final):
    src_g = _gain_sources()
    rope_tabs = tuple(jnp.asarray(t) for t in _rope_tables())
    a1, m3, e, c256, s256 = (jnp.asarray(t).astype(BF16) for t in _dft_tables())
    sel_qk, sel_v = (jnp.asarray(t).astype(BF16) for t in _head_selectors())
    sel_v_t = sel_v.T
    tz_src, tz_mask = _na_toeplitz_tables()
    col_scale = np.ones((1, 1, W_IN), np.float32)
    col_scale[..., W_NQ:W_NK] = LOG2E * ND ** -0.5
    w_in_b = (w_in * jnp.asarray(col_scale)).astype(BF16)
    w_out_b = w_out.astype(BF16)
    w1b, w3b, w2b = w1.astype(BF16), w3.astype(BF16), w2.astype(BF16)

    tz_base = jnp.pad(_take_padded(na_rpb.astype(F32).reshape(DEPTH * NH, 15, 31), tz_src, 2), ((0, 0), (0, 1), (0, 0)))
    na_bias = _na_bias(tz_base, jnp.asarray(tz_mask))

    cvec = jnp.zeros((8, D), F32).at[0:B].set(c).at[B].set(c_ctx)
    mods_all = _ada(cvec, w_ada, b_ada).reshape(DEPTH, 8, 6, D)

    xl = x.reshape(B * L, D)
    xc = ctx.reshape(B * LC, D)
    for i in range(DEPTH):
        last = i == DEPTH - 1
        mods = mods_all[i]
        gain = _take_padded(ret_norm_g[i], src_g, 0).reshape(1, RH * LANE)
        wfour = w_four[i].astype(BF16)
        lg = jax.nn.log_sigmoid(ret_decay_logit[i].astype(F32))

        f_l, p_l = _proj(xl, mods, g_mix[i], w_in_b, i, sel_qk, sel_v, rope_tabs, ctx=False)
        f_c, p_c, s0 = _proj(xc, mods, g_mix[i], w_in_b, i, sel_qk, sel_v, lg, ctx=True)
        four_l = _fourier(f_l, wfour, a1, m3, e)
        ret_l = _retention(p_l, s0, lg, gain)
        na_l = _natten(p_l, p_c, na_bias, i)
        if last:
            xl = _ffn(xl, four_l, ret_l, na_l, None, mods, g_ffn[i], w_out_b, sel_v_t, w1b, w3b, w2b, i, g_final)
        else:
            ctx_acts = (xc,) + tuple(_ctx_mix(f_c, p_c, lg, gain, wfour, c256, s256, e))
            xl, xc = _ffn(xl, four_l, ret_l, na_l, ctx_acts, mods, g_ffn[i], w_out_b, sel_v_t, w1b, w3b, w2b, i, None)
    return xl.reshape(B, L, D)
```

```python
import functools

import numpy as np
import jax
import jax.numpy as jnp
from jax import lax
from jax.experimental import pallas as pl
from jax.experimental.pallas import tpu as pltpu

F32 = jnp.float32
BF16 = jnp.bfloat16

D = 1024
B = 2
L = 8192
LC = 256
DEPTH = 2
GRID_W = 64
GRID_H = L // GRID_W
FW = 256
FGW = 64
RH = 4
RD = 96
NH = 6
ND = 64
NPAIR = NH // 2
DFF = 2816
EPS = 1e-6
NEG = -1e30
LOG2E = float(np.log2(np.e))
LANE = 128

C_RQ, C_RK, C_RV, C_RG = 0, 512, 1024, 1536
C_NQ, C_NK, C_NV = 2048, 2432, 2816
NP = 3200
W_RQ, W_RK, W_RV, W_RG = 256, 640, 1024, 1408
W_NQ, W_NK, W_NV = 1792, 2176, 2560
W_IN = 2944

TM = 512
TM_PROJ = 1024
RET_C = 256
NA_R = 4
NA_KR = NA_R + 8
NA_NB = GRID_H // NA_R
DFT_N1 = 64
DFT_N2 = 128
F1_R = 16
F3_K = 8
VMEM_LIMIT = 56 * 1024 * 1024


def _qk_lane_dims():
    m = np.full((LANE,), -1, np.int64)
    m[0:24] = np.arange(0, 24)
    m[24:48] = np.arange(48, 72)
    m[64:88] = np.arange(24, 48)
    m[88:112] = np.arange(72, 96)
    return m


def _v_lane_dims():
    m = np.full((LANE,), -1, np.int64)
    m[:RD] = np.arange(RD)
    return m


def _head_selectors():
    sels = []
    for lanes in (_qk_lane_dims(), _v_lane_dims()):
        sel = np.zeros((RH * RD, RH * LANE), np.float32)
        ok = np.nonzero(lanes >= 0)[0]
        for h in range(RH):
            sel[RD * h + lanes[ok], LANE * h + ok] = 1.0
        sels.append(sel)
    return sels


def _gain_sources():
    src = np.full((RH * LANE,), -1, np.int64)
    for h in range(RH):
        src[LANE * h: LANE * h + RD] = RD * h + np.arange(RD)
    return src


def _rope_tables():
    pos = np.arange(L)
    prow, pcol = pos // GRID_W, pos % GRID_W
    half = RD // 4
    inv = 10000.0 ** (-np.arange(half, dtype=np.float64) / half)
    ar = prow[:, None] * inv[None, :]
    ac = pcol[:, None] * inv[None, :]
    cos = np.ones((L, LANE), np.float64)
    sin = np.zeros((L, LANE), np.float64)
    for off, sign in ((0, -1.0), (64, 1.0)):
        cos[:, off:off + 24] = np.cos(ar)
        cos[:, off + 24:off + 48] = np.cos(ac)
        sin[:, off:off + 24] = sign * np.sin(ar)
        sin[:, off + 24:off + 48] = sign * np.sin(ac)
    return cos.astype(np.float32), sin.astype(np.float32)


def _dft_tables():
    k1 = np.arange(DFT_N1)
    a = 2 * np.pi * ((k1[:, None] * k1[None, :]) % DFT_N1) / DFT_N1
    a1 = np.kron(np.concatenate([np.cos(a), -np.sin(a)], axis=0), np.eye(F1_R))
    k2 = np.arange(DFT_N2)
    l2 = np.arange(DFT_N2)
    kk = k1[:, None, None] + DFT_N1 * k2[None, :, None]
    ang = 2 * np.pi * ((kk * l2[None, None, :]) % L) / L
    ct, st = np.cos(ang) / np.sqrt(L), np.sin(ang) / np.sqrt(L)
    m3 = np.concatenate([np.concatenate([ct, st], axis=2), np.concatenate([st, -ct], axis=2)], axis=1)
    c = np.arange(FW)
    same = (c[:, None] // FGW) == (c[None, :] // FGW)
    ac = 2 * np.pi * (((c[:, None] % FGW) * (c[None, :] % FGW)) % FGW) / FGW
    e = np.concatenate([np.where(same, np.cos(ac), 0.0), -np.where(same, np.sin(ac), 0.0)], axis=0) / 8.0
    p = np.arange(LC)
    ap = 2 * np.pi * ((p[:, None] * p[None, :]) % LC) / LC
    return tuple(t.astype(np.float32) for t in (a1, m3, e, np.cos(ap) / 16.0, np.sin(ap) / 16.0))


NA_DR = 16


def _na_toeplitz_tables():
    src = np.full((LANE,), -1, np.int64)
    src[0:16] = np.arange(0, 16) + 15
    src[113:128] = np.arange(113, 128) - 113
    src[49:80] = np.arange(49, 80) - 49
    qc = np.arange(GRID_W)[:, None]
    kc = np.arange(LANE)[None, :] % GRID_W
    ws = np.clip(qc - 8, 0, GRID_W - 16)
    mask = np.where((kc >= ws) & (kc < ws + 16), 0.0, NEG).astype(np.float32)
    return src, mask


def _dot(a, b):
    return jnp.dot(a, b, preferred_element_type=F32)


def _dot_nt(a, b):
    return lax.dot_general(a, b, (((1,), (1,)), ((), ())), preferred_element_type=F32)


def _silu(x):
    return x * jax.nn.sigmoid(x)


def _rmsnorm(x, g):
    return x * lax.rsqrt(jnp.mean(x * x, axis=-1, keepdims=True) + EPS) * g


def _ada_kernel(c_ref, w_ref, b_ref, o_ref):
    s = _silu(c_ref[...]).astype(BF16)
    o_ref[0] = _dot(s, w_ref[0].astype(BF16)) + b_ref[0]


def _proj_kernel(x_ref, xc_ref, mods_ref, g_ref, w_ref, pqk_ref, pv_ref, cos_ref, sin_ref, lg_ref,
                 f_ref, o_ref, fc_ref, oc_ref, s0_ref, wret_ref, *, mod_row):
    s = pl.program_id(0)

    def project(x, mod, rope, f_out, o_out):
        hb = (_rmsnorm(x, g_ref[...]) * (1.0 + mod[1:2, :]) + mod[0:1, :]).astype(BF16)
        f_out[...] = _dot(hb, w_ref[:, :FW]).astype(BF16)
        k_scale = RD ** -0.5
        for c0, scale in ((C_RQ, None), (C_RK, k_scale)):
            t4 = _dot(hb, wret_ref[:, c0:c0 + RH * LANE])
            for hh in range(RH):
                t = t4[:, LANE * hh:LANE * (hh + 1)]
                if rope:
                    t = t * cos_ref[...] + pltpu.roll(t, 64, 1) * sin_ref[...]
                if scale is not None:
                    t = t * scale
                o_out[:, c0 + LANE * hh:c0 + LANE * (hh + 1)] = t.astype(BF16)
        o_out[:, C_RV:C_RG] = _dot(hb, wret_ref[:, C_RV:C_RG]).astype(BF16)
        o_out[:, C_RG:C_NQ] = _silu(_dot(hb, wret_ref[:, C_RG:C_NQ])).astype(BF16)
        o_out[:, C_NQ:] = _dot(hb, w_ref[:, W_NQ:]).astype(BF16)

    @pl.when(s == 0)
    def _():
        for slot, (src0, sel_ref) in enumerate(((W_RQ, pqk_ref), (W_RK, pqk_ref), (W_RV, pv_ref), (W_RG, pv_ref))):
            wret_ref[:, RH * LANE * slot:RH * LANE * (slot + 1)] = _dot(
                w_ref[:, src0:src0 + RH * RD], sel_ref[...]).astype(BF16)
        project(xc_ref[...], mods_ref[B], False, fc_ref, oc_ref)
        idx = lax.broadcasted_iota(jnp.int32, (LC, LANE), 0).astype(F32)
        for b in range(B):
            for h in range(RH):
                k = oc_ref[LC * b:LC * (b + 1), C_RK + LANE * h:C_RK + LANE * (h + 1)].astype(F32)
                v = oc_ref[LC * b:LC * (b + 1), C_RV + LANE * h:C_RV + LANE * (h + 1)]
                wf = jnp.exp((LC - 1.0 - idx) * lg_ref[0, h])
                wb = jnp.exp(idx * lg_ref[1, h])
                s0_ref[b, h, 0] = _dot((k * wf).T.astype(BF16), v)
                s0_ref[b, h, 1] = _dot((k * wb).T.astype(BF16), v)

    @pl.when(s > 0)
    def _():
        project(x_ref[...], mods_ref[mod_row(s - 1)], True, f_ref, o_ref)


def _four1_kernel(a_ref, x_ref, y_ref):
    x = x_ref[0].reshape(DFT_N1 * F1_R, FW)
    y = _dot(a_ref[...], x).astype(BF16)
    y_ref[0] = y.reshape(2 * DFT_N1, F1_R, FW)


def _four3_kernel(m_ref, y_ref, e_ref, wf_ref, o_ref):
    ew = _dot(e_ref[...], wf_ref[...]).astype(BF16)
    for b in range(B):
        for kk in range(F3_K):
            y = jnp.concatenate([y_ref[b, 0, kk], y_ref[b, 1, kk]], axis=0)
            z = _dot(m_ref[kk], y).astype(BF16)
            o_ref[b, kk] = _dot(jnp.concatenate([z[:DFT_N2], z[DFT_N2:]], axis=1), ew)


def _decay_matrix(n, lgf, lgb):
    ii = lax.broadcasted_iota(jnp.int32, (n, n), 0)
    jj = lax.broadcasted_iota(jnp.int32, (n, n), 1)
    diff = (ii - jj).astype(F32)
    fwd = jnp.where(diff >= 0, jnp.exp(jnp.maximum(diff, 0.0) * lgf), 0.0)
    bwd = jnp.where(diff <= 0, jnp.exp(jnp.maximum(-diff, 0.0) * lgb), 0.0)
    return fwd + bwd


def _ret_readout(o, gate, gain):
    ms = jnp.sum(o * o, axis=-1, keepdims=True) * (1.0 / RD)
    return (o * lax.rsqrt(ms + EPS) * gain * gate.astype(F32)).astype(BF16)


def _ret_kernel(lg_ref, q_ref, k_ref, v_ref, g_ref, s0_ref, gain_ref, o_ref, acc_ref, stf_ref, stb_ref):
    h = pl.program_id(1)
    lgf = lg_ref[0, h]
    lgb = lg_ref[1, h]
    c = RET_C
    n = L // c
    dmat = _decay_matrix(c, lgf, lgb)
    idx = lax.broadcasted_iota(jnp.int32, (c, LANE), 0).astype(F32)
    qdf = jnp.exp((idx + 1.0) * lgf)
    kdf = jnp.exp((c - 1.0 - idx) * lgf)
    qdb = jnp.exp((c - idx) * lgb)
    kdb = jnp.exp(idx * lgb)
    cdf = jnp.exp(jnp.full((1, LANE), float(c), F32) * lgf)
    cdb = jnp.exp(jnp.full((1, LANE), float(c), F32) * lgb)
    gain = gain_ref[...]

    dirs = ((stf_ref, qdf, kdf, cdf), (stb_ref, qdb, kdb, cdb))
    stf_ref[...] = s0_ref[0, 0, 0]
    stb_ref[...] = s0_ref[0, 0, 1]

    def visit(r0, direction, second):
        st_ref, qdec, kdec, cdec = dirs[direction]
        rows = pl.ds(pl.multiple_of(r0, c), c)
        q = q_ref[rows, :]
        k = k_ref[rows, :]
        v = v_ref[rows, :]
        cross = _dot(q, st_ref[...].astype(BF16)) * qdec
        if second:
            o_ref[rows, :] = _ret_readout(acc_ref[rows, :] + cross, g_ref[rows, :], gain)
        else:
            s = _dot_nt(q, k) * dmat
            acc_ref[rows, :] = _dot(s.astype(BF16), v) + cross
        kt = (k.astype(F32) * kdec).T.astype(BF16)
        st_ref[...] = st_ref[...] * cdec + _dot(kt, v)

    def body(second):
        def step(i, carry):
            visit(i * c, 0, second)
            visit((n - 1 - i) * c, 1, second)
            return carry
        return step

    lax.fori_loop(0, n // 2, body(False), 0, unroll=4)
    lax.fori_loop(n // 2, n, body(True), 0, unroll=4)


def _pair_attention(q, score_fn, value_fn):
    lane = lax.broadcasted_iota(jnp.int32, q.shape, 1)
    outs = []
    for hh in range(2):
        sel = (lane >= ND * hh) & (lane < ND * (hh + 1))
        qm = jnp.where(sel, q, jnp.zeros_like(q))
        scores = score_fn(qm, hh)
        m = functools.reduce(jnp.maximum, [jnp.max(s, axis=-1, keepdims=True) for s in scores])
        ps = [jnp.exp2(s - m) for s in scores]
        den = functools.reduce(jnp.add, [jnp.sum(p, axis=-1, keepdims=True) for p in ps])
        outs.append(value_fn([p.astype(BF16) for p in ps]) / den)
    return jnp.where(lane < ND, outs[0], outs[1])


def _ctx_mix_kernel(lg_ref, f_ref, p_ref, gain_ref, wf_ref, c_ref, s_ref, e_ref, fo_ref, ro_ref, no_ref):
    f = f_ref[...]
    pr = _dot(c_ref[...], f)
    qi = _dot(s_ref[...], f)
    fr = _dot(pr.astype(BF16), e_ref[:FW, :]) + _dot(qi.astype(BF16), e_ref[FW:, :])
    fo_ref[...] = _dot(fr.astype(BF16), wf_ref[...]).astype(BF16)
    for h in range(RH):
        sl = lambda c0: slice(c0 + LANE * h, c0 + LANE * (h + 1))
        dmat = _decay_matrix(LC, lg_ref[0, h], lg_ref[1, h])
        s = _dot_nt(p_ref[:, sl(C_RQ)], p_ref[:, sl(C_RK)]) * dmat
        o = _dot(s.astype(BF16), p_ref[:, sl(C_RV)])
        ro_ref[:, LANE * h:LANE * (h + 1)] = _ret_readout(o, p_ref[:, sl(C_RG)], gain_ref[:, LANE * h:LANE * (h + 1)])
    for pair in range(NPAIR):
        sl = lambda c0: slice(c0 + LANE * pair, c0 + LANE * (pair + 1))
        k = p_ref[:, sl(C_NK)]
        v = p_ref[:, sl(C_NV)]
        out = _pair_attention(p_ref[:, sl(C_NQ)], lambda qm, hh: [_dot_nt(qm, k)], lambda ps: _dot(ps[0], v))
        no_ref[:, LANE * pair:LANE * (pair + 1)] = out.astype(BF16)


def _na_key_start(r0):
    return min(max(r0 - 4, 0), GRID_H - NA_KR)


def _na_bias_kernel(base_ref, mask_ref, o_ref, tz_ref):
    for dr in range(NA_DR - 1):
        t = jnp.broadcast_to(base_ref[0, dr:dr + 1, :], (GRID_W, LANE))
        t = pltpu.roll(t, 0, 1, stride=1, stride_axis=0)
        tz_ref[dr] = t * LOG2E + mask_ref[...]
    tz_ref[NA_DR - 1] = jnp.full((GRID_W, LANE), NEG, F32)
    low_half = lax.broadcasted_iota(jnp.int32, (GRID_W, LANE), 1) < GRID_W
    for var, r0 in enumerate((0, 2 * NA_R, GRID_H - NA_R)):
        ks = _na_key_start(r0)
        for rl in range(NA_R):
            r = r0 + rl
            rs = min(max(r - 4, 0), GRID_H - 8)
            slot = [kr - r + 7 if rs <= kr < rs + 8 else NA_DR - 1 for kr in range(ks, ks + NA_KR)]
            for m in range(NA_KR // 2):
                o_ref[0, var, GRID_W * rl:GRID_W * (rl + 1), LANE * m:LANE * (m + 1)] = jnp.where(
                    low_half, tz_ref[slot[2 * m]], tz_ref[slot[2 * m + 1]])


def _natten_kernel(q_ref, k_ref, v_ref, kc_ref, vc_ref, bias_ref, o_ref, pa_ref, pb_ref):
    nq = NA_R * GRID_W
    nk = NA_KR * GRID_W
    kc = kc_ref[...]
    vc = vc_ref[...]
    lane = lax.broadcasted_iota(jnp.int32, (nq, LANE), 1)

    def rows(i):
        ks = jnp.clip(NA_R * i - 4, 0, GRID_H - NA_KR)
        return pl.ds(pl.multiple_of(ks * GRID_W, nq), nk), pl.ds(pl.multiple_of(i * nq, nq), nq)

    def probabilities(i, p_ref):
        i = jnp.minimum(i, NA_NB - 1)
        var = jnp.where(i == 0, 0, jnp.where(i == NA_NB - 1, 2, 1))
        krows, qrows = rows(i)
        q = q_ref[qrows, :]
        kw = k_ref[krows, :]
        for hh in range(2):
            qm = jnp.where((lane >= ND * hh) & (lane < ND * (hh + 1)), q, jnp.zeros_like(q))
            s_win = _dot_nt(qm, kw) + bias_ref[hh, var]
            s_ctx = _dot_nt(qm, kc)
            m = jnp.maximum(jnp.max(s_win, axis=-1, keepdims=True), jnp.max(s_ctx, axis=-1, keepdims=True))
            p_ref[hh, :, :nk] = jnp.exp2(s_win - m).astype(BF16)
            p_ref[hh, :, nk:] = jnp.exp2(s_ctx - m).astype(BF16)

    def values(i, p_ref):
        krows, qrows = rows(i)
        vals = jnp.concatenate([v_ref[krows, :], vc], axis=0)
        vals = jnp.concatenate([vals, jnp.ones_like(vals)], axis=1)
        o2 = [_dot(p_ref[hh], vals) for hh in range(2)]
        outs = [o[:, :LANE] / o[:, LANE:] for o in o2]
        o_ref[qrows, :] = jnp.where(lane < ND, outs[0], outs[1]).astype(BF16)

    probabilities(0, pa_ref)

    def step(j, carry):
        values(2 * j, pa_ref)
        probabilities(2 * j + 1, pb_ref)
        values(2 * j + 1, pb_ref)
        probabilities(2 * j + 2, pa_ref)
        return carry

    lax.fori_loop(0, NA_NB // 2, step, 0)


def _ffn_kernel(*refs, final, ctx_tile, mod_row):
    refs = list(refs)
    x_ref, f_ref, r_ref, n_ref = refs[:4]
    del refs[:4]
    if ctx_tile:
        xc_ref, fc_ref, rc_ref, nc_ref = refs[:4]
        del refs[:4]
    mods_ref, g_ref, wo_ref, sel_ref, w1_ref, w3_ref, w2_ref = refs[:7]
    del refs[:7]
    gf_ref = refs.pop(0) if final else None
    o_ref = refs.pop(0)
    oc_ref = refs.pop(0) if ctx_tile else None
    acc_ref, wor_ref = refs
    s = pl.program_id(0)

    @pl.when(s == 0)
    def _():
        wor_ref[...] = _dot(sel_ref[...], wo_ref[FW:FW + RH * RD, :]).astype(BF16)

    def tile(x, f, r, n, mod, out_ref):
        y = _dot(f.astype(BF16), wo_ref[:FW, :]) + _dot(r, wor_ref[...]) + _dot(n, wo_ref[FW + RH * RD:, :])
        x1 = x + mod[2:3, :] * y
        hb = (_rmsnorm(x1, g_ref[...]) * (1.0 + mod[4:5, :]) + mod[3:4, :]).astype(BF16)
        fc = 256
        for c in range(DFF // fc):
            a = _dot(hb, w1_ref[:, c * fc:(c + 1) * fc])
            b = _dot(hb, w3_ref[:, c * fc:(c + 1) * fc])
            acc_ref[:, c * fc:(c + 1) * fc] = (_silu(a) * b).astype(BF16)
        x2 = x1 + mod[5:6, :] * _dot(acc_ref[...], w2_ref[...])
        if final:
            x2 = _rmsnorm(x2, gf_ref[...])
        out_ref[...] = x2

    def latent_tile(t):
        f = jnp.concatenate([f_ref[:, r, :] for r in range(f_ref.shape[1])], axis=0)
        tile(x_ref[...], f, r_ref[...], n_ref[...], mods_ref[mod_row(t)], o_ref)

    if ctx_tile:
        @pl.when(s == 0)
        def _():
            tile(xc_ref[...], fc_ref[...], rc_ref[...], nc_ref[...], mods_ref[B], oc_ref)

        @pl.when(s > 0)
        def _():
            latent_tile(s - 1)
    else:
        latent_tile(s)


def _params(sem):
    return pltpu.CompilerParams(dimension_semantics=sem, vmem_limit_bytes=VMEM_LIMIT)


def _const_spec(shape):
    nd = len(shape)
    return pl.BlockSpec(shape, lambda *_: (0,) * nd, pipeline_mode=pl.Buffered(1))


def _smem_spec():
    return pl.BlockSpec(memory_space=pltpu.SMEM)


def _ada(cvec, w_ada, b_ada):
    tn = 2048
    return pl.pallas_call(
        _ada_kernel,
        grid=(DEPTH, 6 * D // tn),
        in_specs=[pl.BlockSpec((8, D), lambda i, j: (0, 0)),
                  pl.BlockSpec((1, D, tn), lambda i, j: (i, 0, j)),
                  pl.BlockSpec((1, 1, tn), lambda i, j: (i, 0, j))],
        out_specs=pl.BlockSpec((1, 8, tn), lambda i, j: (i, 0, j)),
        out_shape=jax.ShapeDtypeStruct((DEPTH, 8, 6 * D), F32),
        compiler_params=_params(("arbitrary", "arbitrary")),
        name="ada",
    )(cvec, w_ada, b_ada.reshape(DEPTH, 1, 6 * D))


def _mod_row(tm):
    per_batch = L // tm
    return lambda tile: tile // per_batch


def _layer_spec(shape, layer):
    nd = len(shape)
    return pl.BlockSpec((None,) + tuple(shape), lambda *_: (layer,) + (0,) * nd, pipeline_mode=pl.Buffered(1))


def _proj(x2d, xc2d, mods, g, w_all, layer, sel_qk, sel_v, rope_tabs, lg):
    rows, crows = x2d.shape[0], xc2d.shape[0]
    assert crows == B * LC
    tm = TM_PROJ
    per_batch = L // tm
    tile = lambda i: jnp.maximum(i - 1, 0)
    return pl.pallas_call(
        functools.partial(_proj_kernel, mod_row=_mod_row(tm)),
        grid=(rows // tm + 1,),
        in_specs=[pl.BlockSpec((tm, D), lambda i: (tile(i), 0)),
                  _const_spec((crows, D)),
                  _const_spec((8, 6, D)),
                  _const_spec((1, D)),
                  _layer_spec((D, W_IN), layer),
                  _const_spec((RH * RD, RH * LANE)),
                  _const_spec((RH * RD, RH * LANE)),
                  pl.BlockSpec((tm, LANE), lambda i: (tile(i) % per_batch, 0)),
                  pl.BlockSpec((tm, LANE), lambda i: (tile(i) % per_batch, 0)),
                  _smem_spec()],
        out_specs=[pl.BlockSpec((tm, FW), lambda i: (tile(i), 0)),
                   pl.BlockSpec((tm, NP), lambda i: (tile(i), 0)),
                   pl.BlockSpec((crows, FW), lambda i: (0, 0)),
                   pl.BlockSpec((crows, NP), lambda i: (0, 0)),
                   pl.BlockSpec((B, RH, 2, LANE, LANE), lambda i: (0, 0, 0, 0, 0))],
        out_shape=[jax.ShapeDtypeStruct((rows, FW), BF16), jax.ShapeDtypeStruct((rows, NP), BF16),
                   jax.ShapeDtypeStruct((crows, FW), BF16), jax.ShapeDtypeStruct((crows, NP), BF16),
                   jax.ShapeDtypeStruct((B, RH, 2, LANE, LANE), F32)],
        scratch_shapes=[pltpu.VMEM((D, 4 * RH * LANE), BF16)],
        compiler_params=_params(("arbitrary",)),
        name="proj",
    )(x2d, xc2d, mods, g.reshape(1, D), w_all, sel_qk, sel_v, *rope_tabs, lg)


def _fourier(f2d, wf, a1, m3, e):
    y = pl.pallas_call(
        _four1_kernel,
        grid=(B, DFT_N2 // F1_R),
        in_specs=[_const_spec((2 * DFT_N1 * F1_R, DFT_N1 * F1_R)),
                  pl.BlockSpec((1, DFT_N1, F1_R, FW), lambda b, j: (b, 0, j, 0))],
        out_specs=pl.BlockSpec((1, 2 * DFT_N1, F1_R, FW), lambda b, j: (b, 0, j, 0)),
        out_shape=jax.ShapeDtypeStruct((B, 2 * DFT_N1, DFT_N2, FW), BF16),
        compiler_params=_params(("arbitrary", "arbitrary")),
        name="four1",
    )(a1, f2d.reshape(B, DFT_N1, DFT_N2, FW))
    out = pl.pallas_call(
        _four3_kernel,
        grid=(DFT_N1 // F3_K,),
        in_specs=[pl.BlockSpec((F3_K, 2 * DFT_N2, 2 * DFT_N2), lambda k: (k, 0, 0)),
                  pl.BlockSpec((B, 2, F3_K, DFT_N2, FW), lambda k: (0, 0, k, 0, 0)),
                  _const_spec((2 * FW, FW)),
                  _const_spec((FW, FW))],
        out_specs=pl.BlockSpec((B, F3_K, DFT_N2, FW), lambda k: (0, k, 0, 0)),
        out_shape=jax.ShapeDtypeStruct((B, DFT_N1, DFT_N2, FW), F32),
        compiler_params=_params(("arbitrary",)),
        name="four3",
    )(m3, y.reshape(B, 2, DFT_N1, DFT_N2, FW), e, wf)
    return out.reshape(B * DFT_N1, DFT_N2, FW)


def _retention(p, s0, lg, gain):
    col = lambda c0: (lambda b, h: (b, c0 // LANE + h))
    return pl.pallas_call(
        _ret_kernel,
        grid=(B, RH),
        in_specs=[_smem_spec(),
                  pl.BlockSpec((L, LANE), col(C_RQ)),
                  pl.BlockSpec((L, LANE), col(C_RK)),
                  pl.BlockSpec((L, LANE), col(C_RV)),
                  pl.BlockSpec((L, LANE), col(C_RG)),
                  pl.BlockSpec((1, 1, 2, LANE, LANE), lambda b, h: (b, h, 0, 0, 0)),
                  pl.BlockSpec((1, LANE), lambda b, h: (0, h))],
        out_specs=pl.BlockSpec((L, LANE), lambda b, h: (b, h)),
        out_shape=jax.ShapeDtypeStruct((B * L, RH * LANE), BF16),
        scratch_shapes=[pltpu.VMEM((L, LANE), F32), pltpu.VMEM((LANE, LANE), F32), pltpu.VMEM((LANE, LANE), F32)],
        compiler_params=_params(("arbitrary", "arbitrary")),
        name="ret",
    )(lg, p, p, p, p, s0, gain)


def _ctx_mix(fc, pc, lg, gain, wf, c256, s256, e):
    return pl.pallas_call(
        _ctx_mix_kernel,
        grid=(B,),
        in_specs=[_smem_spec(),
                  pl.BlockSpec((LC, FW), lambda b: (b, 0)),
                  pl.BlockSpec((LC, NP), lambda b: (b, 0)),
                  _const_spec((1, RH * LANE)),
                  _const_spec((FW, FW)),
                  _const_spec((LC, LC)),
                  _const_spec((LC, LC)),
                  _const_spec((2 * FW, FW))],
        out_specs=[pl.BlockSpec((LC, FW), lambda b: (b, 0)),
                   pl.BlockSpec((LC, RH * LANE), lambda b: (b, 0)),
                   pl.BlockSpec((LC, NH * ND), lambda b: (b, 0))],
        out_shape=[jax.ShapeDtypeStruct((B * LC, FW), BF16),
                   jax.ShapeDtypeStruct((B * LC, RH * LANE), BF16),
                   jax.ShapeDtypeStruct((B * LC, NH * ND), BF16)],
        compiler_params=_params(("arbitrary",)),
        name="ctx_mix",
    )(lg, fc, pc, gain, wf, c256, s256, e)


def _na_bias(base, mask):
    nq, nk = NA_R * GRID_W, NA_KR * GRID_W
    return pl.pallas_call(
        _na_bias_kernel,
        grid=(base.shape[0],),
        in_specs=[pl.BlockSpec((1, NA_DR, LANE), lambda h: (h, 0, 0)),
                  _const_spec((GRID_W, LANE))],
        out_specs=pl.BlockSpec((1, 3, nq, nk), lambda h: (h, 0, 0, 0)),
        out_shape=jax.ShapeDtypeStruct((base.shape[0], 3, nq, nk), F32),
        scratch_shapes=[pltpu.VMEM((NA_DR, GRID_W, LANE), F32)],
        compiler_params=_params(("arbitrary",)),
        name="na_bias",
    )(base, mask)


def _natten(p, pc, bias, layer):
    nq, nk = NA_R * GRID_W, NA_KR * GRID_W
    return pl.pallas_call(
        _natten_kernel,
        grid=(B, NPAIR),
        in_specs=[pl.BlockSpec((L, LANE), lambda b, pr: (b, C_NQ // LANE + pr)),
                  pl.BlockSpec((L, LANE), lambda b, pr: (b, C_NK // LANE + pr)),
                  pl.BlockSpec((L, LANE), lambda b, pr: (b, C_NV // LANE + pr)),
                  pl.BlockSpec((LC, LANE), lambda b, pr: (b, C_NK // LANE + pr)),
                  pl.BlockSpec((LC, LANE), lambda b, pr: (b, C_NV // LANE + pr)),
                  pl.BlockSpec((2, 3, nq, nk), lambda b, pr: (NPAIR * layer + pr, 0, 0, 0))],
        out_specs=pl.BlockSpec((L, LANE), lambda b, pr: (b, pr)),
        out_shape=jax.ShapeDtypeStruct((B * L, NH * ND), BF16),
        scratch_shapes=[pltpu.VMEM((2, nq, nk + LC), BF16)] * 2,
        compiler_params=_params(("arbitrary", "arbitrary")),
        name="natten",
    )(p, p, p, pc, pc, bias)


def _ffn(x2d, four, ret, na, ctx_acts, mods, g, wo_all, sel_v_t, w1_all, w3_all, w2_all, layer, g_final):
    rows = x2d.shape[0]
    final = g_final is not None
    ctx_tile = ctx_acts is not None
    off = 1 if ctx_tile else 0
    per_batch = L // TM
    tile = lambda i: jnp.maximum(i - off, 0)
    row = lambda w: pl.BlockSpec((TM, w), lambda i: (tile(i), 0))
    four_spec = pl.BlockSpec((DFT_N1, TM // DFT_N1, FW), lambda i: (tile(i) // per_batch, tile(i) % per_batch, 0))
    in_specs = [row(D), four_spec, row(RH * LANE), row(NH * ND)]
    args = [x2d, four, ret, na]
    if ctx_tile:
        assert all(a.shape[0] == TM for a in ctx_acts)
        in_specs += [_const_spec(a.shape) for a in ctx_acts]
        args += list(ctx_acts)
    in_specs += [_const_spec((8, 6, D)), _const_spec((1, D)),
                 _layer_spec((D, D), layer), _const_spec((RH * LANE, RH * RD)),
                 _layer_spec((D, DFF), layer), _layer_spec((D, DFF), layer), _layer_spec((DFF, D), layer)]
    args += [mods, g.reshape(1, D), wo_all, sel_v_t, w1_all, w3_all, w2_all]
    if final:
        in_specs.append(_const_spec((1, D)))
        args.append(g_final.reshape(1, D))
    out_specs = [row(D)]
    out_shape = [jax.ShapeDtypeStruct((rows, D), F32)]
    if ctx_tile:
        out_specs.append(pl.BlockSpec((TM, D), lambda i: (0, 0)))
        out_shape.append(jax.ShapeDtypeStruct((TM, D), F32))
    outs = pl.pallas_call(
        functools.partial(_ffn_kernel, final=final, ctx_tile=ctx_tile, mod_row=_mod_row(TM)),
        grid=(rows // TM + off,),
        in_specs=in_specs,
        out_specs=out_specs,
        out_shape=out_shape,
        scratch_shapes=[pltpu.VMEM((TM, DFF), BF16), pltpu.VMEM((RH * LANE, D), BF16)],
        compiler_params=_params(("arbitrary",)),
        name="ffn_final" if final else "ffn",
    )(*args)
    return outs if ctx_tile else outs[0]


def _take_padded(a, src, axis):
    pieces = []
    i, n = 0, len(src)
    while i < n:
        j = i + 1
        if src[i] < 0:
            while j < n and src[j] < 0:
                j += 1
            shape = list(a.shape)
            shape[axis] = j - i
            pieces.append(jnp.zeros(shape, a.dtype))
        else:
            while j < n and src[j] == src[j - 1] + 1:
                j += 1
            pieces.append(lax.slice_in_dim(a, int(src[i]), int(src[j - 1]) + 1, axis=axis))
        i = j
    return jnp.concatenate(pieces, axis=axis)


def kernel(x, c, ctx, c_ctx, w_ada, b_ada, g_mix, w_in, ret_decay_logit, ret_norm_g, w_four,
           na_rpb, w_out, g_ffn, w1, w3, w2, g_final):
    src_g = _gain_sources()
    rope_tabs = tuple(jnp.asarray(t) for t in _rope_tables())
    a1, m3, e, c256, s256 = (jnp.asarray(t).astype(BF16) for t in _dft_tables())
    sel_qk, sel_v = (jnp.asarray(t).astype(BF16) for t in _head_selectors())
    sel_v_t = sel_v.T
    tz_src, tz_mask = _na_toeplitz_tables()
    col_scale = np.ones((1, 1, W_IN), np.float32)
    col_scale[..., W_NQ:W_NK] = LOG2E * ND ** -0.5
    w_in_b = (w_in * jnp.asarray(col_scale)).astype(BF16)
    w_out_b = w_out.astype(BF16)
    w1b, w3b, w2b = w1.astype(BF16), w3.astype(BF16), w2.astype(BF16)

    tz_base = jnp.pad(_take_padded(na_rpb.astype(F32).reshape(DEPTH * NH, 15, 31), tz_src, 2), ((0, 0), (0, 1), (0, 0)))
    na_bias = _na_bias(tz_base, jnp.asarray(tz_mask))

    cvec = jnp.zeros((8, D), F32).at[0:B].set(c).at[B].set(c_ctx)
    mods_all = _ada(cvec, w_ada, b_ada).reshape(DEPTH, 8, 6, D)

    xl = x.reshape(B * L, D)
    xc = ctx.reshape(B * LC, D)
    for i in range(DEPTH):
        last = i == DEPTH - 1
        mods = mods_all[i]
        gain = _take_padded(ret_norm_g[i], src_g, 0).reshape(1, RH * LANE)
        wfour = w_four[i].astype(BF16)
        lg = jax.nn.log_sigmoid(ret_decay_logit[i].astype(F32))

        f_l, p_l, f_c, p_c, s0 = _proj(xl, xc, mods, g_mix[i], w_in_b, i, sel_qk, sel_v, rope_tabs, lg)
        four_l = _fourier(f_l, wfour, a1, m3, e)
        ret_l = _retention(p_l, s0, lg, gain)
        na_l = _natten(p_l, p_c, na_bias, i)
        if last:
            xl = _ffn(xl, four_l, ret_l, na_l, None, mods, g_ffn[i], w_out_b, sel_v_t, w1b, w3b, w2b, i, g_final)
        else:
            ctx_acts = (xc,) + tuple(_ctx_mix(f_c, p_c, lg, gain, wfour, c256, s256, e))
            xl, xc = _ffn(xl, four_l, ret_l, na_l, ctx_acts, mods, g_ffn[i], w_out_b, sel_v_t, w1b, w3b, w2b, i, None)
    return xl.reshape(B, L, D)
```

```python
import functools

import numpy as np
import jax
import jax.numpy as jnp
from jax import lax
from jax.experimental import pallas as pl
from jax.experimental.pallas import tpu as pltpu

F32 = jnp.float32
BF16 = jnp.bfloat16

D = 1024
B = 2
L = 8192
LC = 256
DEPTH = 2
GRID_W = 64
GRID_H = L // GRID_W
FW = 256
FGW = 64
RH = 4
RD = 96
NH = 6
ND = 64
NPAIR = NH // 2
DFF = 2816
EPS = 1e-6
NEG = -1e30
LOG2E = float(np.log2(np.e))
LANE = 128

C_RQ, C_RK, C_RV, C_RG = 0, 512, 1024, 1536
C_NQ, C_NK, C_NV = 2048, 2432, 2816
NP = 3200
W_RQ, W_RK, W_RV, W_RG = 256, 640, 1024, 1408
W_NQ, W_NK, W_NV = 1792, 2176, 2560
W_IN = 2944

TM = 512
TM_PROJ = 1024
RET_C = 256
NA_R = 4
NA_KR = NA_R + 8
NA_NB = GRID_H // NA_R
DFT_N1 = 64
DFT_N2 = 128
F1_R = 16
F3_K = 8
VMEM_LIMIT = 56 * 1024 * 1024


def _qk_lane_dims():
    m = np.full((LANE,), -1, np.int64)
    m[0:24] = np.arange(0, 24)
    m[24:48] = np.arange(48, 72)
    m[64:88] = np.arange(24, 48)
    m[88:112] = np.arange(72, 96)
    return m


def _v_lane_dims():
    m = np.full((LANE,), -1, np.int64)
    m[:RD] = np.arange(RD)
    return m


def _head_selectors():
    sels = []
    for lanes in (_qk_lane_dims(), _v_lane_dims()):
        sel = np.zeros((RH * RD, RH * LANE), np.float32)
        ok = np.nonzero(lanes >= 0)[0]
        for h in range(RH):
            sel[RD * h + lanes[ok], LANE * h + ok] = 1.0
        sels.append(sel)
    return sels


def _gain_sources():
    src = np.full((RH * LANE,), -1, np.int64)
    for h in range(RH):
        src[LANE * h: LANE * h + RD] = RD * h + np.arange(RD)
    return src


def _rope_tables():
    pos = np.arange(L)
    prow, pcol = pos // GRID_W, pos % GRID_W
    half = RD // 4
    inv = 10000.0 ** (-np.arange(half, dtype=np.float64) / half)
    ar = prow[:, None] * inv[None, :]
    ac = pcol[:, None] * inv[None, :]
    cos = np.ones((L, LANE), np.float64)
    sin = np.zeros((L, LANE), np.float64)
    for off, sign in ((0, -1.0), (64, 1.0)):
        cos[:, off:off + 24] = np.cos(ar)
        cos[:, off + 24:off + 48] = np.cos(ac)
        sin[:, off:off + 24] = sign * np.sin(ar)
        sin[:, off + 24:off + 48] = sign * np.sin(ac)
    return cos.astype(np.float32), sin.astype(np.float32)


def _dft_tables():
    k1 = np.arange(DFT_N1)
    a = 2 * np.pi * ((k1[:, None] * k1[None, :]) % DFT_N1) / DFT_N1
    a1 = np.kron(np.concatenate([np.cos(a), -np.sin(a)], axis=0), np.eye(F1_R))
    k2 = np.arange(DFT_N2)
    l2 = np.arange(DFT_N2)
    kk = k1[:, None, None] + DFT_N1 * k2[None, :, None]
    ang = 2 * np.pi * ((kk * l2[None, None, :]) % L) / L
    ct, st = np.cos(ang) / np.sqrt(L), np.sin(ang) / np.sqrt(L)
    m3 = np.concatenate([np.concatenate([ct, st], axis=2), np.concatenate([st, -ct], axis=2)], axis=1)
    c = np.arange(FW)
    same = (c[:, None] // FGW) == (c[None, :] // FGW)
    ac = 2 * np.pi * (((c[:, None] % FGW) * (c[None, :] % FGW)) % FGW) / FGW
    e = np.concatenate([np.where(same, np.cos(ac), 0.0), -np.where(same, np.sin(ac), 0.0)], axis=0) / 8.0
    p = np.arange(LC)
    ap = 2 * np.pi * ((p[:, None] * p[None, :]) % LC) / LC
    return tuple(t.astype(np.float32) for t in (a1, m3, e, np.cos(ap) / 16.0, np.sin(ap) / 16.0))


NA_DR = 16


def _na_toeplitz_tables():
    src = np.full((LANE,), -1, np.int64)
    src[0:16] = np.arange(0, 16) + 15
    src[113:128] = np.arange(113, 128) - 113
    src[49:80] = np.arange(49, 80) - 49
    qc = np.arange(GRID_W)[:, None]
    kc = np.arange(LANE)[None, :] % GRID_W
    ws = np.clip(qc - 8, 0, GRID_W - 16)
    mask = np.where((kc >= ws) & (kc < ws + 16), 0.0, NEG).astype(np.float32)
    return src, mask


def _dot(a, b):
    return jnp.dot(a, b, preferred_element_type=F32)


def _dot_nt(a, b):
    return lax.dot_general(a, b, (((1,), (1,)), ((), ())), preferred_element_type=F32)


def _silu(x):
    return x * jax.nn.sigmoid(x)


def _rmsnorm(x, g):
    return x * lax.rsqrt(jnp.mean(x * x, axis=-1, keepdims=True) + EPS) * g


def _ada_kernel(c_ref, w_ref, b_ref, o_ref):
    s = _silu(c_ref[...]).astype(BF16)
    o_ref[0] = _dot(s, w_ref[0].astype(BF16)) + b_ref[0]


def _proj_kernel(x_ref, xc_ref, mods_ref, g_ref, w_ref, pqk_ref, pv_ref, cos_ref, sin_ref, lg_ref,
                 f_ref, o_ref, fc_ref, oc_ref, s0_ref, wret_ref, *, mod_row):
    s = pl.program_id(0)

    def project(x, mod, rope, f_out, o_out):
        hb = (_rmsnorm(x, g_ref[...]) * (1.0 + mod[1:2, :]) + mod[0:1, :]).astype(BF16)
        f_out[...] = _dot(hb, w_ref[:, :FW]).astype(BF16)
        k_scale = RD ** -0.5
        for c0, scale in ((C_RQ, None), (C_RK, k_scale)):
            t4 = _dot(hb, wret_ref[:, c0:c0 + RH * LANE])
            for hh in range(RH):
                t = t4[:, LANE * hh:LANE * (hh + 1)]
                if rope:
                    t = t * cos_ref[...] + pltpu.roll(t, 64, 1) * sin_ref[...]
                if scale is not None:
                    t = t * scale
                o_out[:, c0 + LANE * hh:c0 + LANE * (hh + 1)] = t.astype(BF16)
        o_out[:, C_RV:C_RG] = _dot(hb, wret_ref[:, C_RV:C_RG]).astype(BF16)
        o_out[:, C_RG:C_NQ] = _silu(_dot(hb, wret_ref[:, C_RG:C_NQ])).astype(BF16)
        o_out[:, C_NQ:] = _dot(hb, w_ref[:, W_NQ:]).astype(BF16)

    @pl.when(s == 0)
    def _():
        for slot, (src0, sel_ref) in enumerate(((W_RQ, pqk_ref), (W_RK, pqk_ref), (W_RV, pv_ref), (W_RG, pv_ref))):
            wret_ref[:, RH * LANE * slot:RH * LANE * (slot + 1)] = _dot(
                w_ref[:, src0:src0 + RH * RD], sel_ref[...]).astype(BF16)
        project(xc_ref[...], mods_ref[B], False, fc_ref, oc_ref)
        idx = lax.broadcasted_iota(jnp.int32, (LC, LANE), 0).astype(F32)
        for b in range(B):
            for h in range(RH):
                k = oc_ref[LC * b:LC * (b + 1), C_RK + LANE * h:C_RK + LANE * (h + 1)].astype(F32)
                v = oc_ref[LC * b:LC * (b + 1), C_RV + LANE * h:C_RV + LANE * (h + 1)]
                wf = jnp.exp((LC - 1.0 - idx) * lg_ref[0, h])
                wb = jnp.exp(idx * lg_ref[1, h])
                s0_ref[b, h, 0] = _dot((k * wf).T.astype(BF16), v)
                s0_ref[b, h, 1] = _dot((k * wb).T.astype(BF16), v)

    @pl.when(s > 0)
    def _():
        project(x_ref[...], mods_ref[mod_row(s - 1)], True, f_ref, o_ref)


def _four1_kernel(a_ref, x_ref, y_ref):
    x = x_ref[0].reshape(DFT_N1 * F1_R, FW)
    y = _dot(a_ref[...], x).astype(BF16)
    y_ref[0] = y.reshape(2 * DFT_N1, F1_R, FW)


def _four3_kernel(m_ref, y_ref, e_ref, wf_ref, o_ref):
    ew = _dot(e_ref[...], wf_ref[...]).astype(BF16)
    for b in range(B):
        for kk in range(F3_K):
            y = jnp.concatenate([y_ref[b, 0, kk], y_ref[b, 1, kk]], axis=0)
            z = _dot(m_ref[kk], y).astype(BF16)
            o_ref[b, kk] = _dot(jnp.concatenate([z[:DFT_N2], z[DFT_N2:]], axis=1), ew)


def _decay_matrix(n, lgf, lgb):
    ii = lax.broadcasted_iota(jnp.int32, (n, n), 0)
    jj = lax.broadcasted_iota(jnp.int32, (n, n), 1)
    diff = (ii - jj).astype(F32)
    fwd = jnp.where(diff >= 0, jnp.exp(jnp.maximum(diff, 0.0) * lgf), 0.0)
    bwd = jnp.where(diff <= 0, jnp.exp(jnp.maximum(-diff, 0.0) * lgb), 0.0)
    return fwd + bwd


def _ret_readout(o, gate, gain):
    ms = jnp.sum(o * o, axis=-1, keepdims=True) * (1.0 / RD)
    return (o * lax.rsqrt(ms + EPS) * gain * gate.astype(F32)).astype(BF16)


def _ret_kernel(lg_ref, q_ref, k_ref, v_ref, g_ref, s0_ref, gain_ref, o_ref, acc_ref, stf_ref, stb_ref):
    h = pl.program_id(1)
    lgf = lg_ref[0, h]
    lgb = lg_ref[1, h]
    c = RET_C
    n = L // c
    dmat = _decay_matrix(c, lgf, lgb)
    idx = lax.broadcasted_iota(jnp.int32, (c, LANE), 0).astype(F32)
    qdf = jnp.exp((idx + 1.0) * lgf)
    kdf = jnp.exp((c - 1.0 - idx) * lgf)
    qdb = jnp.exp((c - idx) * lgb)
    kdb = jnp.exp(idx * lgb)
    cdf = jnp.exp(jnp.full((1, LANE), float(c), F32) * lgf)
    cdb = jnp.exp(jnp.full((1, LANE), float(c), F32) * lgb)
    gain = gain_ref[...]

    dirs = ((stf_ref, qdf, kdf, cdf), (stb_ref, qdb, kdb, cdb))
    stf_ref[...] = s0_ref[0, 0, 0]
    stb_ref[...] = s0_ref[0, 0, 1]

    def visit(r0, direction, second):
        st_ref, qdec, kdec, cdec = dirs[direction]
        rows = pl.ds(pl.multiple_of(r0, c), c)
        q = q_ref[rows, :]
        k = k_ref[rows, :]
        v = v_ref[rows, :]
        cross = _dot(q, st_ref[...].astype(BF16)) * qdec
        if second:
            o_ref[rows, :] = _ret_readout(acc_ref[rows, :] + cross, g_ref[rows, :], gain)
        else:
            s = _dot_nt(q, k) * dmat
            acc_ref[rows, :] = _dot(s.astype(BF16), v) + cross
        kt = (k.astype(F32) * kdec).T.astype(BF16)
        st_ref[...] = st_ref[...] * cdec + _dot(kt, v)

    def body(second):
        def step(i, carry):
            visit(i * c, 0, second)
            visit((n - 1 - i) * c, 1, second)
            return carry
        return step

    lax.fori_loop(0, n // 2, body(False), 0, unroll=8)
    lax.fori_loop(n // 2, n, body(True), 0, unroll=8)


def _pair_attention(q, score_fn, value_fn):
    lane = lax.broadcasted_iota(jnp.int32, q.shape, 1)
    outs = []
    for hh in range(2):
        sel = (lane >= ND * hh) & (lane < ND * (hh + 1))
        qm = jnp.where(sel, q, jnp.zeros_like(q))
        scores = score_fn(qm, hh)
        m = functools.reduce(jnp.maximum, [jnp.max(s, axis=-1, keepdims=True) for s in scores])
        ps = [jnp.exp2(s - m) for s in scores]
        den = functools.reduce(jnp.add, [jnp.sum(p, axis=-1, keepdims=True) for p in ps])
        outs.append(value_fn([p.astype(BF16) for p in ps]) / den)
    return jnp.where(lane < ND, outs[0], outs[1])


def _ctx_mix_kernel(lg_ref, f_ref, p_ref, gain_ref, wf_ref, c_ref, s_ref, e_ref, fo_ref, ro_ref, no_ref):
    f = f_ref[...]
    pr = _dot(c_ref[...], f)
    qi = _dot(s_ref[...], f)
    fr = _dot(pr.astype(BF16), e_ref[:FW, :]) + _dot(qi.astype(BF16), e_ref[FW:, :])
    fo_ref[...] = _dot(fr.astype(BF16), wf_ref[...]).astype(BF16)
    for h in range(RH):
        sl = lambda c0: slice(c0 + LANE * h, c0 + LANE * (h + 1))
        dmat = _decay_matrix(LC, lg_ref[0, h], lg_ref[1, h])
        s = _dot_nt(p_ref[:, sl(C_RQ)], p_ref[:, sl(C_RK)]) * dmat
        o = _dot(s.astype(BF16), p_ref[:, sl(C_RV)])
        ro_ref[:, LANE * h:LANE * (h + 1)] = _ret_readout(o, p_ref[:, sl(C_RG)], gain_ref[:, LANE * h:LANE * (h + 1)])
    for pair in range(NPAIR):
        sl = lambda c0: slice(c0 + LANE * pair, c0 + LANE * (pair + 1))
        k = p_ref[:, sl(C_NK)]
        v = p_ref[:, sl(C_NV)]
        out = _pair_attention(p_ref[:, sl(C_NQ)], lambda qm, hh: [_dot_nt(qm, k)], lambda ps: _dot(ps[0], v))
        no_ref[:, LANE * pair:LANE * (pair + 1)] = out.astype(BF16)


def _na_key_start(r0):
    return min(max(r0 - 4, 0), GRID_H - NA_KR)


def _na_bias_kernel(base_ref, mask_ref, o_ref, tz_ref):
    for dr in range(NA_DR - 1):
        t = jnp.broadcast_to(base_ref[0, dr:dr + 1, :], (GRID_W, LANE))
        t = pltpu.roll(t, 0, 1, stride=1, stride_axis=0)
        tz_ref[dr] = t * LOG2E + mask_ref[...]
    tz_ref[NA_DR - 1] = jnp.full((GRID_W, LANE), NEG, F32)
    low_half = lax.broadcasted_iota(jnp.int32, (GRID_W, LANE), 1) < GRID_W
    for var, r0 in enumerate((0, 2 * NA_R, GRID_H - NA_R)):
        ks = _na_key_start(r0)
        for rl in range(NA_R):
            r = r0 + rl
            rs = min(max(r - 4, 0), GRID_H - 8)
            slot = [kr - r + 7 if rs <= kr < rs + 8 else NA_DR - 1 for kr in range(ks, ks + NA_KR)]
            for m in range(NA_KR // 2):
                o_ref[0, var, GRID_W * rl:GRID_W * (rl + 1), LANE * m:LANE * (m + 1)] = jnp.where(
                    low_half, tz_ref[slot[2 * m]], tz_ref[slot[2 * m + 1]])


def _natten_kernel(q_ref, k_ref, v_ref, kc_ref, vc_ref, bias_ref, o_ref, pa_ref, pb_ref):
    nq = NA_R * GRID_W
    nk = NA_KR * GRID_W
    kc = kc_ref[...]
    vc = vc_ref[...]
    lane = lax.broadcasted_iota(jnp.int32, (nq, LANE), 1)

    def rows(i):
        ks = jnp.clip(NA_R * i - 4, 0, GRID_H - NA_KR)
        return pl.ds(pl.multiple_of(ks * GRID_W, nq), nk), pl.ds(pl.multiple_of(i * nq, nq), nq)

    def probabilities(i, p_ref):
        i = jnp.minimum(i, NA_NB - 1)
        var = jnp.where(i == 0, 0, jnp.where(i == NA_NB - 1, 2, 1))
        krows, qrows = rows(i)
        q = q_ref[qrows, :]
        kw = k_ref[krows, :]
        for hh in range(2):
            qm = jnp.where((lane >= ND * hh) & (lane < ND * (hh + 1)), q, jnp.zeros_like(q))
            s_win = _dot_nt(qm, kw) + bias_ref[hh, var]
            s_ctx = _dot_nt(qm, kc)
            m = jnp.maximum(jnp.max(s_win, axis=-1, keepdims=True), jnp.max(s_ctx, axis=-1, keepdims=True))
            p_ref[hh, :, :nk] = jnp.exp2(s_win - m).astype(BF16)
            p_ref[hh, :, nk:] = jnp.exp2(s_ctx - m).astype(BF16)

    def values(i, p_ref):
        krows, qrows = rows(i)
        vals = jnp.concatenate([v_ref[krows, :], vc], axis=0)
        vals = jnp.concatenate([vals, jnp.ones_like(vals)], axis=1)
        o2 = [_dot(p_ref[hh], vals) for hh in range(2)]
        outs = [o[:, :LANE] / o[:, LANE:] for o in o2]
        o_ref[qrows, :] = jnp.where(lane < ND, outs[0], outs[1]).astype(BF16)

    probabilities(0, pa_ref)

    def step(j, carry):
        values(2 * j, pa_ref)
        probabilities(2 * j + 1, pb_ref)
        values(2 * j + 1, pb_ref)
        probabilities(2 * j + 2, pa_ref)
        return carry

    lax.fori_loop(0, NA_NB // 2, step, 0)


def _ffn_kernel(*refs, final, ctx_tile, mod_row):
    refs = list(refs)
    x_ref, f_ref, r_ref, n_ref = refs[:4]
    del refs[:4]
    if ctx_tile:
        xc_ref, fc_ref, rc_ref, nc_ref = refs[:4]
        del refs[:4]
    mods_ref, g_ref, wo_ref, sel_ref, w1_ref, w3_ref, w2_ref = refs[:7]
    del refs[:7]
    gf_ref = refs.pop(0) if final else None
    o_ref = refs.pop(0)
    oc_ref = refs.pop(0) if ctx_tile else None
    acc_ref, wor_ref = refs
    s = pl.program_id(0)

    @pl.when(s == 0)
    def _():
        wor_ref[...] = _dot(sel_ref[...], wo_ref[FW:FW + RH * RD, :]).astype(BF16)

    def tile(x, f, r, n, mod, out_ref):
        y = _dot(f.astype(BF16), wo_ref[:FW, :]) + _dot(r, wor_ref[...]) + _dot(n, wo_ref[FW + RH * RD:, :])
        x1 = x + mod[2:3, :] * y
        hb = (_rmsnorm(x1, g_ref[...]) * (1.0 + mod[4:5, :]) + mod[3:4, :]).astype(BF16)
        fc = 256
        for c in range(DFF // fc):
            a = _dot(hb, w1_ref[:, c * fc:(c + 1) * fc])
            b = _dot(hb, w3_ref[:, c * fc:(c + 1) * fc])
            acc_ref[:, c * fc:(c + 1) * fc] = (_silu(a) * b).astype(BF16)
        x2 = x1 + mod[5:6, :] * _dot(acc_ref[...], w2_ref[...])
        if final:
            x2 = _rmsnorm(x2, gf_ref[...])
        out_ref[...] = x2

    def latent_tile(t):
        f = jnp.concatenate([f_ref[:, r, :] for r in range(f_ref.shape[1])], axis=0)
        tile(x_ref[...], f, r_ref[...], n_ref[...], mods_ref[mod_row(t)], o_ref)

    if ctx_tile:
        @pl.when(s == 0)
        def _():
            tile(xc_ref[...], fc_ref[...], rc_ref[...], nc_ref[...], mods_ref[B], oc_ref)

        @pl.when(s > 0)
        def _():
            latent_tile(s - 1)
    else:
        latent_tile(s)


def _params(sem):
    return pltpu.CompilerParams(dimension_semantics=sem, vmem_limit_bytes=VMEM_LIMIT)


def _const_spec(shape):
    nd = len(shape)
    return pl.BlockSpec(shape, lambda *_: (0,) * nd, pipeline_mode=pl.Buffered(1))


def _smem_spec():
    return pl.BlockSpec(memory_space=pltpu.SMEM)


def _ada(cvec, w_ada, b_ada):
    tn = 2048
    return pl.pallas_call(
        _ada_kernel,
        grid=(DEPTH, 6 * D // tn),
        in_specs=[pl.BlockSpec((8, D), lambda i, j: (0, 0)),
                  pl.BlockSpec((1, D, tn), lambda i, j: (i, 0, j)),
                  pl.BlockSpec((1, 1, tn), lambda i, j: (i, 0, j))],
        out_specs=pl.BlockSpec((1, 8, tn), lambda i, j: (i, 0, j)),
        out_shape=jax.ShapeDtypeStruct((DEPTH, 8, 6 * D), F32),
        compiler_params=_params(("arbitrary", "arbitrary")),
        name="ada",
    )(cvec, w_ada, b_ada.reshape(DEPTH, 1, 6 * D))


def _mod_row(tm):
    per_batch = L // tm
    return lambda tile: tile // per_batch


def _layer_spec(shape, layer):
    nd = len(shape)
    return pl.BlockSpec((None,) + tuple(shape), lambda *_: (layer,) + (0,) * nd, pipeline_mode=pl.Buffered(1))


def _proj(x2d, xc2d, mods, g, w_all, layer, sel_qk, sel_v, rope_tabs, lg):
    rows, crows = x2d.shape[0], xc2d.shape[0]
    assert crows == B * LC
    tm = TM_PROJ
    per_batch = L // tm
    tile = lambda i: jnp.maximum(i - 1, 0)
    return pl.pallas_call(
        functools.partial(_proj_kernel, mod_row=_mod_row(tm)),
        grid=(rows // tm + 1,),
        in_specs=[pl.BlockSpec((tm, D), lambda i: (tile(i), 0)),
                  _const_spec((crows, D)),
                  _const_spec((8, 6, D)),
                  _const_spec((1, D)),
                  _layer_spec((D, W_IN), layer),
                  _const_spec((RH * RD, RH * LANE)),
                  _const_spec((RH * RD, RH * LANE)),
                  pl.BlockSpec((tm, LANE), lambda i: (tile(i) % per_batch, 0)),
                  pl.BlockSpec((tm, LANE), lambda i: (tile(i) % per_batch, 0)),
                  _smem_spec()],
        out_specs=[pl.BlockSpec((tm, FW), lambda i: (tile(i), 0)),
                   pl.BlockSpec((tm, NP), lambda i: (tile(i), 0)),
                   pl.BlockSpec((crows, FW), lambda i: (0, 0)),
                   pl.BlockSpec((crows, NP), lambda i: (0, 0)),
                   pl.BlockSpec((B, RH, 2, LANE, LANE), lambda i: (0, 0, 0, 0, 0))],
        out_shape=[jax.ShapeDtypeStruct((rows, FW), BF16), jax.ShapeDtypeStruct((rows, NP), BF16),
                   jax.ShapeDtypeStruct((crows, FW), BF16), jax.ShapeDtypeStruct((crows, NP), BF16),
                   jax.ShapeDtypeStruct((B, RH, 2, LANE, LANE), F32)],
        scratch_shapes=[pltpu.VMEM((D, 4 * RH * LANE), BF16)],
        compiler_params=_params(("arbitrary",)),
        name="proj",
    )(x2d, xc2d, mods, g.reshape(1, D), w_all, sel_qk, sel_v, *rope_tabs, lg)


def _fourier(f2d, wf, a1, m3, e):
    y = pl.pallas_call(
        _four1_kernel,
        grid=(B, DFT_N2 // F1_R),
        in_specs=[_const_spec((2 * DFT_N1 * F1_R, DFT_N1 * F1_R)),
                  pl.BlockSpec((1, DFT_N1, F1_R, FW), lambda b, j: (b, 0, j, 0))],
        out_specs=pl.BlockSpec((1, 2 * DFT_N1, F1_R, FW), lambda b, j: (b, 0, j, 0)),
        out_shape=jax.ShapeDtypeStruct((B, 2 * DFT_N1, DFT_N2, FW), BF16),
        compiler_params=_params(("arbitrary", "arbitrary")),
        name="four1",
    )(a1, f2d.reshape(B, DFT_N1, DFT_N2, FW))
    out = pl.pallas_call(
        _four3_kernel,
        grid=(DFT_N1 // F3_K,),
        in_specs=[pl.BlockSpec((F3_K, 2 * DFT_N2, 2 * DFT_N2), lambda k: (k, 0, 0)),
                  pl.BlockSpec((B, 2, F3_K, DFT_N2, FW), lambda k: (0, 0, k, 0, 0)),
                  _const_spec((2 * FW, FW)),
                  _const_spec((FW, FW))],
        out_specs=pl.BlockSpec((B, F3_K, DFT_N2, FW), lambda k: (0, k, 0, 0)),
        out_shape=jax.ShapeDtypeStruct((B, DFT_N1, DFT_N2, FW), F32),
        compiler_params=_params(("arbitrary",)),
        name="four3",
    )(m3, y.reshape(B, 2, DFT_N1, DFT_N2, FW), e, wf)
    return out.reshape(B * DFT_N1, DFT_N2, FW)


def _retention(p, s0, lg, gain):
    col = lambda c0: (lambda b, h: (b, c0 // LANE + h))
    return pl.pallas_call(
        _ret_kernel,
        grid=(B, RH),
        in_specs=[_smem_spec(),
                  pl.BlockSpec((L, LANE), col(C_RQ)),
                  pl.BlockSpec((L, LANE), col(C_RK)),
                  pl.BlockSpec((L, LANE), col(C_RV)),
                  pl.BlockSpec((L, LANE), col(C_RG)),
                  pl.BlockSpec((1, 1, 2, LANE, LANE), lambda b, h: (b, h, 0, 0, 0)),
                  pl.BlockSpec((1, LANE), lambda b, h: (0, h))],
        out_specs=pl.BlockSpec((L, LANE), lambda b, h: (b, h)),
        out_shape=jax.ShapeDtypeStruct((B * L, RH * LANE), BF16),
        scratch_shapes=[pltpu.VMEM((L, LANE), F32), pltpu.VMEM((LANE, LANE), F32), pltpu.VMEM((LANE, LANE), F32)],
        compiler_params=_params(("arbitrary", "arbitrary")),
        name="ret",
    )(lg, p, p, p, p, s0, gain)


def _ctx_mix(fc, pc, lg, gain, wf, c256, s256, e):
    return pl.pallas_call(
        _ctx_mix_kernel,
        grid=(B,),
        in_specs=[_smem_spec(),
                  pl.BlockSpec((LC, FW), lambda b: (b, 0)),
                  pl.BlockSpec((LC, NP), lambda b: (b, 0)),
                  _const_spec((1, RH * LANE)),
                  _const_spec((FW, FW)),
                  _const_spec((LC, LC)),
                  _const_spec((LC, LC)),
                  _const_spec((2 * FW, FW))],
        out_specs=[pl.BlockSpec((LC, FW), lambda b: (b, 0)),
                   pl.BlockSpec((LC, RH * LANE), lambda b: (b, 0)),
                   pl.BlockSpec((LC, NH * ND), lambda b: (b, 0))],
        out_shape=[jax.ShapeDtypeStruct((B * LC, FW), BF16),
                   jax.ShapeDtypeStruct((B * LC, RH * LANE), BF16),
                   jax.ShapeDtypeStruct((B * LC, NH * ND), BF16)],
        compiler_params=_params(("arbitrary",)),
        name="ctx_mix",
    )(lg, fc, pc, gain, wf, c256, s256, e)


def _na_bias(base, mask):
    nq, nk = NA_R * GRID_W, NA_KR * GRID_W
    return pl.pallas_call(
        _na_bias_kernel,
        grid=(base.shape[0],),
        in_specs=[pl.BlockSpec((1, NA_DR, LANE), lambda h: (h, 0, 0)),
                  _const_spec((GRID_W, LANE))],
        out_specs=pl.BlockSpec((1, 3, nq, nk), lambda h: (h, 0, 0, 0)),
        out_shape=jax.ShapeDtypeStruct((base.shape[0], 3, nq, nk), F32),
        scratch_shapes=[pltpu.VMEM((NA_DR, GRID_W, LANE), F32)],
        compiler_params=_params(("arbitrary",)),
        name="na_bias",
    )(base, mask)


def _natten(p, pc, bias, layer):
    nq, nk = NA_R * GRID_W, NA_KR * GRID_W
    return pl.pallas_call(
        _natten_kernel,
        grid=(B, NPAIR),
        in_specs=[pl.BlockSpec((L, LANE), lambda b, pr: (b, C_NQ // LANE + pr)),
                  pl.BlockSpec((L, LANE), lambda b, pr: (b, C_NK // LANE + pr)),
                  pl.BlockSpec((L, LANE), lambda b, pr: (b, C_NV // LANE + pr)),
                  pl.BlockSpec((LC, LANE), lambda b, pr: (b, C_NK // LANE + pr)),
                  pl.BlockSpec((LC, LANE), lambda b, pr: (b, C_NV // LANE + pr)),
                  pl.BlockSpec((2, 3, nq, nk), lambda b, pr: (NPAIR * layer + pr, 0, 0, 0))],
        out_specs=pl.BlockSpec((L, LANE), lambda b, pr: (b, pr)),
        out_shape=jax.ShapeDtypeStruct((B * L, NH * ND), BF16),
        scratch_shapes=[pltpu.VMEM((2, nq, nk + LC), BF16)] * 2,
        compiler_params=_params(("arbitrary", "arbitrary")),
        name="natten",
    )(p, p, p, pc, pc, bias)


def _ffn(x2d, four, ret, na, ctx_acts, mods, g, wo_all, sel_v_t, w1_all, w3_all, w2_all, layer, g_final):
    rows = x2d.shape[0]
    final = g_final is not None
    ctx_tile = ctx_acts is not None
    off = 1 if ctx_tile else 0
    per_batch = L // TM
    tile = lambda i: jnp.maximum(i - off, 0)
    row = lambda w: pl.BlockSpec((TM, w), lambda i: (tile(i), 0))
    four_spec = pl.BlockSpec((DFT_N1, TM // DFT_N1, FW), lambda i: (tile(i) // per_batch, tile(i) % per_batch, 0))
    in_specs = [row(D), four_spec, row(RH * LANE), row(NH * ND)]
    args = [x2d, four, ret, na]
    if ctx_tile:
        assert all(a.shape[0] == TM for a in ctx_acts)
        in_specs += [_const_spec(a.shape) for a in ctx_acts]
        args += list(ctx_acts)
    in_specs += [_const_spec((8, 6, D)), _const_spec((1, D)),
                 _layer_spec((D, D), layer), _const_spec((RH * LANE, RH * RD)),
                 _layer_spec((D, DFF), layer), _layer_spec((D, DFF), layer), _layer_spec((DFF, D), layer)]
    args += [mods, g.reshape(1, D), wo_all, sel_v_t, w1_all, w3_all, w2_all]
    if final:
        in_specs.append(_const_spec((1, D)))
        args.append(g_final.reshape(1, D))
    out_specs = [row(D)]
    out_shape = [jax.ShapeDtypeStruct((rows, D), F32)]
    if ctx_tile:
        out_specs.append(pl.BlockSpec((TM, D), lambda i: (0, 0)))
        out_shape.append(jax.ShapeDtypeStruct((TM, D), F32))
    outs = pl.pallas_call(
        functools.partial(_ffn_kernel, final=final, ctx_tile=ctx_tile, mod_row=_mod_row(TM)),
        grid=(rows // TM + off,),
        in_specs=in_specs,
        out_specs=out_specs,
        out_shape=out_shape,
        scratch_shapes=[pltpu.VMEM((TM, DFF), BF16), pltpu.VMEM((RH * LANE, D), BF16)],
        compiler_params=_params(("arbitrary",)),
        name="ffn_final" if final else "ffn",
    )(*args)
    return outs if ctx_tile else outs[0]


def _take_padded(a, src, axis):
    pieces = []
    i, n = 0, len(src)
    while i < n:
        j = i + 1
        if src[i] < 0:
            while j < n and src[j] < 0:
                j += 1
            shape = list(a.shape)
            shape[axis] = j - i
            pieces.append(jnp.zeros(shape, a.dtype))
        else:
            while j < n and src[j] == src[j - 1] + 1:
                j += 1
            pieces.append(lax.slice_in_dim(a, int(src[i]), int(src[j - 1]) + 1, axis=axis))
        i = j
    return jnp.concatenate(pieces, axis=axis)


def kernel(x, c, ctx, c_ctx, w_ada, b_ada, g_mix, w_in, ret_decay_logit, ret_norm_g, w_four,
           na_rpb, w_out, g_ffn, w1, w3, w2, g_final):
    src_g = _gain_sources()
    rope_tabs = tuple(jnp.asarray(t) for t in _rope_tables())
    a1, m3, e, c256, s256 = (jnp.asarray(t).astype(BF16) for t in _dft_tables())
    sel_qk, sel_v = (jnp.asarray(t).astype(BF16) for t in _head_selectors())
    sel_v_t = sel_v.T
    tz_src, tz_mask = _na_toeplitz_tables()
    col_scale = np.ones((1, 1, W_IN), np.float32)
    col_scale[..., W_NQ:W_NK] = LOG2E * ND ** -0.5
    w_in_b = (w_in * jnp.asarray(col_scale)).astype(BF16)
    w_out_b = w_out.astype(BF16)
    w1b, w3b, w2b = w1.astype(BF16), w3.astype(BF16), w2.astype(BF16)

    tz_base = jnp.pad(_take_padded(na_rpb.astype(F32).reshape(DEPTH * NH, 15, 31), tz_src, 2), ((0, 0), (0, 1), (0, 0)))
    na_bias = _na_bias(tz_base, jnp.asarray(tz_mask))

    cvec = jnp.zeros((8, D), F32).at[0:B].set(c).at[B].set(c_ctx)
    mods_all = _ada(cvec, w_ada, b_ada).reshape(DEPTH, 8, 6, D)

    xl = x.reshape(B * L, D)
    xc = ctx.reshape(B * LC, D)
    for i in range(DEPTH):
        last = i == DEPTH - 1
        mods = mods_all[i]
        gain = _take_padded(ret_norm_g[i], src_g, 0).reshape(1, RH * LANE)
        wfour = w_four[i].astype(BF16)
        lg = jax.nn.log_sigmoid(ret_decay_logit[i].astype(F32))

        f_l, p_l, f_c, p_c, s0 = _proj(xl, xc, mods, g_mix[i], w_in_b, i, sel_qk, sel_v, rope_tabs, lg)
        four_l = _fourier(f_l, wfour, a1, m3, e)
        ret_l = _retention(p_l, s0, lg, gain)
        na_l = _natten(p_l, p_c, na_bias, i)
        if last:
            xl = _ffn(xl, four_l, ret_l, na_l, None, mods, g_ffn[i], w_out_b, sel_v_t, w1b, w3b, w2b, i, g_final)
        else:
            ctx_acts = (xc,) + tuple(_ctx_mix(f_c, p_c, lg, gain, wfour, c256, s256, e))
            xl, xc = _ffn(xl, four_l, ret_l, na_l, ctx_acts, mods, g_ffn[i], w_out_b, sel_v_t, w1b, w3b, w2b, i, None)
    return xl.reshape(B, L, D)
```

```python
import functools

import numpy as np
import jax
import jax.numpy as jnp
from jax import lax
from jax.experimental import pallas as pl
from jax.experimental.pallas import tpu as pltpu

F32 = jnp.float32
BF16 = jnp.bfloat16

D = 1024
B = 2
L = 8192
LC = 256
DEPTH = 2
GRID_W = 64
GRID_H = L // GRID_W
FW = 256
FGW = 64
RH = 4
RD = 96
NH = 6
ND = 64
NPAIR = NH // 2
DFF = 2816
EPS = 1e-6
NEG = -1e30
LOG2E = float(np.log2(np.e))
LANE = 128

C_RQ, C_RK, C_RV, C_RG = 0, 512, 1024, 1536
C_NQ, C_NK, C_NV = 2048, 2432, 2816
NP = 3200
W_RQ, W_RK, W_RV, W_RG = 256, 640, 1024, 1408
W_NQ, W_NK, W_NV = 1792, 2176, 2560
W_IN = 2944

TM = 512
TM_PROJ = 1024
RET_C = 256
NA_R = 4
NA_KR = NA_R + 8
NA_NB = GRID_H // NA_R
DFT_N1 = 64
DFT_N2 = 128
F1_R = 16
F3_K = 8
VMEM_LIMIT = 56 * 1024 * 1024


def _qk_lane_dims():
    m = np.full((LANE,), -1, np.int64)
    m[0:24] = np.arange(0, 24)
    m[24:48] = np.arange(48, 72)
    m[64:88] = np.arange(24, 48)
    m[88:112] = np.arange(72, 96)
    return m


def _v_lane_dims():
    m = np.full((LANE,), -1, np.int64)
    m[:RD] = np.arange(RD)
    return m


def _head_selectors():
    sels = []
    for lanes in (_qk_lane_dims(), _v_lane_dims()):
        sel = np.zeros((RH * RD, RH * LANE), np.float32)
        ok = np.nonzero(lanes >= 0)[0]
        for h in range(RH):
            sel[RD * h + lanes[ok], LANE * h + ok] = 1.0
        sels.append(sel)
    return sels


def _gain_sources():
    src = np.full((RH * LANE,), -1, np.int64)
    for h in range(RH):
        src[LANE * h: LANE * h + RD] = RD * h + np.arange(RD)
    return src


def _rope_tables():
    pos = np.arange(L)
    prow, pcol = pos // GRID_W, pos % GRID_W
    half = RD // 4
    inv = 10000.0 ** (-np.arange(half, dtype=np.float64) / half)
    ar = prow[:, None] * inv[None, :]
    ac = pcol[:, None] * inv[None, :]
    cos = np.ones((L, LANE), np.float64)
    sin = np.zeros((L, LANE), np.float64)
    for off, sign in ((0, -1.0), (64, 1.0)):
        cos[:, off:off + 24] = np.cos(ar)
        cos[:, off + 24:off + 48] = np.cos(ac)
        sin[:, off:off + 24] = sign * np.sin(ar)
        sin[:, off + 24:off + 48] = sign * np.sin(ac)
    return cos.astype(np.float32), sin.astype(np.float32)


def _dft_tables():
    k1 = np.arange(DFT_N1)
    a = 2 * np.pi * ((k1[:, None] * k1[None, :]) % DFT_N1) / DFT_N1
    a1 = np.kron(np.concatenate([np.cos(a), -np.sin(a)], axis=0), np.eye(F1_R))
    k2 = np.arange(DFT_N2)
    l2 = np.arange(DFT_N2)
    kk = k1[:, None, None] + DFT_N1 * k2[None, :, None]
    ang = 2 * np.pi * ((kk * l2[None, None, :]) % L) / L
    ct, st = np.cos(ang) / np.sqrt(L), np.sin(ang) / np.sqrt(L)
    m3 = np.concatenate([np.concatenate([ct, st], axis=2), np.concatenate([st, -ct], axis=2)], axis=1)
    c = np.arange(FW)
    same = (c[:, None] // FGW) == (c[None, :] // FGW)
    ac = 2 * np.pi * (((c[:, None] % FGW) * (c[None, :] % FGW)) % FGW) / FGW
    e = np.concatenate([np.where(same, np.cos(ac), 0.0), -np.where(same, np.sin(ac), 0.0)], axis=0) / 8.0
    p = np.arange(LC)
    ap = 2 * np.pi * ((p[:, None] * p[None, :]) % LC) / LC
    return tuple(t.astype(np.float32) for t in (a1, m3, e, np.cos(ap) / 16.0, np.sin(ap) / 16.0))


NA_DR = 16


def _na_toeplitz_tables():
    src = np.full((LANE,), -1, np.int64)
    src[0:16] = np.arange(0, 16) + 15
    src[113:128] = np.arange(113, 128) - 113
    src[49:80] = np.arange(49, 80) - 49
    qc = np.arange(GRID_W)[:, None]
    kc = np.arange(LANE)[None, :] % GRID_W
    ws = np.clip(qc - 8, 0, GRID_W - 16)
    mask = np.where((kc >= ws) & (kc < ws + 16), 0.0, NEG).astype(np.float32)
    return src, mask


def _dot(a, b):
    return jnp.dot(a, b, preferred_element_type=F32)


def _dot_nt(a, b):
    return lax.dot_general(a, b, (((1,), (1,)), ((), ())), preferred_element_type=F32)


def _silu(x):
    return x * jax.nn.sigmoid(x)


def _rmsnorm(x, g):
    return x * lax.rsqrt(jnp.mean(x * x, axis=-1, keepdims=True) + EPS) * g


def _ada_kernel(c_ref, w_ref, b_ref, o_ref):
    s = _silu(c_ref[...]).astype(BF16)
    o_ref[0] = _dot(s, w_ref[0].astype(BF16)) + b_ref[0]


def _proj_kernel(x_ref, xc_ref, mods_ref, g_ref, w_ref, pqk_ref, pv_ref, cos_ref, sin_ref, lg_ref,
                 f_ref, o_ref, fc_ref, oc_ref, s0_ref, wret_ref, *, mod_row):
    s = pl.program_id(0)

    def project(x, mod, rope, f_out, o_out):
        hb = (_rmsnorm(x, g_ref[...]) * (1.0 + mod[1:2, :]) + mod[0:1, :]).astype(BF16)
        f_out[...] = _dot(hb, w_ref[:, :FW]).astype(BF16)
        k_scale = RD ** -0.5
        for c0, scale in ((C_RQ, None), (C_RK, k_scale)):
            t4 = _dot(hb, wret_ref[:, c0:c0 + RH * LANE])
            for hh in range(RH):
                t = t4[:, LANE * hh:LANE * (hh + 1)]
                if rope:
                    t = t * cos_ref[...] + pltpu.roll(t, 64, 1) * sin_ref[...]
                if scale is not None:
                    t = t * scale
                o_out[:, c0 + LANE * hh:c0 + LANE * (hh + 1)] = t.astype(BF16)
        o_out[:, C_RV:C_RG] = _dot(hb, wret_ref[:, C_RV:C_RG]).astype(BF16)
        o_out[:, C_RG:C_NQ] = _silu(_dot(hb, wret_ref[:, C_RG:C_NQ])).astype(BF16)
        o_out[:, C_NQ:] = _dot(hb, w_ref[:, W_NQ:]).astype(BF16)

    @pl.when(s == 0)
    def _():
        for slot, (src0, sel_ref) in enumerate(((W_RQ, pqk_ref), (W_RK, pqk_ref), (W_RV, pv_ref), (W_RG, pv_ref))):
            wret_ref[:, RH * LANE * slot:RH * LANE * (slot + 1)] = _dot(
                w_ref[:, src0:src0 + RH * RD], sel_ref[...]).astype(BF16)
        project(xc_ref[...], mods_ref[B], False, fc_ref, oc_ref)
        idx = lax.broadcasted_iota(jnp.int32, (LC, LANE), 0).astype(F32)
        for b in range(B):
            for h in range(RH):
                k = oc_ref[LC * b:LC * (b + 1), C_RK + LANE * h:C_RK + LANE * (h + 1)].astype(F32)
                v = oc_ref[LC * b:LC * (b + 1), C_RV + LANE * h:C_RV + LANE * (h + 1)]
                wf = jnp.exp((LC - 1.0 - idx) * lg_ref[0, h])
                wb = jnp.exp(idx * lg_ref[1, h])
                s0_ref[b, h, 0] = _dot((k * wf).T.astype(BF16), v)
                s0_ref[b, h, 1] = _dot((k * wb).T.astype(BF16), v)

    @pl.when(s > 0)
    def _():
        project(x_ref[...], mods_ref[mod_row(s - 1)], True, f_ref, o_ref)


def _four1_kernel(a_ref, x_ref, y_ref):
    x = x_ref[0].reshape(DFT_N1 * F1_R, FW)
    y = _dot(a_ref[...], x).astype(BF16)
    y_ref[0] = y.reshape(2 * DFT_N1, F1_R, FW)


def _four3_kernel(m_ref, y_ref, e_ref, wf_ref, o_ref):
    ew = _dot(e_ref[...], wf_ref[...]).astype(BF16)
    for b in range(B):
        for kk in range(F3_K):
            y = jnp.concatenate([y_ref[b, 0, kk], y_ref[b, 1, kk]], axis=0)
            z = _dot(m_ref[kk], y).astype(BF16)
            o_ref[b, kk] = _dot(jnp.concatenate([z[:DFT_N2], z[DFT_N2:]], axis=1), ew)


def _decay_matrix(n, lgf, lgb):
    ii = lax.broadcasted_iota(jnp.int32, (n, n), 0)
    jj = lax.broadcasted_iota(jnp.int32, (n, n), 1)
    diff = (ii - jj).astype(F32)
    fwd = jnp.where(diff >= 0, jnp.exp(jnp.maximum(diff, 0.0) * lgf), 0.0)
    bwd = jnp.where(diff <= 0, jnp.exp(jnp.maximum(-diff, 0.0) * lgb), 0.0)
    return fwd + bwd


def _ret_readout(o, gate, gain):
    ms = jnp.sum(o * o, axis=-1, keepdims=True) * (1.0 / RD)
    return (o * lax.rsqrt(ms + EPS) * gain * gate.astype(F32)).astype(BF16)


def _ret_kernel(lg_ref, q_ref, k_ref, v_ref, g_ref, s0_ref, gain_ref, o_ref, acc_ref, stf_ref, stb_ref):
    h = pl.program_id(1)
    lgf = lg_ref[0, h]
    lgb = lg_ref[1, h]
    c = RET_C
    n = L // c
    dmat = _decay_matrix(c, lgf, lgb)
    idx = lax.broadcasted_iota(jnp.int32, (c, LANE), 0).astype(F32)
    qdf = jnp.exp((idx + 1.0) * lgf)
    kdf = jnp.exp((c - 1.0 - idx) * lgf)
    qdb = jnp.exp((c - idx) * lgb)
    kdb = jnp.exp(idx * lgb)
    cdf = jnp.exp(jnp.full((1, LANE), float(c), F32) * lgf)
    cdb = jnp.exp(jnp.full((1, LANE), float(c), F32) * lgb)
    gain = gain_ref[...]

    dirs = ((stf_ref, qdf, kdf, cdf), (stb_ref, qdb, kdb, cdb))
    stf_ref[...] = s0_ref[0, 0, 0]
    stb_ref[...] = s0_ref[0, 0, 1]

    def visit(r0, direction, second):
        st_ref, qdec, kdec, cdec = dirs[direction]
        rows = pl.ds(pl.multiple_of(r0, c), c)
        q = q_ref[rows, :]
        k = k_ref[rows, :]
        v = v_ref[rows, :]
        cross = _dot(q, st_ref[...].astype(BF16)) * qdec
        if second:
            o_ref[rows, :] = _ret_readout(acc_ref[rows, :] + cross, g_ref[rows, :], gain)
        else:
            s = _dot_nt(q, k) * dmat
            acc_ref[rows, :] = _dot(s.astype(BF16), v) + cross
        kt = (k.astype(F32) * kdec).T.astype(BF16)
        st_ref[...] = st_ref[...] * cdec + _dot(kt, v)

    def body(second):
        def step(i, carry):
            visit(i * c, 0, second)
            visit((n - 1 - i) * c, 1, second)
            return carry
        return step

    lax.fori_loop(0, n // 2, body(False), 0, unroll=8)
    lax.fori_loop(n // 2, n, body(True), 0, unroll=8)


def _pair_attention(q, score_fn, value_fn):
    lane = lax.broadcasted_iota(jnp.int32, q.shape, 1)
    outs = []
    for hh in range(2):
        sel = (lane >= ND * hh) & (lane < ND * (hh + 1))
        qm = jnp.where(sel, q, jnp.zeros_like(q))
        scores = score_fn(qm, hh)
        m = functools.reduce(jnp.maximum, [jnp.max(s, axis=-1, keepdims=True) for s in scores])
        ps = [jnp.exp2(s - m) for s in scores]
        den = functools.reduce(jnp.add, [jnp.sum(p, axis=-1, keepdims=True) for p in ps])
        outs.append(value_fn([p.astype(BF16) for p in ps]) / den)
    return jnp.where(lane < ND, outs[0], outs[1])


def _ctx_mix_kernel(lg_ref, f_ref, p_ref, gain_ref, wf_ref, c_ref, s_ref, e_ref, fo_ref, ro_ref, no_ref):
    f = f_ref[...]
    pr = _dot(c_ref[...], f)
    qi = _dot(s_ref[...], f)
    fr = _dot(pr.astype(BF16), e_ref[:FW, :]) + _dot(qi.astype(BF16), e_ref[FW:, :])
    fo_ref[...] = _dot(fr.astype(BF16), wf_ref[...]).astype(BF16)
    for h in range(RH):
        sl = lambda c0: slice(c0 + LANE * h, c0 + LANE * (h + 1))
        dmat = _decay_matrix(LC, lg_ref[0, h], lg_ref[1, h])
        s = _dot_nt(p_ref[:, sl(C_RQ)], p_ref[:, sl(C_RK)]) * dmat
        o = _dot(s.astype(BF16), p_ref[:, sl(C_RV)])
        ro_ref[:, LANE * h:LANE * (h + 1)] = _ret_readout(o, p_ref[:, sl(C_RG)], gain_ref[:, LANE * h:LANE * (h + 1)])
    for pair in range(NPAIR):
        sl = lambda c0: slice(c0 + LANE * pair, c0 + LANE * (pair + 1))
        k = p_ref[:, sl(C_NK)]
        v = p_ref[:, sl(C_NV)]
        out = _pair_attention(p_ref[:, sl(C_NQ)], lambda qm, hh: [_dot_nt(qm, k)], lambda ps: _dot(ps[0], v))
        no_ref[:, LANE * pair:LANE * (pair + 1)] = out.astype(BF16)


def _na_key_start(r0):
    return min(max(r0 - 4, 0), GRID_H - NA_KR)


def _na_bias_kernel(base_ref, mask_ref, o_ref, tz_ref):
    for dr in range(NA_DR - 1):
        t = jnp.broadcast_to(base_ref[0, dr:dr + 1, :], (GRID_W, LANE))
        t = pltpu.roll(t, 0, 1, stride=1, stride_axis=0)
        tz_ref[dr] = t * LOG2E + mask_ref[...]
    tz_ref[NA_DR - 1] = jnp.full((GRID_W, LANE), NEG, F32)
    low_half = lax.broadcasted_iota(jnp.int32, (GRID_W, LANE), 1) < GRID_W
    for var, r0 in enumerate((0, 2 * NA_R, GRID_H - NA_R)):
        ks = _na_key_start(r0)
        for rl in range(NA_R):
            r = r0 + rl
            rs = min(max(r - 4, 0), GRID_H - 8)
            slot = [kr - r + 7 if rs <= kr < rs + 8 else NA_DR - 1 for kr in range(ks, ks + NA_KR)]
            for m in range(NA_KR // 2):
                o_ref[0, var, GRID_W * rl:GRID_W * (rl + 1), LANE * m:LANE * (m + 1)] = jnp.where(
                    low_half, tz_ref[slot[2 * m]], tz_ref[slot[2 * m + 1]])


def _natten_kernel(q_ref, k_ref, v_ref, kc_ref, vc_ref, bias_ref, o_ref, pa_ref, pb_ref):
    nq = NA_R * GRID_W
    nk = NA_KR * GRID_W
    kc = kc_ref[...]
    vc = vc_ref[...]
    lane = lax.broadcasted_iota(jnp.int32, (nq, LANE), 1)

    def rows(i):
        ks = jnp.clip(NA_R * i - 4, 0, GRID_H - NA_KR)
        return pl.ds(pl.multiple_of(ks * GRID_W, nq), nk), pl.ds(pl.multiple_of(i * nq, nq), nq)

    def probabilities(i, p_ref):
        i = jnp.minimum(i, NA_NB - 1)
        var = jnp.where(i == 0, 0, jnp.where(i == NA_NB - 1, 2, 1))
        krows, qrows = rows(i)
        q = q_ref[qrows, :]
        kw = k_ref[krows, :]
        for hh in range(2):
            qm = jnp.where((lane >= ND * hh) & (lane < ND * (hh + 1)), q, jnp.zeros_like(q))
            s_win = _dot_nt(qm, kw) + bias_ref[hh, var]
            s_ctx = _dot_nt(qm, kc)
            m = jnp.maximum(jnp.max(s_win, axis=-1, keepdims=True), jnp.max(s_ctx, axis=-1, keepdims=True))
            p_ref[hh, :, :nk] = jnp.exp2(s_win - m).astype(BF16)
            p_ref[hh, :, nk:] = jnp.exp2(s_ctx - m).astype(BF16)

    def values(i, p_ref):
        krows, qrows = rows(i)
        vals = jnp.concatenate([v_ref[krows, :], vc], axis=0)
        vals = jnp.concatenate([vals, jnp.ones_like(vals)], axis=1)
        o2 = _dot(p_ref[...].reshape(2 * nq, nk + LC), vals)
        outs = [o[:, :LANE] / o[:, LANE:] for o in (o2[:nq], o2[nq:])]
        o_ref[qrows, :] = jnp.where(lane < ND, outs[0], outs[1]).astype(BF16)

    probabilities(0, pa_ref)

    def step(j, carry):
        values(2 * j, pa_ref)
        probabilities(2 * j + 1, pb_ref)
        values(2 * j + 1, pb_ref)
        probabilities(2 * j + 2, pa_ref)
        return carry

    lax.fori_loop(0, NA_NB // 2, step, 0)


def _ffn_kernel(*refs, final, ctx_tile, mod_row):
    refs = list(refs)
    x_ref, f_ref, r_ref, n_ref = refs[:4]
    del refs[:4]
    if ctx_tile:
        xc_ref, fc_ref, rc_ref, nc_ref = refs[:4]
        del refs[:4]
    mods_ref, g_ref, wo_ref, sel_ref, w1_ref, w3_ref, w2_ref = refs[:7]
    del refs[:7]
    gf_ref = refs.pop(0) if final else None
    o_ref = refs.pop(0)
    oc_ref = refs.pop(0) if ctx_tile else None
    acc_ref, wor_ref = refs
    s = pl.program_id(0)

    @pl.when(s == 0)
    def _():
        wor_ref[...] = _dot(sel_ref[...], wo_ref[FW:FW + RH * RD, :]).astype(BF16)

    def tile(x, f, r, n, mod, out_ref):
        y = _dot(f.astype(BF16), wo_ref[:FW, :]) + _dot(r, wor_ref[...]) + _dot(n, wo_ref[FW + RH * RD:, :])
        x1 = x + mod[2:3, :] * y
        hb = (_rmsnorm(x1, g_ref[...]) * (1.0 + mod[4:5, :]) + mod[3:4, :]).astype(BF16)
        fc = 256
        for c in range(DFF // fc):
            a = _dot(hb, w1_ref[:, c * fc:(c + 1) * fc])
            b = _dot(hb, w3_ref[:, c * fc:(c + 1) * fc])
            acc_ref[:, c * fc:(c + 1) * fc] = (_silu(a) * b).astype(BF16)
        x2 = x1 + mod[5:6, :] * _dot(acc_ref[...], w2_ref[...])
        if final:
            x2 = _rmsnorm(x2, gf_ref[...])
        out_ref[...] = x2

    def latent_tile(t):
        f = jnp.concatenate([f_ref[:, r, :] for r in range(f_ref.shape[1])], axis=0)
        tile(x_ref[...], f, r_ref[...], n_ref[...], mods_ref[mod_row(t)], o_ref)

    if ctx_tile:
        @pl.when(s == 0)
        def _():
            tile(xc_ref[...], fc_ref[...], rc_ref[...], nc_ref[...], mods_ref[B], oc_ref)

        @pl.when(s > 0)
        def _():
            latent_tile(s - 1)
    else:
        latent_tile(s)


def _params(sem):
    return pltpu.CompilerParams(dimension_semantics=sem, vmem_limit_bytes=VMEM_LIMIT)


def _const_spec(shape):
    nd = len(shape)
    return pl.BlockSpec(shape, lambda *_: (0,) * nd, pipeline_mode=pl.Buffered(1))


def _smem_spec():
    return pl.BlockSpec(memory_space=pltpu.SMEM)


def _ada(cvec, w_ada, b_ada):
    tn = 2048
    return pl.pallas_call(
        _ada_kernel,
        grid=(DEPTH, 6 * D // tn),
        in_specs=[pl.BlockSpec((8, D), lambda i, j: (0, 0)),
                  pl.BlockSpec((1, D, tn), lambda i, j: (i, 0, j)),
                  pl.BlockSpec((1, 1, tn), lambda i, j: (i, 0, j))],
        out_specs=pl.BlockSpec((1, 8, tn), lambda i, j: (i, 0, j)),
        out_shape=jax.ShapeDtypeStruct((DEPTH, 8, 6 * D), F32),
        compiler_params=_params(("arbitrary", "arbitrary")),
        name="ada",
    )(cvec, w_ada, b_ada.reshape(DEPTH, 1, 6 * D))


def _mod_row(tm):
    per_batch = L // tm
    return lambda tile: tile // per_batch


def _layer_spec(shape, layer):
    nd = len(shape)
    return pl.BlockSpec((None,) + tuple(shape), lambda *_: (layer,) + (0,) * nd, pipeline_mode=pl.Buffered(1))


def _proj(x2d, xc2d, mods, g, w_all, layer, sel_qk, sel_v, rope_tabs, lg):
    rows, crows = x2d.shape[0], xc2d.shape[0]
    assert crows == B * LC
    tm = TM_PROJ
    per_batch = L // tm
    tile = lambda i: jnp.maximum(i - 1, 0)
    return pl.pallas_call(
        functools.partial(_proj_kernel, mod_row=_mod_row(tm)),
        grid=(rows // tm + 1,),
        in_specs=[pl.BlockSpec((tm, D), lambda i: (tile(i), 0)),
                  _const_spec((crows, D)),
                  _const_spec((8, 6, D)),
                  _const_spec((1, D)),
                  _layer_spec((D, W_IN), layer),
                  _const_spec((RH * RD, RH * LANE)),
                  _const_spec((RH * RD, RH * LANE)),
                  pl.BlockSpec((tm, LANE), lambda i: (tile(i) % per_batch, 0)),
                  pl.BlockSpec((tm, LANE), lambda i: (tile(i) % per_batch, 0)),
                  _smem_spec()],
        out_specs=[pl.BlockSpec((tm, FW), lambda i: (tile(i), 0)),
                   pl.BlockSpec((tm, NP), lambda i: (tile(i), 0)),
                   pl.BlockSpec((crows, FW), lambda i: (0, 0)),
                   pl.BlockSpec((crows, NP), lambda i: (0, 0)),
                   pl.BlockSpec((B, RH, 2, LANE, LANE), lambda i: (0, 0, 0, 0, 0))],
        out_shape=[jax.ShapeDtypeStruct((rows, FW), BF16), jax.ShapeDtypeStruct((rows, NP), BF16),
                   jax.ShapeDtypeStruct((crows, FW), BF16), jax.ShapeDtypeStruct((crows, NP), BF16),
                   jax.ShapeDtypeStruct((B, RH, 2, LANE, LANE), F32)],
        scratch_shapes=[pltpu.VMEM((D, 4 * RH * LANE), BF16)],
        compiler_params=_params(("arbitrary",)),
        name="proj",
    )(x2d, xc2d, mods, g.reshape(1, D), w_all, sel_qk, sel_v, *rope_tabs, lg)


def _fourier(f2d, wf, a1, m3, e):
    y = pl.pallas_call(
        _four1_kernel,
        grid=(B, DFT_N2 // F1_R),
        in_specs=[_const_spec((2 * DFT_N1 * F1_R, DFT_N1 * F1_R)),
                  pl.BlockSpec((1, DFT_N1, F1_R, FW), lambda b, j: (b, 0, j, 0))],
        out_specs=pl.BlockSpec((1, 2 * DFT_N1, F1_R, FW), lambda b, j: (b, 0, j, 0)),
        out_shape=jax.ShapeDtypeStruct((B, 2 * DFT_N1, DFT_N2, FW), BF16),
        compiler_params=_params(("arbitrary", "arbitrary")),
        name="four1",
    )(a1, f2d.reshape(B, DFT_N1, DFT_N2, FW))
    out = pl.pallas_call(
        _four3_kernel,
        grid=(DFT_N1 // F3_K,),
        in_specs=[pl.BlockSpec((F3_K, 2 * DFT_N2, 2 * DFT_N2), lambda k: (k, 0, 0)),
                  pl.BlockSpec((B, 2, F3_K, DFT_N2, FW), lambda k: (0, 0, k, 0, 0)),
                  _const_spec((2 * FW, FW)),
                  _const_spec((FW, FW))],
        out_specs=pl.BlockSpec((B, F3_K, DFT_N2, FW), lambda k: (0, k, 0, 0)),
        out_shape=jax.ShapeDtypeStruct((B, DFT_N1, DFT_N2, FW), F32),
        compiler_params=_params(("arbitrary",)),
        name="four3",
    )(m3, y.reshape(B, 2, DFT_N1, DFT_N2, FW), e, wf)
    return out.reshape(B * DFT_N1, DFT_N2, FW)


def _retention(p, s0, lg, gain):
    col = lambda c0: (lambda b, h: (b, c0 // LANE + h))
    return pl.pallas_call(
        _ret_kernel,
        grid=(B, RH),
        in_specs=[_smem_spec(),
                  pl.BlockSpec((L, LANE), col(C_RQ)),
                  pl.BlockSpec((L, LANE), col(C_RK)),
                  pl.BlockSpec((L, LANE), col(C_RV)),
                  pl.BlockSpec((L, LANE), col(C_RG)),
                  pl.BlockSpec((1, 1, 2, LANE, LANE), lambda b, h: (b, h, 0, 0, 0)),
                  pl.BlockSpec((1, LANE), lambda b, h: (0, h))],
        out_specs=pl.BlockSpec((L, LANE), lambda b, h: (b, h)),
        out_shape=jax.ShapeDtypeStruct((B * L, RH * LANE), BF16),
        scratch_shapes=[pltpu.VMEM((L, LANE), F32), pltpu.VMEM((LANE, LANE), F32), pltpu.VMEM((LANE, LANE), F32)],
        compiler_params=_params(("arbitrary", "arbitrary")),
        name="ret",
    )(lg, p, p, p, p, s0, gain)


def _ctx_mix(fc, pc, lg, gain, wf, c256, s256, e):
    return pl.pallas_call(
        _ctx_mix_kernel,
        grid=(B,),
        in_specs=[_smem_spec(),
                  pl.BlockSpec((LC, FW), lambda b: (b, 0)),
                  pl.BlockSpec((LC, NP), lambda b: (b, 0)),
                  _const_spec((1, RH * LANE)),
                  _const_spec((FW, FW)),
                  _const_spec((LC, LC)),
                  _const_spec((LC, LC)),
                  _const_spec((2 * FW, FW))],
        out_specs=[pl.BlockSpec((LC, FW), lambda b: (b, 0)),
                   pl.BlockSpec((LC, RH * LANE), lambda b: (b, 0)),
                   pl.BlockSpec((LC, NH * ND), lambda b: (b, 0))],
        out_shape=[jax.ShapeDtypeStruct((B * LC, FW), BF16),
                   jax.ShapeDtypeStruct((B * LC, RH * LANE), BF16),
                   jax.ShapeDtypeStruct((B * LC, NH * ND), BF16)],
        compiler_params=_params(("arbitrary",)),
        name="ctx_mix",
    )(lg, fc, pc, gain, wf, c256, s256, e)


def _na_bias(base, mask):
    nq, nk = NA_R * GRID_W, NA_KR * GRID_W
    return pl.pallas_call(
        _na_bias_kernel,
        grid=(base.shape[0],),
        in_specs=[pl.BlockSpec((1, NA_DR, LANE), lambda h: (h, 0, 0)),
                  _const_spec((GRID_W, LANE))],
        out_specs=pl.BlockSpec((1, 3, nq, nk), lambda h: (h, 0, 0, 0)),
        out_shape=jax.ShapeDtypeStruct((base.shape[0], 3, nq, nk), F32),
        scratch_shapes=[pltpu.VMEM((NA_DR, GRID_W, LANE), F32)],
        compiler_params=_params(("arbitrary",)),
        name="na_bias",
    )(base, mask)


def _natten(p, pc, bias, layer):
    nq, nk = NA_R * GRID_W, NA_KR * GRID_W
    return pl.pallas_call(
        _natten_kernel,
        grid=(B, NPAIR),
        in_specs=[pl.BlockSpec((L, LANE), lambda b, pr: (b, C_NQ // LANE + pr)),
                  pl.BlockSpec((L, LANE), lambda b, pr: (b, C_NK // LANE + pr)),
                  pl.BlockSpec((L, LANE), lambda b, pr: (b, C_NV // LANE + pr)),
                  pl.BlockSpec((LC, LANE), lambda b, pr: (b, C_NK // LANE + pr)),
                  pl.BlockSpec((LC, LANE), lambda b, pr: (b, C_NV // LANE + pr)),
                  pl.BlockSpec((2, 3, nq, nk), lambda b, pr: (NPAIR * layer + pr, 0, 0, 0))],
        out_specs=pl.BlockSpec((L, LANE), lambda b, pr: (b, pr)),
        out_shape=jax.ShapeDtypeStruct((B * L, NH * ND), BF16),
        scratch_shapes=[pltpu.VMEM((2, nq, nk + LC), BF16)] * 2,
        compiler_params=_params(("arbitrary", "arbitrary")),
        name="natten",
    )(p, p, p, pc, pc, bias)


def _ffn(x2d, four, ret, na, ctx_acts, mods, g, wo_all, sel_v_t, w1_all, w3_all, w2_all, layer, g_final):
    rows = x2d.shape[0]
    final = g_final is not None
    ctx_tile = ctx_acts is not None
    off = 1 if ctx_tile else 0
    per_batch = L // TM
    tile = lambda i: jnp.maximum(i - off, 0)
    row = lambda w: pl.BlockSpec((TM, w), lambda i: (tile(i), 0))
    four_spec = pl.BlockSpec((DFT_N1, TM // DFT_N1, FW), lambda i: (tile(i) // per_batch, tile(i) % per_batch, 0))
    in_specs = [row(D), four_spec, row(RH * LANE), row(NH * ND)]
    args = [x2d, four, ret, na]
    if ctx_tile:
        assert all(a.shape[0] == TM for a in ctx_acts)
        in_specs += [_const_spec(a.shape) for a in ctx_acts]
        args += list(ctx_acts)
    in_specs += [_const_spec((8, 6, D)), _const_spec((1, D)),
                 _layer_spec((D, D), layer), _const_spec((RH * LANE, RH * RD)),
                 _layer_spec((D, DFF), layer), _layer_spec((D, DFF), layer), _layer_spec((DFF, D), layer)]
    args += [mods, g.reshape(1, D), wo_all, sel_v_t, w1_all, w3_all, w2_all]
    if final:
        in_specs.append(_const_spec((1, D)))
        args.append(g_final.reshape(1, D))
    out_specs = [row(D)]
    out_shape = [jax.ShapeDtypeStruct((rows, D), F32)]
    if ctx_tile:
        out_specs.append(pl.BlockSpec((TM, D), lambda i: (0, 0)))
        out_shape.append(jax.ShapeDtypeStruct((TM, D), F32))
    outs = pl.pallas_call(
        functools.partial(_ffn_kernel, final=final, ctx_tile=ctx_tile, mod_row=_mod_row(TM)),
        grid=(rows // TM + off,),
        in_specs=in_specs,
        out_specs=out_specs,
        out_shape=out_shape,
        scratch_shapes=[pltpu.VMEM((TM, DFF), BF16), pltpu.VMEM((RH * LANE, D), BF16)],
        compiler_params=_params(("arbitrary",)),
        name="ffn_final" if final else "ffn",
    )(*args)
    return outs if ctx_tile else outs[0]


def _take_padded(a, src, axis):
    pieces = []
    i, n = 0, len(src)
    while i < n:
        j = i + 1
        if src[i] < 0:
            while j < n and src[j] < 0:
                j += 1
            shape = list(a.shape)
            shape[axis] = j - i
            pieces.append(jnp.zeros(shape, a.dtype))
        else:
            while j < n and src[j] == src[j - 1] + 1:
                j += 1
            pieces.append(lax.slice_in_dim(a, int(src[i]), int(src[j - 1]) + 1, axis=axis))
        i = j
    return jnp.concatenate(pieces, axis=axis)


def kernel(x, c, ctx, c_ctx, w_ada, b_ada, g_mix, w_in, ret_decay_logit, ret_norm_g, w_four,
           na_rpb, w_out, g_ffn, w1, w3, w2, g_final):
    src_g = _gain_sources()
    rope_tabs = tuple(jnp.asarray(t) for t in _rope_tables())
    a1, m3, e, c256, s256 = (jnp.asarray(t).astype(BF16) for t in _dft_tables())
    sel_qk, sel_v = (jnp.asarray(t).astype(BF16) for t in _head_selectors())
    sel_v_t = sel_v.T
    tz_src, tz_mask = _na_toeplitz_tables()
    col_scale = np.ones((1, 1, W_IN), np.float32)
    col_scale[..., W_NQ:W_NK] = LOG2E * ND ** -0.5
    w_in_b = (w_in * jnp.asarray(col_scale)).astype(BF16)
    w_out_b = w_out.astype(BF16)
    w1b, w3b, w2b = w1.astype(BF16), w3.astype(BF16), w2.astype(BF16)

    tz_base = jnp.pad(_take_padded(na_rpb.astype(F32).reshape(DEPTH * NH, 15, 31), tz_src, 2), ((0, 0), (0, 1), (0, 0)))
    na_bias = _na_bias(tz_base, jnp.asarray(tz_mask))

    cvec = jnp.zeros((8, D), F32).at[0:B].set(c).at[B].set(c_ctx)
    mods_all = _ada(cvec, w_ada, b_ada).reshape(DEPTH, 8, 6, D)

    xl = x.reshape(B * L, D)
    xc = ctx.reshape(B * LC, D)
    for i in range(DEPTH):
        last = i == DEPTH - 1
        mods = mods_all[i]
        gain = _take_padded(ret_norm_g[i], src_g, 0).reshape(1, RH * LANE)
        wfour = w_four[i].astype(BF16)
        lg = jax.nn.log_sigmoid(ret_decay_logit[i].astype(F32))

        f_l, p_l, f_c, p_c, s0 = _proj(xl, xc, mods, g_mix[i], w_in_b, i, sel_qk, sel_v, rope_tabs, lg)
        four_l = _fourier(f_l, wfour, a1, m3, e)
        ret_l = _retention(p_l, s0, lg, gain)
        na_l = _natten(p_l, p_c, na_bias, i)
        if last:
            xl = _ffn(xl, four_l, ret_l, na_l, None, mods, g_ffn[i], w_out_b, sel_v_t, w1b, w3b, w2b, i, g_final)
        else:
            ctx_acts = (xc,) + tuple(_ctx_mix(f_c, p_c, lg, gain, wfour, c256, s256, e))
            xl, xc = _ffn(xl, four_l, ret_l, na_l, ctx_acts, mods, g_ffn[i], w_out_b, sel_v_t, w1b, w3b, w2b, i, None)
    return xl.reshape(B, L, D)
```

```python
import functools

import numpy as np
import jax
import jax.numpy as jnp
from jax import lax
from jax.experimental import pallas as pl
from jax.experimental.pallas import tpu as pltpu

F32 = jnp.float32
BF16 = jnp.bfloat16

D = 1024
B = 2
L = 8192
LC = 256
DEPTH = 2
GRID_W = 64
GRID_H = L // GRID_W
FW = 256
FGW = 64
RH = 4
RD = 96
NH = 6
ND = 64
NPAIR = NH // 2
DFF = 2816
EPS = 1e-6
NEG = -1e30
LOG2E = float(np.log2(np.e))
LANE = 128

C_RQ, C_RK, C_RV, C_RG = 0, 512, 1024, 1536
C_NQ, C_NK, C_NV = 2048, 2432, 2816
NP = 3200
W_RQ, W_RK, W_RV, W_RG = 256, 640, 1024, 1408
W_NQ, W_NK, W_NV = 1792, 2176, 2560
W_IN = 2944

TM = 512
TM_PROJ = 1024
RET_C = 256
NA_R = 4
NA_KR = NA_R + 8
NA_NB = GRID_H // NA_R
DFT_N1 = 64
DFT_N2 = 128
F1_R = 16
F3_K = 8
VMEM_LIMIT = 56 * 1024 * 1024


def _qk_lane_dims():
    m = np.full((LANE,), -1, np.int64)
    m[0:24] = np.arange(0, 24)
    m[24:48] = np.arange(48, 72)
    m[64:88] = np.arange(24, 48)
    m[88:112] = np.arange(72, 96)
    return m


def _v_lane_dims():
    m = np.full((LANE,), -1, np.int64)
    m[:RD] = np.arange(RD)
    return m


def _head_selectors():
    sels = []
    for lanes in (_qk_lane_dims(), _v_lane_dims()):
        sel = np.zeros((RH * RD, RH * LANE), np.float32)
        ok = np.nonzero(lanes >= 0)[0]
        for h in range(RH):
            sel[RD * h + lanes[ok], LANE * h + ok] = 1.0
        sels.append(sel)
    return sels


def _gain_sources():
    src = np.full((RH * LANE,), -1, np.int64)
    for h in range(RH):
        src[LANE * h: LANE * h + RD] = RD * h + np.arange(RD)
    return src


def _rope_tables():
    pos = np.arange(L)
    prow, pcol = pos // GRID_W, pos % GRID_W
    half = RD // 4
    inv = 10000.0 ** (-np.arange(half, dtype=np.float64) / half)
    ar = prow[:, None] * inv[None, :]
    ac = pcol[:, None] * inv[None, :]
    cos = np.ones((L, LANE), np.float64)
    sin = np.zeros((L, LANE), np.float64)
    for off, sign in ((0, -1.0), (64, 1.0)):
        cos[:, off:off + 24] = np.cos(ar)
        cos[:, off + 24:off + 48] = np.cos(ac)
        sin[:, off:off + 24] = sign * np.sin(ar)
        sin[:, off + 24:off + 48] = sign * np.sin(ac)
    return cos.astype(np.float32), sin.astype(np.float32)


def _dft_tables():
    k1 = np.arange(DFT_N1)
    a = 2 * np.pi * ((k1[:, None] * k1[None, :]) % DFT_N1) / DFT_N1
    a1 = np.kron(np.concatenate([np.cos(a), -np.sin(a)], axis=0), np.eye(F1_R))
    k2 = np.arange(DFT_N2)
    l2 = np.arange(DFT_N2)
    kk = k1[:, None, None] + DFT_N1 * k2[None, :, None]
    ang = 2 * np.pi * ((kk * l2[None, None, :]) % L) / L
    ct, st = np.cos(ang) / np.sqrt(L), np.sin(ang) / np.sqrt(L)
    m3 = np.concatenate([np.concatenate([ct, st], axis=2), np.concatenate([st, -ct], axis=2)], axis=1)
    c = np.arange(FW)
    same = (c[:, None] // FGW) == (c[None, :] // FGW)
    ac = 2 * np.pi * (((c[:, None] % FGW) * (c[None, :] % FGW)) % FGW) / FGW
    e = np.concatenate([np.where(same, np.cos(ac), 0.0), -np.where(same, np.sin(ac), 0.0)], axis=0) / 8.0
    p = np.arange(LC)
    ap = 2 * np.pi * ((p[:, None] * p[None, :]) % LC) / LC
    return tuple(t.astype(np.float32) for t in (a1, m3, e, np.cos(ap) / 16.0, np.sin(ap) / 16.0))


NA_DR = 16


def _na_toeplitz_tables():
    src = np.full((LANE,), -1, np.int64)
    src[0:16] = np.arange(0, 16) + 15
    src[113:128] = np.arange(113, 128) - 113
    src[49:80] = np.arange(49, 80) - 49
    qc = np.arange(GRID_W)[:, None]
    kc = np.arange(LANE)[None, :] % GRID_W
    ws = np.clip(qc - 8, 0, GRID_W - 16)
    mask = np.where((kc >= ws) & (kc < ws + 16), 0.0, NEG).astype(np.float32)
    return src, mask


def _dot(a, b):
    return jnp.dot(a, b, preferred_element_type=F32)


def _dot_nt(a, b):
    return lax.dot_general(a, b, (((1,), (1,)), ((), ())), preferred_element_type=F32)


def _silu(x):
    return x * jax.nn.sigmoid(x)


def _rmsnorm(x, g):
    return x * lax.rsqrt(jnp.mean(x * x, axis=-1, keepdims=True) + EPS) * g


def _ada_kernel(c_ref, w_ref, b_ref, o_ref):
    s = _silu(c_ref[...]).astype(BF16)
    o_ref[0] = _dot(s, w_ref[0].astype(BF16)) + b_ref[0]


def _proj_kernel(x_ref, xc_ref, mods_ref, g_ref, w_ref, pqk_ref, pv_ref, cos_ref, sin_ref, lg_ref,
                 f_ref, o_ref, fc_ref, oc_ref, s0_ref, wret_ref, *, mod_row):
    s = pl.program_id(0)

    def project(x, mod, rope, f_out, o_out):
        hb = (_rmsnorm(x, g_ref[...]) * (1.0 + mod[1:2, :]) + mod[0:1, :]).astype(BF16)
        f_out[...] = _dot(hb, w_ref[:, :FW]).astype(BF16)
        k_scale = RD ** -0.5
        for c0, scale in ((C_RQ, None), (C_RK, k_scale)):
            t4 = _dot(hb, wret_ref[:, c0:c0 + RH * LANE])
            for hh in range(RH):
                t = t4[:, LANE * hh:LANE * (hh + 1)]
                if rope:
                    t = t * cos_ref[...] + pltpu.roll(t, 64, 1) * sin_ref[...]
                if scale is not None:
                    t = t * scale
                o_out[:, c0 + LANE * hh:c0 + LANE * (hh + 1)] = t.astype(BF16)
        o_out[:, C_RV:C_RG] = _dot(hb, wret_ref[:, C_RV:C_RG]).astype(BF16)
        o_out[:, C_RG:C_NQ] = _silu(_dot(hb, wret_ref[:, C_RG:C_NQ])).astype(BF16)
        o_out[:, C_NQ:] = _dot(hb, w_ref[:, W_NQ:]).astype(BF16)

    @pl.when(s == 0)
    def _():
        for slot, (src0, sel_ref) in enumerate(((W_RQ, pqk_ref), (W_RK, pqk_ref), (W_RV, pv_ref), (W_RG, pv_ref))):
            wret_ref[:, RH * LANE * slot:RH * LANE * (slot + 1)] = _dot(
                w_ref[:, src0:src0 + RH * RD], sel_ref[...]).astype(BF16)
        project(xc_ref[...], mods_ref[B], False, fc_ref, oc_ref)
        idx = lax.broadcasted_iota(jnp.int32, (LC, LANE), 0).astype(F32)
        for b in range(B):
            for h in range(RH):
                k = oc_ref[LC * b:LC * (b + 1), C_RK + LANE * h:C_RK + LANE * (h + 1)].astype(F32)
                v = oc_ref[LC * b:LC * (b + 1), C_RV + LANE * h:C_RV + LANE * (h + 1)]
                wf = jnp.exp((LC - 1.0 - idx) * lg_ref[0, h])
                wb = jnp.exp(idx * lg_ref[1, h])
                s0_ref[b, h, 0] = _dot((k * wf).T.astype(BF16), v)
                s0_ref[b, h, 1] = _dot((k * wb).T.astype(BF16), v)

    @pl.when(s > 0)
    def _():
        project(x_ref[...], mods_ref[mod_row(s - 1)], True, f_ref, o_ref)


def _four1_kernel(a_ref, x_ref, y_ref):
    x = x_ref[0].reshape(DFT_N1 * F1_R, FW)
    y = _dot(a_ref[...], x).astype(BF16)
    y_ref[0] = y.reshape(2 * DFT_N1, F1_R, FW)


def _four3_kernel(m_ref, y_ref, e_ref, wf_ref, o_ref):
    ew = _dot(e_ref[...], wf_ref[...]).astype(BF16)
    for b in range(B):
        for kk in range(F3_K):
            y = jnp.concatenate([y_ref[b, 0, kk], y_ref[b, 1, kk]], axis=0)
            z = _dot(m_ref[kk], y).astype(BF16)
            o_ref[b, kk] = _dot(jnp.concatenate([z[:DFT_N2], z[DFT_N2:]], axis=1), ew)


def _decay_matrix(n, lgf, lgb):
    ii = lax.broadcasted_iota(jnp.int32, (n, n), 0)
    jj = lax.broadcasted_iota(jnp.int32, (n, n), 1)
    diff = (ii - jj).astype(F32)
    fwd = jnp.where(diff >= 0, jnp.exp(jnp.maximum(diff, 0.0) * lgf), 0.0)
    bwd = jnp.where(diff <= 0, jnp.exp(jnp.maximum(-diff, 0.0) * lgb), 0.0)
    return fwd + bwd


def _ret_readout(o, gate, gain):
    ms = jnp.sum(o * o, axis=-1, keepdims=True) * (1.0 / RD)
    return (o * lax.rsqrt(ms + EPS) * gain * gate.astype(F32)).astype(BF16)


def _ret_kernel(lg_ref, q_ref, k_ref, v_ref, g_ref, s0_ref, gain_ref, o_ref, acc_ref, stf_ref, stb_ref):
    h = pl.program_id(1)
    lgf = lg_ref[0, h]
    lgb = lg_ref[1, h]
    c = RET_C
    n = L // c
    dmat = _decay_matrix(c, lgf, lgb)
    idx = lax.broadcasted_iota(jnp.int32, (c, LANE), 0).astype(F32)
    qdf = jnp.exp((idx + 1.0) * lgf)
    kdf = jnp.exp((c - 1.0 - idx) * lgf)
    qdb = jnp.exp((c - idx) * lgb)
    kdb = jnp.exp(idx * lgb)
    cdf = jnp.exp(jnp.full((1, LANE), float(c), F32) * lgf)
    cdb = jnp.exp(jnp.full((1, LANE), float(c), F32) * lgb)
    gain = gain_ref[...]

    dirs = ((stf_ref, qdf, kdf, cdf), (stb_ref, qdb, kdb, cdb))
    stf_ref[...] = s0_ref[0, 0, 0]
    stb_ref[...] = s0_ref[0, 0, 1]

    def visit(r0, direction, second):
        st_ref, qdec, kdec, cdec = dirs[direction]
        rows = pl.ds(pl.multiple_of(r0, c), c)
        q = q_ref[rows, :]
        k = k_ref[rows, :]
        v = v_ref[rows, :]
        cross = _dot(q, st_ref[...].astype(BF16)) * qdec
        if second:
            o_ref[rows, :] = _ret_readout(acc_ref[rows, :] + cross, g_ref[rows, :], gain)
        else:
            s = _dot_nt(q, k) * dmat
            acc_ref[rows, :] = _dot(s.astype(BF16), v) + cross
        kt = (k.astype(F32) * kdec).T.astype(BF16)
        st_ref[...] = st_ref[...] * cdec + _dot(kt, v)

    def body(second):
        def step(i, carry):
            visit(i * c, 0, second)
            visit((n - 1 - i) * c, 1, second)
            return carry
        return step

    lax.fori_loop(0, n // 2, body(False), 0, unroll=8)
    lax.fori_loop(n // 2, n, body(True), 0, unroll=8)


def _pair_attention(q, score_fn, value_fn):
    lane = lax.broadcasted_iota(jnp.int32, q.shape, 1)
    outs = []
    for hh in range(2):
        sel = (lane >= ND * hh) & (lane < ND * (hh + 1))
        qm = jnp.where(sel, q, jnp.zeros_like(q))
        scores = score_fn(qm, hh)
        m = functools.reduce(jnp.maximum, [jnp.max(s, axis=-1, keepdims=True) for s in scores])
        ps = [jnp.exp2(s - m) for s in scores]
        den = functools.reduce(jnp.add, [jnp.sum(p, axis=-1, keepdims=True) for p in ps])
        outs.append(value_fn([p.astype(BF16) for p in ps]) / den)
    return jnp.where(lane < ND, outs[0], outs[1])


def _ctx_mix_kernel(lg_ref, f_ref, p_ref, gain_ref, wf_ref, c_ref, s_ref, e_ref, fo_ref, ro_ref, no_ref):
    f = f_ref[...]
    pr = _dot(c_ref[...], f)
    qi = _dot(s_ref[...], f)
    fr = _dot(pr.astype(BF16), e_ref[:FW, :]) + _dot(qi.astype(BF16), e_ref[FW:, :])
    fo_ref[...] = _dot(fr.astype(BF16), wf_ref[...]).astype(BF16)
    for h in range(RH):
        sl = lambda c0: slice(c0 + LANE * h, c0 + LANE * (h + 1))
        dmat = _decay_matrix(LC, lg_ref[0, h], lg_ref[1, h])
        s = _dot_nt(p_ref[:, sl(C_RQ)], p_ref[:, sl(C_RK)]) * dmat
        o = _dot(s.astype(BF16), p_ref[:, sl(C_RV)])
        ro_ref[:, LANE * h:LANE * (h + 1)] = _ret_readout(o, p_ref[:, sl(C_RG)], gain_ref[:, LANE * h:LANE * (h + 1)])
    for pair in range(NPAIR):
        sl = lambda c0: slice(c0 + LANE * pair, c0 + LANE * (pair + 1))
        k = p_ref[:, sl(C_NK)]
        v = p_ref[:, sl(C_NV)]
        out = _pair_attention(p_ref[:, sl(C_NQ)], lambda qm, hh: [_dot_nt(qm, k)], lambda ps: _dot(ps[0], v))
        no_ref[:, LANE * pair:LANE * (pair + 1)] = out.astype(BF16)


def _na_key_start(r0):
    return min(max(r0 - 4, 0), GRID_H - NA_KR)


def _na_bias_kernel(base_ref, mask_ref, o_ref, tz_ref):
    for dr in range(NA_DR - 1):
        t = jnp.broadcast_to(base_ref[0, dr:dr + 1, :], (GRID_W, LANE))
        t = pltpu.roll(t, 0, 1, stride=1, stride_axis=0)
        tz_ref[dr] = t * LOG2E + mask_ref[...]
    tz_ref[NA_DR - 1] = jnp.full((GRID_W, LANE), NEG, F32)
    low_half = lax.broadcasted_iota(jnp.int32, (GRID_W, LANE), 1) < GRID_W
    for var, r0 in enumerate((0, 2 * NA_R, GRID_H - NA_R)):
        ks = _na_key_start(r0)
        for rl in range(NA_R):
            r = r0 + rl
            rs = min(max(r - 4, 0), GRID_H - 8)
            slot = [kr - r + 7 if rs <= kr < rs + 8 else NA_DR - 1 for kr in range(ks, ks + NA_KR)]
            for m in range(NA_KR // 2):
                o_ref[0, var, GRID_W * rl:GRID_W * (rl + 1), LANE * m:LANE * (m + 1)] = jnp.where(
                    low_half, tz_ref[slot[2 * m]], tz_ref[slot[2 * m + 1]])


def _natten_kernel(q_ref, k_ref, v_ref, kc_ref, vc_ref, bias_ref, o_ref, pa_ref, pb_ref):
    nq = NA_R * GRID_W
    nk = NA_KR * GRID_W
    kc = kc_ref[...]
    vc = vc_ref[...]
    lane = lax.broadcasted_iota(jnp.int32, (nq, LANE), 1)

    def rows(i):
        ks = jnp.clip(NA_R * i - 4, 0, GRID_H - NA_KR)
        return pl.ds(pl.multiple_of(ks * GRID_W, nq), nk), pl.ds(pl.multiple_of(i * nq, nq), nq)

    def probabilities(i, p_ref):
        i = jnp.minimum(i, NA_NB - 1)
        var = jnp.where(i == 0, 0, jnp.where(i == NA_NB - 1, 2, 1))
        krows, qrows = rows(i)
        q = q_ref[qrows, :]
        kw = k_ref[krows, :]
        for hh in range(2):
            qm = jnp.where((lane >= ND * hh) & (lane < ND * (hh + 1)), q, jnp.zeros_like(q))
            s_win = _dot_nt(qm, kw) + bias_ref[hh, var]
            s_ctx = _dot_nt(qm, kc)
            m = jnp.maximum(jnp.max(s_win, axis=-1, keepdims=True), jnp.max(s_ctx, axis=-1, keepdims=True))
            p_ref[hh, :, :nk] = jnp.exp2(s_win - m).astype(BF16)
            p_ref[hh, :, nk:] = jnp.exp2(s_ctx - m).astype(BF16)

    def values(i, p_ref):
        krows, qrows = rows(i)
        vals = jnp.concatenate([v_ref[krows, :], vc], axis=0)
        vals = jnp.concatenate([vals, jnp.ones_like(vals)], axis=1)
        o2 = _dot(p_ref[...].reshape(2 * nq, nk + LC), vals)
        outs = [o[:, :LANE] / o[:, LANE:] for o in (o2[:nq], o2[nq:])]
        o_ref[qrows, :] = jnp.where(lane < ND, outs[0], outs[1]).astype(BF16)

    probabilities(0, pa_ref)

    def step(j, carry):
        values(2 * j, pa_ref)
        probabilities(2 * j + 1, pb_ref)
        values(2 * j + 1, pb_ref)
        probabilities(2 * j + 2, pa_ref)
        return carry

    lax.fori_loop(0, NA_NB // 2, step, 0, unroll=4)


def _ffn_kernel(*refs, final, ctx_tile, mod_row):
    refs = list(refs)
    x_ref, f_ref, r_ref, n_ref = refs[:4]
    del refs[:4]
    if ctx_tile:
        xc_ref, fc_ref, rc_ref, nc_ref = refs[:4]
        del refs[:4]
    mods_ref, g_ref, wo_ref, sel_ref, w1_ref, w3_ref, w2_ref = refs[:7]
    del refs[:7]
    gf_ref = refs.pop(0) if final else None
    o_ref = refs.pop(0)
    oc_ref = refs.pop(0) if ctx_tile else None
    acc_ref, wor_ref = refs
    s = pl.program_id(0)

    @pl.when(s == 0)
    def _():
        wor_ref[...] = _dot(sel_ref[...], wo_ref[FW:FW + RH * RD, :]).astype(BF16)

    def tile(x, f, r, n, mod, out_ref):
        y = _dot(f.astype(BF16), wo_ref[:FW, :]) + _dot(r, wor_ref[...]) + _dot(n, wo_ref[FW + RH * RD:, :])
        x1 = x + mod[2:3, :] * y
        hb = (_rmsnorm(x1, g_ref[...]) * (1.0 + mod[4:5, :]) + mod[3:4, :]).astype(BF16)
        fc = 256
        for c in range(DFF // fc):
            a = _dot(hb, w1_ref[:, c * fc:(c + 1) * fc])
            b = _dot(hb, w3_ref[:, c * fc:(c + 1) * fc])
            acc_ref[:, c * fc:(c + 1) * fc] = (_silu(a) * b).astype(BF16)
        x2 = x1 + mod[5:6, :] * _dot(acc_ref[...], w2_ref[...])
        if final:
            x2 = _rmsnorm(x2, gf_ref[...])
        out_ref[...] = x2

    def latent_tile(t):
        f = jnp.concatenate([f_ref[:, r, :] for r in range(f_ref.shape[1])], axis=0)
        tile(x_ref[...], f, r_ref[...], n_ref[...], mods_ref[mod_row(t)], o_ref)

    if ctx_tile:
        @pl.when(s == 0)
        def _():
            tile(xc_ref[...], fc_ref[...], rc_ref[...], nc_ref[...], mods_ref[B], oc_ref)

        @pl.when(s > 0)
        def _():
            latent_tile(s - 1)
    else:
        latent_tile(s)


def _params(sem):
    return pltpu.CompilerParams(dimension_semantics=sem, vmem_limit_bytes=VMEM_LIMIT)


def _const_spec(shape):
    nd = len(shape)
    return pl.BlockSpec(shape, lambda *_: (0,) * nd, pipeline_mode=pl.Buffered(1))


def _smem_spec():
    return pl.BlockSpec(memory_space=pltpu.SMEM)


def _ada(cvec, w_ada, b_ada):
    tn = 2048
    return pl.pallas_call(
        _ada_kernel,
        grid=(DEPTH, 6 * D // tn),
        in_specs=[pl.BlockSpec((8, D), lambda i, j: (0, 0)),
                  pl.BlockSpec((1, D, tn), lambda i, j: (i, 0, j)),
                  pl.BlockSpec((1, 1, tn), lambda i, j: (i, 0, j))],
        out_specs=pl.BlockSpec((1, 8, tn), lambda i, j: (i, 0, j)),
        out_shape=jax.ShapeDtypeStruct((DEPTH, 8, 6 * D), F32),
        compiler_params=_params(("arbitrary", "arbitrary")),
        name="ada",
    )(cvec, w_ada, b_ada.reshape(DEPTH, 1, 6 * D))


def _mod_row(tm):
    per_batch = L // tm
    return lambda tile: tile // per_batch


def _layer_spec(shape, layer):
    nd = len(shape)
    return pl.BlockSpec((None,) + tuple(shape), lambda *_: (layer,) + (0,) * nd, pipeline_mode=pl.Buffered(1))


def _proj(x2d, xc2d, mods, g, w_all, layer, sel_qk, sel_v, rope_tabs, lg):
    rows, crows = x2d.shape[0], xc2d.shape[0]
    assert crows == B * LC
    tm = TM_PROJ
    per_batch = L // tm
    tile = lambda i: jnp.maximum(i - 1, 0)
    return pl.pallas_call(
        functools.partial(_proj_kernel, mod_row=_mod_row(tm)),
        grid=(rows // tm + 1,),
        in_specs=[pl.BlockSpec((tm, D), lambda i: (tile(i), 0)),
                  _const_spec((crows, D)),
                  _const_spec((8, 6, D)),
                  _const_spec((1, D)),
                  _layer_spec((D, W_IN), layer),
                  _const_spec((RH * RD, RH * LANE)),
                  _const_spec((RH * RD, RH * LANE)),
                  pl.BlockSpec((tm, LANE), lambda i: (tile(i) % per_batch, 0)),
                  pl.BlockSpec((tm, LANE), lambda i: (tile(i) % per_batch, 0)),
                  _smem_spec()],
        out_specs=[pl.BlockSpec((tm, FW), lambda i: (tile(i), 0)),
                   pl.BlockSpec((tm, NP), lambda i: (tile(i), 0)),
                   pl.BlockSpec((crows, FW), lambda i: (0, 0)),
                   pl.BlockSpec((crows, NP), lambda i: (0, 0)),
                   pl.BlockSpec((B, RH, 2, LANE, LANE), lambda i: (0, 0, 0, 0, 0))],
        out_shape=[jax.ShapeDtypeStruct((rows, FW), BF16), jax.ShapeDtypeStruct((rows, NP), BF16),
                   jax.ShapeDtypeStruct((crows, FW), BF16), jax.ShapeDtypeStruct((crows, NP), BF16),
                   jax.ShapeDtypeStruct((B, RH, 2, LANE, LANE), F32)],
        scratch_shapes=[pltpu.VMEM((D, 4 * RH * LANE), BF16)],
        compiler_params=_params(("arbitrary",)),
        name="proj",
    )(x2d, xc2d, mods, g.reshape(1, D), w_all, sel_qk, sel_v, *rope_tabs, lg)


def _fourier(f2d, wf, a1, m3, e):
    y = pl.pallas_call(
        _four1_kernel,
        grid=(B, DFT_N2 // F1_R),
        in_specs=[_const_spec((2 * DFT_N1 * F1_R, DFT_N1 * F1_R)),
                  pl.BlockSpec((1, DFT_N1, F1_R, FW), lambda b, j: (b, 0, j, 0))],
        out_specs=pl.BlockSpec((1, 2 * DFT_N1, F1_R, FW), lambda b, j: (b, 0, j, 0)),
        out_shape=jax.ShapeDtypeStruct((B, 2 * DFT_N1, DFT_N2, FW), BF16),
        compiler_params=_params(("arbitrary", "arbitrary")),
        name="four1",
    )(a1, f2d.reshape(B, DFT_N1, DFT_N2, FW))
    out = pl.pallas_call(
        _four3_kernel,
        grid=(DFT_N1 // F3_K,),
        in_specs=[pl.BlockSpec((F3_K, 2 * DFT_N2, 2 * DFT_N2), lambda k: (k, 0, 0)),
                  pl.BlockSpec((B, 2, F3_K, DFT_N2, FW), lambda k: (0, 0, k, 0, 0)),
                  _const_spec((2 * FW, FW)),
                  _const_spec((FW, FW))],
        out_specs=pl.BlockSpec((B, F3_K, DFT_N2, FW), lambda k: (0, k, 0, 0)),
        out_shape=jax.ShapeDtypeStruct((B, DFT_N1, DFT_N2, FW), F32),
        compiler_params=_params(("arbitrary",)),
        name="four3",
    )(m3, y.reshape(B, 2, DFT_N1, DFT_N2, FW), e, wf)
    return out.reshape(B * DFT_N1, DFT_N2, FW)


def _retention(p, s0, lg, gain):
    col = lambda c0: (lambda b, h: (b, c0 // LANE + h))
    return pl.pallas_call(
        _ret_kernel,
        grid=(B, RH),
        in_specs=[_smem_spec(),
                  pl.BlockSpec((L, LANE), col(C_RQ)),
                  pl.BlockSpec((L, LANE), col(C_RK)),
                  pl.BlockSpec((L, LANE), col(C_RV)),
                  pl.BlockSpec((L, LANE), col(C_RG)),
                  pl.BlockSpec((1, 1, 2, LANE, LANE), lambda b, h: (b, h, 0, 0, 0)),
                  pl.BlockSpec((1, LANE), lambda b, h: (0, h))],
        out_specs=pl.BlockSpec((L, LANE), lambda b, h: (b, h)),
        out_shape=jax.ShapeDtypeStruct((B * L, RH * LANE), BF16),
        scratch_shapes=[pltpu.VMEM((L, LANE), F32), pltpu.VMEM((LANE, LANE), F32), pltpu.VMEM((LANE, LANE), F32)],
        compiler_params=_params(("arbitrary", "arbitrary")),
        name="ret",
    )(lg, p, p, p, p, s0, gain)


def _ctx_mix(fc, pc, lg, gain, wf, c256, s256, e):
    return pl.pallas_call(
        _ctx_mix_kernel,
        grid=(B,),
        in_specs=[_smem_spec(),
                  pl.BlockSpec((LC, FW), lambda b: (b, 0)),
                  pl.BlockSpec((LC, NP), lambda b: (b, 0)),
                  _const_spec((1, RH * LANE)),
                  _const_spec((FW, FW)),
                  _const_spec((LC, LC)),
                  _const_spec((LC, LC)),
                  _const_spec((2 * FW, FW))],
        out_specs=[pl.BlockSpec((LC, FW), lambda b: (b, 0)),
                   pl.BlockSpec((LC, RH * LANE), lambda b: (b, 0)),
                   pl.BlockSpec((LC, NH * ND), lambda b: (b, 0))],
        out_shape=[jax.ShapeDtypeStruct((B * LC, FW), BF16),
                   jax.ShapeDtypeStruct((B * LC, RH * LANE), BF16),
                   jax.ShapeDtypeStruct((B * LC, NH * ND), BF16)],
        compiler_params=_params(("arbitrary",)),
        name="ctx_mix",
    )(lg, fc, pc, gain, wf, c256, s256, e)


def _na_bias(base, mask):
    nq, nk = NA_R * GRID_W, NA_KR * GRID_W
    return pl.pallas_call(
        _na_bias_kernel,
        grid=(base.shape[0],),
        in_specs=[pl.BlockSpec((1, NA_DR, LANE), lambda h: (h, 0, 0)),
                  _const_spec((GRID_W, LANE))],
        out_specs=pl.BlockSpec((1, 3, nq, nk), lambda h: (h, 0, 0, 0)),
        out_shape=jax.ShapeDtypeStruct((base.shape[0], 3, nq, nk), F32),
        scratch_shapes=[pltpu.VMEM((NA_DR, GRID_W, LANE), F32)],
        compiler_params=_params(("arbitrary",)),
        name="na_bias",
    )(base, mask)


def _natten(p, pc, bias, layer):
    nq, nk = NA_R * GRID_W, NA_KR * GRID_W
    return pl.pallas_call(
        _natten_kernel,
        grid=(B, NPAIR),
        in_specs=[pl.BlockSpec((L, LANE), lambda b, pr: (b, C_NQ // LANE + pr)),
                  pl.BlockSpec((L, LANE), lambda b, pr: (b, C_NK // LANE + pr)),
                  pl.BlockSpec((L, LANE), lambda b, pr: (b, C_NV // LANE + pr)),
                  pl.BlockSpec((LC, LANE), lambda b, pr: (b, C_NK // LANE + pr)),
                  pl.BlockSpec((LC, LANE), lambda b, pr: (b, C_NV // LANE + pr)),
                  pl.BlockSpec((2, 3, nq, nk), lambda b, pr: (NPAIR * layer + pr, 0, 0, 0))],
        out_specs=pl.BlockSpec((L, LANE), lambda b, pr: (b, pr)),
        out_shape=jax.ShapeDtypeStruct((B * L, NH * ND), BF16),
        scratch_shapes=[pltpu.VMEM((2, nq, nk + LC), BF16)] * 2,
        compiler_params=_params(("arbitrary", "arbitrary")),
        name="natten",
    )(p, p, p, pc, pc, bias)


def _ffn(x2d, four, ret, na, ctx_acts, mods, g, wo_all, sel_v_t, w1_all, w3_all, w2_all, layer, g_final):
    rows = x2d.shape[0]
    final = g_final is not None
    ctx_tile = ctx_acts is not None
    off = 1 if ctx_tile else 0
    per_batch = L // TM
    tile = lambda i: jnp.maximum(i - off, 0)
    row = lambda w: pl.BlockSpec((TM, w), lambda i: (tile(i), 0))
    four_spec = pl.BlockSpec((DFT_N1, TM // DFT_N1, FW), lambda i: (tile(i) // per_batch, tile(i) % per_batch, 0))
    in_specs = [row(D), four_spec, row(RH * LANE), row(NH * ND)]
    args = [x2d, four, ret, na]
    if ctx_tile:
        assert all(a.shape[0] == TM for a in ctx_acts)
        in_specs += [_const_spec(a.shape) for a in ctx_acts]
        args += list(ctx_acts)
    in_specs += [_const_spec((8, 6, D)), _const_spec((1, D)),
                 _layer_spec((D, D), layer), _const_spec((RH * LANE, RH * RD)),
                 _layer_spec((D, DFF), layer), _layer_spec((D, DFF), layer), _layer_spec((DFF, D), layer)]
    args += [mods, g.reshape(1, D), wo_all, sel_v_t, w1_all, w3_all, w2_all]
    if final:
        in_specs.append(_const_spec((1, D)))
        args.append(g_final.reshape(1, D))
    out_specs = [row(D)]
    out_shape = [jax.ShapeDtypeStruct((rows, D), F32)]
    if ctx_tile:
        out_specs.append(pl.BlockSpec((TM, D), lambda i: (0, 0)))
        out_shape.append(jax.ShapeDtypeStruct((TM, D), F32))
    outs = pl.pallas_call(
        functools.partial(_ffn_kernel, final=final, ctx_tile=ctx_tile, mod_row=_mod_row(TM)),
        grid=(rows // TM + off,),
        in_specs=in_specs,
        out_specs=out_specs,
        out_shape=out_shape,
        scratch_shapes=[pltpu.VMEM((TM, DFF), BF16), pltpu.VMEM((RH * LANE, D), BF16)],
        compiler_params=_params(("arbitrary",)),
        name="ffn_final" if final else "ffn",
    )(*args)
    return outs if ctx_tile else outs[0]


def _take_padded(a, src, axis):
    pieces = []
    i, n = 0, len(src)
    while i < n:
        j = i + 1
        if src[i] < 0:
            while j < n and src[j] < 0:
                j += 1
            shape = list(a.shape)
            shape[axis] = j - i
            pieces.append(jnp.zeros(shape, a.dtype))
        else:
            while j < n and src[j] == src[j - 1] + 1:
                j += 1
            pieces.append(lax.slice_in_dim(a, int(src[i]), int(src[j - 1]) + 1, axis=axis))
        i = j
    return jnp.concatenate(pieces, axis=axis)


def kernel(x, c, ctx, c_ctx, w_ada, b_ada, g_mix, w_in, ret_decay_logit, ret_norm_g, w_four,
           na_rpb, w_out, g_ffn, w1, w3, w2, g_final):
    src_g = _gain_sources()
    rope_tabs = tuple(jnp.asarray(t) for t in _rope_tables())
    a1, m3, e, c256, s256 = (jnp.asarray(t).astype(BF16) for t in _dft_tables())
    sel_qk, sel_v = (jnp.asarray(t).astype(BF16) for t in _head_selectors())
    sel_v_t = sel_v.T
    tz_src, tz_mask = _na_toeplitz_tables()
    col_scale = np.ones((1, 1, W_IN), np.float32)
    col_scale[..., W_NQ:W_NK] = LOG2E * ND ** -0.5
    w_in_b = (w_in * jnp.asarray(col_scale)).astype(BF16)
    w_out_b = w_out.astype(BF16)
    w1b, w3b, w2b = w1.astype(BF16), w3.astype(BF16), w2.astype(BF16)

    tz_base = jnp.pad(_take_padded(na_rpb.astype(F32).reshape(DEPTH * NH, 15, 31), tz_src, 2), ((0, 0), (0, 1), (0, 0)))
    na_bias = _na_bias(tz_base, jnp.asarray(tz_mask))

    cvec = jnp.zeros((8, D), F32).at[0:B].set(c).at[B].set(c_ctx)
    mods_all = _ada(cvec, w_ada, b_ada).reshape(DEPTH, 8, 6, D)

    xl = x.reshape(B * L, D)
    xc = ctx.reshape(B * LC, D)
    for i in range(DEPTH):
        last = i == DEPTH - 1
        mods = mods_all[i]
        gain = _take_padded(ret_norm_g[i], src_g, 0).reshape(1, RH * LANE)
        wfour = w_four[i].astype(BF16)
        lg = jax.nn.log_sigmoid(ret_decay_logit[i].astype(F32))

        f_l, p_l, f_c, p_c, s0 = _proj(xl, xc, mods, g_mix[i], w_in_b, i, sel_qk, sel_v, rope_tabs, lg)
        four_l = _fourier(f_l, wfour, a1, m3, e)
        ret_l = _retention(p_l, s0, lg, gain)
        na_l = _natten(p_l, p_c, na_bias, i)
        if last:
            xl = _ffn(xl, four_l, ret_l, na_l, None, mods, g_ffn[i], w_out_b, sel_v_t, w1b, w3b, w2b, i, g_final)
        else:
            ctx_acts = (xc,) + tuple(_ctx_mix(f_c, p_c, lg, gain, wfour, c256, s256, e))
            xl, xc = _ffn(xl, four_l, ret_l, na_l, ctx_acts, mods, g_ffn[i], w_out_b, sel_v_t, w1b, w3b, w2b, i, None)
    return xl.reshape(B, L, D)
```

```python
import functools

import numpy as np
import jax
import jax.numpy as jnp
from jax import lax
from jax.experimental import pallas as pl
from jax.experimental.pallas import tpu as pltpu

F32 = jnp.float32
BF16 = jnp.bfloat16

D = 1024
B = 2
L = 8192
LC = 256
DEPTH = 2
GRID_W = 64
GRID_H = L // GRID_W
FW = 256
FGW = 64
RH = 4
RD = 96
NH = 6
ND = 64
NPAIR = NH // 2
DFF = 2816
EPS = 1e-6
NEG = -1e30
LOG2E = float(np.log2(np.e))
LANE = 128

C_RQ, C_RK, C_RV, C_RG = 0, 512, 1024, 1536
C_NQ, C_NK, C_NV = 2048, 2432, 2816
NP = 3200
W_RQ, W_RK, W_RV, W_RG = 256, 640, 1024, 1408
W_NQ, W_NK, W_NV = 1792, 2176, 2560
W_IN = 2944

TM = 512
TM_PROJ = 1024
RET_C = 256
NA_R = 4
NA_KR = NA_R + 8
NA_NB = GRID_H // NA_R
DFT_N1 = 64
DFT_N2 = 128
F1_R = 16
F3_K = 8
VMEM_LIMIT = 56 * 1024 * 1024


def _qk_lane_dims():
    m = np.full((LANE,), -1, np.int64)
    m[0:24] = np.arange(0, 24)
    m[24:48] = np.arange(48, 72)
    m[64:88] = np.arange(24, 48)
    m[88:112] = np.arange(72, 96)
    return m


def _v_lane_dims():
    m = np.full((LANE,), -1, np.int64)
    m[:RD] = np.arange(RD)
    return m


def _head_selectors():
    sels = []
    for lanes in (_qk_lane_dims(), _v_lane_dims()):
        sel = np.zeros((RH * RD, RH * LANE), np.float32)
        ok = np.nonzero(lanes >= 0)[0]
        for h in range(RH):
            sel[RD * h + lanes[ok], LANE * h + ok] = 1.0
        sels.append(sel)
    return sels


def _gain_sources():
    src = np.full((RH * LANE,), -1, np.int64)
    for h in range(RH):
        src[LANE * h: LANE * h + RD] = RD * h + np.arange(RD)
    return src


def _rope_tables():
    pos = np.arange(L)
    prow, pcol = pos // GRID_W, pos % GRID_W
    half = RD // 4
    inv = 10000.0 ** (-np.arange(half, dtype=np.float64) / half)
    ar = prow[:, None] * inv[None, :]
    ac = pcol[:, None] * inv[None, :]
    cos = np.ones((L, LANE), np.float64)
    sin = np.zeros((L, LANE), np.float64)
    for off, sign in ((0, -1.0), (64, 1.0)):
        cos[:, off:off + 24] = np.cos(ar)
        cos[:, off + 24:off + 48] = np.cos(ac)
        sin[:, off:off + 24] = sign * np.sin(ar)
        sin[:, off + 24:off + 48] = sign * np.sin(ac)
    return cos.astype(np.float32), sin.astype(np.float32)


def _dft_tables():
    k1 = np.arange(DFT_N1)
    a = 2 * np.pi * ((k1[:, None] * k1[None, :]) % DFT_N1) / DFT_N1
    a1 = np.kron(np.concatenate([np.cos(a), -np.sin(a)], axis=0), np.eye(F1_R))
    k2 = np.arange(DFT_N2)
    l2 = np.arange(DFT_N2)
    kk = k1[:, None, None] + DFT_N1 * k2[None, :, None]
    ang = 2 * np.pi * ((kk * l2[None, None, :]) % L) / L
    ct, st = np.cos(ang) / np.sqrt(L), np.sin(ang) / np.sqrt(L)
    m3 = np.concatenate([np.concatenate([ct, st], axis=2), np.concatenate([st, -ct], axis=2)], axis=1)
    c = np.arange(FW)
    same = (c[:, None] // FGW) == (c[None, :] // FGW)
    ac = 2 * np.pi * (((c[:, None] % FGW) * (c[None, :] % FGW)) % FGW) / FGW
    e = np.concatenate([np.where(same, np.cos(ac), 0.0), -np.where(same, np.sin(ac), 0.0)], axis=0) / 8.0
    p = np.arange(LC)
    ap = 2 * np.pi * ((p[:, None] * p[None, :]) % LC) / LC
    return tuple(t.astype(np.float32) for t in (a1, m3, e, np.cos(ap) / 16.0, np.sin(ap) / 16.0))


NA_DR = 16


def _na_toeplitz_tables():
    src = np.full((LANE,), -1, np.int64)
    src[0:16] = np.arange(0, 16) + 15
    src[113:128] = np.arange(113, 128) - 113
    src[49:80] = np.arange(49, 80) - 49
    qc = np.arange(GRID_W)[:, None]
    kc = np.arange(LANE)[None, :] % GRID_W
    ws = np.clip(qc - 8, 0, GRID_W - 16)
    mask = np.where((kc >= ws) & (kc < ws + 16), 0.0, NEG).astype(np.float32)
    return src, mask


def _dot(a, b):
    return jnp.dot(a, b, preferred_element_type=F32)


def _dot_nt(a, b):
    return lax.dot_general(a, b, (((1,), (1,)), ((), ())), preferred_element_type=F32)


def _silu(x):
    return x * jax.nn.sigmoid(x)


def _rmsnorm(x, g):
    return x * lax.rsqrt(jnp.mean(x * x, axis=-1, keepdims=True) + EPS) * g


def _ada_kernel(c_ref, w_ref, b_ref, o_ref):
    s = _silu(c_ref[...]).astype(BF16)
    o_ref[0] = _dot(s, w_ref[0].astype(BF16)) + b_ref[0]


def _proj_kernel(x_ref, xc_ref, mods_ref, g_ref, w_ref, pqk_ref, pv_ref, cos_ref, sin_ref, lg_ref,
                 f_ref, o_ref, fc_ref, oc_ref, s0_ref, wret_ref, *, mod_row):
    s = pl.program_id(0)

    def project(x, mod, rope, f_out, o_out):
        hb = (_rmsnorm(x, g_ref[...]) * (1.0 + mod[1:2, :]) + mod[0:1, :]).astype(BF16)
        f_out[...] = _dot(hb, w_ref[:, :FW]).astype(BF16)
        k_scale = RD ** -0.5
        for c0, scale in ((C_RQ, None), (C_RK, k_scale)):
            t4 = _dot(hb, wret_ref[:, c0:c0 + RH * LANE])
            for hh in range(RH):
                t = t4[:, LANE * hh:LANE * (hh + 1)]
                if rope:
                    t = t * cos_ref[...] + pltpu.roll(t, 64, 1) * sin_ref[...]
                if scale is not None:
                    t = t * scale
                o_out[:, c0 + LANE * hh:c0 + LANE * (hh + 1)] = t.astype(BF16)
        o_out[:, C_RV:C_RG] = _dot(hb, wret_ref[:, C_RV:C_RG]).astype(BF16)
        o_out[:, C_RG:C_NQ] = _silu(_dot(hb, wret_ref[:, C_RG:C_NQ])).astype(BF16)
        o_out[:, C_NQ:] = _dot(hb, w_ref[:, W_NQ:]).astype(BF16)

    @pl.when(s == 0)
    def _():
        for slot, (src0, sel_ref) in enumerate(((W_RQ, pqk_ref), (W_RK, pqk_ref), (W_RV, pv_ref), (W_RG, pv_ref))):
            wret_ref[:, RH * LANE * slot:RH * LANE * (slot + 1)] = _dot(
                w_ref[:, src0:src0 + RH * RD], sel_ref[...]).astype(BF16)
        project(xc_ref[...], mods_ref[B], False, fc_ref, oc_ref)
        idx = lax.broadcasted_iota(jnp.int32, (LC, LANE), 0).astype(F32)
        for b in range(B):
            for h in range(RH):
                k = oc_ref[LC * b:LC * (b + 1), C_RK + LANE * h:C_RK + LANE * (h + 1)].astype(F32)
                v = oc_ref[LC * b:LC * (b + 1), C_RV + LANE * h:C_RV + LANE * (h + 1)]
                wf = jnp.exp((LC - 1.0 - idx) * lg_ref[0, h])
                wb = jnp.exp(idx * lg_ref[1, h])
                s0_ref[b, h, 0] = _dot((k * wf).T.astype(BF16), v)
                s0_ref[b, h, 1] = _dot((k * wb).T.astype(BF16), v)

    @pl.when(s > 0)
    def _():
        project(x_ref[...], mods_ref[mod_row(s - 1)], True, f_ref, o_ref)


def _four1_kernel(a_ref, x_ref, y_ref):
    x = x_ref[0].reshape(DFT_N1 * F1_R, FW)
    y = _dot(a_ref[...], x).astype(BF16)
    y_ref[0] = y.reshape(2 * DFT_N1, F1_R, FW)


def _four3_kernel(m_ref, y_ref, e_ref, wf_ref, o_ref):
    ew = _dot(e_ref[...], wf_ref[...]).astype(BF16)
    for b in range(B):
        for kk in range(F3_K):
            y = jnp.concatenate([y_ref[b, 0, kk], y_ref[b, 1, kk]], axis=0)
            z = _dot(m_ref[kk], y).astype(BF16)
            o_ref[b, kk] = _dot(jnp.concatenate([z[:DFT_N2], z[DFT_N2:]], axis=1), ew)


def _decay_matrix(n, lgf, lgb):
    ii = lax.broadcasted_iota(jnp.int32, (n, n), 0)
    jj = lax.broadcasted_iota(jnp.int32, (n, n), 1)
    diff = (ii - jj).astype(F32)
    fwd = jnp.where(diff >= 0, jnp.exp(jnp.maximum(diff, 0.0) * lgf), 0.0)
    bwd = jnp.where(diff <= 0, jnp.exp(jnp.maximum(-diff, 0.0) * lgb), 0.0)
    return fwd + bwd


def _ret_readout(o, gate, gain):
    ms = jnp.sum(o * o, axis=-1, keepdims=True) * (1.0 / RD)
    return (o * lax.rsqrt(ms + EPS) * gain * gate.astype(F32)).astype(BF16)


def _ret_kernel(lg_ref, q_ref, k_ref, v_ref, g_ref, s0_ref, gain_ref, o_ref, acc_ref, stf_ref, stb_ref):
    h = pl.program_id(1)
    lgf = lg_ref[0, h]
    lgb = lg_ref[1, h]
    c = RET_C
    n = L // c
    dmat = _decay_matrix(c, lgf, lgb)
    idx = lax.broadcasted_iota(jnp.int32, (c, LANE), 0).astype(F32)
    qdf = jnp.exp((idx + 1.0) * lgf)
    kdf = jnp.exp((c - 1.0 - idx) * lgf)
    qdb = jnp.exp((c - idx) * lgb)
    kdb = jnp.exp(idx * lgb)
    cdf = jnp.exp(jnp.full((1, LANE), float(c), F32) * lgf)
    cdb = jnp.exp(jnp.full((1, LANE), float(c), F32) * lgb)
    gain = gain_ref[...]

    dirs = ((stf_ref, qdf, kdf, cdf), (stb_ref, qdb, kdb, cdb))
    stf_ref[...] = s0_ref[0, 0, 0]
    stb_ref[...] = s0_ref[0, 0, 1]

    def visit(r0, direction, second):
        st_ref, qdec, kdec, cdec = dirs[direction]
        rows = pl.ds(pl.multiple_of(r0, c), c)
        q = q_ref[rows, :]
        k = k_ref[rows, :]
        v = v_ref[rows, :]
        cross = _dot(q, st_ref[...].astype(BF16)) * qdec
        if second:
            o_ref[rows, :] = _ret_readout(acc_ref[rows, :] + cross, g_ref[rows, :], gain)
        else:
            s = _dot_nt(q, k) * dmat
            acc_ref[rows, :] = _dot(s.astype(BF16), v) + cross
        kt = (k.astype(F32) * kdec).T.astype(BF16)
        st_ref[...] = st_ref[...] * cdec + _dot(kt, v)

    def body(second):
        def step(i, carry):
            visit(i * c, 0, second)
            visit((n - 1 - i) * c, 1, second)
            return carry
        return step

    lax.fori_loop(0, n // 2, body(False), 0, unroll=16)
    lax.fori_loop(n // 2, n, body(True), 0, unroll=16)


def _pair_attention(q, score_fn, value_fn):
    lane = lax.broadcasted_iota(jnp.int32, q.shape, 1)
    outs = []
    for hh in range(2):
        sel = (lane >= ND * hh) & (lane < ND * (hh + 1))
        qm = jnp.where(sel, q, jnp.zeros_like(q))
        scores = score_fn(qm, hh)
        m = functools.reduce(jnp.maximum, [jnp.max(s, axis=-1, keepdims=True) for s in scores])
        ps = [jnp.exp2(s - m) for s in scores]
        den = functools.reduce(jnp.add, [jnp.sum(p, axis=-1, keepdims=True) for p in ps])
        outs.append(value_fn([p.astype(BF16) for p in ps]) / den)
    return jnp.where(lane < ND, outs[0], outs[1])


def _ctx_mix_kernel(lg_ref, f_ref, p_ref, gain_ref, wf_ref, c_ref, s_ref, e_ref, fo_ref, ro_ref, no_ref):
    f = f_ref[...]
    pr = _dot(c_ref[...], f)
    qi = _dot(s_ref[...], f)
    fr = _dot(pr.astype(BF16), e_ref[:FW, :]) + _dot(qi.astype(BF16), e_ref[FW:, :])
    fo_ref[...] = _dot(fr.astype(BF16), wf_ref[...]).astype(BF16)
    for h in range(RH):
        sl = lambda c0: slice(c0 + LANE * h, c0 + LANE * (h + 1))
        dmat = _decay_matrix(LC, lg_ref[0, h], lg_ref[1, h])
        s = _dot_nt(p_ref[:, sl(C_RQ)], p_ref[:, sl(C_RK)]) * dmat
        o = _dot(s.astype(BF16), p_ref[:, sl(C_RV)])
        ro_ref[:, LANE * h:LANE * (h + 1)] = _ret_readout(o, p_ref[:, sl(C_RG)], gain_ref[:, LANE * h:LANE * (h + 1)])
    for pair in range(NPAIR):
        sl = lambda c0: slice(c0 + LANE * pair, c0 + LANE * (pair + 1))
        k = p_ref[:, sl(C_NK)]
        v = p_ref[:, sl(C_NV)]
        out = _pair_attention(p_ref[:, sl(C_NQ)], lambda qm, hh: [_dot_nt(qm, k)], lambda ps: _dot(ps[0], v))
        no_ref[:, LANE * pair:LANE * (pair + 1)] = out.astype(BF16)


def _na_key_start(r0):
    return min(max(r0 - 4, 0), GRID_H - NA_KR)


def _na_bias_kernel(base_ref, mask_ref, o_ref, tz_ref):
    for dr in range(NA_DR - 1):
        t = jnp.broadcast_to(base_ref[0, dr:dr + 1, :], (GRID_W, LANE))
        t = pltpu.roll(t, 0, 1, stride=1, stride_axis=0)
        tz_ref[dr] = t * LOG2E + mask_ref[...]
    tz_ref[NA_DR - 1] = jnp.full((GRID_W, LANE), NEG, F32)
    low_half = lax.broadcasted_iota(jnp.int32, (GRID_W, LANE), 1) < GRID_W
    for var, r0 in enumerate((0, 2 * NA_R, GRID_H - NA_R)):
        ks = _na_key_start(r0)
        for rl in range(NA_R):
            r = r0 + rl
            rs = min(max(r - 4, 0), GRID_H - 8)
            slot = [kr - r + 7 if rs <= kr < rs + 8 else NA_DR - 1 for kr in range(ks, ks + NA_KR)]
            for m in range(NA_KR // 2):
                o_ref[0, var, GRID_W * rl:GRID_W * (rl + 1), LANE * m:LANE * (m + 1)] = jnp.where(
                    low_half, tz_ref[slot[2 * m]], tz_ref[slot[2 * m + 1]])


def _natten_kernel(q_ref, k_ref, v_ref, kc_ref, vc_ref, bias_ref, o_ref, pa_ref, pb_ref):
    nq = NA_R * GRID_W
    nk = NA_KR * GRID_W
    kc = kc_ref[...]
    vc = vc_ref[...]
    lane = lax.broadcasted_iota(jnp.int32, (nq, LANE), 1)

    def rows(i):
        ks = jnp.clip(NA_R * i - 4, 0, GRID_H - NA_KR)
        return pl.ds(pl.multiple_of(ks * GRID_W, nq), nk), pl.ds(pl.multiple_of(i * nq, nq), nq)

    def probabilities(i, p_ref):
        i = jnp.minimum(i, NA_NB - 1)
        var = jnp.where(i == 0, 0, jnp.where(i == NA_NB - 1, 2, 1))
        krows, qrows = rows(i)
        q = q_ref[qrows, :]
        kw = k_ref[krows, :]
        for hh in range(2):
            qm = jnp.where((lane >= ND * hh) & (lane < ND * (hh + 1)), q, jnp.zeros_like(q))
            s_win = _dot_nt(qm, kw) + bias_ref[hh, var]
            s_ctx = _dot_nt(qm, kc)
            m = jnp.maximum(jnp.max(s_win, axis=-1, keepdims=True), jnp.max(s_ctx, axis=-1, keepdims=True))
            p_ref[hh, :, :nk] = jnp.exp2(s_win - m).astype(BF16)
            p_ref[hh, :, nk:] = jnp.exp2(s_ctx - m).astype(BF16)

    def values(i, p_ref):
        krows, qrows = rows(i)
        vals = jnp.concatenate([v_ref[krows, :], vc], axis=0)
        vals = jnp.concatenate([vals, jnp.ones_like(vals)], axis=1)
        o2 = _dot(p_ref[...].reshape(2 * nq, nk + LC), vals)
        outs = [o[:, :LANE] / o[:, LANE:] for o in (o2[:nq], o2[nq:])]
        o_ref[qrows, :] = jnp.where(lane < ND, outs[0], outs[1]).astype(BF16)

    probabilities(0, pa_ref)

    def step(j, carry):
        values(2 * j, pa_ref)
        probabilities(2 * j + 1, pb_ref)
        values(2 * j + 1, pb_ref)
        probabilities(2 * j + 2, pa_ref)
        return carry

    lax.fori_loop(0, NA_NB // 2, step, 0, unroll=4)


def _ffn_kernel(*refs, final, ctx_tile, mod_row):
    refs = list(refs)
    x_ref, f_ref, r_ref, n_ref = refs[:4]
    del refs[:4]
    if ctx_tile:
        xc_ref, fc_ref, rc_ref, nc_ref = refs[:4]
        del refs[:4]
    mods_ref, g_ref, wo_ref, sel_ref, w1_ref, w3_ref, w2_ref = refs[:7]
    del refs[:7]
    gf_ref = refs.pop(0) if final else None
    o_ref = refs.pop(0)
    oc_ref = refs.pop(0) if ctx_tile else None
    acc_ref, wor_ref = refs
    s = pl.program_id(0)

    @pl.when(s == 0)
    def _():
        wor_ref[...] = _dot(sel_ref[...], wo_ref[FW:FW + RH * RD, :]).astype(BF16)

    def tile(x, f, r, n, mod, out_ref):
        y = _dot(f.astype(BF16), wo_ref[:FW, :]) + _dot(r, wor_ref[...]) + _dot(n, wo_ref[FW + RH * RD:, :])
        x1 = x + mod[2:3, :] * y
        hb = (_rmsnorm(x1, g_ref[...]) * (1.0 + mod[4:5, :]) + mod[3:4, :]).astype(BF16)
        fc = 256
        for c in range(DFF // fc):
            a = _dot(hb, w1_ref[:, c * fc:(c + 1) * fc])
            b = _dot(hb, w3_ref[:, c * fc:(c + 1) * fc])
            acc_ref[:, c * fc:(c + 1) * fc] = (_silu(a) * b).astype(BF16)
        x2 = x1 + mod[5:6, :] * _dot(acc_ref[...], w2_ref[...])
        if final:
            x2 = _rmsnorm(x2, gf_ref[...])
        out_ref[...] = x2

    def latent_tile(t):
        f = jnp.concatenate([f_ref[:, r, :] for r in range(f_ref.shape[1])], axis=0)
        tile(x_ref[...], f, r_ref[...], n_ref[...], mods_ref[mod_row(t)], o_ref)

    if ctx_tile:
        @pl.when(s == 0)
        def _():
            tile(xc_ref[...], fc_ref[...], rc_ref[...], nc_ref[...], mods_ref[B], oc_ref)

        @pl.when(s > 0)
        def _():
            latent_tile(s - 1)
    else:
        latent_tile(s)


def _params(sem):
    return pltpu.CompilerParams(dimension_semantics=sem, vmem_limit_bytes=VMEM_LIMIT)


def _const_spec(shape):
    nd = len(shape)
    return pl.BlockSpec(shape, lambda *_: (0,) * nd, pipeline_mode=pl.Buffered(1))


def _smem_spec():
    return pl.BlockSpec(memory_space=pltpu.SMEM)


def _ada(cvec, w_ada, b_ada):
    tn = 2048
    return pl.pallas_call(
        _ada_kernel,
        grid=(DEPTH, 6 * D // tn),
        in_specs=[pl.BlockSpec((8, D), lambda i, j: (0, 0)),
                  pl.BlockSpec((1, D, tn), lambda i, j: (i, 0, j)),
                  pl.BlockSpec((1, 1, tn), lambda i, j: (i, 0, j))],
        out_specs=pl.BlockSpec((1, 8, tn), lambda i, j: (i, 0, j)),
        out_shape=jax.ShapeDtypeStruct((DEPTH, 8, 6 * D), F32),
        compiler_params=_params(("arbitrary", "arbitrary")),
        name="ada",
    )(cvec, w_ada, b_ada.reshape(DEPTH, 1, 6 * D))


def _mod_row(tm):
    per_batch = L // tm
    return lambda tile: tile // per_batch


def _layer_spec(shape, layer):
    nd = len(shape)
    return pl.BlockSpec((None,) + tuple(shape), lambda *_: (layer,) + (0,) * nd, pipeline_mode=pl.Buffered(1))


def _proj(x2d, xc2d, mods, g, w_all, layer, sel_qk, sel_v, rope_tabs, lg):
    rows, crows = x2d.shape[0], xc2d.shape[0]
    assert crows == B * LC
    tm = TM_PROJ
    per_batch = L // tm
    tile = lambda i: jnp.maximum(i - 1, 0)
    return pl.pallas_call(
        functools.partial(_proj_kernel, mod_row=_mod_row(tm)),
        grid=(rows // tm + 1,),
        in_specs=[pl.BlockSpec((tm, D), lambda i: (tile(i), 0)),
                  _const_spec((crows, D)),
                  _const_spec((8, 6, D)),
                  _const_spec((1, D)),
                  _layer_spec((D, W_IN), layer),
                  _const_spec((RH * RD, RH * LANE)),
                  _const_spec((RH * RD, RH * LANE)),
                  pl.BlockSpec((tm, LANE), lambda i: (tile(i) % per_batch, 0)),
                  pl.BlockSpec((tm, LANE), lambda i: (tile(i) % per_batch, 0)),
                  _smem_spec()],
        out_specs=[pl.BlockSpec((tm, FW), lambda i: (tile(i), 0)),
                   pl.BlockSpec((tm, NP), lambda i: (tile(i), 0)),
                   pl.BlockSpec((crows, FW), lambda i: (0, 0)),
                   pl.BlockSpec((crows, NP), lambda i: (0, 0)),
                   pl.BlockSpec((B, RH, 2, LANE, LANE), lambda i: (0, 0, 0, 0, 0))],
        out_shape=[jax.ShapeDtypeStruct((rows, FW), BF16), jax.ShapeDtypeStruct((rows, NP), BF16),
                   jax.ShapeDtypeStruct((crows, FW), BF16), jax.ShapeDtypeStruct((crows, NP), BF16),
                   jax.ShapeDtypeStruct((B, RH, 2, LANE, LANE), F32)],
        scratch_shapes=[pltpu.VMEM((D, 4 * RH * LANE), BF16)],
        compiler_params=_params(("arbitrary",)),
        name="proj",
    )(x2d, xc2d, mods, g.reshape(1, D), w_all, sel_qk, sel_v, *rope_tabs, lg)


def _fourier(f2d, wf, a1, m3, e):
    y = pl.pallas_call(
        _four1_kernel,
        grid=(B, DFT_N2 // F1_R),
        in_specs=[_const_spec((2 * DFT_N1 * F1_R, DFT_N1 * F1_R)),
                  pl.BlockSpec((1, DFT_N1, F1_R, FW), lambda b, j: (b, 0, j, 0))],
        out_specs=pl.BlockSpec((1, 2 * DFT_N1, F1_R, FW), lambda b, j: (b, 0, j, 0)),
        out_shape=jax.ShapeDtypeStruct((B, 2 * DFT_N1, DFT_N2, FW), BF16),
        compiler_params=_params(("arbitrary", "arbitrary")),
        name="four1",
    )(a1, f2d.reshape(B, DFT_N1, DFT_N2, FW))
    out = pl.pallas_call(
        _four3_kernel,
        grid=(DFT_N1 // F3_K,),
        in_specs=[pl.BlockSpec((F3_K, 2 * DFT_N2, 2 * DFT_N2), lambda k: (k, 0, 0)),
                  pl.BlockSpec((B, 2, F3_K, DFT_N2, FW), lambda k: (0, 0, k, 0, 0)),
                  _const_spec((2 * FW, FW)),
                  _const_spec((FW, FW))],
        out_specs=pl.BlockSpec((B, F3_K, DFT_N2, FW), lambda k: (0, k, 0, 0)),
        out_shape=jax.ShapeDtypeStruct((B, DFT_N1, DFT_N2, FW), F32),
        compiler_params=_params(("arbitrary",)),
        name="four3",
    )(m3, y.reshape(B, 2, DFT_N1, DFT_N2, FW), e, wf)
    return out.reshape(B * DFT_N1, DFT_N2, FW)


def _retention(p, s0, lg, gain):
    col = lambda c0: (lambda b, h: (b, c0 // LANE + h))
    return pl.pallas_call(
        _ret_kernel,
        grid=(B, RH),
        in_specs=[_smem_spec(),
                  pl.BlockSpec((L, LANE), col(C_RQ)),
                  pl.BlockSpec((L, LANE), col(C_RK)),
                  pl.BlockSpec((L, LANE), col(C_RV)),
                  pl.BlockSpec((L, LANE), col(C_RG)),
                  pl.BlockSpec((1, 1, 2, LANE, LANE), lambda b, h: (b, h, 0, 0, 0)),
                  pl.BlockSpec((1, LANE), lambda b, h: (0, h))],
        out_specs=pl.BlockSpec((L, LANE), lambda b, h: (b, h)),
        out_shape=jax.ShapeDtypeStruct((B * L, RH * LANE), BF16),
        scratch_shapes=[pltpu.VMEM((L, LANE), F32), pltpu.VMEM((LANE, LANE), F32), pltpu.VMEM((LANE, LANE), F32)],
        compiler_params=_params(("arbitrary", "arbitrary")),
        name="ret",
    )(lg, p, p, p, p, s0, gain)


def _ctx_mix(fc, pc, lg, gain, wf, c256, s256, e):
    return pl.pallas_call(
        _ctx_mix_kernel,
        grid=(B,),
        in_specs=[_smem_spec(),
                  pl.BlockSpec((LC, FW), lambda b: (b, 0)),
                  pl.BlockSpec((LC, NP), lambda b: (b, 0)),
                  _const_spec((1, RH * LANE)),
                  _const_spec((FW, FW)),
                  _const_spec((LC, LC)),
                  _const_spec((LC, LC)),
                  _const_spec((2 * FW, FW))],
        out_specs=[pl.BlockSpec((LC, FW), lambda b: (b, 0)),
                   pl.BlockSpec((LC, RH * LANE), lambda b: (b, 0)),
                   pl.BlockSpec((LC, NH * ND), lambda b: (b, 0))],
        out_shape=[jax.ShapeDtypeStruct((B * LC, FW), BF16),
                   jax.ShapeDtypeStruct((B * LC, RH * LANE), BF16),
                   jax.ShapeDtypeStruct((B * LC, NH * ND), BF16)],
        compiler_params=_params(("arbitrary",)),
        name="ctx_mix",
    )(lg, fc, pc, gain, wf, c256, s256, e)


def _na_bias(base, mask):
    nq, nk = NA_R * GRID_W, NA_KR * GRID_W
    return pl.pallas_call(
        _na_bias_kernel,
        grid=(base.shape[0],),
        in_specs=[pl.BlockSpec((1, NA_DR, LANE), lambda h: (h, 0, 0)),
                  _const_spec((GRID_W, LANE))],
        out_specs=pl.BlockSpec((1, 3, nq, nk), lambda h: (h, 0, 0, 0)),
        out_shape=jax.ShapeDtypeStruct((base.shape[0], 3, nq, nk), F32),
        scratch_shapes=[pltpu.VMEM((NA_DR, GRID_W, LANE), F32)],
        compiler_params=_params(("arbitrary",)),
        name="na_bias",
    )(base, mask)


def _natten(p, pc, bias, layer):
    nq, nk = NA_R * GRID_W, NA_KR * GRID_W
    return pl.pallas_call(
        _natten_kernel,
        grid=(B, NPAIR),
        in_specs=[pl.BlockSpec((L, LANE), lambda b, pr: (b, C_NQ // LANE + pr)),
                  pl.BlockSpec((L, LANE), lambda b, pr: (b, C_NK // LANE + pr)),
                  pl.BlockSpec((L, LANE), lambda b, pr: (b, C_NV // LANE + pr)),
                  pl.BlockSpec((LC, LANE), lambda b, pr: (b, C_NK // LANE + pr)),
                  pl.BlockSpec((LC, LANE), lambda b, pr: (b, C_NV // LANE + pr)),
                  pl.BlockSpec((2, 3, nq, nk), lambda b, pr: (NPAIR * layer + pr, 0, 0, 0))],
        out_specs=pl.BlockSpec((L, LANE), lambda b, pr: (b, pr)),
        out_shape=jax.ShapeDtypeStruct((B * L, NH * ND), BF16),
        scratch_shapes=[pltpu.VMEM((2, nq, nk + LC), BF16)] * 2,
        compiler_params=_params(("arbitrary", "arbitrary")),
        name="natten",
    )(p, p, p, pc, pc, bias)


def _ffn(x2d, four, ret, na, ctx_acts, mods, g, wo_all, sel_v_t, w1_all, w3_all, w2_all, layer, g_final):
    rows = x2d.shape[0]
    final = g_final is not None
    ctx_tile = ctx_acts is not None
    off = 1 if ctx_tile else 0
    per_batch = L // TM
    tile = lambda i: jnp.maximum(i - off, 0)
    row = lambda w: pl.BlockSpec((TM, w), lambda i: (tile(i), 0))
    four_spec = pl.BlockSpec((DFT_N1, TM // DFT_N1, FW), lambda i: (tile(i) // per_batch, tile(i) % per_batch, 0))
    in_specs = [row(D), four_spec, row(RH * LANE), row(NH * ND)]
    args = [x2d, four, ret, na]
    if ctx_tile:
        assert all(a.shape[0] == TM for a in ctx_acts)
        in_specs += [_const_spec(a.shape) for a in ctx_acts]
        args += list(ctx_acts)
    in_specs += [_const_spec((8, 6, D)), _const_spec((1, D)),
                 _layer_spec((D, D), layer), _const_spec((RH * LANE, RH * RD)),
                 _layer_spec((D, DFF), layer), _layer_spec((D, DFF), layer), _layer_spec((DFF, D), layer)]
    args += [mods, g.reshape(1, D), wo_all, sel_v_t, w1_all, w3_all, w2_all]
    if final:
        in_specs.append(_const_spec((1, D)))
        args.append(g_final.reshape(1, D))
    out_specs = [row(D)]
    out_shape = [jax.ShapeDtypeStruct((rows, D), F32)]
    if ctx_tile:
        out_specs.append(pl.BlockSpec((TM, D), lambda i: (0, 0)))
        out_shape.append(jax.ShapeDtypeStruct((TM, D), F32))
    outs = pl.pallas_call(
        functools.partial(_ffn_kernel, final=final, ctx_tile=ctx_tile, mod_row=_mod_row(TM)),
        grid=(rows // TM + off,),
        in_specs=in_specs,
        out_specs=out_specs,
        out_shape=out_shape,
        scratch_shapes=[pltpu.VMEM((TM, DFF), BF16), pltpu.VMEM((RH * LANE, D), BF16)],
        compiler_params=_params(("arbitrary",)),
        name="ffn_final" if final else "ffn",
    )(*args)
    return outs if ctx_tile else outs[0]


def _take_padded(a, src, axis):
    pieces = []
    i, n = 0, len(src)
    while i < n:
        j = i + 1
        if src[i] < 0:
            while j < n and src[j] < 0:
                j += 1
            shape = list(a.shape)
            shape[axis] = j - i
            pieces.append(jnp.zeros(shape, a.dtype))
        else:
            while j < n and src[j] == src[j - 1] + 1:
                j += 1
            pieces.append(lax.slice_in_dim(a, int(src[i]), int(src[j - 1]) + 1, axis=axis))
        i = j
    return jnp.concatenate(pieces, axis=axis)


def kernel(x, c, ctx, c_ctx, w_ada, b_ada, g_mix, w_in, ret_decay_logit, ret_norm_g, w_four,
           na_rpb, w_out, g_ffn, w1, w3, w2, g_final):
    src_g = _gain_sources()
    rope_tabs = tuple(jnp.asarray(t) for t in _rope_tables())
    a1, m3, e, c256, s256 = (jnp.asarray(t).astype(BF16) for t in _dft_tables())
    sel_qk, sel_v = (jnp.asarray(t).astype(BF16) for t in _head_selectors())
    sel_v_t = sel_v.T
    tz_src, tz_mask = _na_toeplitz_tables()
    col_scale = np.ones((1, 1, W_IN), np.float32)
    col_scale[..., W_NQ:W_NK] = LOG2E * ND ** -0.5
    w_in_b = (w_in * jnp.asarray(col_scale)).astype(BF16)
    w_out_b = w_out.astype(BF16)
    w1b, w3b, w2b = w1.astype(BF16), w3.astype(BF16), w2.astype(BF16)

    tz_base = jnp.pad(_take_padded(na_rpb.astype(F32).reshape(DEPTH * NH, 15, 31), tz_src, 2), ((0, 0), (0, 1), (0, 0)))
    na_bias = _na_bias(tz_base, jnp.asarray(tz_mask))

    cvec = jnp.zeros((8, D), F32).at[0:B].set(c).at[B].set(c_ctx)
    mods_all = _ada(cvec, w_ada, b_ada).reshape(DEPTH, 8, 6, D)

    xl = x.reshape(B * L, D)
    xc = ctx.reshape(B * LC, D)
    for i in range(DEPTH):
        last = i == DEPTH - 1
        mods = mods_all[i]
        gain = _take_padded(ret_norm_g[i], src_g, 0).reshape(1, RH * LANE)
        wfour = w_four[i].astype(BF16)
        lg = jax.nn.log_sigmoid(ret_decay_logit[i].astype(F32))

        f_l, p_l, f_c, p_c, s0 = _proj(xl, xc, mods, g_mix[i], w_in_b, i, sel_qk, sel_v, rope_tabs, lg)
        four_l = _fourier(f_l, wfour, a1, m3, e)
        ret_l = _retention(p_l, s0, lg, gain)
        na_l = _natten(p_l, p_c, na_bias, i)
        if last:
            xl = _ffn(xl, four_l, ret_l, na_l, None, mods, g_ffn[i], w_out_b, sel_v_t, w1b, w3b, w2b, i, g_final)
        else:
            ctx_acts = (xc,) + tuple(_ctx_mix(f_c, p_c, lg, gain, wfour, c256, s256, e))
            xl, xc = _ffn(xl, four_l, ret_l, na_l, ctx_acts, mods, g_ffn[i], w_out_b, sel_v_t, w1b, w3b, w2b, i, None)
    return xl.reshape(B, L, D)
```

```python
import functools

import numpy as np
import jax
import jax.numpy as jnp
from jax import lax
from jax.experimental import pallas as pl
from jax.experimental.pallas import tpu as pltpu

F32 = jnp.float32
BF16 = jnp.bfloat16

D = 1024
B = 2
L = 8192
LC = 256
DEPTH = 2
GRID_W = 64
GRID_H = L // GRID_W
FW = 256
FGW = 64
RH = 4
RD = 96
NH = 6
ND = 64
NPAIR = NH // 2
DFF = 2816
EPS = 1e-6
NEG = -1e30
LOG2E = float(np.log2(np.e))
LANE = 128

C_RQ, C_RK, C_RV, C_RG = 0, 512, 1024, 1536
C_NQ, C_NK, C_NV = 2048, 2432, 2816
NP = 3200
W_RQ, W_RK, W_RV, W_RG = 256, 640, 1024, 1408
W_NQ, W_NK, W_NV = 1792, 2176, 2560
W_IN = 2944

TM = 512
TM_PROJ = 1024
RET_C = 256
NA_R = 4
NA_KR = NA_R + 8
NA_NB = GRID_H // NA_R
DFT_N1 = 64
DFT_N2 = 128
F1_R = 16
F3_K = 8
VMEM_LIMIT = 56 * 1024 * 1024


def _qk_lane_dims():
    m = np.full((LANE,), -1, np.int64)
    m[0:24] = np.arange(0, 24)
    m[24:48] = np.arange(48, 72)
    m[64:88] = np.arange(24, 48)
    m[88:112] = np.arange(72, 96)
    return m


def _v_lane_dims():
    m = np.full((LANE,), -1, np.int64)
    m[:RD] = np.arange(RD)
    return m


def _head_selectors():
    sels = []
    for lanes in (_qk_lane_dims(), _v_lane_dims()):
        sel = np.zeros((RH * RD, RH * LANE), np.float32)
        ok = np.nonzero(lanes >= 0)[0]
        for h in range(RH):
            sel[RD * h + lanes[ok], LANE * h + ok] = 1.0
        sels.append(sel)
    return sels


def _gain_sources():
    src = np.full((RH * LANE,), -1, np.int64)
    for h in range(RH):
        src[LANE * h: LANE * h + RD] = RD * h + np.arange(RD)
    return src


def _rope_tables():
    pos = np.arange(L)
    prow, pcol = pos // GRID_W, pos % GRID_W
    half = RD // 4
    inv = 10000.0 ** (-np.arange(half, dtype=np.float64) / half)
    ar = prow[:, None] * inv[None, :]
    ac = pcol[:, None] * inv[None, :]
    cos = np.ones((L, LANE), np.float64)
    sin = np.zeros((L, LANE), np.float64)
    for off, sign in ((0, -1.0), (64, 1.0)):
        cos[:, off:off + 24] = np.cos(ar)
        cos[:, off + 24:off + 48] = np.cos(ac)
        sin[:, off:off + 24] = sign * np.sin(ar)
        sin[:, off + 24:off + 48] = sign * np.sin(ac)
    return cos.astype(np.float32), sin.astype(np.float32)


def _dft_tables():
    k1 = np.arange(DFT_N1)
    a = 2 * np.pi * ((k1[:, None] * k1[None, :]) % DFT_N1) / DFT_N1
    a1 = np.kron(np.concatenate([np.cos(a), -np.sin(a)], axis=0), np.eye(F1_R))
    k2 = np.arange(DFT_N2)
    l2 = np.arange(DFT_N2)
    kk = k1[:, None, None] + DFT_N1 * k2[None, :, None]
    ang = 2 * np.pi * ((kk * l2[None, None, :]) % L) / L
    ct, st = np.cos(ang) / np.sqrt(L), np.sin(ang) / np.sqrt(L)
    m3 = np.concatenate([np.concatenate([ct, st], axis=2), np.concatenate([st, -ct], axis=2)], axis=1)
    c = np.arange(FW)
    same = (c[:, None] // FGW) == (c[None, :] // FGW)
    ac = 2 * np.pi * (((c[:, None] % FGW) * (c[None, :] % FGW)) % FGW) / FGW
    e = np.concatenate([np.where(same, np.cos(ac), 0.0), -np.where(same, np.sin(ac), 0.0)], axis=0) / 8.0
    p = np.arange(LC)
    ap = 2 * np.pi * ((p[:, None] * p[None, :]) % LC) / LC
    return tuple(t.astype(np.float32) for t in (a1, m3, e, np.cos(ap) / 16.0, np.sin(ap) / 16.0))


NA_DR = 16


def _na_toeplitz_tables():
    src = np.full((LANE,), -1, np.int64)
    src[0:16] = np.arange(0, 16) + 15
    src[113:128] = np.arange(113, 128) - 113
    src[49:80] = np.arange(49, 80) - 49
    qc = np.arange(GRID_W)[:, None]
    kc = np.arange(LANE)[None, :] % GRID_W
    ws = np.clip(qc - 8, 0, GRID_W - 16)
    mask = np.where((kc >= ws) & (kc < ws + 16), 0.0, NEG).astype(np.float32)
    return src, mask


def _dot(a, b):
    return jnp.dot(a, b, preferred_element_type=F32)


def _dot_nt(a, b):
    return lax.dot_general(a, b, (((1,), (1,)), ((), ())), preferred_element_type=F32)


def _silu(x):
    return x * jax.nn.sigmoid(x)


def _rmsnorm(x, g):
    return x * lax.rsqrt(jnp.mean(x * x, axis=-1, keepdims=True) + EPS) * g


def _ada_kernel(c_ref, w_ref, b_ref, o_ref):
    s = _silu(c_ref[...]).astype(BF16)
    o_ref[0] = _dot(s, w_ref[0].astype(BF16)) + b_ref[0]


def _proj_kernel(x_ref, xc_ref, mods_ref, g_ref, w_ref, pqk_ref, pv_ref, cos_ref, sin_ref, lg_ref,
                 f_ref, o_ref, fc_ref, oc_ref, s0_ref, wret_ref, *, mod_row):
    s = pl.program_id(0)

    def project(x, mod, rope, f_out, o_out):
        hb = (_rmsnorm(x, g_ref[...]) * (1.0 + mod[1:2, :]) + mod[0:1, :]).astype(BF16)
        f_out[...] = _dot(hb, w_ref[:, :FW]).astype(BF16)
        k_scale = RD ** -0.5
        for c0, scale in ((C_RQ, None), (C_RK, k_scale)):
            t4 = _dot(hb, wret_ref[:, c0:c0 + RH * LANE])
            for hh in range(RH):
                t = t4[:, LANE * hh:LANE * (hh + 1)]
                if rope:
                    t = t * cos_ref[...] + pltpu.roll(t, 64, 1) * sin_ref[...]
                if scale is not None:
                    t = t * scale
                o_out[:, c0 + LANE * hh:c0 + LANE * (hh + 1)] = t.astype(BF16)
        o_out[:, C_RV:C_RG] = _dot(hb, wret_ref[:, C_RV:C_RG]).astype(BF16)
        o_out[:, C_RG:C_NQ] = _silu(_dot(hb, wret_ref[:, C_RG:C_NQ])).astype(BF16)
        o_out[:, C_NQ:] = _dot(hb, w_ref[:, W_NQ:]).astype(BF16)

    @pl.when(s == 0)
    def _():
        for slot, (src0, sel_ref) in enumerate(((W_RQ, pqk_ref), (W_RK, pqk_ref), (W_RV, pv_ref), (W_RG, pv_ref))):
            wret_ref[:, RH * LANE * slot:RH * LANE * (slot + 1)] = _dot(
                w_ref[:, src0:src0 + RH * RD], sel_ref[...]).astype(BF16)
        project(xc_ref[...], mods_ref[B], False, fc_ref, oc_ref)
        idx = lax.broadcasted_iota(jnp.int32, (LC, LANE), 0).astype(F32)
        for b in range(B):
            for h in range(RH):
                k = oc_ref[LC * b:LC * (b + 1), C_RK + LANE * h:C_RK + LANE * (h + 1)].astype(F32)
                v = oc_ref[LC * b:LC * (b + 1), C_RV + LANE * h:C_RV + LANE * (h + 1)]
                wf = jnp.exp((LC - 1.0 - idx) * lg_ref[0, h])
                wb = jnp.exp(idx * lg_ref[1, h])
                s0_ref[b, h, 0] = _dot((k * wf).T.astype(BF16), v)
                s0_ref[b, h, 1] = _dot((k * wb).T.astype(BF16), v)

    @pl.when(s > 0)
    def _():
        project(x_ref[...], mods_ref[mod_row(s - 1)], True, f_ref, o_ref)


def _four1_kernel(a_ref, x_ref, y_ref):
    x = x_ref[0].reshape(DFT_N1 * F1_R, FW)
    y = _dot(a_ref[...], x).astype(BF16)
    y_ref[0] = y.reshape(2 * DFT_N1, F1_R, FW)


def _four3_kernel(m_ref, y_ref, e_ref, wf_ref, o_ref):
    ew = _dot(e_ref[...], wf_ref[...]).astype(BF16)
    for b in range(B):
        for kk in range(F3_K):
            y = jnp.concatenate([y_ref[b, 0, kk], y_ref[b, 1, kk]], axis=0)
            z = _dot(m_ref[kk], y).astype(BF16)
            o_ref[b, kk] = _dot(jnp.concatenate([z[:DFT_N2], z[DFT_N2:]], axis=1), ew)


def _decay_matrix(n, lgf, lgb):
    ii = lax.broadcasted_iota(jnp.int32, (n, n), 0)
    jj = lax.broadcasted_iota(jnp.int32, (n, n), 1)
    diff = (ii - jj).astype(F32)
    fwd = jnp.where(diff >= 0, jnp.exp(jnp.maximum(diff, 0.0) * lgf), 0.0)
    bwd = jnp.where(diff <= 0, jnp.exp(jnp.maximum(-diff, 0.0) * lgb), 0.0)
    return fwd + bwd


def _ret_readout(o, gate, gain):
    ms = jnp.sum(o * o, axis=-1, keepdims=True) * (1.0 / RD)
    return (o * lax.rsqrt(ms + EPS) * gain * gate.astype(F32)).astype(BF16)


def _ret_kernel(lg_ref, q_ref, k_ref, v_ref, g_ref, s0_ref, gain_ref, o_ref, acc_ref, stf_ref, stb_ref):
    h = pl.program_id(1)
    lgf = lg_ref[0, h]
    lgb = lg_ref[1, h]
    c = RET_C
    n = L // c
    dmat = _decay_matrix(c, lgf, lgb)
    idx = lax.broadcasted_iota(jnp.int32, (c, LANE), 0).astype(F32)
    qdf = jnp.exp((idx + 1.0) * lgf)
    kdf = jnp.exp((c - 1.0 - idx) * lgf)
    qdb = jnp.exp((c - idx) * lgb)
    kdb = jnp.exp(idx * lgb)
    cdf = jnp.exp(jnp.full((1, LANE), float(c), F32) * lgf)
    cdb = jnp.exp(jnp.full((1, LANE), float(c), F32) * lgb)
    gain = gain_ref[...]

    dirs = ((stf_ref, qdf, kdf, cdf), (stb_ref, qdb, kdb, cdb))
    stf_ref[...] = s0_ref[0, 0, 0]
    stb_ref[...] = s0_ref[0, 0, 1]

    def visit(r0, direction, second):
        st_ref, qdec, kdec, cdec = dirs[direction]
        rows = pl.ds(pl.multiple_of(r0, c), c)
        q = q_ref[rows, :]
        k = k_ref[rows, :]
        v = v_ref[rows, :]
        cross = _dot(q, st_ref[...].astype(BF16)) * qdec
        if second:
            o_ref[rows, :] = _ret_readout(acc_ref[rows, :] + cross, g_ref[rows, :], gain)
        else:
            s = _dot_nt(q, k) * dmat
            acc_ref[rows, :] = _dot(s.astype(BF16), v) + cross
        kt = (k.astype(F32) * kdec).T.astype(BF16)
        st_ref[...] = st_ref[...] * cdec + _dot(kt, v)

    def body(second):
        def step(i, carry):
            visit(i * c, 0, second)
            visit((n - 1 - i) * c, 1, second)
            return carry
        return step

    lax.fori_loop(0, n // 2, body(False), 0, unroll=16)
    lax.fori_loop(n // 2, n, body(True), 0, unroll=16)


def _pair_attention(q, score_fn, value_fn):
    lane = lax.broadcasted_iota(jnp.int32, q.shape, 1)
    outs = []
    for hh in range(2):
        sel = (lane >= ND * hh) & (lane < ND * (hh + 1))
        qm = jnp.where(sel, q, jnp.zeros_like(q))
        scores = score_fn(qm, hh)
        m = functools.reduce(jnp.maximum, [jnp.max(s, axis=-1, keepdims=True) for s in scores])
        ps = [jnp.exp2(s - m) for s in scores]
        den = functools.reduce(jnp.add, [jnp.sum(p, axis=-1, keepdims=True) for p in ps])
        outs.append(value_fn([p.astype(BF16) for p in ps]) / den)
    return jnp.where(lane < ND, outs[0], outs[1])


def _ctx_mix_kernel(lg_ref, f_ref, p_ref, gain_ref, wf_ref, c_ref, s_ref, e_ref, fo_ref, ro_ref, no_ref):
    f = f_ref[...]
    pr = _dot(c_ref[...], f)
    qi = _dot(s_ref[...], f)
    fr = _dot(pr.astype(BF16), e_ref[:FW, :]) + _dot(qi.astype(BF16), e_ref[FW:, :])
    fo_ref[...] = _dot(fr.astype(BF16), wf_ref[...]).astype(BF16)
    for h in range(RH):
        sl = lambda c0: slice(c0 + LANE * h, c0 + LANE * (h + 1))
        dmat = _decay_matrix(LC, lg_ref[0, h], lg_ref[1, h])
        s = _dot_nt(p_ref[:, sl(C_RQ)], p_ref[:, sl(C_RK)]) * dmat
        o = _dot(s.astype(BF16), p_ref[:, sl(C_RV)])
        ro_ref[:, LANE * h:LANE * (h + 1)] = _ret_readout(o, p_ref[:, sl(C_RG)], gain_ref[:, LANE * h:LANE * (h + 1)])
    for pair in range(NPAIR):
        sl = lambda c0: slice(c0 + LANE * pair, c0 + LANE * (pair + 1))
        k = p_ref[:, sl(C_NK)]
        v = p_ref[:, sl(C_NV)]
        out = _pair_attention(p_ref[:, sl(C_NQ)], lambda qm, hh: [_dot_nt(qm, k)], lambda ps: _dot(ps[0], v))
        no_ref[:, LANE * pair:LANE * (pair + 1)] = out.astype(BF16)


def _na_key_start(r0):
    return min(max(r0 - 4, 0), GRID_H - NA_KR)


def _na_bias_kernel(base_ref, mask_ref, o_ref, tz_ref):
    for dr in range(NA_DR - 1):
        t = jnp.broadcast_to(base_ref[0, dr:dr + 1, :], (GRID_W, LANE))
        t = pltpu.roll(t, 0, 1, stride=1, stride_axis=0)
        tz_ref[dr] = t * LOG2E + mask_ref[...]
    tz_ref[NA_DR - 1] = jnp.full((GRID_W, LANE), NEG, F32)
    low_half = lax.broadcasted_iota(jnp.int32, (GRID_W, LANE), 1) < GRID_W
    for var, r0 in enumerate((0, 2 * NA_R, GRID_H - NA_R)):
        ks = _na_key_start(r0)
        for rl in range(NA_R):
            r = r0 + rl
            rs = min(max(r - 4, 0), GRID_H - 8)
            slot = [kr - r + 7 if rs <= kr < rs + 8 else NA_DR - 1 for kr in range(ks, ks + NA_KR)]
            for m in range(NA_KR // 2):
                o_ref[0, var, GRID_W * rl:GRID_W * (rl + 1), LANE * m:LANE * (m + 1)] = jnp.where(
                    low_half, tz_ref[slot[2 * m]], tz_ref[slot[2 * m + 1]])


def _natten_kernel(q_ref, k_ref, v_ref, kc_ref, vc_ref, bias_ref, o_ref, pa_ref, pb_ref):
    nq = NA_R * GRID_W
    nk = NA_KR * GRID_W
    kc = kc_ref[...]
    vc = vc_ref[...]
    lane = lax.broadcasted_iota(jnp.int32, (nq, LANE), 1)

    def rows(i):
        ks = jnp.clip(NA_R * i - 4, 0, GRID_H - NA_KR)
        return pl.ds(pl.multiple_of(ks * GRID_W, nq), nk), pl.ds(pl.multiple_of(i * nq, nq), nq)

    def probabilities(i, p_ref):
        i = jnp.minimum(i, NA_NB - 1)
        var = jnp.where(i == 0, 0, jnp.where(i == NA_NB - 1, 2, 1))
        krows, qrows = rows(i)
        q = q_ref[qrows, :]
        kw = k_ref[krows, :]
        for hh in range(2):
            qm = jnp.where((lane >= ND * hh) & (lane < ND * (hh + 1)), q, jnp.zeros_like(q))
            s_win = _dot_nt(qm, kw) + bias_ref[hh, var]
            s_ctx = _dot_nt(qm, kc)
            m = jnp.maximum(jnp.max(s_win, axis=-1, keepdims=True), jnp.max(s_ctx, axis=-1, keepdims=True))
            p_ref[hh, :, :nk] = jnp.exp2(s_win - m).astype(BF16)
            p_ref[hh, :, nk:] = jnp.exp2(s_ctx - m).astype(BF16)

    def values(i, p_ref):
        krows, qrows = rows(i)
        vals = jnp.concatenate([v_ref[krows, :], vc], axis=0)
        vals = jnp.concatenate([vals, jnp.ones_like(vals)], axis=1)
        o2 = _dot(p_ref[...].reshape(2 * nq, nk + LC), vals)
        outs = [o[:, :LANE] / o[:, LANE:] for o in (o2[:nq], o2[nq:])]
        o_ref[qrows, :] = jnp.where(lane < ND, outs[0], outs[1]).astype(BF16)

    probabilities(0, pa_ref)

    def step(j, carry):
        values(2 * j, pa_ref)
        probabilities(2 * j + 1, pb_ref)
        values(2 * j + 1, pb_ref)
        probabilities(2 * j + 2, pa_ref)
        return carry

    lax.fori_loop(0, NA_NB // 2, step, 0, unroll=8)


def _ffn_kernel(*refs, final, ctx_tile, mod_row):
    refs = list(refs)
    x_ref, f_ref, r_ref, n_ref = refs[:4]
    del refs[:4]
    if ctx_tile:
        xc_ref, fc_ref, rc_ref, nc_ref = refs[:4]
        del refs[:4]
    mods_ref, g_ref, wo_ref, sel_ref, w1_ref, w3_ref, w2_ref = refs[:7]
    del refs[:7]
    gf_ref = refs.pop(0) if final else None
    o_ref = refs.pop(0)
    oc_ref = refs.pop(0) if ctx_tile else None
    acc_ref, wor_ref = refs
    s = pl.program_id(0)

    @pl.when(s == 0)
    def _():
        wor_ref[...] = _dot(sel_ref[...], wo_ref[FW:FW + RH * RD, :]).astype(BF16)

    def tile(x, f, r, n, mod, out_ref):
        y = _dot(f.astype(BF16), wo_ref[:FW, :]) + _dot(r, wor_ref[...]) + _dot(n, wo_ref[FW + RH * RD:, :])
        x1 = x + mod[2:3, :] * y
        hb = (_rmsnorm(x1, g_ref[...]) * (1.0 + mod[4:5, :]) + mod[3:4, :]).astype(BF16)
        fc = 256
        for c in range(DFF // fc):
            a = _dot(hb, w1_ref[:, c * fc:(c + 1) * fc])
            b = _dot(hb, w3_ref[:, c * fc:(c + 1) * fc])
            acc_ref[:, c * fc:(c + 1) * fc] = (_silu(a) * b).astype(BF16)
        x2 = x1 + mod[5:6, :] * _dot(acc_ref[...], w2_ref[...])
        if final:
            x2 = _rmsnorm(x2, gf_ref[...])
        out_ref[...] = x2

    def latent_tile(t):
        f = jnp.concatenate([f_ref[:, r, :] for r in range(f_ref.shape[1])], axis=0)
        tile(x_ref[...], f, r_ref[...], n_ref[...], mods_ref[mod_row(t)], o_ref)

    if ctx_tile:
        @pl.when(s == 0)
        def _():
            tile(xc_ref[...], fc_ref[...], rc_ref[...], nc_ref[...], mods_ref[B], oc_ref)

        @pl.when(s > 0)
        def _():
            latent_tile(s - 1)
    else:
        latent_tile(s)


def _params(sem):
    return pltpu.CompilerParams(dimension_semantics=sem, vmem_limit_bytes=VMEM_LIMIT)


def _const_spec(shape):
    nd = len(shape)
    return pl.BlockSpec(shape, lambda *_: (0,) * nd, pipeline_mode=pl.Buffered(1))


def _smem_spec():
    return pl.BlockSpec(memory_space=pltpu.SMEM)


def _ada(cvec, w_ada, b_ada):
    tn = 2048
    return pl.pallas_call(
        _ada_kernel,
        grid=(DEPTH, 6 * D // tn),
        in_specs=[pl.BlockSpec((8, D), lambda i, j: (0, 0)),
                  pl.BlockSpec((1, D, tn), lambda i, j: (i, 0, j)),
                  pl.BlockSpec((1, 1, tn), lambda i, j: (i, 0, j))],
        out_specs=pl.BlockSpec((1, 8, tn), lambda i, j: (i, 0, j)),
        out_shape=jax.ShapeDtypeStruct((DEPTH, 8, 6 * D), F32),
        compiler_params=_params(("arbitrary", "arbitrary")),
        name="ada",
    )(cvec, w_ada, b_ada.reshape(DEPTH, 1, 6 * D))


def _mod_row(tm):
    per_batch = L // tm
    return lambda tile: tile // per_batch


def _layer_spec(shape, layer):
    nd = len(shape)
    return pl.BlockSpec((None,) + tuple(shape), lambda *_: (layer,) + (0,) * nd, pipeline_mode=pl.Buffered(1))


def _proj(x2d, xc2d, mods, g, w_all, layer, sel_qk, sel_v, rope_tabs, lg):
    rows, crows = x2d.shape[0], xc2d.shape[0]
    assert crows == B * LC
    tm = TM_PROJ
    per_batch = L // tm
    tile = lambda i: jnp.maximum(i - 1, 0)
    return pl.pallas_call(
        functools.partial(_proj_kernel, mod_row=_mod_row(tm)),
        grid=(rows // tm + 1,),
        in_specs=[pl.BlockSpec((tm, D), lambda i: (tile(i), 0)),
                  _const_spec((crows, D)),
                  _const_spec((8, 6, D)),
                  _const_spec((1, D)),
                  _layer_spec((D, W_IN), layer),
                  _const_spec((RH * RD, RH * LANE)),
                  _const_spec((RH * RD, RH * LANE)),
                  pl.BlockSpec((tm, LANE), lambda i: (tile(i) % per_batch, 0)),
                  pl.BlockSpec((tm, LANE), lambda i: (tile(i) % per_batch, 0)),
                  _smem_spec()],
        out_specs=[pl.BlockSpec((tm, FW), lambda i: (tile(i), 0)),
                   pl.BlockSpec((tm, NP), lambda i: (tile(i), 0)),
                   pl.BlockSpec((crows, FW), lambda i: (0, 0)),
                   pl.BlockSpec((crows, NP), lambda i: (0, 0)),
                   pl.BlockSpec((B, RH, 2, LANE, LANE), lambda i: (0, 0, 0, 0, 0))],
        out_shape=[jax.ShapeDtypeStruct((rows, FW), BF16), jax.ShapeDtypeStruct((rows, NP), BF16),
                   jax.ShapeDtypeStruct((crows, FW), BF16), jax.ShapeDtypeStruct((crows, NP), BF16),
                   jax.ShapeDtypeStruct((B, RH, 2, LANE, LANE), F32)],
        scratch_shapes=[pltpu.VMEM((D, 4 * RH * LANE), BF16)],
        compiler_params=_params(("arbitrary",)),
        name="proj",
    )(x2d, xc2d, mods, g.reshape(1, D), w_all, sel_qk, sel_v, *rope_tabs, lg)


def _fourier(f2d, wf, a1, m3, e):
    y = pl.pallas_call(
        _four1_kernel,
        grid=(B, DFT_N2 // F1_R),
        in_specs=[_const_spec((2 * DFT_N1 * F1_R, DFT_N1 * F1_R)),
                  pl.BlockSpec((1, DFT_N1, F1_R, FW), lambda b, j: (b, 0, j, 0))],
        out_specs=pl.BlockSpec((1, 2 * DFT_N1, F1_R, FW), lambda b, j: (b, 0, j, 0)),
        out_shape=jax.ShapeDtypeStruct((B, 2 * DFT_N1, DFT_N2, FW), BF16),
        compiler_params=_params(("arbitrary", "arbitrary")),
        name="four1",
    )(a1, f2d.reshape(B, DFT_N1, DFT_N2, FW))
    out = pl.pallas_call(
        _four3_kernel,
        grid=(DFT_N1 // F3_K,),
        in_specs=[pl.BlockSpec((F3_K, 2 * DFT_N2, 2 * DFT_N2), lambda k: (k, 0, 0)),
                  pl.BlockSpec((B, 2, F3_K, DFT_N2, FW), lambda k: (0, 0, k, 0, 0)),
                  _const_spec((2 * FW, FW)),
                  _const_spec((FW, FW))],
        out_specs=pl.BlockSpec((B, F3_K, DFT_N2, FW), lambda k: (0, k, 0, 0)),
        out_shape=jax.ShapeDtypeStruct((B, DFT_N1, DFT_N2, FW), F32),
        compiler_params=_params(("arbitrary",)),
        name="four3",
    )(m3, y.reshape(B, 2, DFT_N1, DFT_N2, FW), e, wf)
    return out.reshape(B * DFT_N1, DFT_N2, FW)


def _retention(p, s0, lg, gain):
    col = lambda c0: (lambda b, h: (b, c0 // LANE + h))
    return pl.pallas_call(
        _ret_kernel,
        grid=(B, RH),
        in_specs=[_smem_spec(),
                  pl.BlockSpec((L, LANE), col(C_RQ)),
                  pl.BlockSpec((L, LANE), col(C_RK)),
                  pl.BlockSpec((L, LANE), col(C_RV)),
                  pl.BlockSpec((L, LANE), col(C_RG)),
                  pl.BlockSpec((1, 1, 2, LANE, LANE), lambda b, h: (b, h, 0, 0, 0)),
                  pl.BlockSpec((1, LANE), lambda b, h: (0, h))],
        out_specs=pl.BlockSpec((L, LANE), lambda b, h: (b, h)),
        out_shape=jax.ShapeDtypeStruct((B * L, RH * LANE), BF16),
        scratch_shapes=[pltpu.VMEM((L, LANE), F32), pltpu.VMEM((LANE, LANE), F32), pltpu.VMEM((LANE, LANE), F32)],
        compiler_params=_params(("arbitrary", "arbitrary")),
        name="ret",
    )(lg, p, p, p, p, s0, gain)


def _ctx_mix(fc, pc, lg, gain, wf, c256, s256, e):
    return pl.pallas_call(
        _ctx_mix_kernel,
        grid=(B,),
        in_specs=[_smem_spec(),
                  pl.BlockSpec((LC, FW), lambda b: (b, 0)),
                  pl.BlockSpec((LC, NP), lambda b: (b, 0)),
                  _const_spec((1, RH * LANE)),
                  _const_spec((FW, FW)),
                  _const_spec((LC, LC)),
                  _const_spec((LC, LC)),
                  _const_spec((2 * FW, FW))],
        out_specs=[pl.BlockSpec((LC, FW), lambda b: (b, 0)),
                   pl.BlockSpec((LC, RH * LANE), lambda b: (b, 0)),
                   pl.BlockSpec((LC, NH * ND), lambda b: (b, 0))],
        out_shape=[jax.ShapeDtypeStruct((B * LC, FW), BF16),
                   jax.ShapeDtypeStruct((B * LC, RH * LANE), BF16),
                   jax.ShapeDtypeStruct((B * LC, NH * ND), BF16)],
        compiler_params=_params(("arbitrary",)),
        name="ctx_mix",
    )(lg, fc, pc, gain, wf, c256, s256, e)


def _na_bias(base, mask):
    nq, nk = NA_R * GRID_W, NA_KR * GRID_W
    return pl.pallas_call(
        _na_bias_kernel,
        grid=(base.shape[0],),
        in_specs=[pl.BlockSpec((1, NA_DR, LANE), lambda h: (h, 0, 0)),
                  _const_spec((GRID_W, LANE))],
        out_specs=pl.BlockSpec((1, 3, nq, nk), lambda h: (h, 0, 0, 0)),
        out_shape=jax.ShapeDtypeStruct((base.shape[0], 3, nq, nk), F32),
        scratch_shapes=[pltpu.VMEM((NA_DR, GRID_W, LANE), F32)],
        compiler_params=_params(("arbitrary",)),
        name="na_bias",
    )(base, mask)


def _natten(p, pc, bias, layer):
    nq, nk = NA_R * GRID_W, NA_KR * GRID_W
    return pl.pallas_call(
        _natten_kernel,
        grid=(B, NPAIR),
        in_specs=[pl.BlockSpec((L, LANE), lambda b, pr: (b, C_NQ // LANE + pr)),
                  pl.BlockSpec((L, LANE), lambda b, pr: (b, C_NK // LANE + pr)),
                  pl.BlockSpec((L, LANE), lambda b, pr: (b, C_NV // LANE + pr)),
                  pl.BlockSpec((LC, LANE), lambda b, pr: (b, C_NK // LANE + pr)),
                  pl.BlockSpec((LC, LANE), lambda b, pr: (b, C_NV // LANE + pr)),
                  pl.BlockSpec((2, 3, nq, nk), lambda b, pr: (NPAIR * layer + pr, 0, 0, 0))],
        out_specs=pl.BlockSpec((L, LANE), lambda b, pr: (b, pr)),
        out_shape=jax.ShapeDtypeStruct((B * L, NH * ND), BF16),
        scratch_shapes=[pltpu.VMEM((2, nq, nk + LC), BF16)] * 2,
        compiler_params=_params(("arbitrary", "arbitrary")),
        name="natten",
    )(p, p, p, pc, pc, bias)


def _ffn(x2d, four, ret, na, ctx_acts, mods, g, wo_all, sel_v_t, w1_all, w3_all, w2_all, layer, g_final):
    rows = x2d.shape[0]
    final = g_final is not None
    ctx_tile = ctx_acts is not None
    off = 1 if ctx_tile else 0
    per_batch = L // TM
    tile = lambda i: jnp.maximum(i - off, 0)
    row = lambda w: pl.BlockSpec((TM, w), lambda i: (tile(i), 0))
    four_spec = pl.BlockSpec((DFT_N1, TM // DFT_N1, FW), lambda i: (tile(i) // per_batch, tile(i) % per_batch, 0))
    in_specs = [row(D), four_spec, row(RH * LANE), row(NH * ND)]
    args = [x2d, four, ret, na]
    if ctx_tile:
        assert all(a.shape[0] == TM for a in ctx_acts)
        in_specs += [_const_spec(a.shape) for a in ctx_acts]
        args += list(ctx_acts)
    in_specs += [_const_spec((8, 6, D)), _const_spec((1, D)),
                 _layer_spec((D, D), layer), _const_spec((RH * LANE, RH * RD)),
                 _layer_spec((D, DFF), layer), _layer_spec((D, DFF), layer), _layer_spec((DFF, D), layer)]
    args += [mods, g.reshape(1, D), wo_all, sel_v_t, w1_all, w3_all, w2_all]
    if final:
        in_specs.append(_const_spec((1, D)))
        args.append(g_final.reshape(1, D))
    out_specs = [row(D)]
    out_shape = [jax.ShapeDtypeStruct((rows, D), F32)]
    if ctx_tile:
        out_specs.append(pl.BlockSpec((TM, D), lambda i: (0, 0)))
        out_shape.append(jax.ShapeDtypeStruct((TM, D), F32))
    outs = pl.pallas_call(
        functools.partial(_ffn_kernel, final=final, ctx_tile=ctx_tile, mod_row=_mod_row(TM)),
        grid=(rows // TM + off,),
        in_specs=in_specs,
        out_specs=out_specs,
        out_shape=out_shape,
        scratch_shapes=[pltpu.VMEM((TM, DFF), BF16), pltpu.VMEM((RH * LANE, D), BF16)],
        compiler_params=_params(("arbitrary",)),
        name="ffn_final" if final else "ffn",
    )(*args)
    return outs if ctx_tile else outs[0]


def _take_padded(a, src, axis):
    pieces = []
    i, n = 0, len(src)
    while i < n:
        j = i + 1
        if src[i] < 0:
            while j < n and src[j] < 0:
                j += 1
            shape = list(a.shape)
            shape[axis] = j - i
            pieces.append(jnp.zeros(shape, a.dtype))
        else:
            while j < n and src[j] == src[j - 1] + 1:
                j += 1
            pieces.append(lax.slice_in_dim(a, int(src[i]), int(src[j - 1]) + 1, axis=axis))
        i = j
    return jnp.concatenate(pieces, axis=axis)


def kernel(x, c, ctx, c_ctx, w_ada, b_ada, g_mix, w_in, ret_decay_logit, ret_norm_g, w_four,
           na_rpb, w_out, g_ffn, w1, w3, w2, g_final):
    src_g = _gain_sources()
    rope_tabs = tuple(jnp.asarray(t) for t in _rope_tables())
    a1, m3, e, c256, s256 = (jnp.asarray(t).astype(BF16) for t in _dft_tables())
    sel_qk, sel_v = (jnp.asarray(t).astype(BF16) for t in _head_selectors())
    sel_v_t = sel_v.T
    tz_src, tz_mask = _na_toeplitz_tables()
    col_scale = np.ones((1, 1, W_IN), np.float32)
    col_scale[..., W_NQ:W_NK] = LOG2E * ND ** -0.5
    w_in_b = (w_in * jnp.asarray(col_scale)).astype(BF16)
    w_out_b = w_out.astype(BF16)
    w1b, w3b, w2b = w1.astype(BF16), w3.astype(BF16), w2.astype(BF16)

    tz_base = jnp.pad(_take_padded(na_rpb.astype(F32).reshape(DEPTH * NH, 15, 31), tz_src, 2), ((0, 0), (0, 1), (0, 0)))
    na_bias = _na_bias(tz_base, jnp.asarray(tz_mask))

    cvec = jnp.zeros((8, D), F32).at[0:B].set(c).at[B].set(c_ctx)
    mods_all = _ada(cvec, w_ada, b_ada).reshape(DEPTH, 8, 6, D)

    xl = x.reshape(B * L, D)
    xc = ctx.reshape(B * LC, D)
    for i in range(DEPTH):
        last = i == DEPTH - 1
        mods = mods_all[i]
        gain = _take_padded(ret_norm_g[i], src_g, 0).reshape(1, RH * LANE)
        wfour = w_four[i].astype(BF16)
        lg = jax.nn.log_sigmoid(ret_decay_logit[i].astype(F32))

        f_l, p_l, f_c, p_c, s0 = _proj(xl, xc, mods, g_mix[i], w_in_b, i, sel_qk, sel_v, rope_tabs, lg)
        four_l = _fourier(f_l, wfour, a1, m3, e)
        ret_l = _retention(p_l, s0, lg, gain)
        na_l = _natten(p_l, p_c, na_bias, i)
        if last:
            xl = _ffn(xl, four_l, ret_l, na_l, None, mods, g_ffn[i], w_out_b, sel_v_t, w1b, w3b, w2b, i, g_final)
        else:
            ctx_acts = (xc,) + tuple(_ctx_mix(f_c, p_c, lg, gain, wfour, c256, s256, e))
            xl, xc = _ffn(xl, four_l, ret_l, na_l, ctx_acts, mods, g_ffn[i], w_out_b, sel_v_t, w1b, w3b, w2b, i, None)
    return xl.reshape(B, L, D)
```

```python
import functools

import numpy as np
import jax
import jax.numpy as jnp
from jax import lax
from jax.experimental import pallas as pl
from jax.experimental.pallas import tpu as pltpu

F32 = jnp.float32
BF16 = jnp.bfloat16

D = 1024
B = 2
L = 8192
LC = 256
DEPTH = 2
GRID_W = 64
GRID_H = L // GRID_W
FW = 256
FGW = 64
RH = 4
RD = 96
NH = 6
ND = 64
NPAIR = NH // 2
DFF = 2816
EPS = 1e-6
NEG = -1e30
LOG2E = float(np.log2(np.e))
LANE = 128

C_RQ, C_RK, C_RV, C_RG = 0, 512, 1024, 1536
C_NQ, C_NK, C_NV = 2048, 2432, 2816
NP = 3200
W_RQ, W_RK, W_RV, W_RG = 256, 640, 1024, 1408
W_NQ, W_NK, W_NV = 1792, 2176, 2560
W_IN = 2944

TM = 512
TM_PROJ = 1024
RET_C = 256
NA_R = 4
NA_KR = NA_R + 8
NA_NB = GRID_H // NA_R
DFT_N1 = 64
DFT_N2 = 128
F1_R = 16
F3_K = 16
VMEM_LIMIT = 56 * 1024 * 1024


def _qk_lane_dims():
    m = np.full((LANE,), -1, np.int64)
    m[0:24] = np.arange(0, 24)
    m[24:48] = np.arange(48, 72)
    m[64:88] = np.arange(24, 48)
    m[88:112] = np.arange(72, 96)
    return m


def _v_lane_dims():
    m = np.full((LANE,), -1, np.int64)
    m[:RD] = np.arange(RD)
    return m


def _head_selectors():
    sels = []
    for lanes in (_qk_lane_dims(), _v_lane_dims()):
        sel = np.zeros((RH * RD, RH * LANE), np.float32)
        ok = np.nonzero(lanes >= 0)[0]
        for h in range(RH):
            sel[RD * h + lanes[ok], LANE * h + ok] = 1.0
        sels.append(sel)
    return sels


def _gain_sources():
    src = np.full((RH * LANE,), -1, np.int64)
    for h in range(RH):
        src[LANE * h: LANE * h + RD] = RD * h + np.arange(RD)
    return src


def _rope_tables():
    pos = np.arange(L)
    prow, pcol = pos // GRID_W, pos % GRID_W
    half = RD // 4
    inv = 10000.0 ** (-np.arange(half, dtype=np.float64) / half)
    ar = prow[:, None] * inv[None, :]
    ac = pcol[:, None] * inv[None, :]
    cos = np.ones((L, LANE), np.float64)
    sin = np.zeros((L, LANE), np.float64)
    for off, sign in ((0, -1.0), (64, 1.0)):
        cos[:, off:off + 24] = np.cos(ar)
        cos[:, off + 24:off + 48] = np.cos(ac)
        sin[:, off:off + 24] = sign * np.sin(ar)
        sin[:, off + 24:off + 48] = sign * np.sin(ac)
    return cos.astype(np.float32), sin.astype(np.float32)


def _dft_tables():
    k1 = np.arange(DFT_N1)
    a = 2 * np.pi * ((k1[:, None] * k1[None, :]) % DFT_N1) / DFT_N1
    a1 = np.kron(np.concatenate([np.cos(a), -np.sin(a)], axis=0), np.eye(F1_R))
    k2 = np.arange(DFT_N2)
    l2 = np.arange(DFT_N2)
    kk = k1[:, None, None] + DFT_N1 * k2[None, :, None]
    ang = 2 * np.pi * ((kk * l2[None, None, :]) % L) / L
    ct, st = np.cos(ang) / np.sqrt(L), np.sin(ang) / np.sqrt(L)
    m3 = np.concatenate([np.concatenate([ct, st], axis=2), np.concatenate([st, -ct], axis=2)], axis=1)
    c = np.arange(FW)
    same = (c[:, None] // FGW) == (c[None, :] // FGW)
    ac = 2 * np.pi * (((c[:, None] % FGW) * (c[None, :] % FGW)) % FGW) / FGW
    e = np.concatenate([np.where(same, np.cos(ac), 0.0), -np.where(same, np.sin(ac), 0.0)], axis=0) / 8.0
    p = np.arange(LC)
    ap = 2 * np.pi * ((p[:, None] * p[None, :]) % LC) / LC
    return tuple(t.astype(np.float32) for t in (a1, m3, e, np.cos(ap) / 16.0, np.sin(ap) / 16.0))


NA_DR = 16


def _na_toeplitz_tables():
    src = np.full((LANE,), -1, np.int64)
    src[0:16] = np.arange(0, 16) + 15
    src[113:128] = np.arange(113, 128) - 113
    src[49:80] = np.arange(49, 80) - 49
    qc = np.arange(GRID_W)[:, None]
    kc = np.arange(LANE)[None, :] % GRID_W
    ws = np.clip(qc - 8, 0, GRID_W - 16)
    mask = np.where((kc >= ws) & (kc < ws + 16), 0.0, NEG).astype(np.float32)
    return src, mask


def _dot(a, b):
    return jnp.dot(a, b, preferred_element_type=F32)


def _dot_nt(a, b):
    return lax.dot_general(a, b, (((1,), (1,)), ((), ())), preferred_element_type=F32)


def _silu(x):
    return x * jax.nn.sigmoid(x)


def _rmsnorm(x, g):
    return x * lax.rsqrt(jnp.mean(x * x, axis=-1, keepdims=True) + EPS) * g


def _ada_kernel(c_ref, w_ref, b_ref, o_ref):
    s = _silu(c_ref[...]).astype(BF16)
    o_ref[0] = _dot(s, w_ref[0].astype(BF16)) + b_ref[0]


def _proj_kernel(x_ref, xc_ref, mods_ref, g_ref, w_ref, pqk_ref, pv_ref, cos_ref, sin_ref, lg_ref,
                 f_ref, o_ref, fc_ref, oc_ref, s0_ref, wret_ref, *, mod_row):
    s = pl.program_id(0)

    def project(x, mod, rope, f_out, o_out):
        hb = (_rmsnorm(x, g_ref[...]) * (1.0 + mod[1:2, :]) + mod[0:1, :]).astype(BF16)
        f_out[...] = _dot(hb, w_ref[:, :FW]).astype(BF16)
        k_scale = RD ** -0.5
        for c0, scale in ((C_RQ, None), (C_RK, k_scale)):
            t4 = _dot(hb, wret_ref[:, c0:c0 + RH * LANE])
            for hh in range(RH):
                t = t4[:, LANE * hh:LANE * (hh + 1)]
                if rope:
                    t = t * cos_ref[...] + pltpu.roll(t, 64, 1) * sin_ref[...]
                if scale is not None:
                    t = t * scale
                o_out[:, c0 + LANE * hh:c0 + LANE * (hh + 1)] = t.astype(BF16)
        o_out[:, C_RV:C_RG] = _dot(hb, wret_ref[:, C_RV:C_RG]).astype(BF16)
        o_out[:, C_RG:C_NQ] = _silu(_dot(hb, wret_ref[:, C_RG:C_NQ])).astype(BF16)
        o_out[:, C_NQ:] = _dot(hb, w_ref[:, W_NQ:]).astype(BF16)

    @pl.when(s == 0)
    def _():
        for slot, (src0, sel_ref) in enumerate(((W_RQ, pqk_ref), (W_RK, pqk_ref), (W_RV, pv_ref), (W_RG, pv_ref))):
            wret_ref[:, RH * LANE * slot:RH * LANE * (slot + 1)] = _dot(
                w_ref[:, src0:src0 + RH * RD], sel_ref[...]).astype(BF16)
        project(xc_ref[...], mods_ref[B], False, fc_ref, oc_ref)
        idx = lax.broadcasted_iota(jnp.int32, (LC, LANE), 0).astype(F32)
        for b in range(B):
            for h in range(RH):
                k = oc_ref[LC * b:LC * (b + 1), C_RK + LANE * h:C_RK + LANE * (h + 1)].astype(F32)
                v = oc_ref[LC * b:LC * (b + 1), C_RV + LANE * h:C_RV + LANE * (h + 1)]
                wf = jnp.exp((LC - 1.0 - idx) * lg_ref[0, h])
                wb = jnp.exp(idx * lg_ref[1, h])
                s0_ref[b, h, 0] = _dot((k * wf).T.astype(BF16), v)
                s0_ref[b, h, 1] = _dot((k * wb).T.astype(BF16), v)

    @pl.when(s > 0)
    def _():
        project(x_ref[...], mods_ref[mod_row(s - 1)], True, f_ref, o_ref)


def _four1_kernel(a_ref, x_ref, y_ref):
    x = x_ref[0].reshape(DFT_N1 * F1_R, FW)
    y = _dot(a_ref[...], x).astype(BF16)
    y_ref[0] = y.reshape(2 * DFT_N1, F1_R, FW)


def _four3_kernel(m_ref, y_ref, e_ref, wf_ref, o_ref):
    ew = _dot(e_ref[...], wf_ref[...]).astype(BF16)
    for b in range(B):
        for kk in range(F3_K):
            y = jnp.concatenate([y_ref[b, 0, kk], y_ref[b, 1, kk]], axis=0)
            z = _dot(m_ref[kk], y).astype(BF16)
            o_ref[b, kk] = _dot(jnp.concatenate([z[:DFT_N2], z[DFT_N2:]], axis=1), ew)


def _decay_matrix(n, lgf, lgb):
    ii = lax.broadcasted_iota(jnp.int32, (n, n), 0)
    jj = lax.broadcasted_iota(jnp.int32, (n, n), 1)
    diff = (ii - jj).astype(F32)
    fwd = jnp.where(diff >= 0, jnp.exp(jnp.maximum(diff, 0.0) * lgf), 0.0)
    bwd = jnp.where(diff <= 0, jnp.exp(jnp.maximum(-diff, 0.0) * lgb), 0.0)
    return fwd + bwd


def _ret_readout(o, gate, gain):
    ms = jnp.sum(o * o, axis=-1, keepdims=True) * (1.0 / RD)
    return (o * lax.rsqrt(ms + EPS) * gain * gate.astype(F32)).astype(BF16)


def _ret_kernel(lg_ref, q_ref, k_ref, v_ref, g_ref, s0_ref, gain_ref, o_ref, acc_ref, stf_ref, stb_ref):
    h = pl.program_id(1)
    lgf = lg_ref[0, h]
    lgb = lg_ref[1, h]
    c = RET_C
    n = L // c
    dmat = _decay_matrix(c, lgf, lgb)
    idx = lax.broadcasted_iota(jnp.int32, (c, LANE), 0).astype(F32)
    qdf = jnp.exp((idx + 1.0) * lgf)
    kdf = jnp.exp((c - 1.0 - idx) * lgf)
    qdb = jnp.exp((c - idx) * lgb)
    kdb = jnp.exp(idx * lgb)
    cdf = jnp.exp(jnp.full((1, LANE), float(c), F32) * lgf)
    cdb = jnp.exp(jnp.full((1, LANE), float(c), F32) * lgb)
    gain = gain_ref[...]

    dirs = ((stf_ref, qdf, kdf, cdf), (stb_ref, qdb, kdb, cdb))
    stf_ref[...] = s0_ref[0, 0, 0]
    stb_ref[...] = s0_ref[0, 0, 1]

    def visit(r0, direction, second):
        st_ref, qdec, kdec, cdec = dirs[direction]
        rows = pl.ds(pl.multiple_of(r0, c), c)
        q = q_ref[rows, :]
        k = k_ref[rows, :]
        v = v_ref[rows, :]
        cross = _dot(q, st_ref[...].astype(BF16)) * qdec
        if second:
            o_ref[rows, :] = _ret_readout(acc_ref[rows, :] + cross, g_ref[rows, :], gain)
        else:
            s = _dot_nt(q, k) * dmat
            acc_ref[rows, :] = _dot(s.astype(BF16), v) + cross
        kt = (k.astype(F32) * kdec).T.astype(BF16)
        st_ref[...] = st_ref[...] * cdec + _dot(kt, v)

    def body(second):
        def step(i, carry):
            visit(i * c, 0, second)
            visit((n - 1 - i) * c, 1, second)
            return carry
        return step

    lax.fori_loop(0, n // 2, body(False), 0, unroll=16)
    lax.fori_loop(n // 2, n, body(True), 0, unroll=16)


def _pair_attention(q, score_fn, value_fn):
    lane = lax.broadcasted_iota(jnp.int32, q.shape, 1)
    outs = []
    for hh in range(2):
        sel = (lane >= ND * hh) & (lane < ND * (hh + 1))
        qm = jnp.where(sel, q, jnp.zeros_like(q))
        scores = score_fn(qm, hh)
        m = functools.reduce(jnp.maximum, [jnp.max(s, axis=-1, keepdims=True) for s in scores])
        ps = [jnp.exp2(s - m) for s in scores]
        den = functools.reduce(jnp.add, [jnp.sum(p, axis=-1, keepdims=True) for p in ps])
        outs.append(value_fn([p.astype(BF16) for p in ps]) / den)
    return jnp.where(lane < ND, outs[0], outs[1])


def _ctx_mix_kernel(lg_ref, f_ref, p_ref, gain_ref, wf_ref, c_ref, s_ref, e_ref, fo_ref, ro_ref, no_ref):
    f = f_ref[...]
    pr = _dot(c_ref[...], f)
    qi = _dot(s_ref[...], f)
    fr = _dot(pr.astype(BF16), e_ref[:FW, :]) + _dot(qi.astype(BF16), e_ref[FW:, :])
    fo_ref[...] = _dot(fr.astype(BF16), wf_ref[...]).astype(BF16)
    for h in range(RH):
        sl = lambda c0: slice(c0 + LANE * h, c0 + LANE * (h + 1))
        dmat = _decay_matrix(LC, lg_ref[0, h], lg_ref[1, h])
        s = _dot_nt(p_ref[:, sl(C_RQ)], p_ref[:, sl(C_RK)]) * dmat
        o = _dot(s.astype(BF16), p_ref[:, sl(C_RV)])
        ro_ref[:, LANE * h:LANE * (h + 1)] = _ret_readout(o, p_ref[:, sl(C_RG)], gain_ref[:, LANE * h:LANE * (h + 1)])
    for pair in range(NPAIR):
        sl = lambda c0: slice(c0 + LANE * pair, c0 + LANE * (pair + 1))
        k = p_ref[:, sl(C_NK)]
        v = p_ref[:, sl(C_NV)]
        out = _pair_attention(p_ref[:, sl(C_NQ)], lambda qm, hh: [_dot_nt(qm, k)], lambda ps: _dot(ps[0], v))
        no_ref[:, LANE * pair:LANE * (pair + 1)] = out.astype(BF16)


def _na_key_start(r0):
    return min(max(r0 - 4, 0), GRID_H - NA_KR)


def _na_bias_kernel(base_ref, mask_ref, o_ref, tz_ref):
    for dr in range(NA_DR - 1):
        t = jnp.broadcast_to(base_ref[0, dr:dr + 1, :], (GRID_W, LANE))
        t = pltpu.roll(t, 0, 1, stride=1, stride_axis=0)
        tz_ref[dr] = t * LOG2E + mask_ref[...]
    tz_ref[NA_DR - 1] = jnp.full((GRID_W, LANE), NEG, F32)
    low_half = lax.broadcasted_iota(jnp.int32, (GRID_W, LANE), 1) < GRID_W
    for var, r0 in enumerate((0, 2 * NA_R, GRID_H - NA_R)):
        ks = _na_key_start(r0)
        for rl in range(NA_R):
            r = r0 + rl
            rs = min(max(r - 4, 0), GRID_H - 8)
            slot = [kr - r + 7 if rs <= kr < rs + 8 else NA_DR - 1 for kr in range(ks, ks + NA_KR)]
            for m in range(NA_KR // 2):
                o_ref[0, var, GRID_W * rl:GRID_W * (rl + 1), LANE * m:LANE * (m + 1)] = jnp.where(
                    low_half, tz_ref[slot[2 * m]], tz_ref[slot[2 * m + 1]])


def _natten_kernel(q_ref, k_ref, v_ref, kc_ref, vc_ref, bias_ref, o_ref, pa_ref, pb_ref):
    nq = NA_R * GRID_W
    nk = NA_KR * GRID_W
    kc = kc_ref[...]
    vc = vc_ref[...]
    lane = lax.broadcasted_iota(jnp.int32, (nq, LANE), 1)

    def rows(i):
        ks = jnp.clip(NA_R * i - 4, 0, GRID_H - NA_KR)
        return pl.ds(pl.multiple_of(ks * GRID_W, nq), nk), pl.ds(pl.multiple_of(i * nq, nq), nq)

    def probabilities(i, p_ref):
        i = jnp.minimum(i, NA_NB - 1)
        var = jnp.where(i == 0, 0, jnp.where(i == NA_NB - 1, 2, 1))
        krows, qrows = rows(i)
        q = q_ref[qrows, :]
        kw = k_ref[krows, :]
        for hh in range(2):
            qm = jnp.where((lane >= ND * hh) & (lane < ND * (hh + 1)), q, jnp.zeros_like(q))
            s_win = _dot_nt(qm, kw) + bias_ref[hh, var]
            s_ctx = _dot_nt(qm, kc)
            m = jnp.maximum(jnp.max(s_win, axis=-1, keepdims=True), jnp.max(s_ctx, axis=-1, keepdims=True))
            p_ref[hh, :, :nk] = jnp.exp2(s_win - m).astype(BF16)
            p_ref[hh, :, nk:] = jnp.exp2(s_ctx - m).astype(BF16)

    def values(i, p_ref):
        krows, qrows = rows(i)
        vals = jnp.concatenate([v_ref[krows, :], vc], axis=0)
        vals = jnp.concatenate([vals, jnp.ones_like(vals)], axis=1)
        o2 = _dot(p_ref[...].reshape(2 * nq, nk + LC), vals)
        outs = [o[:, :LANE] / o[:, LANE:] for o in (o2[:nq], o2[nq:])]
        o_ref[qrows, :] = jnp.where(lane < ND, outs[0], outs[1]).astype(BF16)

    probabilities(0, pa_ref)

    def step(j, carry):
        values(2 * j, pa_ref)
        probabilities(2 * j + 1, pb_ref)
        values(2 * j + 1, pb_ref)
        probabilities(2 * j + 2, pa_ref)
        return carry

    lax.fori_loop(0, NA_NB // 2, step, 0, unroll=8)


def _ffn_kernel(*refs, final, ctx_tile, mod_row):
    refs = list(refs)
    x_ref, f_ref, r_ref, n_ref = refs[:4]
    del refs[:4]
    if ctx_tile:
        xc_ref, fc_ref, rc_ref, nc_ref = refs[:4]
        del refs[:4]
    mods_ref, g_ref, wo_ref, sel_ref, w1_ref, w3_ref, w2_ref = refs[:7]
    del refs[:7]
    gf_ref = refs.pop(0) if final else None
    o_ref = refs.pop(0)
    oc_ref = refs.pop(0) if ctx_tile else None
    acc_ref, wor_ref = refs
    s = pl.program_id(0)

    @pl.when(s == 0)
    def _():
        wor_ref[...] = _dot(sel_ref[...], wo_ref[FW:FW + RH * RD, :]).astype(BF16)

    def tile(x, f, r, n, mod, out_ref):
        y = _dot(f.astype(BF16), wo_ref[:FW, :]) + _dot(r, wor_ref[...]) + _dot(n, wo_ref[FW + RH * RD:, :])
        x1 = x + mod[2:3, :] * y
        hb = (_rmsnorm(x1, g_ref[...]) * (1.0 + mod[4:5, :]) + mod[3:4, :]).astype(BF16)
        fc = 256
        for c in range(DFF // fc):
            a = _dot(hb, w1_ref[:, c * fc:(c + 1) * fc])
            b = _dot(hb, w3_ref[:, c * fc:(c + 1) * fc])
            acc_ref[:, c * fc:(c + 1) * fc] = (_silu(a) * b).astype(BF16)
        x2 = x1 + mod[5:6, :] * _dot(acc_ref[...], w2_ref[...])
        if final:
            x2 = _rmsnorm(x2, gf_ref[...])
        out_ref[...] = x2

    def latent_tile(t):
        f = jnp.concatenate([f_ref[:, r, :] for r in range(f_ref.shape[1])], axis=0)
        tile(x_ref[...], f, r_ref[...], n_ref[...], mods_ref[mod_row(t)], o_ref)

    if ctx_tile:
        @pl.when(s == 0)
        def _():
            tile(xc_ref[...], fc_ref[...], rc_ref[...], nc_ref[...], mods_ref[B], oc_ref)

        @pl.when(s > 0)
        def _():
            latent_tile(s - 1)
    else:
        latent_tile(s)


def _params(sem):
    return pltpu.CompilerParams(dimension_semantics=sem, vmem_limit_bytes=VMEM_LIMIT)


def _const_spec(shape):
    nd = len(shape)
    return pl.BlockSpec(shape, lambda *_: (0,) * nd, pipeline_mode=pl.Buffered(1))


def _smem_spec():
    return pl.BlockSpec(memory_space=pltpu.SMEM)


def _ada(cvec, w_ada, b_ada):
    tn = 2048
    return pl.pallas_call(
        _ada_kernel,
        grid=(DEPTH, 6 * D // tn),
        in_specs=[pl.BlockSpec((8, D), lambda i, j: (0, 0)),
                  pl.BlockSpec((1, D, tn), lambda i, j: (i, 0, j)),
                  pl.BlockSpec((1, 1, tn), lambda i, j: (i, 0, j))],
        out_specs=pl.BlockSpec((1, 8, tn), lambda i, j: (i, 0, j)),
        out_shape=jax.ShapeDtypeStruct((DEPTH, 8, 6 * D), F32),
        compiler_params=_params(("arbitrary", "arbitrary")),
        name="ada",
    )(cvec, w_ada, b_ada.reshape(DEPTH, 1, 6 * D))


def _mod_row(tm):
    per_batch = L // tm
    return lambda tile: tile // per_batch


def _layer_spec(shape, layer):
    nd = len(shape)
    return pl.BlockSpec((None,) + tuple(shape), lambda *_: (layer,) + (0,) * nd, pipeline_mode=pl.Buffered(1))


def _proj(x2d, xc2d, mods, g, w_all, layer, sel_qk, sel_v, rope_tabs, lg):
    rows, crows = x2d.shape[0], xc2d.shape[0]
    assert crows == B * LC
    tm = TM_PROJ
    per_batch = L // tm
    tile = lambda i: jnp.maximum(i - 1, 0)
    return pl.pallas_call(
        functools.partial(_proj_kernel, mod_row=_mod_row(tm)),
        grid=(rows // tm + 1,),
        in_specs=[pl.BlockSpec((tm, D), lambda i: (tile(i), 0)),
                  _const_spec((crows, D)),
                  _const_spec((8, 6, D)),
                  _const_spec((1, D)),
                  _layer_spec((D, W_IN), layer),
                  _const_spec((RH * RD, RH * LANE)),
                  _const_spec((RH * RD, RH * LANE)),
                  pl.BlockSpec((tm, LANE), lambda i: (tile(i) % per_batch, 0)),
                  pl.BlockSpec((tm, LANE), lambda i: (tile(i) % per_batch, 0)),
                  _smem_spec()],
        out_specs=[pl.BlockSpec((tm, FW), lambda i: (tile(i), 0)),
                   pl.BlockSpec((tm, NP), lambda i: (tile(i), 0)),
                   pl.BlockSpec((crows, FW), lambda i: (0, 0)),
                   pl.BlockSpec((crows, NP), lambda i: (0, 0)),
                   pl.BlockSpec((B, RH, 2, LANE, LANE), lambda i: (0, 0, 0, 0, 0))],
        out_shape=[jax.ShapeDtypeStruct((rows, FW), BF16), jax.ShapeDtypeStruct((rows, NP), BF16),
                   jax.ShapeDtypeStruct((crows, FW), BF16), jax.ShapeDtypeStruct((crows, NP), BF16),
                   jax.ShapeDtypeStruct((B, RH, 2, LANE, LANE), F32)],
        scratch_shapes=[pltpu.VMEM((D, 4 * RH * LANE), BF16)],
        compiler_params=_params(("arbitrary",)),
        name="proj",
    )(x2d, xc2d, mods, g.reshape(1, D), w_all, sel_qk, sel_v, *rope_tabs, lg)


def _fourier(f2d, wf, a1, m3, e):
    y = pl.pallas_call(
        _four1_kernel,
        grid=(B, DFT_N2 // F1_R),
        in_specs=[_const_spec((2 * DFT_N1 * F1_R, DFT_N1 * F1_R)),
                  pl.BlockSpec((1, DFT_N1, F1_R, FW), lambda b, j: (b, 0, j, 0))],
        out_specs=pl.BlockSpec((1, 2 * DFT_N1, F1_R, FW), lambda b, j: (b, 0, j, 0)),
        out_shape=jax.ShapeDtypeStruct((B, 2 * DFT_N1, DFT_N2, FW), BF16),
        compiler_params=_params(("arbitrary", "arbitrary")),
        name="four1",
    )(a1, f2d.reshape(B, DFT_N1, DFT_N2, FW))
    out = pl.pallas_call(
        _four3_kernel,
        grid=(DFT_N1 // F3_K,),
        in_specs=[pl.BlockSpec((F3_K, 2 * DFT_N2, 2 * DFT_N2), lambda k: (k, 0, 0)),
                  pl.BlockSpec((B, 2, F3_K, DFT_N2, FW), lambda k: (0, 0, k, 0, 0)),
                  _const_spec((2 * FW, FW)),
                  _const_spec((FW, FW))],
        out_specs=pl.BlockSpec((B, F3_K, DFT_N2, FW), lambda k: (0, k, 0, 0)),
        out_shape=jax.ShapeDtypeStruct((B, DFT_N1, DFT_N2, FW), F32),
        compiler_params=_params(("arbitrary",)),
        name="four3",
    )(m3, y.reshape(B, 2, DFT_N1, DFT_N2, FW), e, wf)
    return out.reshape(B * DFT_N1, DFT_N2, FW)


def _retention(p, s0, lg, gain):
    col = lambda c0: (lambda b, h: (b, c0 // LANE + h))
    return pl.pallas_call(
        _ret_kernel,
        grid=(B, RH),
        in_specs=[_smem_spec(),
                  pl.BlockSpec((L, LANE), col(C_RQ)),
                  pl.BlockSpec((L, LANE), col(C_RK)),
                  pl.BlockSpec((L, LANE), col(C_RV)),
                  pl.BlockSpec((L, LANE), col(C_RG)),
                  pl.BlockSpec((1, 1, 2, LANE, LANE), lambda b, h: (b, h, 0, 0, 0)),
                  pl.BlockSpec((1, LANE), lambda b, h: (0, h))],
        out_specs=pl.BlockSpec((L, LANE), lambda b, h: (b, h)),
        out_shape=jax.ShapeDtypeStruct((B * L, RH * LANE), BF16),
        scratch_shapes=[pltpu.VMEM((L, LANE), F32), pltpu.VMEM((LANE, LANE), F32), pltpu.VMEM((LANE, LANE), F32)],
        compiler_params=_params(("arbitrary", "arbitrary")),
        name="ret",
    )(lg, p, p, p, p, s0, gain)


def _ctx_mix(fc, pc, lg, gain, wf, c256, s256, e):
    return pl.pallas_call(
        _ctx_mix_kernel,
        grid=(B,),
        in_specs=[_smem_spec(),
                  pl.BlockSpec((LC, FW), lambda b: (b, 0)),
                  pl.BlockSpec((LC, NP), lambda b: (b, 0)),
                  _const_spec((1, RH * LANE)),
                  _const_spec((FW, FW)),
                  _const_spec((LC, LC)),
                  _const_spec((LC, LC)),
                  _const_spec((2 * FW, FW))],
        out_specs=[pl.BlockSpec((LC, FW), lambda b: (b, 0)),
                   pl.BlockSpec((LC, RH * LANE), lambda b: (b, 0)),
                   pl.BlockSpec((LC, NH * ND), lambda b: (b, 0))],
        out_shape=[jax.ShapeDtypeStruct((B * LC, FW), BF16),
                   jax.ShapeDtypeStruct((B * LC, RH * LANE), BF16),
                   jax.ShapeDtypeStruct((B * LC, NH * ND), BF16)],
        compiler_params=_params(("arbitrary",)),
        name="ctx_mix",
    )(lg, fc, pc, gain, wf, c256, s256, e)


def _na_bias(base, mask):
    nq, nk = NA_R * GRID_W, NA_KR * GRID_W
    return pl.pallas_call(
        _na_bias_kernel,
        grid=(base.shape[0],),
        in_specs=[pl.BlockSpec((1, NA_DR, LANE), lambda h: (h, 0, 0)),
                  _const_spec((GRID_W, LANE))],
        out_specs=pl.BlockSpec((1, 3, nq, nk), lambda h: (h, 0, 0, 0)),
        out_shape=jax.ShapeDtypeStruct((base.shape[0], 3, nq, nk), F32),
        scratch_shapes=[pltpu.VMEM((NA_DR, GRID_W, LANE), F32)],
        compiler_params=_params(("arbitrary",)),
        name="na_bias",
    )(base, mask)


def _natten(p, pc, bias, layer):
    nq, nk = NA_R * GRID_W, NA_KR * GRID_W
    return pl.pallas_call(
        _natten_kernel,
        grid=(B, NPAIR),
        in_specs=[pl.BlockSpec((L, LANE), lambda b, pr: (b, C_NQ // LANE + pr)),
                  pl.BlockSpec((L, LANE), lambda b, pr: (b, C_NK // LANE + pr)),
                  pl.BlockSpec((L, LANE), lambda b, pr: (b, C_NV // LANE + pr)),
                  pl.BlockSpec((LC, LANE), lambda b, pr: (b, C_NK // LANE + pr)),
                  pl.BlockSpec((LC, LANE), lambda b, pr: (b, C_NV // LANE + pr)),
                  pl.BlockSpec((2, 3, nq, nk), lambda b, pr: (NPAIR * layer + pr, 0, 0, 0))],
        out_specs=pl.BlockSpec((L, LANE), lambda b, pr: (b, pr)),
        out_shape=jax.ShapeDtypeStruct((B * L, NH * ND), BF16),
        scratch_shapes=[pltpu.VMEM((2, nq, nk + LC), BF16)] * 2,
        compiler_params=_params(("arbitrary", "arbitrary")),
        name="natten",
    )(p, p, p, pc, pc, bias)


def _ffn(x2d, four, ret, na, ctx_acts, mods, g, wo_all, sel_v_t, w1_all, w3_all, w2_all, layer, g_final):
    rows = x2d.shape[0]
    final = g_final is not None
    ctx_tile = ctx_acts is not None
    off = 1 if ctx_tile else 0
    per_batch = L // TM
    tile = lambda i: jnp.maximum(i - off, 0)
    row = lambda w: pl.BlockSpec((TM, w), lambda i: (tile(i), 0))
    four_spec = pl.BlockSpec((DFT_N1, TM // DFT_N1, FW), lambda i: (tile(i) // per_batch, tile(i) % per_batch, 0))
    in_specs = [row(D), four_spec, row(RH * LANE), row(NH * ND)]
    args = [x2d, four, ret, na]
    if ctx_tile:
        assert all(a.shape[0] == TM for a in ctx_acts)
        in_specs += [_const_spec(a.shape) for a in ctx_acts]
        args += list(ctx_acts)
    in_specs += [_const_spec((8, 6, D)), _const_spec((1, D)),
                 _layer_spec((D, D), layer), _const_spec((RH * LANE, RH * RD)),
                 _layer_spec((D, DFF), layer), _layer_spec((D, DFF), layer), _layer_spec((DFF, D), layer)]
    args += [mods, g.reshape(1, D), wo_all, sel_v_t, w1_all, w3_all, w2_all]
    if final:
        in_specs.append(_const_spec((1, D)))
        args.append(g_final.reshape(1, D))
    out_specs = [row(D)]
    out_shape = [jax.ShapeDtypeStruct((rows, D), F32)]
    if ctx_tile:
        out_specs.append(pl.BlockSpec((TM, D), lambda i: (0, 0)))
        out_shape.append(jax.ShapeDtypeStruct((TM, D), F32))
    outs = pl.pallas_call(
        functools.partial(_ffn_kernel, final=final, ctx_tile=ctx_tile, mod_row=_mod_row(TM)),
        grid=(rows // TM + off,),
        in_specs=in_specs,
        out_specs=out_specs,
        out_shape=out_shape,
        scratch_shapes=[pltpu.VMEM((TM, DFF), BF16), pltpu.VMEM((RH * LANE, D), BF16)],
        compiler_params=_params(("arbitrary",)),
        name="ffn_final" if final else "ffn",
    )(*args)
    return outs if ctx_tile else outs[0]


def _take_padded(a, src, axis):
    pieces = []
    i, n = 0, len(src)
    while i < n:
        j = i + 1
        if src[i] < 0:
            while j < n and src[j] < 0:
                j += 1
            shape = list(a.shape)
            shape[axis] = j - i
            pieces.append(jnp.zeros(shape, a.dtype))
        else:
            while j < n and src[j] == src[j - 1] + 1:
                j += 1
            pieces.append(lax.slice_in_dim(a, int(src[i]), int(src[j - 1]) + 1, axis=axis))
        i = j
    return jnp.concatenate(pieces, axis=axis)


def kernel(x, c, ctx, c_ctx, w_ada, b_ada, g_mix, w_in, ret_decay_logit, ret_norm_g, w_four,
           na_rpb, w_out, g_ffn, w1, w3, w2, g_final):
    src_g = _gain_sources()
    rope_tabs = tuple(jnp.asarray(t) for t in _rope_tables())
    a1, m3, e, c256, s256 = (jnp.asarray(t).astype(BF16) for t in _dft_tables())
    sel_qk, sel_v = (jnp.asarray(t).astype(BF16) for t in _head_selectors())
    sel_v_t = sel_v.T
    tz_src, tz_mask = _na_toeplitz_tables()
    col_scale = np.ones((1, 1, W_IN), np.float32)
    col_scale[..., W_NQ:W_NK] = LOG2E * ND ** -0.5
    w_in_b = (w_in * jnp.asarray(col_scale)).astype(BF16)
    w_out_b = w_out.astype(BF16)
    w1b, w3b, w2b = w1.astype(BF16), w3.astype(BF16), w2.astype(BF16)

    tz_base = jnp.pad(_take_padded(na_rpb.astype(F32).reshape(DEPTH * NH, 15, 31), tz_src, 2), ((0, 0), (0, 1), (0, 0)))
    na_bias = _na_bias(tz_base, jnp.asarray(tz_mask))

    cvec = jnp.zeros((8, D), F32).at[0:B].set(c).at[B].set(c_ctx)
    mods_all = _ada(cvec, w_ada, b_ada).reshape(DEPTH, 8, 6, D)

    xl = x.reshape(B * L, D)
    xc = ctx.reshape(B * LC, D)
    for i in range(DEPTH):
        last = i == DEPTH - 1
        mods = mods_all[i]
        gain = _take_padded(ret_norm_g[i], src_g, 0).reshape(1, RH * LANE)
        wfour = w_four[i].astype(BF16)
        lg = jax.nn.log_sigmoid(ret_decay_logit[i].astype(F32))

        f_l, p_l, f_c, p_c, s0 = _proj(xl, xc, mods, g_mix[i], w_in_b, i, sel_qk, sel_v, rope_tabs, lg)
        four_l = _fourier(f_l, wfour, a1, m3, e)
        ret_l = _retention(p_l, s0, lg, gain)
        na_l = _natten(p_l, p_c, na_bias, i)
        if last:
            xl = _ffn(xl, four_l, ret_l, na_l, None, mods, g_ffn[i], w_out_b, sel_v_t, w1b, w3b, w2b, i, g_final)
        else:
            ctx_acts = (xc,) + tuple(_ctx_mix(f_c, p_c, lg, gain, wfour, c256, s256, e))
            xl, xc = _ffn(xl, four_l, ret_l, na_l, ctx_acts, mods, g_ffn[i], w_out_b, sel_v_t, w1b, w3b, w2b, i, None)
    return xl.reshape(B, L, D)
```
